```python
import math
import jax, jax.numpy as jnp
from jax import lax
import numpy as np

D_MODEL = 1024
BATCH = 4
SEQ = 8192
DEPTH = 2

MOBA_HEAD_DIM = 128
MOBA_HEADS = D_MODEL // (2 * MOBA_HEAD_DIM)
MOBA_BLOCK = 256
MOBA_TOPK = 3
MOBA_QBLOCK = 32
GLA_DV = 128
GLA_HEADS = D_MODEL // (2 * GLA_DV)
GLA_DK = GLA_DV // 2
GLA_GATE_RANK = 16
GLA_GATE_NORM = 16.0
GLA_CHUNK = 64
GDN_DK = 128
GDN_DV = 128
GDN_HEADS = D_MODEL // GDN_DV
GDN_CONV = 4
GDN_CHUNK = 64
RPE_BUCKETS = 32
RPE_MAX_DIST = 2048
N_EXPERTS = 64
TOP_K = 6
N_GROUPS = 8
TOPK_GROUPS = 4
D_EXPERT = 256
D_SHARED = 256
ROUTED_SCALE = 2.5
MOE_BLOCK = 128
DEEPNORM_ALPHA = float((2 * DEPTH) ** 0.25)
DEEPNORM_BETA = float((8 * DEPTH) ** -0.25)
LN_EPS = 1e-5
NORM_EPS = 1e-6
N_EVEN = (DEPTH + 1) // 2
N_ODD = DEPTH // 2

MOBA_W = MOBA_HEADS * MOBA_HEAD_DIM
GLA_QK_W = GLA_HEADS * GLA_DK
GLA_V_W = GLA_HEADS * GLA_DV
EVEN_SPLITS = (MOBA_W, MOBA_W, MOBA_W, GLA_QK_W, GLA_QK_W, GLA_V_W, GLA_V_W, GLA_GATE_RANK)
EVEN_IN = sum(EVEN_SPLITS)
EVEN_MIX = MOBA_W + GLA_V_W
GDN_QK_W = GDN_HEADS * GDN_DK
GDN_V_W = GDN_HEADS * GDN_DV
GDN_QKV_W = 2 * GDN_QK_W + GDN_V_W
ODD_SPLITS = (GDN_QK_W, GDN_QK_W, GDN_V_W, GDN_V_W, GDN_HEADS, GDN_HEADS)
ODD_IN = sum(ODD_SPLITS)
ODD_MIX = GDN_V_W

kernel_name = "hybrid_moba_gla_gdn_moe_block"

F32 = jnp.float32


def _split(t, sizes):
    return jnp.split(t, np.cumsum(sizes)[:-1].tolist(), axis=-1)


def layer_norm(x, g, b):
    xf = x.astype(F32)
    mu = jnp.mean(xf, -1, keepdims=True)
    var = jnp.mean(jnp.square(xf - mu), -1, keepdims=True)
    return ((xf - mu) * lax.rsqrt(var + LN_EPS)).astype(x.dtype) * g + b


def l2norm(t):
    tf = t.astype(F32)
    return tf * lax.rsqrt(jnp.sum(tf * tf, -1, keepdims=True) + NORM_EPS)


def gated_rmsnorm(o, gate, w):
    of = o.astype(F32)
    of = of * lax.rsqrt(jnp.mean(of * of, -1, keepdims=True) + NORM_EPS)
    return of * w.astype(F32) * jax.nn.silu(gate.astype(F32))


def rpe_bucket(dist):
    exact = RPE_BUCKETS // 2
    d = jnp.maximum(dist, 0)
    logd = jnp.log(jnp.maximum(d, 1).astype(F32) / exact)
    large = exact + (logd / math.log(RPE_MAX_DIST / exact) * (RPE_BUCKETS - exact)).astype(jnp.int32)
    large = jnp.minimum(large, RPE_BUCKETS - 1)
    return jnp.where(d < exact, d, large)


def causal_dwconv(x, w):
    k = w.shape[0]
    return lax.conv_general_dilated(x, w[:, None, :], window_strides=(1,), padding=[(k - 1, 0)],
                                    dimension_numbers=('NWC', 'WIO', 'NWC'),
                                    feature_group_count=x.shape[-1])


def moba_attention(q, k, v, rpe_bias):
    B, S, H, Dh = q.shape
    nkb = -(-S // MOBA_BLOCK)
    nqb = S // MOBA_QBLOCK
    kk = min(MOBA_TOPK, nkb)
    pad = nkb * MOBA_BLOCK - S

    def blocks(t):
        t = jnp.pad(t, ((0, 0), (0, pad), (0, 0), (0, 0)))
        return t.reshape(B, nkb, MOBA_BLOCK, H, Dh).transpose(0, 3, 1, 2, 4)

    kb, vb = blocks(k), blocks(v)
    k_mean = jnp.mean(kb.astype(F32), axis=3)
    qb = q.reshape(B, nqb, MOBA_QBLOCK, H, Dh).transpose(1, 0, 3, 2, 4)
    rpe_hb = rpe_bias.T.astype(F32)
    scale = Dh ** -0.5
    b_idx = jnp.arange(B)[:, None, None, None]
    h_idx = jnp.arange(H)[None, :, None, None]
    offs = jnp.arange(MOBA_BLOCK)

    def one_block(args):
        qi, qblk = args
        q0 = qi * MOBA_QBLOCK
        qpos = q0 + jnp.arange(MOBA_QBLOCK)
        cur = q0 // MOBA_BLOCK
        gate = jnp.einsum('bhqd,bhnd->bhqn', qblk.astype(F32), k_mean)
        gate = jnp.where(jnp.arange(nkb) < cur, gate, -jnp.inf)
        _, sel = lax.top_k(gate, kk)
        valid = jnp.arange(kk) < cur
        k_sel = kb[b_idx, h_idx, sel]
        v_sel = vb[b_idx, h_idx, sel]
        kpos = sel[..., None] * MOBA_BLOCK + offs
        bias_sel = rpe_hb[h_idx[..., None], rpe_bucket(qpos[:, None, None] - kpos)]
        l_sel = jnp.einsum('bhqd,bhqrtd->bhqrt', qblk, k_sel).astype(F32) * scale + bias_sel
        l_sel = jnp.where(valid[:, None], l_sel, -jnp.inf)
        k_own = lax.dynamic_index_in_dim(kb, cur, axis=2, keepdims=False)
        v_own = lax.dynamic_index_in_dim(vb, cur, axis=2, keepdims=False)
        dist = qpos[:, None] - (cur * MOBA_BLOCK + offs)[None, :]
        l_own = jnp.einsum('bhqd,bhtd->bhqt', qblk, k_own).astype(F32) * scale + rpe_hb[:, rpe_bucket(dist)]
        l_own = jnp.where(dist >= 0, l_own, -jnp.inf)
        logits = jnp.concatenate([l_sel.reshape(B, H, MOBA_QBLOCK, kk * MOBA_BLOCK), l_own], axis=-1)
        p = jax.nn.softmax(logits, axis=-1).astype(v.dtype)
        p_sel = p[..., :kk * MOBA_BLOCK].reshape(B, H, MOBA_QBLOCK, kk, MOBA_BLOCK)
        p_own = p[..., kk * MOBA_BLOCK:]
        return (jnp.einsum('bhqrt,bhqrtd->bhqd', p_sel, v_sel)
                + jnp.einsum('bhqt,bhtd->bhqd', p_own, v_own))

    out = lax.map(one_block, (jnp.arange(nqb), qb))
    return out.transpose(1, 0, 3, 2, 4).reshape(B, S, H, Dh)


def gla_chunked(q, k, v, g):
    B, S, H, DK = q.shape
    DV = v.shape[-1]
    n = S // GLA_CHUNK

    def chunks(t):
        return t.astype(F32).reshape(B, n, GLA_CHUNK, H, t.shape[-1]).transpose(1, 0, 3, 2, 4)

    qc = chunks(q) * DK ** -0.5
    kc, vc, gc = chunks(k), chunks(v), chunks(g)
    b = jnp.cumsum(gc, axis=3)
    b_last = b[:, :, :, -1:, :]
    q_e = qc * jnp.exp(b)
    k_e = kc * jnp.exp(-b)
    k_end = kc * jnp.exp(b_last - b)
    causal = jnp.tril(jnp.ones((GLA_CHUNK, GLA_CHUNK), F32))
    o_intra = jnp.einsum('nbhij,nbhjv->nbhiv', jnp.einsum('nbhid,nbhjd->nbhij', q_e, k_e) * causal, vc)

    def step(state, xs):
        q_i, k_i, v_i, d_i = xs
        o_i = jnp.einsum('bhid,bhdv->bhiv', q_i, state)
        state = state * d_i[..., None] + jnp.einsum('bhjd,bhjv->bhdv', k_i, v_i)
        return state, o_i

    s0 = jnp.zeros((B, H, DK, DV), F32)
    _, o_inter = lax.scan(step, s0, (q_e, k_end, vc, jnp.exp(b_last[:, :, :, 0, :])))
    return (o_intra + o_inter).transpose(1, 0, 3, 2, 4).reshape(B, S, H, DV)


def gated_delta_chunked(q, k, v, g, beta):
    B, S, H, DK = q.shape
    DV = v.shape[-1]
    C = GDN_CHUNK
    n = S // C

    def chunks(t):
        return t.astype(F32).reshape(B, n, C, H, t.shape[-1]).transpose(1, 0, 3, 2, 4)

    qc = chunks(q) * DK ** -0.5
    kc, vc = chunks(k), chunks(v)
    gc = jnp.cumsum(chunks(g[..., None])[..., 0], axis=-1)
    bc = chunks(beta[..., None])
    incl = jnp.tril(jnp.ones((C, C), bool))
    strict = jnp.tril(jnp.ones((C, C), bool), -1)
    diff = gc[..., :, None] - gc[..., None, :]
    decay = jnp.where(incl, jnp.exp(jnp.where(incl, diff, 0.0)), 0.0)
    k_beta = kc * bc
    v_beta = vc * bc
    a = jnp.where(strict, jnp.einsum('nbhid,nbhjd->nbhij', k_beta, kc) * decay, 0.0)
    eye = jnp.eye(C, dtype=F32)
    t_mat = lax.linalg.triangular_solve(a + eye, jnp.broadcast_to(eye, a.shape), left_side=True,
                                        lower=True, unit_diagonal=True)
    w_val = t_mat @ v_beta
    k_cum = t_mat @ (k_beta * jnp.exp(gc)[..., None])
    attn = jnp.einsum('nbhid,nbhjd->nbhij', qc, kc) * decay
    q_g = qc * jnp.exp(gc)[..., None]
    g_last = gc[..., -1:]
    k_end = kc * jnp.exp(g_last - gc)[..., None]
    d_last = jnp.exp(g_last[..., 0])

    def step(state, xs):
        q_i, kc_i, w_i, at_i, ke_i, dl_i = xs
        v_new = w_i - jnp.einsum('bhcd,bhdv->bhcv', kc_i, state)
        o_i = jnp.einsum('bhcd,bhdv->bhcv', q_i, state) + jnp.einsum('bhij,bhjv->bhiv', at_i, v_new)
        state = state * dl_i[..., None, None] + jnp.einsum('bhcd,bhcv->bhdv', ke_i, v_new)
        return state, o_i

    s0 = jnp.zeros((B, H, DK, DV), F32)
    _, o = lax.scan(step, s0, (q_g, k_cum, w_val, attn, k_end, d_last))
    return o.transpose(1, 0, 3, 2, 4).reshape(B, S, H, DV)


def moba_gla_mixer(h, rpe_bias, w_in, gk_w2, gk_b, o_norm, w_out):
    B, S, _ = h.shape
    mq, mk, mv, gq, gk, gv, gg, glr = _split(h @ w_in, EVEN_SPLITS)
    o_a = moba_attention(mq.reshape(B, S, MOBA_HEADS, MOBA_HEAD_DIM),
                         mk.reshape(B, S, MOBA_HEADS, MOBA_HEAD_DIM),
                         mv.reshape(B, S, MOBA_HEADS, MOBA_HEAD_DIM), rpe_bias)
    log_gate = jax.nn.log_sigmoid((glr @ gk_w2 + gk_b).astype(F32)) / GLA_GATE_NORM
    o_b = gla_chunked(gq.reshape(B, S, GLA_HEADS, GLA_DK), gk.reshape(B, S, GLA_HEADS, GLA_DK),
                      gv.reshape(B, S, GLA_HEADS, GLA_DV), log_gate.reshape(B, S, GLA_HEADS, GLA_DK))
    o_b = gated_rmsnorm(o_b, gg.reshape(B, S, GLA_HEADS, GLA_DV), o_norm).astype(h.dtype)
    mixed = jnp.concatenate([o_a.reshape(B, S, MOBA_W), o_b.reshape(B, S, GLA_V_W)], axis=-1)
    return mixed @ w_out


def gated_deltanet_mixer(h, w_in, conv_w, a_log, dt_bias, o_norm, w_out):
    B, S, _ = h.shape
    proj = h @ w_in
    qkv = jax.nn.silu(causal_dwconv(proj[..., :GDN_QKV_W], conv_w))
    q, k, v = _split(qkv, (GDN_QK_W, GDN_QK_W, GDN_V_W))
    gate, b_raw, a_raw = _split(proj[..., GDN_QKV_W:], (GDN_V_W, GDN_HEADS, GDN_HEADS))
    q = l2norm(q.reshape(B, S, GDN_HEADS, GDN_DK))
    k = l2norm(k.reshape(B, S, GDN_HEADS, GDN_DK))
    beta = jax.nn.sigmoid(b_raw.astype(F32))
    g = -jnp.exp(a_log.astype(F32)) * jax.nn.softplus(a_raw.astype(F32) + dt_bias.astype(F32))
    o = gated_delta_chunked(q, k, v.reshape(B, S, GDN_HEADS, GDN_DV), g, beta)
    o = gated_rmsnorm(o, gate.reshape(B, S, GDN_HEADS, GDN_DV), o_norm).astype(h.dtype)
    return o.reshape(B, S, GDN_V_W) @ w_out


def routed_experts(xf, topi, topw, w_gate, w_up, w_down):
    T, D = xf.shape
    M = T * TOP_K
    eid = topi.reshape(M)
    tok = jnp.arange(M, dtype=jnp.int32) // TOP_K
    wts = topw.reshape(M)
    order = jnp.argsort(eid)
    eid_s, tok_s, w_s = eid[order], tok[order], wts[order]
    counts = jnp.bincount(eid, length=N_EXPERTS)
    padded = (counts + MOE_BLOCK - 1) // MOE_BLOCK * MOE_BLOCK
    pad_end = jnp.cumsum(padded)
    pad_start = pad_end - padded
    start = jnp.cumsum(counts) - counts
    dest = pad_start[eid_s] + jnp.arange(M) - start[eid_s]
    nblk = (M + N_EXPERTS * (MOE_BLOCK - 1) + MOE_BLOCK - 1) // MOE_BLOCK
    P = nblk * MOE_BLOCK
    row_tok = jnp.zeros((P,), jnp.int32).at[dest].set(tok_s)
    row_w = jnp.zeros((P,), F32).at[dest].set(w_s)
    blk_exp = jnp.minimum(jnp.searchsorted(pad_end, jnp.arange(nblk) * MOE_BLOCK, side='right'),
                          N_EXPERTS - 1)

    def step(acc, xs):
        e, rt, rw = xs
        xb = xf[rt]
        hb = jax.nn.silu(xb @ w_gate[e]) * (xb @ w_up[e])
        yb = (hb @ w_down[e]).astype(F32) * rw[:, None]
        return acc.at[rt].add(yb), None

    out, _ = lax.scan(step, jnp.zeros((T, D), F32),
                      (blk_exp, row_tok.reshape(nblk, MOE_BLOCK), row_w.reshape(nblk, MOE_BLOCK)))
    return out.astype(xf.dtype)


def moe_ffn(h, router_w, router_b, w_gate, w_up, w_down, sh_gate, sh_up, sh_down):
    B, S, D = h.shape
    xf = h.reshape(B * S, D)
    scores = jax.nn.sigmoid((xf @ router_w).astype(F32))
    sel_scores = scores + router_b.astype(F32)
    grp = sel_scores.reshape(-1, N_GROUPS, N_EXPERTS // N_GROUPS)
    grp_score = jnp.sum(lax.top_k(grp, 2)[0], axis=-1)
    _, top_g = lax.top_k(grp_score, TOPK_GROUPS)
    gmask = jnp.any(top_g[..., None] == jnp.arange(N_GROUPS), axis=1)
    emask = jnp.repeat(gmask, N_EXPERTS // N_GROUPS, axis=1)
    _, topi = lax.top_k(jnp.where(emask, sel_scores, -jnp.inf), TOP_K)
    topw = jnp.take_along_axis(scores, topi, axis=1)
    topw = topw / (jnp.sum(topw, -1, keepdims=True) + 1e-20) * ROUTED_SCALE
    routed = routed_experts(xf, topi, topw, w_gate, w_up, w_down)
    shared = (jax.nn.silu(xf @ sh_gate) * (xf @ sh_up)) @ sh_down
    return (routed + shared).reshape(B, S, D)


def setup_inputs(seed: int = 0) -> dict:
    key = jax.random.key(seed)
    ks = iter(jax.random.split(key, 32))

    def nrm(shape, s):
        return jax.random.normal(next(ks), shape, F32) * s

    D = D_MODEL
    u_a = jax.random.uniform(next(ks), (N_ODD, GDN_HEADS), F32, 1.0, 16.0)
    u_dt = jax.random.uniform(next(ks), (N_ODD, GDN_HEADS), F32)
    dt = jnp.exp(u_dt * (math.log(0.1) - math.log(0.001)) + math.log(0.001))
    return {
        "x": nrm((BATCH, SEQ, D), 1.0),
        "c": nrm((BATCH, D), 1.0),
        "rpe_bias": nrm((RPE_BUCKETS, MOBA_HEADS), 0.5),
        "ada_w": nrm((DEPTH, D, 6 * D), 0.5 * D ** -0.5),
        "ada_b": nrm((DEPTH, 6 * D), 0.02),
        "ln_mix_g": 1.0 + nrm((DEPTH, D), 0.02),
        "ln_mix_b": nrm((DEPTH, D), 0.02),
        "ln_ffn_g": 1.0 + nrm((DEPTH, D), 0.02),
        "ln_ffn_b": nrm((DEPTH, D), 0.02),
        "ev_w_in": nrm((N_EVEN, D, EVEN_IN), D ** -0.5),
        "ev_gk_w2": nrm((N_EVEN, GLA_GATE_RANK, GLA_QK_W), GLA_GATE_RANK ** -0.5),
        "ev_gk_b": nrm((N_EVEN, GLA_QK_W), 0.1),
        "ev_norm": 1.0 + nrm((N_EVEN, GLA_DV), 0.02),
        "ev_w_out": nrm((N_EVEN, EVEN_MIX, D), EVEN_MIX ** -0.5 * DEEPNORM_BETA),
        "od_w_in": nrm((N_ODD, D, ODD_IN), D ** -0.5),
        "od_conv_w": nrm((N_ODD, GDN_CONV, GDN_QKV_W), GDN_CONV ** -0.5),
        "od_a_log": jnp.log(u_a),
        "od_dt_bias": dt + jnp.log(-jnp.expm1(-dt)),
        "od_norm": 1.0 + nrm((N_ODD, GDN_DV), 0.02),
        "od_w_out": nrm((N_ODD, ODD_MIX, D), ODD_MIX ** -0.5 * DEEPNORM_BETA),
        "moe_router_w": nrm((DEPTH, D, N_EXPERTS), D ** -0.5),
        "moe_router_b": nrm((DEPTH, N_EXPERTS), 0.01),
        "moe_w_gate": nrm((DEPTH, N_EXPERTS, D, D_EXPERT), D ** -0.5),
        "moe_w_up": nrm((DEPTH, N_EXPERTS, D, D_EXPERT), D ** -0.5),
        "moe_w_down": nrm((DEPTH, N_EXPERTS, D_EXPERT, D), D_EXPERT ** -0.5 * DEEPNORM_BETA),
        "sh_w_gate": nrm((DEPTH, D, D_SHARED), D ** -0.5),
        "sh_w_up": nrm((DEPTH, D, D_SHARED), D ** -0.5),
        "sh_w_down": nrm((DEPTH, D_SHARED, D), D_SHARED ** -0.5 * DEEPNORM_BETA),
    }


def reference(x, c, rpe_bias, ada_w, ada_b, ln_mix_g, ln_mix_b, ln_ffn_g, ln_ffn_b,
              ev_w_in, ev_gk_w2, ev_gk_b, ev_norm, ev_w_out,
              od_w_in, od_conv_w, od_a_log, od_dt_bias, od_norm, od_w_out,
              moe_router_w, moe_router_b, moe_w_gate, moe_w_up, moe_w_down,
              sh_w_gate, sh_w_up, sh_w_down):
    c_act = jax.nn.silu(c)
    for layer in range(DEPTH):
        mod = c_act @ ada_w[layer] + ada_b[layer]
        sh1, sc1, g1, sh2, sc2, g2 = jnp.split(mod[:, None, :], 6, axis=-1)
        h = x * (1.0 + sc1) + sh1
        i = layer // 2
        if layer % 2 == 0:
            y = moba_gla_mixer(h, rpe_bias, ev_w_in[i], ev_gk_w2[i], ev_gk_b[i], ev_norm[i], ev_w_out[i])
        else:
            y = gated_deltanet_mixer(h, od_w_in[i], od_conv_w[i], od_a_log[i], od_dt_bias[i],
                                     od_norm[i], od_w_out[i])
        x = layer_norm(DEEPNORM_ALPHA * x + g1 * y, ln_mix_g[layer], ln_mix_b[layer])
        h = x * (1.0 + sc2) + sh2
        y = moe_ffn(h, moe_router_w[layer], moe_router_b[layer], moe_w_gate[layer], moe_w_up[layer],
                    moe_w_down[layer], sh_w_gate[layer], sh_w_up[layer], sh_w_down[layer])
        x = layer_norm(DEEPNORM_ALPHA * x + g2 * y, ln_ffn_g[layer], ln_ffn_b[layer])
    return x
```

```python
import functools
import math

import numpy as np
import jax
import jax.numpy as jnp
from jax import lax
from jax.experimental import pallas as pl
from jax.experimental.pallas import tpu as pltpu

F32 = jnp.float32
BF16 = jnp.bfloat16
HI = lax.Precision.HIGHEST
NT = (((1,), (1,)), ((), ()))
TN = (((0,), (0,)), ((), ()))
NEG = -1e30

LANES = 128
VMEM_LIMIT = 56 * 1024 * 1024

DEPTH = 2
MOBA_HEAD_DIM = 128
MOBA_HEADS = 4
MOBA_BLOCK = 256
MOBA_TOPK = 3
GLA_DV = 128
GLA_HEADS = 4
GLA_DK = 64
GLA_GATE_RANK = 16
GLA_GATE_NORM = 16.0
GLA_CHUNK = 64
GDN_DK = 128
GDN_DV = 128
GDN_HEADS = 8
GDN_CONV = 4
GDN_CHUNK = 64
GDN_GROUP = 4
RPE_BUCKETS = 32
RPE_MAX_DIST = 2048
RPE_TILES = 8
N_EXPERTS = 64
TOP_K = 6
N_GROUPS = 8
TOPK_GROUPS = 4
D_EXPERT = 256
ROUTED_SCALE = 2.5
DEEPNORM_ALPHA = float((2 * DEPTH) ** 0.25)
LN_EPS = 1e-5
NORM_EPS = 1e-6

MOBA_W = MOBA_HEADS * MOBA_HEAD_DIM
GLA_QK_W = GLA_HEADS * GLA_DK
GLA_V_W = GLA_HEADS * GLA_DV
GDN_W = GDN_HEADS * GDN_DK


def _cparams(sem):
    return pltpu.CompilerParams(dimension_semantics=sem, vmem_limit_bytes=VMEM_LIMIT)


def _sigmoid(x):
    return 1.0 / (1.0 + jnp.exp(-x))


def _silu(x):
    return x * _sigmoid(x)


def _softplus(x):
    return jnp.maximum(x, 0.0) + jnp.log(1.0 + jnp.exp(-jnp.abs(x)))


def _layer_norm(z, g, b):
    mu = jnp.mean(z, axis=-1, keepdims=True)
    zc = z - mu
    var = jnp.mean(zc * zc, axis=-1, keepdims=True)
    return zc * lax.rsqrt(var + LN_EPS) * g + b


def _ada_kernel(c_ref, w_ref, b_ref, o_ref):
    ca = _silu(c_ref[...])
    o_ref[0] = jnp.dot(ca, w_ref[0], precision=HI, preferred_element_type=F32) + b_ref[0]


def _ada(c, ada_w, ada_b):
    depth, d, n = ada_w.shape
    bsz = c.shape[0]
    tn = 6 * LANES
    return pl.pallas_call(
        _ada_kernel,
        grid=(depth, n // tn),
        in_specs=[pl.BlockSpec((bsz, d), lambda l, j: (0, 0)),
                  pl.BlockSpec((1, d, tn), lambda l, j: (l, 0, j)),
                  pl.BlockSpec((1, 1, tn), lambda l, j: (l, 0, j))],
        out_specs=pl.BlockSpec((1, bsz, tn), lambda l, j: (l, 0, j)),
        out_shape=jax.ShapeDtypeStruct((depth, bsz, n), F32),
        compiler_params=_cparams(("arbitrary", "arbitrary")),
        name="ada_mod",
    )(c, ada_w, ada_b.reshape(depth, 1, n))


def _modmm_kernel(x_ref, sc_ref, sh_ref, w_ref, o_ref, h_scr):
    @pl.when(pl.program_id(2) == 0)
    def _():
        h_scr[...] = (x_ref[0] * (1.0 + sc_ref[0]) + sh_ref[0]).astype(BF16)

    o_ref[0] = jnp.dot(h_scr[...], w_ref[...], preferred_element_type=F32).astype(o_ref.dtype)


def _col_tile(n, cap):
    best = LANES
    for t in range(LANES, cap + 1, LANES):
        if n % t == 0:
            best = t
    return best


def _mod_matmul(x, sc, sh, w, tm=1024, tn_cap=768):
    bsz, s, d = x.shape
    n = w.shape[1]
    tm = min(tm, s)
    tn = _col_tile(n, tn_cap)
    return pl.pallas_call(
        _modmm_kernel,
        grid=(bsz, s // tm, n // tn),
        in_specs=[pl.BlockSpec((1, tm, d), lambda b, i, j: (b, i, 0)),
                  pl.BlockSpec((1, 1, d), lambda b, i, j: (b, 0, 0)),
                  pl.BlockSpec((1, 1, d), lambda b, i, j: (b, 0, 0)),
                  pl.BlockSpec((d, tn), lambda b, i, j: (0, j))],
        out_specs=pl.BlockSpec((1, tm, tn), lambda b, i, j: (b, i, j)),
        out_shape=jax.ShapeDtypeStruct((bsz, s, n), F32),
        scratch_shapes=[pltpu.VMEM((tm, d), BF16)],
        compiler_params=_cparams(("arbitrary", "arbitrary", "arbitrary")),
        name="mod_matmul",
    )(x, sc, sh, w)


def _rpe_lower_bounds():
    exact = RPE_BUCKETS // 2
    d = np.arange(0, 2 * RPE_MAX_DIST, dtype=np.int64)
    logd = np.log(np.maximum(d, 1).astype(np.float64) / exact)
    large = exact + (logd / math.log(RPE_MAX_DIST / exact) * (RPE_BUCKETS - exact)).astype(np.int64)
    large = np.minimum(large, RPE_BUCKETS - 1)
    bucket = np.where(d < exact, d, large)
    return [int(np.argmax(bucket >= k)) for k in range(RPE_BUCKETS)]


def _rpe_tiles_kernel(lo, rpe_ref, o_ref):
    h = pl.program_id(0)
    j = pl.program_id(1)
    blk = o_ref.shape[-1]
    key = lax.broadcasted_iota(jnp.int32, (blk, blk), 0)
    qry = lax.broadcasted_iota(jnp.int32, (blk, blk), 1)
    dist = j * blk + qry - key
    val = jnp.full((blk, blk), rpe_ref[0, h], F32)
    for k in range(1, RPE_BUCKETS):
        val = jnp.where(dist >= lo[k], rpe_ref[k, h], val)
    o_ref[0, 0] = jnp.where(dist >= 0, val, NEG)


def _rpe_tiles(rpe_bias):
    heads = rpe_bias.shape[1]
    return pl.pallas_call(
        functools.partial(_rpe_tiles_kernel, _rpe_lower_bounds()),
        grid=(heads, RPE_TILES),
        in_specs=[pl.BlockSpec(memory_space=pltpu.SMEM)],
        out_specs=pl.BlockSpec((1, 1, MOBA_BLOCK, MOBA_BLOCK), lambda h, j: (h, j, 0, 0)),
        out_shape=jax.ShapeDtypeStruct((heads, RPE_TILES, MOBA_BLOCK, MOBA_BLOCK), F32),
        compiler_params=_cparams(("arbitrary", "arbitrary")),
        name="rpe_tiles",
    )(rpe_bias)


def _moba_kernel(q_ref, k_ref, v_ref, t_ref, o_ref, kb_scr, vt_scr, km_scr, sel_scr):
    i = pl.program_id(2)
    nkb, blk, dh = kb_scr.shape

    @pl.when(i == 0)
    def _():
        for n in range(nkb):
            kn = k_ref[0, n * blk:(n + 1) * blk, :]
            kb_scr[n] = kn.astype(BF16)
            km_scr[n:n + 1, :] = jnp.mean(kn, axis=0, keepdims=True)
            vt_scr[n] = v_ref[0, n * blk:(n + 1) * blk, :].T.astype(BF16)

    q = q_ref[0]
    gate = lax.dot_general(km_scr[...], q, NT, precision=HI, preferred_element_type=F32)
    bidx = lax.broadcasted_iota(jnp.int32, gate.shape, 0)
    past = bidx < i
    g = jnp.where(past, gate, -jnp.inf)
    sel = None
    for _ in range(MOBA_TOPK):
        m = jnp.max(g, axis=0, keepdims=True)
        first = jnp.min(jnp.where(g == m, bidx, nkb), axis=0, keepdims=True)
        hit = bidx == first
        sel = hit if sel is None else jnp.logical_or(sel, hit)
        g = jnp.where(hit, -jnp.inf, g)
    sel_scr[...] = jnp.where(jnp.logical_and(sel, past), 0.0, NEG)

    qs = (q * dh ** -0.5).astype(BF16)

    s = lax.dot_general(kb_scr[i], qs, NT, preferred_element_type=F32) + t_ref[0, 0]
    m0 = jnp.max(s, axis=0, keepdims=True)
    p = jnp.exp(s - m0)
    l0 = jnp.sum(p, axis=0, keepdims=True)
    acc0 = jnp.dot(vt_scr[i], p.astype(BF16), preferred_element_type=F32)

    def body(n, carry):
        m, l, acc = carry
        j = jnp.minimum(i - n, RPE_TILES - 1)
        s = (lax.dot_general(kb_scr[n], qs, NT, preferred_element_type=F32)
             + t_ref[0, j] + sel_scr[pl.ds(n, 1), :])
        m_new = jnp.maximum(m, jnp.max(s, axis=0, keepdims=True))
        alpha = jnp.exp(m - m_new)
        p = jnp.exp(s - m_new)
        l = alpha * l + jnp.sum(p, axis=0, keepdims=True)
        acc = alpha * acc + jnp.dot(vt_scr[n], p.astype(BF16), preferred_element_type=F32)
        return m_new, l, acc

    _, l, acc = lax.fori_loop(0, i, body, (m0, l0, acc0))
    o_ref[0] = (acc / l).T


def _moba(proj, tiles, q_col, k_col, v_col):
    bsz, s, _ = proj.shape
    dh, blk, heads = MOBA_HEAD_DIM, MOBA_BLOCK, MOBA_HEADS
    nkb = s // blk
    return pl.pallas_call(
        _moba_kernel,
        grid=(bsz, heads, nkb),
        in_specs=[pl.BlockSpec((1, blk, dh), lambda b, h, i: (b, i, q_col + h)),
                  pl.BlockSpec((1, s, dh), lambda b, h, i: (b, 0, k_col + h)),
                  pl.BlockSpec((1, s, dh), lambda b, h, i: (b, 0, v_col + h)),
                  pl.BlockSpec((1, RPE_TILES, blk, blk), lambda b, h, i: (h, 0, 0, 0))],
        out_specs=pl.BlockSpec((1, blk, dh), lambda b, h, i: (b, i, h)),
        out_shape=jax.ShapeDtypeStruct((bsz, s, heads * dh), F32),
        scratch_shapes=[pltpu.VMEM((nkb, blk, dh), BF16),
                        pltpu.VMEM((nkb, dh, blk), BF16),
                        pltpu.VMEM((nkb, dh), F32),
                        pltpu.VMEM((nkb, blk), F32)],
        compiler_params=_cparams(("arbitrary", "arbitrary", "arbitrary")),
        name="moba_attention",
    )(proj, proj, proj, tiles)


def _gla_kernel(q_ref, k_ref, v_ref, gg_ref, glr_ref, w2_ref, gb_ref, nw_ref, o_ref, st_scr):
    @pl.when(pl.program_id(1) == 0)
    def _():
        st_scr[...] = jnp.zeros(st_scr.shape, F32)

    tg = q_ref.shape[1]
    c = GLA_CHUNK
    x = jnp.dot(glr_ref[0], w2_ref[...], precision=HI, preferred_element_type=F32) + gb_ref[...]
    lg = -_softplus(-x) * (1.0 / GLA_GATE_NORM)
    row = lax.broadcasted_iota(jnp.int32, (c, c), 0)
    col = lax.broadcasted_iota(jnp.int32, (c, c), 1)
    incl = row >= col
    tri = incl.astype(F32)
    nw = nw_ref[...]
    for ci in range(tg // c):
        rows = slice(ci * c, (ci + 1) * c)
        b = jnp.dot(tri, lg[rows], precision=HI, preferred_element_type=F32)
        bl = b[c - 1:c, :]
        q = q_ref[0, rows, :] * GLA_DK ** -0.5
        k = k_ref[0, rows, :]
        q_e = (q * jnp.exp(b)).astype(BF16)
        k_e = (k * jnp.exp(-b)).astype(BF16)
        k_end = (k * jnp.exp(bl - b)).astype(BF16)
        d = jnp.exp(bl)
        for h in range(GLA_HEADS):
            ks = slice(h * GLA_DK, (h + 1) * GLA_DK)
            vs = slice(h * GLA_DV, (h + 1) * GLA_DV)
            vh = v_ref[0, rows, vs].astype(BF16)
            a = lax.dot_general(q_e[:, ks], k_e[:, ks], NT, preferred_element_type=F32)
            a = jnp.where(incl, a, 0.0).astype(BF16)
            st = st_scr[h]
            o = (jnp.dot(a, vh, preferred_element_type=F32)
                 + lax.dot_general(q_e[:, ks], st.astype(BF16), NT, preferred_element_type=F32))
            st_scr[h] = st * d[:, ks] + lax.dot_general(vh, k_end[:, ks], TN, preferred_element_type=F32)
            o = o * lax.rsqrt(jnp.mean(o * o, axis=-1, keepdims=True) + NORM_EPS)
            o_ref[0, rows, vs] = o * nw * _silu(gg_ref[0, rows, vs])


def _gla(proj, gk_w2p, gk_b, o_norm, q_col, k_col, v_col, g_col, r_col, tg=256):
    bsz, s, _ = proj.shape
    tg = min(tg, s)
    qk, vw = GLA_QK_W, GLA_V_W
    return pl.pallas_call(
        _gla_kernel,
        grid=(bsz, s // tg),
        in_specs=[pl.BlockSpec((1, tg, qk), lambda b, t: (b, t, q_col)),
                  pl.BlockSpec((1, tg, qk), lambda b, t: (b, t, k_col)),
                  pl.BlockSpec((1, tg, vw), lambda b, t: (b, t, v_col)),
                  pl.BlockSpec((1, tg, vw), lambda b, t: (b, t, g_col)),
                  pl.BlockSpec((1, tg, LANES), lambda b, t: (b, t, r_col)),
                  pl.BlockSpec((LANES, qk), lambda b, t: (0, 0)),
                  pl.BlockSpec((1, qk), lambda b, t: (0, 0)),
                  pl.BlockSpec((1, GLA_DV), lambda b, t: (0, 0))],
        out_specs=pl.BlockSpec((1, tg, vw), lambda b, t: (b, t, 0)),
        out_shape=jax.ShapeDtypeStruct((bsz, s, vw), F32),
        scratch_shapes=[pltpu.VMEM((GLA_HEADS, GLA_DV, GLA_DK), F32)],
        compiler_params=_cparams(("arbitrary", "arbitrary")),
        name="gla_mixer",
    )(proj, proj, proj, proj, proj, gk_w2p, gk_b, o_norm)


def _gdn_kernel(qkv_ref, gate_ref, ba_ref, cw_ref, pv_ref, nw_ref, o_ref, tail_scr, s_scr):
    @pl.when(pl.program_id(1) == 0)
    def _():
        tail_scr[...] = jnp.zeros(tail_scr.shape, F32)
        s_scr[...] = jnp.zeros(s_scr.shape, F32)

    tg = qkv_ref.shape[1]
    c, dk, grp = GDN_CHUNK, GDN_DK, GDN_GROUP
    gr = grp * c
    w = GDN_W

    x = qkv_ref[0]
    tail = tail_scr[...]
    tail_scr[...] = x[tg - 8:, :]
    r8 = lax.broadcasted_iota(jnp.int32, (8, 1), 0)
    y = x * cw_ref[GDN_CONV - 1:GDN_CONV, :]
    for sft in range(1, GDN_CONV):
        xs = pltpu.roll(x, sft, axis=0)
        head = jnp.where(r8 < sft, pltpu.roll(tail, sft, axis=0), xs[:8, :])
        xs = jnp.concatenate([head, xs[8:, :]], axis=0)
        y = y + xs * cw_ref[GDN_CONV - 1 - sft:GDN_CONV - sft, :]
    y = _silu(y)

    ba = ba_ref[0]
    beta_t = _sigmoid(ba)
    g_t = -jnp.exp(pv_ref[0:1, :]) * _softplus(ba + pv_ref[1:2, :])

    row = lax.broadcasted_iota(jnp.int32, (c, c), 0)
    col = lax.broadcasted_iota(jnp.int32, (c, c), 1)
    tri = (row >= col).astype(F32)
    rr = lax.broadcasted_iota(jnp.int32, (gr, gr), 0)
    cc = lax.broadcasted_iota(jnp.int32, (gr, gr), 1)
    same = (rr // c) == (cc // c)
    incl = jnp.logical_and(same, rr >= cc)
    strict = jnp.logical_and(same, rr > cc)
    eye = (rr == cc).astype(F32)
    nw = nw_ref[...]

    for ci in range(tg // c):
        rows = slice(ci * c, (ci + 1) * c)
        gcum = jnp.dot(tri, g_t[rows], precision=HI, preferred_element_type=F32)
        gcum_t = gcum.T
        for gi in range(GDN_HEADS // grp):
            hs = [gi * grp + u for u in range(grp)]

            def stack(a, off):
                return jnp.concatenate([a[rows, off + h * dk: off + (h + 1) * dk] for h in hs], axis=0)

            q = stack(y, 0)
            k = stack(y, w)
            v = stack(y, 2 * w)
            q = q * lax.rsqrt(jnp.sum(q * q, axis=-1, keepdims=True) + NORM_EPS) * dk ** -0.5
            k = k * lax.rsqrt(jnp.sum(k * k, axis=-1, keepdims=True) + NORM_EPS)
            beta = jnp.concatenate([beta_t[rows, h:h + 1] for h in hs], axis=0)
            gc = jnp.concatenate([gcum[:, GDN_HEADS + h:GDN_HEADS + h + 1] for h in hs], axis=0)
            gc_row = jnp.concatenate([gcum_t[GDN_HEADS + h:GDN_HEADS + h + 1, :] for h in hs], axis=1)
            gl = jnp.concatenate([jnp.broadcast_to(gcum[c - 1:c, GDN_HEADS + h:GDN_HEADS + h + 1], (c, 1))
                                  for h in hs], axis=0)

            decay = jnp.where(incl, jnp.exp(jnp.where(incl, gc - gc_row, 0.0)), 0.0)
            kb = k * beta
            vb = v * beta
            k16 = k.astype(BF16)
            a = lax.dot_general(kb.astype(BF16), k16, NT, preferred_element_type=F32)
            a = jnp.where(strict, a * decay, 0.0)
            t_mat = eye - a
            pw = a
            for _ in range(int(math.log2(c)) - 1):
                pw16 = pw.astype(BF16)
                pw = jnp.dot(pw16, pw16, preferred_element_type=F32)
                t_mat = t_mat + jnp.dot(t_mat.astype(BF16), pw.astype(BF16), preferred_element_type=F32)
            eg = jnp.exp(gc)
            rhs = jnp.concatenate([vb, kb * eg], axis=1).astype(BF16)
            wk = jnp.dot(t_mat.astype(BF16), rhs, preferred_element_type=F32)
            w_val = wk[:, :GDN_DV]
            k_cum = wk[:, GDN_DV:].astype(BF16)
            attn = lax.dot_general(q.astype(BF16), k16, NT, preferred_element_type=F32) * decay
            q_g = (q * eg).astype(BF16)
            k_end = (k * jnp.exp(gl - gc)).astype(BF16)

            v_new = []
            for u, h in enumerate(hs):
                hr = slice(u * c, (u + 1) * c)
                st16 = s_scr[h].astype(BF16)
                v_new.append(w_val[hr] - jnp.dot(k_cum[hr], st16, preferred_element_type=F32))
            v_new = jnp.concatenate(v_new, axis=0)
            v16 = v_new.astype(BF16)
            o_intra = jnp.dot(attn.astype(BF16), v16, preferred_element_type=F32)
            for u, h in enumerate(hs):
                hr = slice(u * c, (u + 1) * c)
                st = s_scr[h]
                o = o_intra[hr] + jnp.dot(q_g[hr], st.astype(BF16), preferred_element_type=F32)
                d_last = jnp.exp(gcum[c - 1:c, GDN_HEADS + h:GDN_HEADS + h + 1])
                s_scr[h] = st * d_last + lax.dot_general(k_end[hr], v16[hr], TN, preferred_element_type=F32)
                o = o * lax.rsqrt(jnp.mean(o * o, axis=-1, keepdims=True) + NORM_EPS)
                cs = slice(h * GDN_DV, (h + 1) * GDN_DV)
                o_ref[0, rows, cs] = o * nw * _silu(gate_ref[0, rows, cs])


def _gdn(proj, conv_w, pvec, o_norm, tg=256):
    bsz, s, _ = proj.shape
    tg = min(tg, s)
    w = GDN_W
    return pl.pallas_call(
        _gdn_kernel,
        grid=(bsz, s // tg),
        in_specs=[pl.BlockSpec((1, tg, 3 * w), lambda b, t: (b, t, 0)),
                  pl.BlockSpec((1, tg, w), lambda b, t: (b, t, 3)),
                  pl.BlockSpec((1, tg, LANES), lambda b, t: (b, t, 4 * w // LANES)),
                  pl.BlockSpec((GDN_CONV, 3 * w), lambda b, t: (0, 0)),
                  pl.BlockSpec((2, LANES), lambda b, t: (0, 0)),
                  pl.BlockSpec((1, GDN_DV), lambda b, t: (0, 0))],
        out_specs=pl.BlockSpec((1, tg, w), lambda b, t: (b, t, 0)),
        out_shape=jax.ShapeDtypeStruct((bsz, s, w), F32),
        scratch_shapes=[pltpu.VMEM((8, 3 * w), F32),
                        pltpu.VMEM((GDN_HEADS, GDN_DK, GDN_DV), F32)],
        compiler_params=_cparams(("arbitrary", "arbitrary")),
        name="gdn_mixer",
    )(proj, proj, proj, conv_w, pvec, o_norm)


def _mixout_kernel(n_act, *refs):
    acts = refs[:n_act]
    ws = refs[n_act:2 * n_act]
    x_ref, g1_ref, lng_ref, lnb_ref, sc_ref, sh_ref, rw_ref = refs[2 * n_act:2 * n_act + 7]
    xo_ref, h_ref, rl_ref = refs[2 * n_act + 7:]
    y = None
    for a_ref, w_ref in zip(acts, ws):
        t = jnp.dot(a_ref[0].astype(BF16), w_ref[...], preferred_element_type=F32)
        y = t if y is None else y + t
    xn = _layer_norm(DEEPNORM_ALPHA * x_ref[0] + g1_ref[0] * y, lng_ref[...], lnb_ref[...])
    xo_ref[0] = xn
    h = xn * (1.0 + sc_ref[0]) + sh_ref[0]
    h_ref[0] = h.astype(BF16)
    rl_ref[0] = lax.dot_general(rw_ref[...], h, NT, precision=HI, preferred_element_type=F32)


def _mixout(acts, ws, x, g1, ln_g, ln_b, sc2, sh2, router_wt, tm=512):
    bsz, s, d = x.shape
    tm = min(tm, s)
    n_act = len(acts)
    ne = router_wt.shape[0]
    vec = pl.BlockSpec((1, 1, d), lambda b, i: (b, 0, 0))
    par = pl.BlockSpec((1, d), lambda b, i: (0, 0))
    in_specs = ([pl.BlockSpec((1, tm, a.shape[-1]), lambda b, i: (b, i, 0)) for a in acts]
                + [pl.BlockSpec(w.shape, lambda b, i: (0, 0)) for w in ws]
                + [pl.BlockSpec((1, tm, d), lambda b, i: (b, i, 0)), vec, par, par, vec, vec,
                   pl.BlockSpec((ne, d), lambda b, i: (0, 0))])
    return pl.pallas_call(
        functools.partial(_mixout_kernel, n_act),
        grid=(bsz, s // tm),
        in_specs=in_specs,
        out_specs=[pl.BlockSpec((1, tm, d), lambda b, i: (b, i, 0)),
                   pl.BlockSpec((1, tm, d), lambda b, i: (b, i, 0)),
                   pl.BlockSpec((1, ne, tm), lambda b, i: (b, 0, i))],
        out_shape=[jax.ShapeDtypeStruct((bsz, s, d), F32),
                   jax.ShapeDtypeStruct((bsz, s, d), BF16),
                   jax.ShapeDtypeStruct((bsz, ne, s), F32)],
        compiler_params=_cparams(("arbitrary", "arbitrary")),
        name="mix_out",
    )(*acts, *ws, x, g1, ln_g, ln_b, sc2, sh2, router_wt)


def _first_max(vals, idx, axis, sentinel):
    m = jnp.max(vals, axis=axis, keepdims=True)
    first = jnp.min(jnp.where(vals == m, idx, sentinel), axis=axis, keepdims=True)
    return m, idx == first


def _router_kernel(rl_ref, rb_ref, g_ref):
    ne, tn = rl_ref.shape[1], rl_ref.shape[2]
    gsz = ne // N_GROUPS
    scores = _sigmoid(rl_ref[0])
    sel = scores + rb_ref[...]
    ridx = lax.broadcasted_iota(jnp.int32, (gsz, tn), 0)
    gidx = lax.broadcasted_iota(jnp.int32, (N_GROUPS, tn), 0)
    gs = jnp.zeros((N_GROUPS, tn), F32)
    for g in range(N_GROUPS):
        sg = sel[g * gsz:(g + 1) * gsz, :]
        m1, hit = _first_max(sg, ridx, 0, gsz)
        m2 = jnp.max(jnp.where(hit, -jnp.inf, sg), axis=0, keepdims=True)
        gs = jnp.where(gidx == g, m1 + m2, gs)
    gsel = None
    for _ in range(TOPK_GROUPS):
        _, hit = _first_max(gs, gidx, 0, N_GROUPS)
        gsel = hit if gsel is None else jnp.logical_or(gsel, hit)
        gs = jnp.where(hit, -jnp.inf, gs)
    gself = gsel.astype(F32)
    emask = jnp.concatenate([jnp.broadcast_to(gself[g:g + 1, :], (gsz, tn)) for g in range(N_GROUPS)], axis=0)
    cand = jnp.where(emask > 0.5, sel, -jnp.inf)
    eidx = lax.broadcasted_iota(jnp.int32, cand.shape, 0)
    chosen = None
    for _ in range(TOP_K):
        _, hit = _first_max(cand, eidx, 0, ne)
        chosen = hit if chosen is None else jnp.logical_or(chosen, hit)
        cand = jnp.where(hit, -jnp.inf, cand)
    wsel = jnp.where(chosen, scores, 0.0)
    tot = jnp.sum(wsel, axis=0, keepdims=True)
    gates = wsel / (tot + 1e-20) * ROUTED_SCALE
    g_ref[0] = gates.T


def _router(rl, router_b, tn=1024):
    bsz, ne, s = rl.shape
    tn = min(tn, s)
    return pl.pallas_call(
        _router_kernel,
        grid=(bsz, s // tn),
        in_specs=[pl.BlockSpec((1, ne, tn), lambda b, i: (b, 0, i)),
                  pl.BlockSpec((ne, 1), lambda b, i: (0, 0))],
        out_specs=pl.BlockSpec((1, tn, ne), lambda b, i: (b, i, 0)),
        out_shape=jax.ShapeDtypeStruct((bsz, s, ne), F32),
        compiler_params=_cparams(("arbitrary", "arbitrary")),
        name="moe_router",
    )(rl, router_b.reshape(ne, 1))


def _moe_kernel(h_ref, g_ref, wg_ref, wu_ref, wd_ref, sg_ref, su_ref, sd_ref,
                x_ref, g2_ref, lng_ref, lnb_ref, o_ref, acc_scr):
    e = pl.program_id(1)
    ne = pl.num_programs(1)
    h = h_ref[...]

    def ffn(wg, wu, wd, gate):
        a = jnp.dot(h, wg, preferred_element_type=F32)
        u = jnp.dot(h, wu, preferred_element_type=F32)
        act = _silu(a) * u
        if gate is not None:
            act = act * gate
        return jnp.dot(act.astype(BF16), wd, preferred_element_type=F32)

    @pl.when(e == 0)
    def _():
        acc_scr[...] = ffn(sg_ref[...], su_ref[...], sd_ref[...], None)

    onehot = (lax.broadcasted_iota(jnp.int32, (g_ref.shape[1], LANES), 0) == e).astype(F32)
    gcol = jnp.dot(g_ref[...], onehot, precision=HI, preferred_element_type=F32)
    gate = jnp.concatenate([gcol] * (D_EXPERT // LANES), axis=1)
    acc_scr[...] += ffn(wg_ref[0], wu_ref[0], wd_ref[0], gate)

    @pl.when(e == ne - 1)
    def _():
        z = DEEPNORM_ALPHA * x_ref[...] + g2_ref[0] * acc_scr[...]
        o_ref[...] = _layer_norm(z, lng_ref[...], lnb_ref[...])


def _moe(h2, gates, wg, wu, wd, sg, su, sd, x, g2, ln_g, ln_b, tm=1024):
    bsz, s, d = x.shape
    tm = min(tm, s)
    t = bsz * s
    ne, _, de = wg.shape
    per_b = s // tm
    tok = pl.BlockSpec((tm, d), lambda i, e: (i, 0))
    par = pl.BlockSpec((1, d), lambda i, e: (0, 0))
    out = pl.pallas_call(
        _moe_kernel,
        grid=(t // tm, ne),
        in_specs=[tok,
                  pl.BlockSpec((tm, ne), lambda i, e: (i, 0)),
                  pl.BlockSpec((1, d, de), lambda i, e: (e, 0, 0)),
                  pl.BlockSpec((1, d, de), lambda i, e: (e, 0, 0)),
                  pl.BlockSpec((1, de, d), lambda i, e: (e, 0, 0)),
                  pl.BlockSpec(sg.shape, lambda i, e: (0, 0)),
                  pl.BlockSpec(su.shape, lambda i, e: (0, 0)),
                  pl.BlockSpec(sd.shape, lambda i, e: (0, 0)),
                  tok,
                  pl.BlockSpec((1, 1, d), lambda i, e: (i // per_b, 0, 0)),
                  par, par],
        out_specs=tok,
        out_shape=jax.ShapeDtypeStruct((t, d), F32),
        scratch_shapes=[pltpu.VMEM((tm, d), F32)],
        compiler_params=_cparams(("arbitrary", "arbitrary")),
        name="moe_experts",
    )(h2.reshape(t, d), gates.reshape(t, ne), wg, wu, wd, sg, su, sd, x.reshape(t, d), g2, ln_g, ln_b)
    return out.reshape(bsz, s, d)


def _pad_cols(w, n):
    return jnp.pad(w, ((0, 0), (0, n - w.shape[1])))


def kernel(x, c, rpe_bias, ada_w, ada_b, ln_mix_g, ln_mix_b, ln_ffn_g, ln_ffn_b, ev_w_in, ev_gk_w2, ev_gk_b, ev_norm, ev_w_out, od_w_in, od_conv_w, od_a_log, od_dt_bias, od_norm, od_w_out, moe_router_w, moe_router_b, moe_w_gate, moe_w_up, moe_w_down, sh_w_gate, sh_w_up, sh_w_down):
    bsz, s, d = x.shape
    mod = _ada(c, ada_w, ada_b)
    tiles = _rpe_tiles(rpe_bias)

    for layer in range(DEPTH):
        sh1, sc1, g1, sh2, sc2, g2 = [mod[layer, :, u * d:(u + 1) * d].reshape(bsz, 1, d) for u in range(6)]
        i = layer // 2
        if layer % 2 == 0:
            n_main = 3 * MOBA_W + 2 * GLA_QK_W + 2 * GLA_V_W
            w_in = jnp.concatenate([ev_w_in[i][:, :n_main], _pad_cols(ev_w_in[i][:, n_main:], LANES)], axis=1)
            proj = _mod_matmul(x, sc1, sh1, w_in.astype(BF16))
            nb = MOBA_W // LANES
            o_a = _moba(proj, tiles, 0, nb, 2 * nb)
            gk_w2p = jnp.pad(ev_gk_w2[i], ((0, LANES - GLA_GATE_RANK), (0, 0)))
            gla0 = 3 * MOBA_W
            o_b = _gla(proj, gk_w2p, ev_gk_b[i].reshape(1, -1), ev_norm[i].reshape(1, -1),
                       gla0 // GLA_QK_W, gla0 // GLA_QK_W + 1,
                       (gla0 + 2 * GLA_QK_W) // GLA_V_W, (gla0 + 2 * GLA_QK_W) // GLA_V_W + 1,
                       n_main // LANES)
            w_out = ev_w_out[i].astype(BF16)
            acts, ws = [o_a, o_b], [w_out[:MOBA_W], w_out[MOBA_W:]]
        else:
            n_main = 4 * GDN_W
            w_in = jnp.concatenate([od_w_in[i][:, :n_main], _pad_cols(od_w_in[i][:, n_main:], LANES)], axis=1)
            proj = _mod_matmul(x, sc1, sh1, w_in.astype(BF16))
            pvec = jnp.zeros((2, LANES), F32)
            pvec = pvec.at[0, GDN_HEADS:2 * GDN_HEADS].set(od_a_log[i])
            pvec = pvec.at[1, GDN_HEADS:2 * GDN_HEADS].set(od_dt_bias[i])
            o = _gdn(proj, od_conv_w[i], pvec, od_norm[i].reshape(1, -1))
            acts, ws = [o], [od_w_out[i].astype(BF16)]

        x, h2, rl = _mixout(acts, ws, x, g1, ln_mix_g[layer].reshape(1, d), ln_mix_b[layer].reshape(1, d),
                            sc2, sh2, moe_router_w[layer].T)
        gates = _router(rl, moe_router_b[layer])
        x = _moe(h2, gates, moe_w_gate[layer].astype(BF16), moe_w_up[layer].astype(BF16),
                 moe_w_down[layer].astype(BF16), sh_w_gate[layer].astype(BF16), sh_w_up[layer].astype(BF16),
                 sh_w_down[layer].astype(BF16), x, g2, ln_ffn_g[layer].reshape(1, d), ln_ffn_b[layer].reshape(1, d))
    return x
```

```python
import functools
import math

import numpy as np
import jax
import jax.numpy as jnp
from jax import lax
from jax.experimental import pallas as pl
from jax.experimental.pallas import tpu as pltpu

F32 = jnp.float32
BF16 = jnp.bfloat16
HI = lax.Precision.HIGHEST
NT = (((1,), (1,)), ((), ()))
TN = (((0,), (0,)), ((), ()))
NEG = -1e30

LANES = 128
VMEM_LIMIT = 56 * 1024 * 1024

DEPTH = 2
MOBA_HEAD_DIM = 128
MOBA_HEADS = 4
MOBA_BLOCK = 256
MOBA_TOPK = 3
GLA_DV = 128
GLA_HEADS = 4
GLA_DK = 64
GLA_GATE_RANK = 16
GLA_GATE_NORM = 16.0
GLA_CHUNK = 64
GDN_DK = 128
GDN_DV = 128
GDN_HEADS = 8
GDN_CONV = 4
GDN_CHUNK = 64
GDN_GROUP = 4
RPE_BUCKETS = 32
RPE_MAX_DIST = 2048
RPE_TILES = 8
N_EXPERTS = 64
TOP_K = 6
N_GROUPS = 8
TOPK_GROUPS = 4
D_EXPERT = 256
ROUTED_SCALE = 2.5
DEEPNORM_ALPHA = float((2 * DEPTH) ** 0.25)
LN_EPS = 1e-5
NORM_EPS = 1e-6

MOBA_W = MOBA_HEADS * MOBA_HEAD_DIM
GLA_QK_W = GLA_HEADS * GLA_DK
GLA_V_W = GLA_HEADS * GLA_DV
GDN_W = GDN_HEADS * GDN_DK


def _cparams(sem):
    return pltpu.CompilerParams(dimension_semantics=sem, vmem_limit_bytes=VMEM_LIMIT)


def _sigmoid(x):
    return 1.0 / (1.0 + jnp.exp(-x))


def _silu(x):
    return x * _sigmoid(x)


def _softplus(x):
    return jnp.maximum(x, 0.0) + jnp.log(1.0 + jnp.exp(-jnp.abs(x)))


def _layer_norm(z, g, b):
    mu = jnp.mean(z, axis=-1, keepdims=True)
    zc = z - mu
    var = jnp.mean(zc * zc, axis=-1, keepdims=True)
    return zc * lax.rsqrt(var + LN_EPS) * g + b


def _ada_kernel(c_ref, w_ref, b_ref, o_ref):
    ca = _silu(c_ref[...])
    o_ref[0] = jnp.dot(ca, w_ref[0], precision=HI, preferred_element_type=F32) + b_ref[0]


def _ada(c, ada_w, ada_b):
    depth, d, n = ada_w.shape
    bsz = c.shape[0]
    tn = 6 * LANES
    return pl.pallas_call(
        _ada_kernel,
        grid=(depth, n // tn),
        in_specs=[pl.BlockSpec((bsz, d), lambda l, j: (0, 0)),
                  pl.BlockSpec((1, d, tn), lambda l, j: (l, 0, j)),
                  pl.BlockSpec((1, 1, tn), lambda l, j: (l, 0, j))],
        out_specs=pl.BlockSpec((1, bsz, tn), lambda l, j: (l, 0, j)),
        out_shape=jax.ShapeDtypeStruct((depth, bsz, n), F32),
        compiler_params=_cparams(("arbitrary", "arbitrary")),
        name="ada_mod",
    )(c, ada_w, ada_b.reshape(depth, 1, n))


def _modmm_kernel(x_ref, sc_ref, sh_ref, w_ref, o_ref, h_scr):
    @pl.when(pl.program_id(2) == 0)
    def _():
        h_scr[...] = (x_ref[0] * (1.0 + sc_ref[0]) + sh_ref[0]).astype(BF16)

    o_ref[0] = jnp.dot(h_scr[...], w_ref[...], preferred_element_type=F32).astype(o_ref.dtype)


def _col_tile(n, cap):
    best = LANES
    for t in range(LANES, cap + 1, LANES):
        if n % t == 0:
            best = t
    return best


def _mod_matmul(x, sc, sh, w, tm=1024, tn_cap=768):
    bsz, s, d = x.shape
    n = w.shape[1]
    tm = min(tm, s)
    tn = _col_tile(n, tn_cap)
    return pl.pallas_call(
        _modmm_kernel,
        grid=(bsz, s // tm, n // tn),
        in_specs=[pl.BlockSpec((1, tm, d), lambda b, i, j: (b, i, 0)),
                  pl.BlockSpec((1, 1, d), lambda b, i, j: (b, 0, 0)),
                  pl.BlockSpec((1, 1, d), lambda b, i, j: (b, 0, 0)),
                  pl.BlockSpec((d, tn), lambda b, i, j: (0, j))],
        out_specs=pl.BlockSpec((1, tm, tn), lambda b, i, j: (b, i, j)),
        out_shape=jax.ShapeDtypeStruct((bsz, s, n), F32),
        scratch_shapes=[pltpu.VMEM((tm, d), BF16)],
        compiler_params=_cparams(("arbitrary", "arbitrary", "arbitrary")),
        name="mod_matmul",
    )(x, sc, sh, w)


def _rpe_lower_bounds():
    exact = RPE_BUCKETS // 2
    d = np.arange(0, 2 * RPE_MAX_DIST, dtype=np.int64)
    logd = np.log(np.maximum(d, 1).astype(np.float64) / exact)
    large = exact + (logd / math.log(RPE_MAX_DIST / exact) * (RPE_BUCKETS - exact)).astype(np.int64)
    large = np.minimum(large, RPE_BUCKETS - 1)
    bucket = np.where(d < exact, d, large)
    return [int(np.argmax(bucket >= k)) for k in range(RPE_BUCKETS)]


def _rpe_tiles_kernel(lo, rpe_ref, o_ref):
    h = pl.program_id(0)
    j = pl.program_id(1)
    blk = o_ref.shape[-1]
    key = lax.broadcasted_iota(jnp.int32, (blk, blk), 0)
    qry = lax.broadcasted_iota(jnp.int32, (blk, blk), 1)
    dist = j * blk + qry - key
    val = jnp.full((blk, blk), rpe_ref[0, h], F32)
    for k in range(1, RPE_BUCKETS):
        val = jnp.where(dist >= lo[k], rpe_ref[k, h], val)
    o_ref[0, 0] = jnp.where(dist >= 0, val, NEG)


def _rpe_tiles(rpe_bias):
    heads = rpe_bias.shape[1]
    return pl.pallas_call(
        functools.partial(_rpe_tiles_kernel, _rpe_lower_bounds()),
        grid=(heads, RPE_TILES),
        in_specs=[pl.BlockSpec(memory_space=pltpu.SMEM)],
        out_specs=pl.BlockSpec((1, 1, MOBA_BLOCK, MOBA_BLOCK), lambda h, j: (h, j, 0, 0)),
        out_shape=jax.ShapeDtypeStruct((heads, RPE_TILES, MOBA_BLOCK, MOBA_BLOCK), F32),
        compiler_params=_cparams(("arbitrary", "arbitrary")),
        name="rpe_tiles",
    )(rpe_bias)


def _moba_kernel(q_ref, k_ref, v_ref, t_ref, o_ref, kb_scr, vt_scr, km_scr, sel_scr):
    i = pl.program_id(2)
    nkb, blk, dh = kb_scr.shape

    @pl.when(i == 0)
    def _():
        for n in range(nkb):
            kn = k_ref[0, n * blk:(n + 1) * blk, :]
            kb_scr[n] = kn.astype(BF16)
            km_scr[n:n + 1, :] = jnp.mean(kn, axis=0, keepdims=True)
            vt_scr[n] = v_ref[0, n * blk:(n + 1) * blk, :].T.astype(BF16)

    q = q_ref[0]
    gate = lax.dot_general(km_scr[...], q, NT, precision=HI, preferred_element_type=F32)
    bidx = lax.broadcasted_iota(jnp.int32, gate.shape, 0)
    past = bidx < i
    g = jnp.where(past, gate, -jnp.inf)
    sel = None
    for _ in range(MOBA_TOPK):
        m = jnp.max(g, axis=0, keepdims=True)
        first = jnp.min(jnp.where(g == m, bidx, nkb), axis=0, keepdims=True)
        hit = bidx == first
        sel = hit if sel is None else jnp.logical_or(sel, hit)
        g = jnp.where(hit, -jnp.inf, g)
    sel_scr[...] = jnp.where(jnp.logical_and(sel, past), 0.0, NEG)

    qs = (q * dh ** -0.5).astype(BF16)

    s = lax.dot_general(kb_scr[i], qs, NT, preferred_element_type=F32) + t_ref[0, 0]
    m0 = jnp.max(s, axis=0, keepdims=True)
    p = jnp.exp(s - m0)
    l0 = jnp.sum(p, axis=0, keepdims=True)
    acc0 = jnp.dot(vt_scr[i], p.astype(BF16), preferred_element_type=F32)

    def body(n, carry):
        m, l, acc = carry
        j = jnp.minimum(i - n, RPE_TILES - 1)
        s = (lax.dot_general(kb_scr[n], qs, NT, preferred_element_type=F32)
             + t_ref[0, j] + sel_scr[pl.ds(n, 1), :])
        m_new = jnp.maximum(m, jnp.max(s, axis=0, keepdims=True))
        alpha = jnp.exp(m - m_new)
        p = jnp.exp(s - m_new)
        l = alpha * l + jnp.sum(p, axis=0, keepdims=True)
        acc = alpha * acc + jnp.dot(vt_scr[n], p.astype(BF16), preferred_element_type=F32)
        return m_new, l, acc

    _, l, acc = lax.fori_loop(0, i, body, (m0, l0, acc0))
    o_ref[0] = (acc / l).T


def _moba(proj, tiles, q_col, k_col, v_col):
    bsz, s, _ = proj.shape
    dh, blk, heads = MOBA_HEAD_DIM, MOBA_BLOCK, MOBA_HEADS
    nkb = s // blk
    return pl.pallas_call(
        _moba_kernel,
        grid=(bsz, heads, nkb),
        in_specs=[pl.BlockSpec((1, blk, dh), lambda b, h, i: (b, i, q_col + h)),
                  pl.BlockSpec((1, s, dh), lambda b, h, i: (b, 0, k_col + h)),
                  pl.BlockSpec((1, s, dh), lambda b, h, i: (b, 0, v_col + h)),
                  pl.BlockSpec((1, RPE_TILES, blk, blk), lambda b, h, i: (h, 0, 0, 0))],
        out_specs=pl.BlockSpec((1, blk, dh), lambda b, h, i: (b, i, h)),
        out_shape=jax.ShapeDtypeStruct((bsz, s, heads * dh), F32),
        scratch_shapes=[pltpu.VMEM((nkb, blk, dh), BF16),
                        pltpu.VMEM((nkb, dh, blk), BF16),
                        pltpu.VMEM((nkb, dh), F32),
                        pltpu.VMEM((nkb, blk), F32)],
        compiler_params=_cparams(("arbitrary", "arbitrary", "arbitrary")),
        name="moba_attention",
    )(proj, proj, proj, tiles)


def _gla_kernel(q_ref, k_ref, v_ref, gg_ref, glr_ref, w2_ref, gb_ref, nw_ref, o_ref, st_scr):
    @pl.when(pl.program_id(1) == 0)
    def _():
        st_scr[...] = jnp.zeros(st_scr.shape, F32)

    tg = q_ref.shape[1]
    c = GLA_CHUNK
    x = jnp.dot(glr_ref[0], w2_ref[...], precision=HI, preferred_element_type=F32) + gb_ref[...]
    lg = -_softplus(-x) * (1.0 / GLA_GATE_NORM)
    row = lax.broadcasted_iota(jnp.int32, (c, c), 0)
    col = lax.broadcasted_iota(jnp.int32, (c, c), 1)
    incl = row >= col
    tri = incl.astype(F32)
    nw = nw_ref[...]
    for ci in range(tg // c):
        rows = slice(ci * c, (ci + 1) * c)
        b = jnp.dot(tri, lg[rows], precision=HI, preferred_element_type=F32)
        bl = b[c - 1:c, :]
        q = q_ref[0, rows, :] * GLA_DK ** -0.5
        k = k_ref[0, rows, :]
        q_e = (q * jnp.exp(b)).astype(BF16)
        k_e = (k * jnp.exp(-b)).astype(BF16)
        k_end = (k * jnp.exp(bl - b)).astype(BF16)
        d = jnp.exp(bl)
        for h in range(GLA_HEADS):
            ks = slice(h * GLA_DK, (h + 1) * GLA_DK)
            vs = slice(h * GLA_DV, (h + 1) * GLA_DV)
            vh = v_ref[0, rows, vs].astype(BF16)
            a = lax.dot_general(q_e[:, ks], k_e[:, ks], NT, preferred_element_type=F32)
            a = jnp.where(incl, a, 0.0).astype(BF16)
            st = st_scr[h]
            o = (jnp.dot(a, vh, preferred_element_type=F32)
                 + lax.dot_general(q_e[:, ks], st.astype(BF16), NT, preferred_element_type=F32))
            st_scr[h] = st * d[:, ks] + lax.dot_general(vh, k_end[:, ks], TN, preferred_element_type=F32)
            o = o * lax.rsqrt(jnp.mean(o * o, axis=-1, keepdims=True) + NORM_EPS)
            o_ref[0, rows, vs] = o * nw * _silu(gg_ref[0, rows, vs])


def _gla(proj, gk_w2p, gk_b, o_norm, q_col, k_col, v_col, g_col, r_col, tg=256):
    bsz, s, _ = proj.shape
    tg = min(tg, s)
    qk, vw = GLA_QK_W, GLA_V_W
    return pl.pallas_call(
        _gla_kernel,
        grid=(bsz, s // tg),
        in_specs=[pl.BlockSpec((1, tg, qk), lambda b, t: (b, t, q_col)),
                  pl.BlockSpec((1, tg, qk), lambda b, t: (b, t, k_col)),
                  pl.BlockSpec((1, tg, vw), lambda b, t: (b, t, v_col)),
                  pl.BlockSpec((1, tg, vw), lambda b, t: (b, t, g_col)),
                  pl.BlockSpec((1, tg, LANES), lambda b, t: (b, t, r_col)),
                  pl.BlockSpec((LANES, qk), lambda b, t: (0, 0)),
                  pl.BlockSpec((1, qk), lambda b, t: (0, 0)),
                  pl.BlockSpec((1, GLA_DV), lambda b, t: (0, 0))],
        out_specs=pl.BlockSpec((1, tg, vw), lambda b, t: (b, t, 0)),
        out_shape=jax.ShapeDtypeStruct((bsz, s, vw), F32),
        scratch_shapes=[pltpu.VMEM((GLA_HEADS, GLA_DV, GLA_DK), F32)],
        compiler_params=_cparams(("arbitrary", "arbitrary")),
        name="gla_mixer",
    )(proj, proj, proj, proj, proj, gk_w2p, gk_b, o_norm)


def _gdn_kernel(qkv_ref, gate_ref, ba_ref, cw_ref, pv_ref, nw_ref, o_ref, tail_scr, s_scr):
    @pl.when(pl.program_id(1) == 0)
    def _():
        tail_scr[...] = jnp.zeros(tail_scr.shape, F32)
        s_scr[...] = jnp.zeros(s_scr.shape, F32)

    tg = qkv_ref.shape[1]
    c, dk, grp = GDN_CHUNK, GDN_DK, GDN_GROUP
    gr = grp * c
    w = GDN_W

    x = qkv_ref[0]
    tail = tail_scr[...]
    tail_scr[...] = x[tg - 8:, :]
    r8 = lax.broadcasted_iota(jnp.int32, (8, 1), 0)
    y = x * cw_ref[GDN_CONV - 1:GDN_CONV, :]
    for sft in range(1, GDN_CONV):
        xs = pltpu.roll(x, sft, axis=0)
        head = jnp.where(r8 < sft, pltpu.roll(tail, sft, axis=0), xs[:8, :])
        xs = jnp.concatenate([head, xs[8:, :]], axis=0)
        y = y + xs * cw_ref[GDN_CONV - 1 - sft:GDN_CONV - sft, :]
    y = _silu(y)

    ba = ba_ref[0]
    beta_t = _sigmoid(ba)
    g_t = -jnp.exp(pv_ref[0:1, :]) * _softplus(ba + pv_ref[1:2, :])

    row = lax.broadcasted_iota(jnp.int32, (c, c), 0)
    col = lax.broadcasted_iota(jnp.int32, (c, c), 1)
    tri = (row >= col).astype(F32)
    rr = lax.broadcasted_iota(jnp.int32, (gr, gr), 0)
    cc = lax.broadcasted_iota(jnp.int32, (gr, gr), 1)
    same = (rr // c) == (cc // c)
    incl = jnp.logical_and(same, rr >= cc)
    strict = jnp.logical_and(same, rr > cc)
    eye = (rr == cc).astype(F32)
    nw = nw_ref[...]

    for ci in range(tg // c):
        rows = slice(ci * c, (ci + 1) * c)
        gcum = jnp.dot(tri, g_t[rows], precision=HI, preferred_element_type=F32)
        gcum_t = gcum.T
        for gi in range(GDN_HEADS // grp):
            hs = [gi * grp + u for u in range(grp)]

            def stack(a, off):
                return jnp.concatenate([a[rows, off + h * dk: off + (h + 1) * dk] for h in hs], axis=0)

            q = stack(y, 0)
            k = stack(y, w)
            v = stack(y, 2 * w)
            q = q * lax.rsqrt(jnp.sum(q * q, axis=-1, keepdims=True) + NORM_EPS) * dk ** -0.5
            k = k * lax.rsqrt(jnp.sum(k * k, axis=-1, keepdims=True) + NORM_EPS)
            beta = jnp.concatenate([beta_t[rows, h:h + 1] for h in hs], axis=0)
            gc = jnp.concatenate([gcum[:, GDN_HEADS + h:GDN_HEADS + h + 1] for h in hs], axis=0)
            gc_row = jnp.concatenate([gcum_t[GDN_HEADS + h:GDN_HEADS + h + 1, :] for h in hs], axis=1)
            gl = jnp.concatenate([jnp.broadcast_to(gcum[c - 1:c, GDN_HEADS + h:GDN_HEADS + h + 1], (c, 1))
                                  for h in hs], axis=0)

            decay = jnp.where(incl, jnp.exp(jnp.where(incl, gc - gc_row, 0.0)), 0.0)
            kb = k * beta
            vb = v * beta
            k16 = k.astype(BF16)
            a = lax.dot_general(kb.astype(BF16), k16, NT, preferred_element_type=F32)
            a = jnp.where(strict, a * decay, 0.0)
            t_mat = eye - a
            pw = a
            for _ in range(int(math.log2(c)) - 1):
                pw16 = pw.astype(BF16)
                pw = jnp.dot(pw16, pw16, preferred_element_type=F32)
                t_mat = t_mat + jnp.dot(t_mat.astype(BF16), pw.astype(BF16), preferred_element_type=F32)
            eg = jnp.exp(gc)
            rhs = jnp.concatenate([vb, kb * eg], axis=1).astype(BF16)
            wk = jnp.dot(t_mat.astype(BF16), rhs, preferred_element_type=F32)
            w_val = wk[:, :GDN_DV]
            k_cum = wk[:, GDN_DV:].astype(BF16)
            attn = lax.dot_general(q.astype(BF16), k16, NT, preferred_element_type=F32) * decay
            q_g = (q * eg).astype(BF16)
            k_end = (k * jnp.exp(gl - gc)).astype(BF16)

            v_new = []
            for u, h in enumerate(hs):
                hr = slice(u * c, (u + 1) * c)
                st16 = s_scr[h].astype(BF16)
                v_new.append(w_val[hr] - jnp.dot(k_cum[hr], st16, preferred_element_type=F32))
            v_new = jnp.concatenate(v_new, axis=0)
            v16 = v_new.astype(BF16)
            o_intra = jnp.dot(attn.astype(BF16), v16, preferred_element_type=F32)
            for u, h in enumerate(hs):
                hr = slice(u * c, (u + 1) * c)
                st = s_scr[h]
                o = o_intra[hr] + jnp.dot(q_g[hr], st.astype(BF16), preferred_element_type=F32)
                d_last = jnp.exp(gcum[c - 1:c, GDN_HEADS + h:GDN_HEADS + h + 1])
                s_scr[h] = st * d_last + lax.dot_general(k_end[hr], v16[hr], TN, preferred_element_type=F32)
                o = o * lax.rsqrt(jnp.mean(o * o, axis=-1, keepdims=True) + NORM_EPS)
                cs = slice(h * GDN_DV, (h + 1) * GDN_DV)
                o_ref[0, rows, cs] = o * nw * _silu(gate_ref[0, rows, cs])


def _gdn(proj, conv_w, pvec, o_norm, tg=256):
    bsz, s, _ = proj.shape
    tg = min(tg, s)
    w = GDN_W
    return pl.pallas_call(
        _gdn_kernel,
        grid=(bsz, s // tg),
        in_specs=[pl.BlockSpec((1, tg, 3 * w), lambda b, t: (b, t, 0)),
                  pl.BlockSpec((1, tg, w), lambda b, t: (b, t, 3)),
                  pl.BlockSpec((1, tg, LANES), lambda b, t: (b, t, 4 * w // LANES)),
                  pl.BlockSpec((GDN_CONV, 3 * w), lambda b, t: (0, 0)),
                  pl.BlockSpec((2, LANES), lambda b, t: (0, 0)),
                  pl.BlockSpec((1, GDN_DV), lambda b, t: (0, 0))],
        out_specs=pl.BlockSpec((1, tg, w), lambda b, t: (b, t, 0)),
        out_shape=jax.ShapeDtypeStruct((bsz, s, w), F32),
        scratch_shapes=[pltpu.VMEM((8, 3 * w), F32),
                        pltpu.VMEM((GDN_HEADS, GDN_DK, GDN_DV), F32)],
        compiler_params=_cparams(("arbitrary", "arbitrary")),
        name="gdn_mixer",
    )(proj, proj, proj, conv_w, pvec, o_norm)


def _mixout_kernel(n_act, *refs):
    acts = refs[:n_act]
    ws = refs[n_act:2 * n_act]
    x_ref, g1_ref, lng_ref, lnb_ref, sc_ref, sh_ref, rw_ref = refs[2 * n_act:2 * n_act + 7]
    xo_ref, h_ref, rl_ref = refs[2 * n_act + 7:]
    y = None
    for a_ref, w_ref in zip(acts, ws):
        t = jnp.dot(a_ref[0].astype(BF16), w_ref[...], preferred_element_type=F32)
        y = t if y is None else y + t
    xn = _layer_norm(DEEPNORM_ALPHA * x_ref[0] + g1_ref[0] * y, lng_ref[...], lnb_ref[...])
    xo_ref[0] = xn
    h = xn * (1.0 + sc_ref[0]) + sh_ref[0]
    h_ref[0] = h.astype(BF16)
    rl_ref[0] = lax.dot_general(rw_ref[...], h, NT, precision=HI, preferred_element_type=F32)


def _mixout(acts, ws, x, g1, ln_g, ln_b, sc2, sh2, router_wt, tm=512):
    bsz, s, d = x.shape
    tm = min(tm, s)
    n_act = len(acts)
    ne = router_wt.shape[0]
    vec = pl.BlockSpec((1, 1, d), lambda b, i: (b, 0, 0))
    par = pl.BlockSpec((1, d), lambda b, i: (0, 0))
    in_specs = ([pl.BlockSpec((1, tm, a.shape[-1]), lambda b, i: (b, i, 0)) for a in acts]
                + [pl.BlockSpec(w.shape, lambda b, i: (0, 0)) for w in ws]
                + [pl.BlockSpec((1, tm, d), lambda b, i: (b, i, 0)), vec, par, par, vec, vec,
                   pl.BlockSpec((ne, d), lambda b, i: (0, 0))])
    return pl.pallas_call(
        functools.partial(_mixout_kernel, n_act),
        grid=(bsz, s // tm),
        in_specs=in_specs,
        out_specs=[pl.BlockSpec((1, tm, d), lambda b, i: (b, i, 0)),
                   pl.BlockSpec((1, tm, d), lambda b, i: (b, i, 0)),
                   pl.BlockSpec((1, ne, tm), lambda b, i: (b, 0, i))],
        out_shape=[jax.ShapeDtypeStruct((bsz, s, d), F32),
                   jax.ShapeDtypeStruct((bsz, s, d), BF16),
                   jax.ShapeDtypeStruct((bsz, ne, s), F32)],
        compiler_params=_cparams(("arbitrary", "arbitrary")),
        name="mix_out",
    )(*acts, *ws, x, g1, ln_g, ln_b, sc2, sh2, router_wt)


def _first_max(vals, idx, axis, sentinel):
    m = jnp.max(vals, axis=axis, keepdims=True)
    first = jnp.min(jnp.where(vals == m, idx, sentinel), axis=axis, keepdims=True)
    return m, idx == first


def _router_kernel(rl_ref, rb_ref, g_ref, gt_ref, cnt_ref):
    ne, tn = rl_ref.shape[1], rl_ref.shape[2]
    gsz = ne // N_GROUPS
    scores = _sigmoid(rl_ref[0])
    sel = scores + rb_ref[...]
    ridx = lax.broadcasted_iota(jnp.int32, (gsz, tn), 0)
    gidx = lax.broadcasted_iota(jnp.int32, (N_GROUPS, tn), 0)
    gs = jnp.zeros((N_GROUPS, tn), F32)
    for g in range(N_GROUPS):
        sg = sel[g * gsz:(g + 1) * gsz, :]
        m1, hit = _first_max(sg, ridx, 0, gsz)
        m2 = jnp.max(jnp.where(hit, -jnp.inf, sg), axis=0, keepdims=True)
        gs = jnp.where(gidx == g, m1 + m2, gs)
    gsel = None
    for _ in range(TOPK_GROUPS):
        _, hit = _first_max(gs, gidx, 0, N_GROUPS)
        gsel = hit if gsel is None else jnp.logical_or(gsel, hit)
        gs = jnp.where(hit, -jnp.inf, gs)
    gself = gsel.astype(F32)
    emask = jnp.concatenate([jnp.broadcast_to(gself[g:g + 1, :], (gsz, tn)) for g in range(N_GROUPS)], axis=0)
    cand = jnp.where(emask > 0.5, sel, -jnp.inf)
    eidx = lax.broadcasted_iota(jnp.int32, cand.shape, 0)
    chosen = None
    for _ in range(TOP_K):
        _, hit = _first_max(cand, eidx, 0, ne)
        chosen = hit if chosen is None else jnp.logical_or(chosen, hit)
        cand = jnp.where(hit, -jnp.inf, cand)
    wsel = jnp.where(chosen, scores, 0.0)
    tot = jnp.sum(wsel, axis=0, keepdims=True)
    gates = wsel / (tot + 1e-20) * ROUTED_SCALE
    g_ref[0] = gates.T
    gt_ref[0] = gates
    tile_of = lax.broadcasted_iota(jnp.int32, (tn, LANES), 0) // MOE_TM
    ind = (tile_of == lax.broadcasted_iota(jnp.int32, (tn, LANES), 1)).astype(BF16)
    routed = jnp.where(gates > 0.0, 1.0, 0.0).astype(BF16)
    cnt_ref[0, 0] = jnp.dot(routed, ind, preferred_element_type=F32)


def _router(rl, router_b, tn=1024):
    bsz, ne, s = rl.shape
    tn = min(tn, s)
    return pl.pallas_call(
        _router_kernel,
        grid=(bsz, s // tn),
        in_specs=[pl.BlockSpec((1, ne, tn), lambda b, i: (b, 0, i)),
                  pl.BlockSpec((ne, 1), lambda b, i: (0, 0))],
        out_specs=[pl.BlockSpec((1, tn, ne), lambda b, i: (b, i, 0)),
                   pl.BlockSpec((1, ne, tn), lambda b, i: (b, 0, i)),
                   pl.BlockSpec((1, 1, ne, LANES), lambda b, i: (b, i, 0, 0))],
        out_shape=[jax.ShapeDtypeStruct((bsz, s, ne), F32),
                   jax.ShapeDtypeStruct((bsz, ne, s), F32),
                   jax.ShapeDtypeStruct((bsz, s // tn, ne, LANES), F32)],
        compiler_params=_cparams(("arbitrary", "arbitrary")),
        name="moe_router",
    )(rl, router_b.reshape(ne, 1))


MOE_TM = 256
MOE_ALIGN = 16
MOE_R = 256
MOE_LC = 512
MOE_RUN_BITS = (MOE_TM // MOE_ALIGN).bit_length()
MOE_LMAX = -(-(MOE_TM * TOP_K + N_EXPERTS * (MOE_ALIGN - 1)) // MOE_LC) * MOE_LC


def _ffn(x, wg, wu, wd):
    a = jnp.dot(x, wg, preferred_element_type=F32)
    u = jnp.dot(x, wu, preferred_element_type=F32)
    return jnp.dot((_silu(a) * u).astype(BF16), wd, preferred_element_type=F32)


def _for_each_run(tile, cntp_s, off_s, gst_s, fn):
    def body(e, carry):
        idx = tile * N_EXPERTS + e
        m = cntp_s[idx] // MOE_ALIGN
        o = off_s[idx]
        g = gst_s[idx]
        for b in range(MOE_RUN_BITS):
            lower = (m & ((1 << b) - 1)) * MOE_ALIGN

            @pl.when(((m >> b) & 1) == 1)
            def _():
                fn(pl.multiple_of(o + lower, MOE_ALIGN), pl.multiple_of(g + lower, MOE_ALIGN), MOE_ALIGN << b)
        return carry

    lax.fori_loop(0, N_EXPERTS, body, 0)


def _dispatch_kernel(cntp_s, off_s, gst_s, h_ref, gt_ref, offc_ref, offr_ref, cntr_ref, xs_hbm, sorted_scr, sem):
    i = pl.program_id(0)
    tm = h_ref.shape[0]
    routed = gt_ref[0] > 0.0
    t0 = lax.broadcasted_iota(jnp.int32, (tm, tm), 0)
    t1 = lax.broadcasted_iota(jnp.int32, (tm, tm), 1)
    earlier = jnp.where(t0 < t1, 1.0, 0.0).astype(BF16)
    rank_t = jnp.dot(jnp.where(routed, 1.0, 0.0).astype(BF16), earlier, preferred_element_type=F32)
    pos_t = jnp.where(routed, rank_t + offc_ref[0] + 1.0, 0.0)
    x = h_ref[...]
    offr = offr_ref[0]
    endr = offr + cntr_ref[0]
    for c in range(MOE_LMAX // MOE_LC):
        r = (c * MOE_LC + lax.broadcasted_iota(jnp.int32, (MOE_LC, 1), 0)).astype(F32)
        owner = jnp.where(jnp.logical_and(r >= offr, r < endr), 1.0, 0.0)
        possel = jnp.dot(owner, pos_t, precision=HI, preferred_element_type=F32)
        perm = jnp.where(possel == r + 1.0, 1.0, 0.0).astype(BF16)
        sorted_scr[c * MOE_LC:(c + 1) * MOE_LC, :] = jnp.dot(perm, x, preferred_element_type=F32).astype(BF16)

    def copy(tr, gr, rows):
        return pltpu.make_async_copy(sorted_scr.at[pl.ds(tr, rows)], xs_hbm.at[pl.ds(gr, rows)], sem)

    _for_each_run(i, cntp_s, off_s, gst_s, lambda tr, gr, rows: copy(tr, gr, rows).start())
    _for_each_run(i, cntp_s, off_s, gst_s, lambda tr, gr, rows: copy(tr, gr, rows).wait())


def _expert_ffn_kernel(be_s, nu_s, x_ref, wg_ref, wu_ref, wd_ref, y_ref):
    @pl.when(pl.program_id(0) < nu_s[0])
    def _():
        y_ref[...] = _ffn(x_ref[...], wg_ref[0], wu_ref[0], wd_ref[0]).astype(BF16)


def _combine_kernel(cntp_s, off_s, gst_s, ys_hbm, g_ref, h_ref, offr_ref, offc_ref, cntc_ref,
                    sg_ref, su_ref, sd_ref, x_ref, g2_ref, lng_ref, lnb_ref, o_ref, ys_scr, sem):
    i = pl.program_id(0)
    tm = h_ref.shape[0]

    @pl.when(i == 0)
    def _():
        ys_scr[...] = jnp.zeros(ys_scr.shape, BF16)

    def copy(tr, gr, rows):
        return pltpu.make_async_copy(ys_hbm.at[pl.ds(gr, rows)], ys_scr.at[pl.ds(tr, rows)], sem)

    _for_each_run(i, cntp_s, off_s, gst_s, lambda tr, gr, rows: copy(tr, gr, rows).start())
    acc = _ffn(h_ref[...], sg_ref[...], su_ref[...], sd_ref[...])
    g = g_ref[...]
    routed = g > 0.0
    t0 = lax.broadcasted_iota(jnp.int32, (tm, tm), 0)
    t1 = lax.broadcasted_iota(jnp.int32, (tm, tm), 1)
    earlier = jnp.where(t0 > t1, 1.0, 0.0).astype(BF16)
    rank = jnp.dot(earlier, jnp.where(routed, 1.0, 0.0).astype(BF16), preferred_element_type=F32)
    pos = jnp.where(routed, rank + offr_ref[0] + 1.0, 0.0)
    g16 = g.astype(BF16)
    offc = offc_ref[0]
    endc = offc + cntc_ref[0]
    _for_each_run(i, cntp_s, off_s, gst_s, lambda tr, gr, rows: copy(tr, gr, rows).wait())
    for c in range(MOE_LMAX // MOE_LC):
        r = (c * MOE_LC + lax.broadcasted_iota(jnp.int32, (1, MOE_LC), 1)).astype(F32)
        owner = jnp.where(jnp.logical_and(r >= offc, r < endc), 1.0, 0.0)
        possel = jnp.dot(pos, owner, precision=HI, preferred_element_type=F32)
        gsel = jnp.dot(g16, owner.astype(BF16), preferred_element_type=F32)
        w = jnp.where(possel == r + 1.0, gsel, 0.0).astype(BF16)
        acc = acc + jnp.dot(w, ys_scr[c * MOE_LC:(c + 1) * MOE_LC, :], preferred_element_type=F32)
    z = DEEPNORM_ALPHA * x_ref[...] + g2_ref[0] * acc
    o_ref[...] = _layer_norm(z, lng_ref[...], lnb_ref[...])


def _moe_layout(cnt_raw, tn):
    ne = N_EXPERTS
    nsub = tn // MOE_TM
    cnt = jnp.transpose(cnt_raw[..., :nsub], (0, 1, 3, 2)).reshape(-1, ne).astype(jnp.int32)
    ntiles = cnt.shape[0]
    cntp = (cnt + MOE_ALIGN - 1) // MOE_ALIGN * MOE_ALIGN
    off = jnp.cumsum(cntp, axis=1) - cntp
    tot = jnp.sum(cntp, axis=0)
    totr = (tot + MOE_R - 1) // MOE_R * MOE_R
    end = jnp.cumsum(totr)
    gst = (end - totr)[None, :] + jnp.cumsum(cntp, axis=0) - cntp
    nblk = -(-(ntiles * (MOE_TM * TOP_K + ne * (MOE_ALIGN - 1)) + ne * (MOE_R - 1)) // MOE_R)
    nused = (end[-1] // MOE_R).astype(jnp.int32)
    first_row = jnp.minimum(jnp.arange(nblk, dtype=jnp.int32) * MOE_R, end[-1] - 1)
    blk_exp = jnp.minimum(jnp.searchsorted(end, first_row, side='right'), ne - 1).astype(jnp.int32)
    return cntp, off, gst, nblk, nused.reshape(1), blk_exp


def _moe(h2, gates, gates_t, cnt_raw, tn, wg, wu, wd, sg, su, sd, x, g2, ln_g, ln_b):
    bsz, s, d = x.shape
    t = bsz * s
    tm = MOE_TM
    ne, _, de = wg.shape
    per_b = s // tm
    ntiles = t // tm
    cntp, off, gst, nblk, nused, blk_exp = _moe_layout(cnt_raw, tn)
    scalars = (cntp.reshape(-1), off.reshape(-1), gst.reshape(-1))
    off_f, cnt_f = off.astype(F32), cntp.astype(F32)
    row = pl.BlockSpec((1, 1, ne), lambda i, *_: (i, 0, 0))
    col = pl.BlockSpec((1, ne, 1), lambda i, *_: (i, 0, 0))
    tok = pl.BlockSpec((tm, d), lambda i, *_: (i, 0))
    par = pl.BlockSpec((1, d), lambda i, *_: (0, 0))
    rows = nblk * MOE_R

    xs = pl.pallas_call(
        _dispatch_kernel,
        grid_spec=pltpu.PrefetchScalarGridSpec(
            num_scalar_prefetch=3, grid=(ntiles,),
            in_specs=[tok,
                      pl.BlockSpec((1, ne, tm), lambda i, *_: (i // per_b, 0, i % per_b)),
                      col, row, row],
            out_specs=pl.BlockSpec(memory_space=pl.ANY),
            scratch_shapes=[pltpu.VMEM((MOE_LMAX, d), BF16), pltpu.SemaphoreType.DMA]),
        out_shape=jax.ShapeDtypeStruct((rows, d), BF16),
        compiler_params=_cparams(("arbitrary",)),
        name="moe_dispatch",
    )(*scalars, h2.reshape(t, d), gates_t, off_f.reshape(ntiles, ne, 1), off_f.reshape(ntiles, 1, ne),
      cnt_f.reshape(ntiles, 1, ne))

    blk = pl.BlockSpec((MOE_R, d), lambda j, be, nu: (jnp.minimum(j, nu[0] - 1), 0))
    ys = pl.pallas_call(
        _expert_ffn_kernel,
        grid_spec=pltpu.PrefetchScalarGridSpec(
            num_scalar_prefetch=2, grid=(nblk,),
            in_specs=[blk,
                      pl.BlockSpec((1, d, de), lambda j, be, nu: (be[j], 0, 0)),
                      pl.BlockSpec((1, d, de), lambda j, be, nu: (be[j], 0, 0)),
                      pl.BlockSpec((1, de, d), lambda j, be, nu: (be[j], 0, 0))],
            out_specs=blk),
        out_shape=jax.ShapeDtypeStruct((rows, d), BF16),
        compiler_params=_cparams(("arbitrary",)),
        name="moe_expert_ffn",
    )(blk_exp, nused, xs, wg, wu, wd)

    out = pl.pallas_call(
        _combine_kernel,
        grid_spec=pltpu.PrefetchScalarGridSpec(
            num_scalar_prefetch=3, grid=(ntiles,),
            in_specs=[pl.BlockSpec(memory_space=pl.ANY),
                      pl.BlockSpec((tm, ne), lambda i, *_: (i, 0)),
                      tok, row, col, col,
                      pl.BlockSpec(sg.shape, lambda i, *_: (0, 0)),
                      pl.BlockSpec(su.shape, lambda i, *_: (0, 0)),
                      pl.BlockSpec(sd.shape, lambda i, *_: (0, 0)),
                      tok,
                      pl.BlockSpec((1, 1, d), lambda i, *_: (i // per_b, 0, 0)),
                      par, par],
            out_specs=tok,
            scratch_shapes=[pltpu.VMEM((MOE_LMAX, d), BF16), pltpu.SemaphoreType.DMA]),
        out_shape=jax.ShapeDtypeStruct((t, d), F32),
        compiler_params=_cparams(("arbitrary",)),
        name="moe_combine",
    )(*scalars, ys, gates.reshape(t, ne), h2.reshape(t, d), off_f.reshape(ntiles, 1, ne),
      off_f.reshape(ntiles, ne, 1), cnt_f.reshape(ntiles, ne, 1), sg, su, sd, x.reshape(t, d), g2, ln_g, ln_b)
    return out.reshape(bsz, s, d)


def _pad_cols(w, n):
    return jnp.pad(w, ((0, 0), (0, n - w.shape[1])))


def kernel(x, c, rpe_bias, ada_w, ada_b, ln_mix_g, ln_mix_b, ln_ffn_g, ln_ffn_b, ev_w_in, ev_gk_w2, ev_gk_b, ev_norm, ev_w_out, od_w_in, od_conv_w, od_a_log, od_dt_bias, od_norm, od_w_out, moe_router_w, moe_router_b, moe_w_gate, moe_w_up, moe_w_down, sh_w_gate, sh_w_up, sh_w_down):
    bsz, s, d = x.shape
    mod = _ada(c, ada_w, ada_b)
    tiles = _rpe_tiles(rpe_bias)

    for layer in range(DEPTH):
        sh1, sc1, g1, sh2, sc2, g2 = [mod[layer, :, u * d:(u + 1) * d].reshape(bsz, 1, d) for u in range(6)]
        i = layer // 2
        if layer % 2 == 0:
            n_main = 3 * MOBA_W + 2 * GLA_QK_W + 2 * GLA_V_W
            w_in = jnp.concatenate([ev_w_in[i][:, :n_main], _pad_cols(ev_w_in[i][:, n_main:], LANES)], axis=1)
            proj = _mod_matmul(x, sc1, sh1, w_in.astype(BF16))
            nb = MOBA_W // LANES
            o_a = _moba(proj, tiles, 0, nb, 2 * nb)
            gk_w2p = jnp.pad(ev_gk_w2[i], ((0, LANES - GLA_GATE_RANK), (0, 0)))
            gla0 = 3 * MOBA_W
            o_b = _gla(proj, gk_w2p, ev_gk_b[i].reshape(1, -1), ev_norm[i].reshape(1, -1),
                       gla0 // GLA_QK_W, gla0 // GLA_QK_W + 1,
                       (gla0 + 2 * GLA_QK_W) // GLA_V_W, (gla0 + 2 * GLA_QK_W) // GLA_V_W + 1,
                       n_main // LANES)
            w_out = ev_w_out[i].astype(BF16)
            acts, ws = [o_a, o_b], [w_out[:MOBA_W], w_out[MOBA_W:]]
        else:
            n_main = 4 * GDN_W
            w_in = jnp.concatenate([od_w_in[i][:, :n_main], _pad_cols(od_w_in[i][:, n_main:], LANES)], axis=1)
            proj = _mod_matmul(x, sc1, sh1, w_in.astype(BF16))
            pvec = jnp.zeros((2, LANES), F32)
            pvec = pvec.at[0, GDN_HEADS:2 * GDN_HEADS].set(od_a_log[i])
            pvec = pvec.at[1, GDN_HEADS:2 * GDN_HEADS].set(od_dt_bias[i])
            o = _gdn(proj, od_conv_w[i], pvec, od_norm[i].reshape(1, -1))
            acts, ws = [o], [od_w_out[i].astype(BF16)]

        x, h2, rl = _mixout(acts, ws, x, g1, ln_mix_g[layer].reshape(1, d), ln_mix_b[layer].reshape(1, d),
                            sc2, sh2, moe_router_w[layer].T)
        router_tn = min(1024, s)
        gates, gates_t, cnt_raw = _router(rl, moe_router_b[layer], router_tn)
        x = _moe(h2, gates, gates_t, cnt_raw, router_tn,
                 moe_w_gate[layer].astype(BF16), moe_w_up[layer].astype(BF16),
                 moe_w_down[layer].astype(BF16), sh_w_gate[layer].astype(BF16), sh_w_up[layer].astype(BF16),
                 sh_w_down[layer].astype(BF16), x, g2, ln_ffn_g[layer].reshape(1, d), ln_ffn_b[layer].reshape(1, d))
    return x
```

```python
import functools
import math

import numpy as np
import jax
import jax.numpy as jnp
from jax import lax
from jax.experimental import pallas as pl
from jax.experimental.pallas import tpu as pltpu

F32 = jnp.float32
BF16 = jnp.bfloat16
HI = lax.Precision.HIGHEST
NT = (((1,), (1,)), ((), ()))
TN = (((0,), (0,)), ((), ()))
NEG = -1e30

LANES = 128
VMEM_LIMIT = 56 * 1024 * 1024

DEPTH = 2
MOBA_HEAD_DIM = 128
MOBA_HEADS = 4
MOBA_BLOCK = 256
MOBA_TOPK = 3
GLA_DV = 128
GLA_HEADS = 4
GLA_DK = 64
GLA_GATE_RANK = 16
GLA_GATE_NORM = 16.0
GLA_CHUNK = 64
GDN_DK = 128
GDN_DV = 128
GDN_HEADS = 8
GDN_CONV = 4
GDN_CHUNK = 64
GDN_GROUP = 4
RPE_BUCKETS = 32
RPE_MAX_DIST = 2048
RPE_TILES = 8
N_EXPERTS = 64
TOP_K = 6
N_GROUPS = 8
TOPK_GROUPS = 4
D_EXPERT = 256
ROUTED_SCALE = 2.5
DEEPNORM_ALPHA = float((2 * DEPTH) ** 0.25)
LN_EPS = 1e-5
NORM_EPS = 1e-6

MOBA_W = MOBA_HEADS * MOBA_HEAD_DIM
GLA_QK_W = GLA_HEADS * GLA_DK
GLA_V_W = GLA_HEADS * GLA_DV
GDN_W = GDN_HEADS * GDN_DK


def _cparams(sem):
    return pltpu.CompilerParams(dimension_semantics=sem, vmem_limit_bytes=VMEM_LIMIT)


def _sigmoid(x):
    return 1.0 / (1.0 + jnp.exp(-x))


def _silu(x):
    return x * _sigmoid(x)


def _softplus(x):
    return jnp.maximum(x, 0.0) + jnp.log(1.0 + jnp.exp(-jnp.abs(x)))


def _layer_norm(z, g, b):
    mu = jnp.mean(z, axis=-1, keepdims=True)
    zc = z - mu
    var = jnp.mean(zc * zc, axis=-1, keepdims=True)
    return zc * lax.rsqrt(var + LN_EPS) * g + b


def _ada_kernel(c_ref, w_ref, b_ref, o_ref):
    ca = _silu(c_ref[...])
    o_ref[0] = jnp.dot(ca, w_ref[0], precision=HI, preferred_element_type=F32) + b_ref[0]


def _ada(c, ada_w, ada_b):
    depth, d, n = ada_w.shape
    bsz = c.shape[0]
    tn = 6 * LANES
    return pl.pallas_call(
        _ada_kernel,
        grid=(depth, n // tn),
        in_specs=[pl.BlockSpec((bsz, d), lambda l, j: (0, 0)),
                  pl.BlockSpec((1, d, tn), lambda l, j: (l, 0, j)),
                  pl.BlockSpec((1, 1, tn), lambda l, j: (l, 0, j))],
        out_specs=pl.BlockSpec((1, bsz, tn), lambda l, j: (l, 0, j)),
        out_shape=jax.ShapeDtypeStruct((depth, bsz, n), F32),
        compiler_params=_cparams(("arbitrary", "arbitrary")),
        name="ada_mod",
    )(c, ada_w, ada_b.reshape(depth, 1, n))


def _modmm_kernel(x_ref, sc_ref, sh_ref, w_ref, o_ref, h_scr):
    @pl.when(pl.program_id(2) == 0)
    def _():
        h_scr[...] = (x_ref[0] * (1.0 + sc_ref[0]) + sh_ref[0]).astype(BF16)

    o_ref[0] = jnp.dot(h_scr[...], w_ref[...], preferred_element_type=F32).astype(o_ref.dtype)


def _col_tile(n, cap):
    best = LANES
    for t in range(LANES, cap + 1, LANES):
        if n % t == 0:
            best = t
    return best


def _mod_matmul(x, sc, sh, w, tm=1024, tn_cap=768):
    bsz, s, d = x.shape
    n = w.shape[1]
    tm = min(tm, s)
    tn = _col_tile(n, tn_cap)
    return pl.pallas_call(
        _modmm_kernel,
        grid=(bsz, s // tm, n // tn),
        in_specs=[pl.BlockSpec((1, tm, d), lambda b, i, j: (b, i, 0)),
                  pl.BlockSpec((1, 1, d), lambda b, i, j: (b, 0, 0)),
                  pl.BlockSpec((1, 1, d), lambda b, i, j: (b, 0, 0)),
                  pl.BlockSpec((d, tn), lambda b, i, j: (0, j))],
        out_specs=pl.BlockSpec((1, tm, tn), lambda b, i, j: (b, i, j)),
        out_shape=jax.ShapeDtypeStruct((bsz, s, n), F32),
        scratch_shapes=[pltpu.VMEM((tm, d), BF16)],
        compiler_params=_cparams(("arbitrary", "arbitrary", "arbitrary")),
        name="mod_matmul",
    )(x, sc, sh, w)


def _rpe_lower_bounds():
    exact = RPE_BUCKETS // 2
    d = np.arange(0, 2 * RPE_MAX_DIST, dtype=np.int64)
    logd = np.log(np.maximum(d, 1).astype(np.float64) / exact)
    large = exact + (logd / math.log(RPE_MAX_DIST / exact) * (RPE_BUCKETS - exact)).astype(np.int64)
    large = np.minimum(large, RPE_BUCKETS - 1)
    bucket = np.where(d < exact, d, large)
    return [int(np.argmax(bucket >= k)) for k in range(RPE_BUCKETS)]


def _rpe_tiles_kernel(lo, rpe_ref, o_ref):
    h = pl.program_id(0)
    j = pl.program_id(1)
    blk = o_ref.shape[-1]
    key = lax.broadcasted_iota(jnp.int32, (blk, blk), 0)
    qry = lax.broadcasted_iota(jnp.int32, (blk, blk), 1)
    dist = j * blk + qry - key
    val = jnp.full((blk, blk), rpe_ref[0, h], F32)
    for k in range(1, RPE_BUCKETS):
        val = jnp.where(dist >= lo[k], rpe_ref[k, h], val)
    o_ref[0, 0] = jnp.where(dist >= 0, val, NEG)


def _rpe_tiles(rpe_bias):
    heads = rpe_bias.shape[1]
    return pl.pallas_call(
        functools.partial(_rpe_tiles_kernel, _rpe_lower_bounds()),
        grid=(heads, RPE_TILES),
        in_specs=[pl.BlockSpec(memory_space=pltpu.SMEM)],
        out_specs=pl.BlockSpec((1, 1, MOBA_BLOCK, MOBA_BLOCK), lambda h, j: (h, j, 0, 0)),
        out_shape=jax.ShapeDtypeStruct((heads, RPE_TILES, MOBA_BLOCK, MOBA_BLOCK), F32),
        compiler_params=_cparams(("arbitrary", "arbitrary")),
        name="rpe_tiles",
    )(rpe_bias)


def _moba_kernel(q_ref, k_ref, v_ref, t_ref, o_ref, kb_scr, vt_scr, km_scr, sel_scr):
    i = pl.program_id(2)
    nkb, blk, dh = kb_scr.shape

    @pl.when(i == 0)
    def _():
        for n in range(nkb):
            kn = k_ref[0, n * blk:(n + 1) * blk, :]
            kb_scr[n] = kn.astype(BF16)
            km_scr[n:n + 1, :] = jnp.mean(kn, axis=0, keepdims=True)
            vt_scr[n] = v_ref[0, n * blk:(n + 1) * blk, :].T.astype(BF16)

    q = q_ref[0]
    gate = lax.dot_general(km_scr[...], q, NT, precision=HI, preferred_element_type=F32)
    bidx = lax.broadcasted_iota(jnp.int32, gate.shape, 0)
    past = bidx < i
    g = jnp.where(past, gate, -jnp.inf)
    sel = None
    for _ in range(MOBA_TOPK):
        m = jnp.max(g, axis=0, keepdims=True)
        first = jnp.min(jnp.where(g == m, bidx, nkb), axis=0, keepdims=True)
        hit = bidx == first
        sel = hit if sel is None else jnp.logical_or(sel, hit)
        g = jnp.where(hit, -jnp.inf, g)
    sel_scr[...] = jnp.where(jnp.logical_and(sel, past), 0.0, NEG)

    qs = (q * dh ** -0.5).astype(BF16)

    s = lax.dot_general(kb_scr[i], qs, NT, preferred_element_type=F32) + t_ref[0, 0]
    m0 = jnp.max(s, axis=0, keepdims=True)
    p = jnp.exp(s - m0)
    l0 = jnp.sum(p, axis=0, keepdims=True)
    acc0 = jnp.dot(vt_scr[i], p.astype(BF16), preferred_element_type=F32)

    def body(n, carry):
        m, l, acc = carry
        j = jnp.minimum(i - n, RPE_TILES - 1)
        s = (lax.dot_general(kb_scr[n], qs, NT, preferred_element_type=F32)
             + t_ref[0, j] + sel_scr[pl.ds(n, 1), :])
        m_new = jnp.maximum(m, jnp.max(s, axis=0, keepdims=True))
        alpha = jnp.exp(m - m_new)
        p = jnp.exp(s - m_new)
        l = alpha * l + jnp.sum(p, axis=0, keepdims=True)
        acc = alpha * acc + jnp.dot(vt_scr[n], p.astype(BF16), preferred_element_type=F32)
        return m_new, l, acc

    _, l, acc = lax.fori_loop(0, i, body, (m0, l0, acc0))
    o_ref[0] = (acc / l).T


def _moba(proj, tiles, q_col, k_col, v_col):
    bsz, s, _ = proj.shape
    dh, blk, heads = MOBA_HEAD_DIM, MOBA_BLOCK, MOBA_HEADS
    nkb = s // blk
    return pl.pallas_call(
        _moba_kernel,
        grid=(bsz, heads, nkb),
        in_specs=[pl.BlockSpec((1, blk, dh), lambda b, h, i: (b, i, q_col + h)),
                  pl.BlockSpec((1, s, dh), lambda b, h, i: (b, 0, k_col + h)),
                  pl.BlockSpec((1, s, dh), lambda b, h, i: (b, 0, v_col + h)),
                  pl.BlockSpec((1, RPE_TILES, blk, blk), lambda b, h, i: (h, 0, 0, 0))],
        out_specs=pl.BlockSpec((1, blk, dh), lambda b, h, i: (b, i, h)),
        out_shape=jax.ShapeDtypeStruct((bsz, s, heads * dh), F32),
        scratch_shapes=[pltpu.VMEM((nkb, blk, dh), BF16),
                        pltpu.VMEM((nkb, dh, blk), BF16),
                        pltpu.VMEM((nkb, dh), F32),
                        pltpu.VMEM((nkb, blk), F32)],
        compiler_params=_cparams(("arbitrary", "arbitrary", "arbitrary")),
        name="moba_attention",
    )(proj, proj, proj, tiles)


def _gla_kernel(q_ref, k_ref, v_ref, gg_ref, glr_ref, w2_ref, gb_ref, nw_ref, o_ref, st_scr):
    @pl.when(pl.program_id(1) == 0)
    def _():
        st_scr[...] = jnp.zeros(st_scr.shape, F32)

    tg = q_ref.shape[1]
    c = GLA_CHUNK
    x = jnp.dot(glr_ref[0], w2_ref[...], precision=HI, preferred_element_type=F32) + gb_ref[...]
    lg = -_softplus(-x) * (1.0 / GLA_GATE_NORM)
    row = lax.broadcasted_iota(jnp.int32, (c, c), 0)
    col = lax.broadcasted_iota(jnp.int32, (c, c), 1)
    incl = row >= col
    tri = incl.astype(F32)
    nw = nw_ref[...]
    for ci in range(tg // c):
        rows = slice(ci * c, (ci + 1) * c)
        b = jnp.dot(tri, lg[rows], precision=HI, preferred_element_type=F32)
        bl = b[c - 1:c, :]
        q = q_ref[0, rows, :] * GLA_DK ** -0.5
        k = k_ref[0, rows, :]
        q_e = (q * jnp.exp(b)).astype(BF16)
        k_e = (k * jnp.exp(-b)).astype(BF16)
        k_end = (k * jnp.exp(bl - b)).astype(BF16)
        d = jnp.exp(bl)
        for h in range(GLA_HEADS):
            ks = slice(h * GLA_DK, (h + 1) * GLA_DK)
            vs = slice(h * GLA_DV, (h + 1) * GLA_DV)
            vh = v_ref[0, rows, vs].astype(BF16)
            a = lax.dot_general(q_e[:, ks], k_e[:, ks], NT, preferred_element_type=F32)
            a = jnp.where(incl, a, 0.0).astype(BF16)
            st = st_scr[h]
            o = (jnp.dot(a, vh, preferred_element_type=F32)
                 + lax.dot_general(q_e[:, ks], st.astype(BF16), NT, preferred_element_type=F32))
            st_scr[h] = st * d[:, ks] + lax.dot_general(vh, k_end[:, ks], TN, preferred_element_type=F32)
            o = o * lax.rsqrt(jnp.mean(o * o, axis=-1, keepdims=True) + NORM_EPS)
            o_ref[0, rows, vs] = o * nw * _silu(gg_ref[0, rows, vs])


def _gla(proj, gk_w2p, gk_b, o_norm, q_col, k_col, v_col, g_col, r_col, tg=256):
    bsz, s, _ = proj.shape
    tg = min(tg, s)
    qk, vw = GLA_QK_W, GLA_V_W
    return pl.pallas_call(
        _gla_kernel,
        grid=(bsz, s // tg),
        in_specs=[pl.BlockSpec((1, tg, qk), lambda b, t: (b, t, q_col)),
                  pl.BlockSpec((1, tg, qk), lambda b, t: (b, t, k_col)),
                  pl.BlockSpec((1, tg, vw), lambda b, t: (b, t, v_col)),
                  pl.BlockSpec((1, tg, vw), lambda b, t: (b, t, g_col)),
                  pl.BlockSpec((1, tg, LANES), lambda b, t: (b, t, r_col)),
                  pl.BlockSpec((LANES, qk), lambda b, t: (0, 0)),
                  pl.BlockSpec((1, qk), lambda b, t: (0, 0)),
                  pl.BlockSpec((1, GLA_DV), lambda b, t: (0, 0))],
        out_specs=pl.BlockSpec((1, tg, vw), lambda b, t: (b, t, 0)),
        out_shape=jax.ShapeDtypeStruct((bsz, s, vw), F32),
        scratch_shapes=[pltpu.VMEM((GLA_HEADS, GLA_DV, GLA_DK), F32)],
        compiler_params=_cparams(("arbitrary", "arbitrary")),
        name="gla_mixer",
    )(proj, proj, proj, proj, proj, gk_w2p, gk_b, o_norm)


def _gdn_kernel(qkv_ref, gate_ref, ba_ref, cw_ref, pv_ref, nw_ref, o_ref, tail_scr, s_scr):
    @pl.when(pl.program_id(1) == 0)
    def _():
        tail_scr[...] = jnp.zeros(tail_scr.shape, F32)
        s_scr[...] = jnp.zeros(s_scr.shape, F32)

    tg = qkv_ref.shape[1]
    c, dk, grp = GDN_CHUNK, GDN_DK, GDN_GROUP
    gr = grp * c
    w = GDN_W

    x = qkv_ref[0]
    tail = tail_scr[...]
    tail_scr[...] = x[tg - 8:, :]
    r8 = lax.broadcasted_iota(jnp.int32, (8, 1), 0)
    y = x * cw_ref[GDN_CONV - 1:GDN_CONV, :]
    for sft in range(1, GDN_CONV):
        xs = pltpu.roll(x, sft, axis=0)
        head = jnp.where(r8 < sft, pltpu.roll(tail, sft, axis=0), xs[:8, :])
        xs = jnp.concatenate([head, xs[8:, :]], axis=0)
        y = y + xs * cw_ref[GDN_CONV - 1 - sft:GDN_CONV - sft, :]
    y = _silu(y)

    ba = ba_ref[0]
    beta_t = _sigmoid(ba)
    g_t = -jnp.exp(pv_ref[0:1, :]) * _softplus(ba + pv_ref[1:2, :])

    row = lax.broadcasted_iota(jnp.int32, (c, c), 0)
    col = lax.broadcasted_iota(jnp.int32, (c, c), 1)
    tri = (row >= col).astype(F32)
    rr = lax.broadcasted_iota(jnp.int32, (gr, gr), 0)
    cc = lax.broadcasted_iota(jnp.int32, (gr, gr), 1)
    same = (rr // c) == (cc // c)
    incl = jnp.logical_and(same, rr >= cc)
    strict = jnp.logical_and(same, rr > cc)
    eye = (rr == cc).astype(F32)
    nw = nw_ref[...]

    for ci in range(tg // c):
        rows = slice(ci * c, (ci + 1) * c)
        gcum = jnp.dot(tri, g_t[rows], precision=HI, preferred_element_type=F32)
        gcum_t = gcum.T
        for gi in range(GDN_HEADS // grp):
            hs = [gi * grp + u for u in range(grp)]

            def stack(a, off):
                return jnp.concatenate([a[rows, off + h * dk: off + (h + 1) * dk] for h in hs], axis=0)

            q = stack(y, 0)
            k = stack(y, w)
            v = stack(y, 2 * w)
            q = q * lax.rsqrt(jnp.sum(q * q, axis=-1, keepdims=True) + NORM_EPS) * dk ** -0.5
            k = k * lax.rsqrt(jnp.sum(k * k, axis=-1, keepdims=True) + NORM_EPS)
            beta = jnp.concatenate([beta_t[rows, h:h + 1] for h in hs], axis=0)
            gc = jnp.concatenate([gcum[:, GDN_HEADS + h:GDN_HEADS + h + 1] for h in hs], axis=0)
            gc_row = jnp.concatenate([gcum_t[GDN_HEADS + h:GDN_HEADS + h + 1, :] for h in hs], axis=1)
            gl = jnp.concatenate([jnp.broadcast_to(gcum[c - 1:c, GDN_HEADS + h:GDN_HEADS + h + 1], (c, 1))
                                  for h in hs], axis=0)

            decay = jnp.where(incl, jnp.exp(jnp.where(incl, gc - gc_row, 0.0)), 0.0)
            kb = k * beta
            vb = v * beta
            k16 = k.astype(BF16)
            a = lax.dot_general(kb.astype(BF16), k16, NT, preferred_element_type=F32)
            a = jnp.where(strict, a * decay, 0.0)
            t_mat = eye - a
            pw = a
            for _ in range(int(math.log2(c)) - 1):
                pw16 = pw.astype(BF16)
                pw = jnp.dot(pw16, pw16, preferred_element_type=F32)
                t_mat = t_mat + jnp.dot(t_mat.astype(BF16), pw.astype(BF16), preferred_element_type=F32)
            eg = jnp.exp(gc)
            rhs = jnp.concatenate([vb, kb * eg], axis=1).astype(BF16)
            wk = jnp.dot(t_mat.astype(BF16), rhs, preferred_element_type=F32)
            w_val = wk[:, :GDN_DV]
            k_cum = wk[:, GDN_DV:].astype(BF16)
            attn = lax.dot_general(q.astype(BF16), k16, NT, preferred_element_type=F32) * decay
            q_g = (q * eg).astype(BF16)
            k_end = (k * jnp.exp(gl - gc)).astype(BF16)

            v_new = []
            for u, h in enumerate(hs):
                hr = slice(u * c, (u + 1) * c)
                st16 = s_scr[h].astype(BF16)
                v_new.append(w_val[hr] - jnp.dot(k_cum[hr], st16, preferred_element_type=F32))
            v_new = jnp.concatenate(v_new, axis=0)
            v16 = v_new.astype(BF16)
            o_intra = jnp.dot(attn.astype(BF16), v16, preferred_element_type=F32)
            for u, h in enumerate(hs):
                hr = slice(u * c, (u + 1) * c)
                st = s_scr[h]
                o = o_intra[hr] + jnp.dot(q_g[hr], st.astype(BF16), preferred_element_type=F32)
                d_last = jnp.exp(gcum[c - 1:c, GDN_HEADS + h:GDN_HEADS + h + 1])
                s_scr[h] = st * d_last + lax.dot_general(k_end[hr], v16[hr], TN, preferred_element_type=F32)
                o = o * lax.rsqrt(jnp.mean(o * o, axis=-1, keepdims=True) + NORM_EPS)
                cs = slice(h * GDN_DV, (h + 1) * GDN_DV)
                o_ref[0, rows, cs] = o * nw * _silu(gate_ref[0, rows, cs])


def _gdn(proj, conv_w, pvec, o_norm, tg=256):
    bsz, s, _ = proj.shape
    tg = min(tg, s)
    w = GDN_W
    return pl.pallas_call(
        _gdn_kernel,
        grid=(bsz, s // tg),
        in_specs=[pl.BlockSpec((1, tg, 3 * w), lambda b, t: (b, t, 0)),
                  pl.BlockSpec((1, tg, w), lambda b, t: (b, t, 3)),
                  pl.BlockSpec((1, tg, LANES), lambda b, t: (b, t, 4 * w // LANES)),
                  pl.BlockSpec((GDN_CONV, 3 * w), lambda b, t: (0, 0)),
                  pl.BlockSpec((2, LANES), lambda b, t: (0, 0)),
                  pl.BlockSpec((1, GDN_DV), lambda b, t: (0, 0))],
        out_specs=pl.BlockSpec((1, tg, w), lambda b, t: (b, t, 0)),
        out_shape=jax.ShapeDtypeStruct((bsz, s, w), F32),
        scratch_shapes=[pltpu.VMEM((8, 3 * w), F32),
                        pltpu.VMEM((GDN_HEADS, GDN_DK, GDN_DV), F32)],
        compiler_params=_cparams(("arbitrary", "arbitrary")),
        name="gdn_mixer",
    )(proj, proj, proj, conv_w, pvec, o_norm)


def _mixout_kernel(n_act, *refs):
    acts = refs[:n_act]
    ws = refs[n_act:2 * n_act]
    x_ref, g1_ref, lng_ref, lnb_ref, sc_ref, sh_ref, rw_ref = refs[2 * n_act:2 * n_act + 7]
    xo_ref, h_ref, rl_ref = refs[2 * n_act + 7:]
    y = None
    for a_ref, w_ref in zip(acts, ws):
        t = jnp.dot(a_ref[0].astype(BF16), w_ref[...], preferred_element_type=F32)
        y = t if y is None else y + t
    xn = _layer_norm(DEEPNORM_ALPHA * x_ref[0] + g1_ref[0] * y, lng_ref[...], lnb_ref[...])
    xo_ref[0] = xn
    h = xn * (1.0 + sc_ref[0]) + sh_ref[0]
    h_ref[0] = h.astype(BF16)
    rl_ref[0] = lax.dot_general(rw_ref[...], h, NT, precision=HI, preferred_element_type=F32)


def _mixout(acts, ws, x, g1, ln_g, ln_b, sc2, sh2, router_wt, tm=512):
    bsz, s, d = x.shape
    tm = min(tm, s)
    n_act = len(acts)
    ne = router_wt.shape[0]
    vec = pl.BlockSpec((1, 1, d), lambda b, i: (b, 0, 0))
    par = pl.BlockSpec((1, d), lambda b, i: (0, 0))
    in_specs = ([pl.BlockSpec((1, tm, a.shape[-1]), lambda b, i: (b, i, 0)) for a in acts]
                + [pl.BlockSpec(w.shape, lambda b, i: (0, 0)) for w in ws]
                + [pl.BlockSpec((1, tm, d), lambda b, i: (b, i, 0)), vec, par, par, vec, vec,
                   pl.BlockSpec((ne, d), lambda b, i: (0, 0))])
    return pl.pallas_call(
        functools.partial(_mixout_kernel, n_act),
        grid=(bsz, s // tm),
        in_specs=in_specs,
        out_specs=[pl.BlockSpec((1, tm, d), lambda b, i: (b, i, 0)),
                   pl.BlockSpec((1, tm, d), lambda b, i: (b, i, 0)),
                   pl.BlockSpec((1, ne, tm), lambda b, i: (b, 0, i))],
        out_shape=[jax.ShapeDtypeStruct((bsz, s, d), F32),
                   jax.ShapeDtypeStruct((bsz, s, d), BF16),
                   jax.ShapeDtypeStruct((bsz, ne, s), F32)],
        compiler_params=_cparams(("arbitrary", "arbitrary")),
        name="mix_out",
    )(*acts, *ws, x, g1, ln_g, ln_b, sc2, sh2, router_wt)


def _first_max(vals, idx, axis, sentinel):
    m = jnp.max(vals, axis=axis, keepdims=True)
    first = jnp.min(jnp.where(vals == m, idx, sentinel), axis=axis, keepdims=True)
    return m, idx == first


def _router_kernel(rl_ref, rb_ref, g_ref, gt_ref, cnt_ref):
    ne, tn = rl_ref.shape[1], rl_ref.shape[2]
    gsz = ne // N_GROUPS
    scores = _sigmoid(rl_ref[0])
    sel = scores + rb_ref[...]
    ridx = lax.broadcasted_iota(jnp.int32, (gsz, tn), 0)
    gidx = lax.broadcasted_iota(jnp.int32, (N_GROUPS, tn), 0)
    gs = jnp.zeros((N_GROUPS, tn), F32)
    for g in range(N_GROUPS):
        sg = sel[g * gsz:(g + 1) * gsz, :]
        m1, hit = _first_max(sg, ridx, 0, gsz)
        m2 = jnp.max(jnp.where(hit, -jnp.inf, sg), axis=0, keepdims=True)
        gs = jnp.where(gidx == g, m1 + m2, gs)
    gsel = None
    for _ in range(TOPK_GROUPS):
        _, hit = _first_max(gs, gidx, 0, N_GROUPS)
        gsel = hit if gsel is None else jnp.logical_or(gsel, hit)
        gs = jnp.where(hit, -jnp.inf, gs)
    gself = gsel.astype(F32)
    emask = jnp.concatenate([jnp.broadcast_to(gself[g:g + 1, :], (gsz, tn)) for g in range(N_GROUPS)], axis=0)
    cand = jnp.where(emask > 0.5, sel, -jnp.inf)
    eidx = lax.broadcasted_iota(jnp.int32, cand.shape, 0)
    chosen = None
    for _ in range(TOP_K):
        _, hit = _first_max(cand, eidx, 0, ne)
        chosen = hit if chosen is None else jnp.logical_or(chosen, hit)
        cand = jnp.where(hit, -jnp.inf, cand)
    wsel = jnp.where(chosen, scores, 0.0)
    tot = jnp.sum(wsel, axis=0, keepdims=True)
    gates = wsel / (tot + 1e-20) * ROUTED_SCALE
    g_ref[0] = gates.T
    gt_ref[0] = gates
    tile_of = lax.broadcasted_iota(jnp.int32, (tn, LANES), 0) // MOE_TM
    ind = (tile_of == lax.broadcasted_iota(jnp.int32, (tn, LANES), 1)).astype(BF16)
    routed = jnp.where(gates > 0.0, 1.0, 0.0).astype(BF16)
    cnt_ref[0, 0] = jnp.dot(routed, ind, preferred_element_type=F32)


def _router(rl, router_b, tn=1024):
    bsz, ne, s = rl.shape
    tn = min(tn, s)
    return pl.pallas_call(
        _router_kernel,
        grid=(bsz, s // tn),
        in_specs=[pl.BlockSpec((1, ne, tn), lambda b, i: (b, 0, i)),
                  pl.BlockSpec((ne, 1), lambda b, i: (0, 0))],
        out_specs=[pl.BlockSpec((1, tn, ne), lambda b, i: (b, i, 0)),
                   pl.BlockSpec((1, ne, tn), lambda b, i: (b, 0, i)),
                   pl.BlockSpec((1, 1, ne, LANES), lambda b, i: (b, i, 0, 0))],
        out_shape=[jax.ShapeDtypeStruct((bsz, s, ne), F32),
                   jax.ShapeDtypeStruct((bsz, ne, s), F32),
                   jax.ShapeDtypeStruct((bsz, s // tn, ne, LANES), F32)],
        compiler_params=_cparams(("arbitrary", "arbitrary")),
        name="moe_router",
    )(rl, router_b.reshape(ne, 1))


MOE_TM = 256
MOE_ALIGN = 16
MOE_R = 512
MOE_LC = 512
MOE_LMAX = -(-(MOE_TM * TOP_K + N_EXPERTS * (MOE_ALIGN - 1)) // MOE_LC) * MOE_LC
MOE_NP = MOE_LMAX // MOE_ALIGN
POS_SPLIT = 256.0


def _ffn(x, wg, wu, wd):
    a = jnp.dot(x, wg, preferred_element_type=F32)
    u = jnp.dot(x, wu, preferred_element_type=F32)
    return jnp.dot((_silu(a) * u).astype(BF16), wd, preferred_element_type=F32)


def _split_pos(pos, axis):
    hi = jnp.floor(pos * (1.0 / POS_SPLIT)) * POS_SPLIT
    return jnp.concatenate([hi, pos - hi], axis=axis).astype(BF16)


def _for_each_piece(n_pieces, fn):
    def body(p, carry):
        fn(p)
        return carry

    lax.fori_loop(0, n_pieces, body, 0)


def _dispatch_kernel(dst_s, np_s, h_ref, gt_ref, offc_ref, offr_ref, cntr_ref, xs_hbm, sorted_scr, sems):
    i = pl.program_id(0)
    last = pl.num_programs(0) - 1
    slot = i % 2
    tm = h_ref.shape[0]
    routed = gt_ref[0] > 0.0
    t0 = lax.broadcasted_iota(jnp.int32, (tm, tm), 0)
    t1 = lax.broadcasted_iota(jnp.int32, (tm, tm), 1)
    earlier = jnp.where(t0 < t1, 1.0, 0.0).astype(BF16)
    rank_t = jnp.dot(jnp.where(routed, 1.0, 0.0).astype(BF16), earlier, preferred_element_type=F32)
    pos_t = jnp.where(routed, rank_t + offc_ref[0] + 1.0, 0.0)
    pos2 = _split_pos(pos_t, 0)
    x = h_ref[...]
    offr = offr_ref[0]
    endr = offr + cntr_ref[0]
    for c in range(MOE_LMAX // MOE_LC):
        r = (c * MOE_LC + lax.broadcasted_iota(jnp.int32, (MOE_LC, 1), 0)).astype(F32)
        owner = jnp.where(jnp.logical_and(r >= offr, r < endr), 1.0, 0.0).astype(BF16)
        possel = jnp.dot(jnp.concatenate([owner, owner], axis=1), pos2, preferred_element_type=F32)
        perm = jnp.where(possel == r + 1.0, 1.0, 0.0).astype(BF16)
        sorted_scr[slot, c * MOE_LC:(c + 1) * MOE_LC, :] = (
            jnp.dot(perm, x, preferred_element_type=F32).astype(BF16))

    def piece(sl, tile, p):
        src = sorted_scr.at[sl, pl.ds(pl.multiple_of(p * MOE_ALIGN, MOE_ALIGN), MOE_ALIGN)]
        dst = xs_hbm.at[pl.ds(pl.multiple_of(dst_s[tile * MOE_NP + p], MOE_ALIGN), MOE_ALIGN)]
        return pltpu.make_async_copy(src, dst, sems.at[sl])

    _for_each_piece(np_s[i], lambda p: piece(slot, i, p).start())

    @pl.when(i > 0)
    def _():
        _for_each_piece(np_s[i - 1], lambda p: piece(1 - slot, i - 1, p).wait())

    @pl.when(i == last)
    def _():
        _for_each_piece(np_s[i], lambda p: piece(slot, i, p).wait())


def _expert_ffn_kernel(be_s, nu_s, x_ref, wg_ref, wu_ref, wd_ref, y_ref):
    @pl.when(pl.program_id(0) < nu_s[0])
    def _():
        y_ref[...] = _ffn(x_ref[...], wg_ref[0], wu_ref[0], wd_ref[0]).astype(BF16)


def _combine_kernel(dst_s, np_s, ys_hbm, g_ref, h_ref, offr_ref, offc_ref, cntc_ref,
                    sg_ref, su_ref, sd_ref, x_ref, g2_ref, lng_ref, lnb_ref, o_ref, ys_scr, sems):
    i = pl.program_id(0)
    last = pl.num_programs(0) - 1
    slot = i % 2
    tm = h_ref.shape[0]

    def piece(sl, tile, p):
        src = ys_hbm.at[pl.ds(pl.multiple_of(dst_s[tile * MOE_NP + p], MOE_ALIGN), MOE_ALIGN)]
        dst = ys_scr.at[sl, pl.ds(pl.multiple_of(p * MOE_ALIGN, MOE_ALIGN), MOE_ALIGN)]
        return pltpu.make_async_copy(src, dst, sems.at[sl])

    @pl.when(i == 0)
    def _():
        ys_scr[...] = jnp.zeros(ys_scr.shape, BF16)
        _for_each_piece(np_s[0], lambda p: piece(0, 0, p).start())

    @pl.when(i < last)
    def _():
        _for_each_piece(np_s[i + 1], lambda p: piece(1 - slot, i + 1, p).start())

    acc = _ffn(h_ref[...], sg_ref[...], su_ref[...], sd_ref[...])
    g = g_ref[...]
    routed = g > 0.0
    t0 = lax.broadcasted_iota(jnp.int32, (tm, tm), 0)
    t1 = lax.broadcasted_iota(jnp.int32, (tm, tm), 1)
    earlier = jnp.where(t0 > t1, 1.0, 0.0).astype(BF16)
    rank = jnp.dot(earlier, jnp.where(routed, 1.0, 0.0).astype(BF16), preferred_element_type=F32)
    pos = jnp.where(routed, rank + offr_ref[0] + 1.0, 0.0)
    pos2 = _split_pos(pos, 1)
    g16 = g.astype(BF16)
    offc = offc_ref[0]
    endc = offc + cntc_ref[0]
    _for_each_piece(np_s[i], lambda p: piece(slot, i, p).wait())
    for c in range(MOE_LMAX // MOE_LC):
        r = (c * MOE_LC + lax.broadcasted_iota(jnp.int32, (1, MOE_LC), 1)).astype(F32)
        owner = jnp.where(jnp.logical_and(r >= offc, r < endc), 1.0, 0.0).astype(BF16)
        possel = jnp.dot(pos2, jnp.concatenate([owner, owner], axis=0), preferred_element_type=F32)
        gsel = jnp.dot(g16, owner, preferred_element_type=F32)
        w = jnp.where(possel == r + 1.0, gsel, 0.0).astype(BF16)
        acc = acc + jnp.dot(w, ys_scr[slot, c * MOE_LC:(c + 1) * MOE_LC, :], preferred_element_type=F32)
    z = DEEPNORM_ALPHA * x_ref[...] + g2_ref[0] * acc
    o_ref[...] = _layer_norm(z, lng_ref[...], lnb_ref[...])


def _ceil_to(v, m):
    return jnp.floor((v + (m - 1.0)) * (1.0 / m)) * m


def _moe_layout_kernel(cnt_ref, dst_ref, np_ref, off_ref, cntp_ref, be_ref, nu_ref):
    cnt = cnt_ref[...]
    ntiles, ne = cnt.shape
    cntp = _ceil_to(cnt, MOE_ALIGN)
    e0 = lax.broadcasted_iota(jnp.int32, (ne, ne), 0)
    e1 = lax.broadcasted_iota(jnp.int32, (ne, ne), 1)
    i0 = lax.broadcasted_iota(jnp.int32, (ntiles, ntiles), 0)
    i1 = lax.broadcasted_iota(jnp.int32, (ntiles, ntiles), 1)

    def mm(a, b):
        return jnp.dot(a, b, precision=HI, preferred_element_type=F32)

    off = mm(cntp, jnp.where(e0 < e1, 1.0, 0.0))
    before = mm(jnp.where(i0 > i1, 1.0, 0.0), cntp)
    tot = jnp.broadcast_to(jnp.sum(cntp, axis=0, keepdims=True), (8, ne))
    totr = _ceil_to(tot, MOE_R)
    base = mm(totr, jnp.where(e0 < e1, 1.0, 0.0))[0:1, :]
    delta = base + before - off
    step = mm(delta, jnp.where(e0 == e1, 1.0, 0.0) - jnp.where(e0 + 1 == e1, 1.0, 0.0))
    rowp = (lax.broadcasted_iota(jnp.int32, (1, MOE_NP), 1) * MOE_ALIGN).astype(F32)
    dst = jnp.broadcast_to(rowp, (ntiles, MOE_NP))
    for e in range(ne):
        dst = dst + jnp.where(off[:, e:e + 1] <= rowp, step[:, e:e + 1], 0.0)
    dst_ref[...] = dst.astype(jnp.int32)
    pieces = jnp.sum(cntp, axis=1, keepdims=True) * (1.0 / MOE_ALIGN)
    np_ref[...] = jnp.broadcast_to(pieces, np_ref.shape).astype(jnp.int32)
    off_ref[...] = off
    cntp_ref[...] = cntp
    tot_c = lax.dot_general(cntp, jnp.ones((ntiles, LANES), F32), TN, precision=HI, preferred_element_type=F32)
    end_c = mm(jnp.where(e0 >= e1, 1.0, 0.0), _ceil_to(tot_c, MOE_R))[:, 0:1]
    total = end_c[ne - 1:ne, :]
    first = jnp.minimum((lax.broadcasted_iota(jnp.int32, be_ref.shape, 1) * MOE_R).astype(F32), total - 1.0)
    be = jnp.sum(jnp.where(end_c <= first, 1.0, 0.0), axis=0, keepdims=True)
    be_ref[...] = jnp.minimum(be, ne - 1.0).astype(jnp.int32)
    nu_ref[...] = jnp.broadcast_to(total * (1.0 / MOE_R), nu_ref.shape).astype(jnp.int32)


def _moe_layout(cnt_raw, tn):
    ne = N_EXPERTS
    nsub = tn // MOE_TM
    cnt = jnp.transpose(cnt_raw[..., :nsub], (0, 1, 3, 2)).reshape(-1, ne)
    ntiles = cnt.shape[0]
    nblk = -(-(ntiles * (MOE_TM * TOP_K + ne * (MOE_ALIGN - 1)) + ne * (MOE_R - 1)) // MOE_R)
    nblk_pad = -(-nblk // LANES) * LANES
    dst, npieces, off, cntp, blk_exp, nused = pl.pallas_call(
        _moe_layout_kernel,
        out_shape=[jax.ShapeDtypeStruct((ntiles, MOE_NP), jnp.int32),
                   jax.ShapeDtypeStruct((ntiles, LANES), jnp.int32),
                   jax.ShapeDtypeStruct((ntiles, ne), F32),
                   jax.ShapeDtypeStruct((ntiles, ne), F32),
                   jax.ShapeDtypeStruct((1, nblk_pad), jnp.int32),
                   jax.ShapeDtypeStruct((1, LANES), jnp.int32)],
        compiler_params=pltpu.CompilerParams(vmem_limit_bytes=VMEM_LIMIT),
        name="moe_layout",
    )(cnt)
    return dst.reshape(-1), npieces[:, 0], off, cntp, nblk, nused[0, :1], blk_exp[0, :nblk]


def _moe(h2, gates, gates_t, cnt_raw, tn, wg, wu, wd, sg, su, sd, x, g2, ln_g, ln_b):
    bsz, s, d = x.shape
    t = bsz * s
    tm = MOE_TM
    ne, _, de = wg.shape
    per_b = s // tm
    ntiles = t // tm
    dst, npieces, off_f, cnt_f, nblk, nused, blk_exp = _moe_layout(cnt_raw, tn)
    scalars = (dst, npieces)
    row = pl.BlockSpec((1, 1, ne), lambda i, *_: (i, 0, 0))
    col = pl.BlockSpec((1, ne, 1), lambda i, *_: (i, 0, 0))
    tok = pl.BlockSpec((tm, d), lambda i, *_: (i, 0))
    par = pl.BlockSpec((1, d), lambda i, *_: (0, 0))
    rows = nblk * MOE_R

    xs = pl.pallas_call(
        _dispatch_kernel,
        grid_spec=pltpu.PrefetchScalarGridSpec(
            num_scalar_prefetch=2, grid=(ntiles,),
            in_specs=[tok,
                      pl.BlockSpec((1, ne, tm), lambda i, *_: (i // per_b, 0, i % per_b)),
                      col, row, row],
            out_specs=pl.BlockSpec(memory_space=pl.ANY),
            scratch_shapes=[pltpu.VMEM((2, MOE_LMAX, d), BF16), pltpu.SemaphoreType.DMA((2,))]),
        out_shape=jax.ShapeDtypeStruct((rows, d), BF16),
        compiler_params=_cparams(("arbitrary",)),
        name="moe_dispatch",
    )(*scalars, h2.reshape(t, d), gates_t, off_f.reshape(ntiles, ne, 1), off_f.reshape(ntiles, 1, ne),
      cnt_f.reshape(ntiles, 1, ne))

    blk = pl.BlockSpec((MOE_R, d), lambda j, be, nu: (jnp.minimum(j, nu[0] - 1), 0))
    ys = pl.pallas_call(
        _expert_ffn_kernel,
        grid_spec=pltpu.PrefetchScalarGridSpec(
            num_scalar_prefetch=2, grid=(nblk,),
            in_specs=[blk,
                      pl.BlockSpec((1, d, de), lambda j, be, nu: (be[j], 0, 0)),
                      pl.BlockSpec((1, d, de), lambda j, be, nu: (be[j], 0, 0)),
                      pl.BlockSpec((1, de, d), lambda j, be, nu: (be[j], 0, 0))],
            out_specs=blk),
        out_shape=jax.ShapeDtypeStruct((rows, d), BF16),
        compiler_params=_cparams(("arbitrary",)),
        name="moe_expert_ffn",
    )(blk_exp, nused, xs, wg, wu, wd)

    out = pl.pallas_call(
        _combine_kernel,
        grid_spec=pltpu.PrefetchScalarGridSpec(
            num_scalar_prefetch=2, grid=(ntiles,),
            in_specs=[pl.BlockSpec(memory_space=pl.ANY),
                      pl.BlockSpec((tm, ne), lambda i, *_: (i, 0)),
                      tok, row, col, col,
                      pl.BlockSpec(sg.shape, lambda i, *_: (0, 0)),
                      pl.BlockSpec(su.shape, lambda i, *_: (0, 0)),
                      pl.BlockSpec(sd.shape, lambda i, *_: (0, 0)),
                      tok,
                      pl.BlockSpec((1, 1, d), lambda i, *_: (i // per_b, 0, 0)),
                      par, par],
            out_specs=tok,
            scratch_shapes=[pltpu.VMEM((2, MOE_LMAX, d), BF16), pltpu.SemaphoreType.DMA((2,))]),
        out_shape=jax.ShapeDtypeStruct((t, d), F32),
        compiler_params=_cparams(("arbitrary",)),
        name="moe_combine",
    )(*scalars, ys, gates.reshape(t, ne), h2.reshape(t, d), off_f.reshape(ntiles, 1, ne),
      off_f.reshape(ntiles, ne, 1), cnt_f.reshape(ntiles, ne, 1), sg, su, sd, x.reshape(t, d), g2, ln_g, ln_b)
    return out.reshape(bsz, s, d)


def _pad_cols(w, n):
    return jnp.pad(w, ((0, 0), (0, n - w.shape[1])))


def kernel(x, c, rpe_bias, ada_w, ada_b, ln_mix_g, ln_mix_b, ln_ffn_g, ln_ffn_b, ev_w_in, ev_gk_w2, ev_gk_b, ev_norm, ev_w_out, od_w_in, od_conv_w, od_a_log, od_dt_bias, od_norm, od_w_out, moe_router_w, moe_router_b, moe_w_gate, moe_w_up, moe_w_down, sh_w_gate, sh_w_up, sh_w_down):
    bsz, s, d = x.shape
    mod = _ada(c, ada_w, ada_b)
    tiles = _rpe_tiles(rpe_bias)

    for layer in range(DEPTH):
        sh1, sc1, g1, sh2, sc2, g2 = [mod[layer, :, u * d:(u + 1) * d].reshape(bsz, 1, d) for u in range(6)]
        i = layer // 2
        if layer % 2 == 0:
            n_main = 3 * MOBA_W + 2 * GLA_QK_W + 2 * GLA_V_W
            w_in = jnp.concatenate([ev_w_in[i][:, :n_main], _pad_cols(ev_w_in[i][:, n_main:], LANES)], axis=1)
            proj = _mod_matmul(x, sc1, sh1, w_in.astype(BF16))
            nb = MOBA_W // LANES
            o_a = _moba(proj, tiles, 0, nb, 2 * nb)
            gk_w2p = jnp.pad(ev_gk_w2[i], ((0, LANES - GLA_GATE_RANK), (0, 0)))
            gla0 = 3 * MOBA_W
            o_b = _gla(proj, gk_w2p, ev_gk_b[i].reshape(1, -1), ev_norm[i].reshape(1, -1),
                       gla0 // GLA_QK_W, gla0 // GLA_QK_W + 1,
                       (gla0 + 2 * GLA_QK_W) // GLA_V_W, (gla0 + 2 * GLA_QK_W) // GLA_V_W + 1,
                       n_main // LANES)
            w_out = ev_w_out[i].astype(BF16)
            acts, ws = [o_a, o_b], [w_out[:MOBA_W], w_out[MOBA_W:]]
        else:
            n_main = 4 * GDN_W
            w_in = jnp.concatenate([od_w_in[i][:, :n_main], _pad_cols(od_w_in[i][:, n_main:], LANES)], axis=1)
            proj = _mod_matmul(x, sc1, sh1, w_in.astype(BF16))
            pvec = jnp.zeros((2, LANES), F32)
            pvec = pvec.at[0, GDN_HEADS:2 * GDN_HEADS].set(od_a_log[i])
            pvec = pvec.at[1, GDN_HEADS:2 * GDN_HEADS].set(od_dt_bias[i])
            o = _gdn(proj, od_conv_w[i], pvec, od_norm[i].reshape(1, -1))
            acts, ws = [o], [od_w_out[i].astype(BF16)]

        x, h2, rl = _mixout(acts, ws, x, g1, ln_mix_g[layer].reshape(1, d), ln_mix_b[layer].reshape(1, d),
                            sc2, sh2, moe_router_w[layer].T)
        router_tn = min(1024, s)
        gates, gates_t, cnt_raw = _router(rl, moe_router_b[layer], router_tn)
        x = _moe(h2, gates, gates_t, cnt_raw, router_tn,
                 moe_w_gate[layer].astype(BF16), moe_w_up[layer].astype(BF16),
                 moe_w_down[layer].astype(BF16), sh_w_gate[layer].astype(BF16), sh_w_up[layer].astype(BF16),
                 sh_w_down[layer].astype(BF16), x, g2, ln_ffn_g[layer].reshape(1, d), ln_ffn_b[layer].reshape(1, d))
    return x
```

```python
import functools
import math

import numpy as np
import jax
import jax.numpy as jnp
from jax import lax
from jax.experimental import pallas as pl
from jax.experimental.pallas import tpu as pltpu

F32 = jnp.float32
BF16 = jnp.bfloat16
HI = lax.Precision.HIGHEST
NT = (((1,), (1,)), ((), ()))
TN = (((0,), (0,)), ((), ()))
NEG = -1e30

LANES = 128
VMEM_LIMIT = 56 * 1024 * 1024

DEPTH = 2
MOBA_HEAD_DIM = 128
MOBA_HEADS = 4
MOBA_BLOCK = 256
MOBA_TOPK = 3
MOBA_GROUP = 4
GLA_DV = 128
GLA_HEADS = 4
GLA_DK = 64
GLA_GATE_RANK = 16
GLA_GATE_NORM = 16.0
GLA_CHUNK = 64
GDN_DK = 128
GDN_DV = 128
GDN_HEADS = 8
GDN_CONV = 4
GDN_CHUNK = 64
GDN_GROUP = 4
RPE_BUCKETS = 32
RPE_MAX_DIST = 2048
RPE_TILES = 8
N_EXPERTS = 64
TOP_K = 6
N_GROUPS = 8
TOPK_GROUPS = 4
D_EXPERT = 256
ROUTED_SCALE = 2.5
DEEPNORM_ALPHA = float((2 * DEPTH) ** 0.25)
LN_EPS = 1e-5
NORM_EPS = 1e-6

MOBA_W = MOBA_HEADS * MOBA_HEAD_DIM
GLA_QK_W = GLA_HEADS * GLA_DK
GLA_V_W = GLA_HEADS * GLA_DV
GDN_W = GDN_HEADS * GDN_DK


def _cparams(sem):
    return pltpu.CompilerParams(dimension_semantics=sem, vmem_limit_bytes=VMEM_LIMIT)


def _sigmoid(x):
    return 1.0 / (1.0 + jnp.exp(-x))


def _silu(x):
    return x * _sigmoid(x)


def _softplus(x):
    return jnp.maximum(x, 0.0) + jnp.log(1.0 + jnp.exp(-jnp.abs(x)))


def _layer_norm(z, g, b):
    mu = jnp.mean(z, axis=-1, keepdims=True)
    zc = z - mu
    var = jnp.mean(zc * zc, axis=-1, keepdims=True)
    return zc * lax.rsqrt(var + LN_EPS) * g + b


def _ada_kernel(c_ref, w_ref, b_ref, o_ref):
    ca = _silu(c_ref[...])
    o_ref[0] = jnp.dot(ca, w_ref[0], precision=HI, preferred_element_type=F32) + b_ref[0]


def _ada(c, ada_w, ada_b):
    depth, d, n = ada_w.shape
    bsz = c.shape[0]
    tn = 6 * LANES
    return pl.pallas_call(
        _ada_kernel,
        grid=(depth, n // tn),
        in_specs=[pl.BlockSpec((bsz, d), lambda l, j: (0, 0)),
                  pl.BlockSpec((1, d, tn), lambda l, j: (l, 0, j)),
                  pl.BlockSpec((1, 1, tn), lambda l, j: (l, 0, j))],
        out_specs=pl.BlockSpec((1, bsz, tn), lambda l, j: (l, 0, j)),
        out_shape=jax.ShapeDtypeStruct((depth, bsz, n), F32),
        compiler_params=_cparams(("arbitrary", "arbitrary")),
        name="ada_mod",
    )(c, ada_w, ada_b.reshape(depth, 1, n))


def _modmm_kernel(x_ref, sc_ref, sh_ref, w_ref, o_ref, h_scr):
    @pl.when(pl.program_id(2) == 0)
    def _():
        h_scr[...] = (x_ref[0] * (1.0 + sc_ref[0]) + sh_ref[0]).astype(BF16)

    o_ref[0] = jnp.dot(h_scr[...], w_ref[...], preferred_element_type=F32).astype(o_ref.dtype)


def _col_tile(n, cap):
    best = LANES
    for t in range(LANES, cap + 1, LANES):
        if n % t == 0:
            best = t
    return best


def _mod_matmul(x, sc, sh, w, tm=1024, tn_cap=768):
    bsz, s, d = x.shape
    n = w.shape[1]
    tm = min(tm, s)
    tn = _col_tile(n, tn_cap)
    return pl.pallas_call(
        _modmm_kernel,
        grid=(bsz, s // tm, n // tn),
        in_specs=[pl.BlockSpec((1, tm, d), lambda b, i, j: (b, i, 0)),
                  pl.BlockSpec((1, 1, d), lambda b, i, j: (b, 0, 0)),
                  pl.BlockSpec((1, 1, d), lambda b, i, j: (b, 0, 0)),
                  pl.BlockSpec((d, tn), lambda b, i, j: (0, j))],
        out_specs=pl.BlockSpec((1, tm, tn), lambda b, i, j: (b, i, j)),
        out_shape=jax.ShapeDtypeStruct((bsz, s, n), F32),
        scratch_shapes=[pltpu.VMEM((tm, d), BF16)],
        compiler_params=_cparams(("arbitrary", "arbitrary", "arbitrary")),
        name="mod_matmul",
    )(x, sc, sh, w)


def _rpe_lower_bounds():
    exact = RPE_BUCKETS // 2
    d = np.arange(0, 2 * RPE_MAX_DIST, dtype=np.int64)
    logd = np.log(np.maximum(d, 1).astype(np.float64) / exact)
    large = exact + (logd / math.log(RPE_MAX_DIST / exact) * (RPE_BUCKETS - exact)).astype(np.int64)
    large = np.minimum(large, RPE_BUCKETS - 1)
    bucket = np.where(d < exact, d, large)
    return [int(np.argmax(bucket >= k)) for k in range(RPE_BUCKETS)]


def _rpe_tiles_kernel(lo, rpe_ref, o_ref):
    h = pl.program_id(0)
    j = pl.program_id(1)
    blk = o_ref.shape[-1]
    key = lax.broadcasted_iota(jnp.int32, (blk, blk), 0)
    qry = lax.broadcasted_iota(jnp.int32, (blk, blk), 1)
    dist = j * blk + qry - key
    val = jnp.full((blk, blk), rpe_ref[0, h], F32)
    for k in range(1, RPE_BUCKETS):
        val = jnp.where(dist >= lo[k], rpe_ref[k, h], val)
    o_ref[0, 0] = jnp.where(dist >= 0, val, NEG)


def _rpe_tiles(rpe_bias):
    heads = rpe_bias.shape[1]
    return pl.pallas_call(
        functools.partial(_rpe_tiles_kernel, _rpe_lower_bounds()),
        grid=(heads, RPE_TILES),
        in_specs=[pl.BlockSpec(memory_space=pltpu.SMEM)],
        out_specs=pl.BlockSpec((1, 1, MOBA_BLOCK, MOBA_BLOCK), lambda h, j: (h, j, 0, 0)),
        out_shape=jax.ShapeDtypeStruct((heads, RPE_TILES, MOBA_BLOCK, MOBA_BLOCK), F32),
        compiler_params=_cparams(("arbitrary", "arbitrary")),
        name="rpe_tiles",
    )(rpe_bias)


def _moba_kernel(q_ref, k_ref, v_ref, t_ref, o_ref, kb_scr, vt_scr, km_scr, sel_scr):
    i = pl.program_id(2)
    nkb, blk, dh = kb_scr.shape

    @pl.when(i == 0)
    def _():
        for n in range(nkb):
            kn = k_ref[0, n * blk:(n + 1) * blk, :]
            kb_scr[n] = kn.astype(BF16)
            km_scr[n:n + 1, :] = jnp.mean(kn, axis=0, keepdims=True)
            vt_scr[n] = v_ref[0, n * blk:(n + 1) * blk, :].T.astype(BF16)

    q = q_ref[0]
    gate = lax.dot_general(km_scr[...], q, NT, precision=HI, preferred_element_type=F32)
    bidx = lax.broadcasted_iota(jnp.int32, gate.shape, 0)
    past = bidx < i
    g = jnp.where(past, gate, -jnp.inf)
    sel = None
    for _ in range(MOBA_TOPK):
        m = jnp.max(g, axis=0, keepdims=True)
        first = jnp.min(jnp.where(g == m, bidx, nkb), axis=0, keepdims=True)
        hit = bidx == first
        sel = hit if sel is None else jnp.logical_or(sel, hit)
        g = jnp.where(hit, -jnp.inf, g)
    sel_scr[...] = jnp.where(jnp.logical_and(sel, past), 0.0, NEG)

    qs = (q * dh ** -0.5).astype(BF16)

    s = lax.dot_general(kb_scr[i], qs, NT, preferred_element_type=F32) + t_ref[0, 0]
    m0 = jnp.max(s, axis=0, keepdims=True)
    p = jnp.exp(s - m0)
    l0 = jnp.sum(p, axis=0, keepdims=True)
    acc0 = jnp.dot(vt_scr[i], p.astype(BF16), preferred_element_type=F32)

    def body(g, carry):
        m, l, acc = carry
        ns = [jnp.minimum(g * MOBA_GROUP + u, nkb - 1) for u in range(MOBA_GROUP)]
        ss = []
        for n in ns:
            j = jnp.clip(i - n, 0, RPE_TILES - 1)
            ss.append(lax.dot_general(kb_scr[n], qs, NT, preferred_element_type=F32)
                      + t_ref[0, j] + sel_scr[pl.ds(n, 1), :])
        m_new = m
        for s in ss:
            m_new = jnp.maximum(m_new, jnp.max(s, axis=0, keepdims=True))
        alpha = jnp.exp(m - m_new)
        l = alpha * l
        acc = alpha * acc
        for n, s in zip(ns, ss):
            p = jnp.exp(s - m_new)
            l = l + jnp.sum(p, axis=0, keepdims=True)
            acc = acc + jnp.dot(vt_scr[n], p.astype(BF16), preferred_element_type=F32)
        return m_new, l, acc

    _, l, acc = lax.fori_loop(0, (i + MOBA_GROUP - 1) // MOBA_GROUP, body, (m0, l0, acc0))
    o_ref[0] = (acc / l).T


def _moba(proj, tiles, q_col, k_col, v_col):
    bsz, s, _ = proj.shape
    dh, blk, heads = MOBA_HEAD_DIM, MOBA_BLOCK, MOBA_HEADS
    nkb = s // blk
    return pl.pallas_call(
        _moba_kernel,
        grid=(bsz, heads, nkb),
        in_specs=[pl.BlockSpec((1, blk, dh), lambda b, h, i: (b, i, q_col + h)),
                  pl.BlockSpec((1, s, dh), lambda b, h, i: (b, 0, k_col + h)),
                  pl.BlockSpec((1, s, dh), lambda b, h, i: (b, 0, v_col + h)),
                  pl.BlockSpec((1, RPE_TILES, blk, blk), lambda b, h, i: (h, 0, 0, 0))],
        out_specs=pl.BlockSpec((1, blk, dh), lambda b, h, i: (b, i, h)),
        out_shape=jax.ShapeDtypeStruct((bsz, s, heads * dh), F32),
        scratch_shapes=[pltpu.VMEM((nkb, blk, dh), BF16),
                        pltpu.VMEM((nkb, dh, blk), BF16),
                        pltpu.VMEM((nkb, dh), F32),
                        pltpu.VMEM((nkb, blk), F32)],
        compiler_params=_cparams(("arbitrary", "arbitrary", "arbitrary")),
        name="moba_attention",
    )(proj, proj, proj, tiles)


def _gla_kernel(q_ref, k_ref, v_ref, gg_ref, glr_ref, w2_ref, gb_ref, nw_ref, o_ref, st_scr):
    @pl.when(pl.program_id(1) == 0)
    def _():
        st_scr[...] = jnp.zeros(st_scr.shape, F32)

    tg = q_ref.shape[1]
    c = GLA_CHUNK
    x = jnp.dot(glr_ref[0], w2_ref[...], precision=HI, preferred_element_type=F32) + gb_ref[...]
    lg = -_softplus(-x) * (1.0 / GLA_GATE_NORM)
    row = lax.broadcasted_iota(jnp.int32, (c, c), 0)
    col = lax.broadcasted_iota(jnp.int32, (c, c), 1)
    incl = row >= col
    tri = incl.astype(F32)
    nw = nw_ref[...]
    for ci in range(tg // c):
        rows = slice(ci * c, (ci + 1) * c)
        b = jnp.dot(tri, lg[rows], precision=HI, preferred_element_type=F32)
        bl = b[c - 1:c, :]
        q = q_ref[0, rows, :] * GLA_DK ** -0.5
        k = k_ref[0, rows, :]
        q_e = (q * jnp.exp(b)).astype(BF16)
        k_e = (k * jnp.exp(-b)).astype(BF16)
        k_end = (k * jnp.exp(bl - b)).astype(BF16)
        d = jnp.exp(bl)
        for h in range(GLA_HEADS):
            ks = slice(h * GLA_DK, (h + 1) * GLA_DK)
            vs = slice(h * GLA_DV, (h + 1) * GLA_DV)
            vh = v_ref[0, rows, vs].astype(BF16)
            a = lax.dot_general(q_e[:, ks], k_e[:, ks], NT, preferred_element_type=F32)
            a = jnp.where(incl, a, 0.0).astype(BF16)
            st = st_scr[h]
            o = (jnp.dot(a, vh, preferred_element_type=F32)
                 + lax.dot_general(q_e[:, ks], st.astype(BF16), NT, preferred_element_type=F32))
            st_scr[h] = st * d[:, ks] + lax.dot_general(vh, k_end[:, ks], TN, preferred_element_type=F32)
            o = o * lax.rsqrt(jnp.mean(o * o, axis=-1, keepdims=True) + NORM_EPS)
            o_ref[0, rows, vs] = o * nw * _silu(gg_ref[0, rows, vs])


def _gla(proj, gk_w2p, gk_b, o_norm, q_col, k_col, v_col, g_col, r_col, tg=256):
    bsz, s, _ = proj.shape
    tg = min(tg, s)
    qk, vw = GLA_QK_W, GLA_V_W
    return pl.pallas_call(
        _gla_kernel,
        grid=(bsz, s // tg),
        in_specs=[pl.BlockSpec((1, tg, qk), lambda b, t: (b, t, q_col)),
                  pl.BlockSpec((1, tg, qk), lambda b, t: (b, t, k_col)),
                  pl.BlockSpec((1, tg, vw), lambda b, t: (b, t, v_col)),
                  pl.BlockSpec((1, tg, vw), lambda b, t: (b, t, g_col)),
                  pl.BlockSpec((1, tg, LANES), lambda b, t: (b, t, r_col)),
                  pl.BlockSpec((LANES, qk), lambda b, t: (0, 0)),
                  pl.BlockSpec((1, qk), lambda b, t: (0, 0)),
                  pl.BlockSpec((1, GLA_DV), lambda b, t: (0, 0))],
        out_specs=pl.BlockSpec((1, tg, vw), lambda b, t: (b, t, 0)),
        out_shape=jax.ShapeDtypeStruct((bsz, s, vw), F32),
        scratch_shapes=[pltpu.VMEM((GLA_HEADS, GLA_DV, GLA_DK), F32)],
        compiler_params=_cparams(("arbitrary", "arbitrary")),
        name="gla_mixer",
    )(proj, proj, proj, proj, proj, gk_w2p, gk_b, o_norm)


def _gdn_kernel(qkv_ref, gate_ref, ba_ref, cw_ref, pv_ref, nw_ref, o_ref, tail_scr, s_scr):
    @pl.when(pl.program_id(1) == 0)
    def _():
        tail_scr[...] = jnp.zeros(tail_scr.shape, F32)
        s_scr[...] = jnp.zeros(s_scr.shape, F32)

    tg = qkv_ref.shape[1]
    c, dk, grp = GDN_CHUNK, GDN_DK, GDN_GROUP
    gr = grp * c
    w = GDN_W

    x = qkv_ref[0]
    tail = tail_scr[...]
    tail_scr[...] = x[tg - 8:, :]
    r8 = lax.broadcasted_iota(jnp.int32, (8, 1), 0)
    y = x * cw_ref[GDN_CONV - 1:GDN_CONV, :]
    for sft in range(1, GDN_CONV):
        xs = pltpu.roll(x, sft, axis=0)
        head = jnp.where(r8 < sft, pltpu.roll(tail, sft, axis=0), xs[:8, :])
        xs = jnp.concatenate([head, xs[8:, :]], axis=0)
        y = y + xs * cw_ref[GDN_CONV - 1 - sft:GDN_CONV - sft, :]
    y = _silu(y)

    ba = ba_ref[0]
    beta_t = _sigmoid(ba)
    g_t = -jnp.exp(pv_ref[0:1, :]) * _softplus(ba + pv_ref[1:2, :])

    row = lax.broadcasted_iota(jnp.int32, (c, c), 0)
    col = lax.broadcasted_iota(jnp.int32, (c, c), 1)
    tri = (row >= col).astype(F32)
    rr = lax.broadcasted_iota(jnp.int32, (gr, gr), 0)
    cc = lax.broadcasted_iota(jnp.int32, (gr, gr), 1)
    same = (rr // c) == (cc // c)
    incl = jnp.logical_and(same, rr >= cc)
    strict = jnp.logical_and(same, rr > cc)
    eye = (rr == cc).astype(F32)
    nw = nw_ref[...]

    for ci in range(tg // c):
        rows = slice(ci * c, (ci + 1) * c)
        gcum = jnp.dot(tri, g_t[rows], precision=HI, preferred_element_type=F32)
        gcum_t = gcum.T
        for gi in range(GDN_HEADS // grp):
            hs = [gi * grp + u for u in range(grp)]

            def stack(a, off):
                return jnp.concatenate([a[rows, off + h * dk: off + (h + 1) * dk] for h in hs], axis=0)

            q = stack(y, 0)
            k = stack(y, w)
            v = stack(y, 2 * w)
            q = q * lax.rsqrt(jnp.sum(q * q, axis=-1, keepdims=True) + NORM_EPS) * dk ** -0.5
            k = k * lax.rsqrt(jnp.sum(k * k, axis=-1, keepdims=True) + NORM_EPS)
            beta = jnp.concatenate([beta_t[rows, h:h + 1] for h in hs], axis=0)
            gc = jnp.concatenate([gcum[:, GDN_HEADS + h:GDN_HEADS + h + 1] for h in hs], axis=0)
            gc_row = jnp.concatenate([gcum_t[GDN_HEADS + h:GDN_HEADS + h + 1, :] for h in hs], axis=1)
            gl = jnp.concatenate([jnp.broadcast_to(gcum[c - 1:c, GDN_HEADS + h:GDN_HEADS + h + 1], (c, 1))
                                  for h in hs], axis=0)

            decay = jnp.where(incl, jnp.exp(jnp.where(incl, gc - gc_row, 0.0)), 0.0)
            kb = k * beta
            vb = v * beta
            k16 = k.astype(BF16)
            a = lax.dot_general(kb.astype(BF16), k16, NT, preferred_element_type=F32)
            a = jnp.where(strict, a * decay, 0.0)
            t_mat = eye - a
            pw = a
            for _ in range(int(math.log2(c)) - 1):
                pw16 = pw.astype(BF16)
                pw = jnp.dot(pw16, pw16, preferred_element_type=F32)
                t_mat = t_mat + jnp.dot(t_mat.astype(BF16), pw.astype(BF16), preferred_element_type=F32)
            eg = jnp.exp(gc)
            rhs = jnp.concatenate([vb, kb * eg], axis=1).astype(BF16)
            wk = jnp.dot(t_mat.astype(BF16), rhs, preferred_element_type=F32)
            w_val = wk[:, :GDN_DV]
            k_cum = wk[:, GDN_DV:].astype(BF16)
            attn = lax.dot_general(q.astype(BF16), k16, NT, preferred_element_type=F32) * decay
            q_g = (q * eg).astype(BF16)
            k_end = (k * jnp.exp(gl - gc)).astype(BF16)

            v_new = []
            for u, h in enumerate(hs):
                hr = slice(u * c, (u + 1) * c)
                st16 = s_scr[h].astype(BF16)
                v_new.append(w_val[hr] - jnp.dot(k_cum[hr], st16, preferred_element_type=F32))
            v_new = jnp.concatenate(v_new, axis=0)
            v16 = v_new.astype(BF16)
            o_intra = jnp.dot(attn.astype(BF16), v16, preferred_element_type=F32)
            for u, h in enumerate(hs):
                hr = slice(u * c, (u + 1) * c)
                st = s_scr[h]
                o = o_intra[hr] + jnp.dot(q_g[hr], st.astype(BF16), preferred_element_type=F32)
                d_last = jnp.exp(gcum[c - 1:c, GDN_HEADS + h:GDN_HEADS + h + 1])
                s_scr[h] = st * d_last + lax.dot_general(k_end[hr], v16[hr], TN, preferred_element_type=F32)
                o = o * lax.rsqrt(jnp.mean(o * o, axis=-1, keepdims=True) + NORM_EPS)
                cs = slice(h * GDN_DV, (h + 1) * GDN_DV)
                o_ref[0, rows, cs] = o * nw * _silu(gate_ref[0, rows, cs])


def _gdn(proj, conv_w, pvec, o_norm, tg=256):
    bsz, s, _ = proj.shape
    tg = min(tg, s)
    w = GDN_W
    return pl.pallas_call(
        _gdn_kernel,
        grid=(bsz, s // tg),
        in_specs=[pl.BlockSpec((1, tg, 3 * w), lambda b, t: (b, t, 0)),
                  pl.BlockSpec((1, tg, w), lambda b, t: (b, t, 3)),
                  pl.BlockSpec((1, tg, LANES), lambda b, t: (b, t, 4 * w // LANES)),
                  pl.BlockSpec((GDN_CONV, 3 * w), lambda b, t: (0, 0)),
                  pl.BlockSpec((2, LANES), lambda b, t: (0, 0)),
                  pl.BlockSpec((1, GDN_DV), lambda b, t: (0, 0))],
        out_specs=pl.BlockSpec((1, tg, w), lambda b, t: (b, t, 0)),
        out_shape=jax.ShapeDtypeStruct((bsz, s, w), F32),
        scratch_shapes=[pltpu.VMEM((8, 3 * w), F32),
                        pltpu.VMEM((GDN_HEADS, GDN_DK, GDN_DV), F32)],
        compiler_params=_cparams(("arbitrary", "arbitrary")),
        name="gdn_mixer",
    )(proj, proj, proj, conv_w, pvec, o_norm)


def _mixout_kernel(n_act, *refs):
    acts = refs[:n_act]
    ws = refs[n_act:2 * n_act]
    x_ref, g1_ref, lng_ref, lnb_ref, sc_ref, sh_ref, rw_ref = refs[2 * n_act:2 * n_act + 7]
    xo_ref, h_ref, rl_ref = refs[2 * n_act + 7:]
    y = None
    for a_ref, w_ref in zip(acts, ws):
        t = jnp.dot(a_ref[0].astype(BF16), w_ref[...], preferred_element_type=F32)
        y = t if y is None else y + t
    xn = _layer_norm(DEEPNORM_ALPHA * x_ref[0] + g1_ref[0] * y, lng_ref[...], lnb_ref[...])
    xo_ref[0] = xn
    h = xn * (1.0 + sc_ref[0]) + sh_ref[0]
    h_ref[0] = h.astype(BF16)
    rl_ref[0] = lax.dot_general(rw_ref[...], h, NT, precision=HI, preferred_element_type=F32)


def _mixout(acts, ws, x, g1, ln_g, ln_b, sc2, sh2, router_wt, tm=512):
    bsz, s, d = x.shape
    tm = min(tm, s)
    n_act = len(acts)
    ne = router_wt.shape[0]
    vec = pl.BlockSpec((1, 1, d), lambda b, i: (b, 0, 0))
    par = pl.BlockSpec((1, d), lambda b, i: (0, 0))
    in_specs = ([pl.BlockSpec((1, tm, a.shape[-1]), lambda b, i: (b, i, 0)) for a in acts]
                + [pl.BlockSpec(w.shape, lambda b, i: (0, 0)) for w in ws]
                + [pl.BlockSpec((1, tm, d), lambda b, i: (b, i, 0)), vec, par, par, vec, vec,
                   pl.BlockSpec((ne, d), lambda b, i: (0, 0))])
    return pl.pallas_call(
        functools.partial(_mixout_kernel, n_act),
        grid=(bsz, s // tm),
        in_specs=in_specs,
        out_specs=[pl.BlockSpec((1, tm, d), lambda b, i: (b, i, 0)),
                   pl.BlockSpec((1, tm, d), lambda b, i: (b, i, 0)),
                   pl.BlockSpec((1, ne, tm), lambda b, i: (b, 0, i))],
        out_shape=[jax.ShapeDtypeStruct((bsz, s, d), F32),
                   jax.ShapeDtypeStruct((bsz, s, d), BF16),
                   jax.ShapeDtypeStruct((bsz, ne, s), F32)],
        compiler_params=_cparams(("arbitrary", "arbitrary")),
        name="mix_out",
    )(*acts, *ws, x, g1, ln_g, ln_b, sc2, sh2, router_wt)


def _first_max(vals, idx, axis, sentinel):
    m = jnp.max(vals, axis=axis, keepdims=True)
    first = jnp.min(jnp.where(vals == m, idx, sentinel), axis=axis, keepdims=True)
    return m, idx == first


def _router_kernel(rl_ref, rb_ref, g_ref, gt_ref, cnt_ref):
    ne, tn = rl_ref.shape[1], rl_ref.shape[2]
    gsz = ne // N_GROUPS
    scores = _sigmoid(rl_ref[0])
    sel = scores + rb_ref[...]
    ridx = lax.broadcasted_iota(jnp.int32, (gsz, tn), 0)
    gidx = lax.broadcasted_iota(jnp.int32, (N_GROUPS, tn), 0)
    gs = jnp.zeros((N_GROUPS, tn), F32)
    for g in range(N_GROUPS):
        sg = sel[g * gsz:(g + 1) * gsz, :]
        m1, hit = _first_max(sg, ridx, 0, gsz)
        m2 = jnp.max(jnp.where(hit, -jnp.inf, sg), axis=0, keepdims=True)
        gs = jnp.where(gidx == g, m1 + m2, gs)
    gsel = None
    for _ in range(TOPK_GROUPS):
        _, hit = _first_max(gs, gidx, 0, N_GROUPS)
        gsel = hit if gsel is None else jnp.logical_or(gsel, hit)
        gs = jnp.where(hit, -jnp.inf, gs)
    gself = gsel.astype(F32)
    emask = jnp.concatenate([jnp.broadcast_to(gself[g:g + 1, :], (gsz, tn)) for g in range(N_GROUPS)], axis=0)
    cand = jnp.where(emask > 0.5, sel, -jnp.inf)
    eidx = lax.broadcasted_iota(jnp.int32, cand.shape, 0)
    chosen = None
    for _ in range(TOP_K):
        _, hit = _first_max(cand, eidx, 0, ne)
        chosen = hit if chosen is None else jnp.logical_or(chosen, hit)
        cand = jnp.where(hit, -jnp.inf, cand)
    wsel = jnp.where(chosen, scores, 0.0)
    tot = jnp.sum(wsel, axis=0, keepdims=True)
    gates = wsel / (tot + 1e-20) * ROUTED_SCALE
    g_ref[0] = gates.T
    gt_ref[0] = gates
    tile_of = lax.broadcasted_iota(jnp.int32, (tn, LANES), 0) // MOE_TM
    ind = (tile_of == lax.broadcasted_iota(jnp.int32, (tn, LANES), 1)).astype(BF16)
    routed = jnp.where(gates > 0.0, 1.0, 0.0).astype(BF16)
    cnt_ref[0, 0] = jnp.dot(routed, ind, preferred_element_type=F32)


def _router(rl, router_b, tn=1024):
    bsz, ne, s = rl.shape
    tn = min(tn, s)
    return pl.pallas_call(
        _router_kernel,
        grid=(bsz, s // tn),
        in_specs=[pl.BlockSpec((1, ne, tn), lambda b, i: (b, 0, i)),
                  pl.BlockSpec((ne, 1), lambda b, i: (0, 0))],
        out_specs=[pl.BlockSpec((1, tn, ne), lambda b, i: (b, i, 0)),
                   pl.BlockSpec((1, ne, tn), lambda b, i: (b, 0, i)),
                   pl.BlockSpec((1, 1, ne, LANES), lambda b, i: (b, i, 0, 0))],
        out_shape=[jax.ShapeDtypeStruct((bsz, s, ne), F32),
                   jax.ShapeDtypeStruct((bsz, ne, s), F32),
                   jax.ShapeDtypeStruct((bsz, s // tn, ne, LANES), F32)],
        compiler_params=_cparams(("arbitrary", "arbitrary")),
        name="moe_router",
    )(rl, router_b.reshape(ne, 1))


MOE_TM = 256
MOE_ALIGN = 16
MOE_R = 512
MOE_LC = 512
MOE_LMAX = -(-(MOE_TM * TOP_K + N_EXPERTS * (MOE_ALIGN - 1)) // MOE_LC) * MOE_LC
MOE_NP = MOE_LMAX // MOE_ALIGN
POS_SPLIT = 256.0


def _ffn(x, wg, wu, wd):
    a = jnp.dot(x, wg, preferred_element_type=F32)
    u = jnp.dot(x, wu, preferred_element_type=F32)
    return jnp.dot((_silu(a) * u).astype(BF16), wd, preferred_element_type=F32)


def _split_pos(pos, axis):
    hi = jnp.floor(pos * (1.0 / POS_SPLIT)) * POS_SPLIT
    return jnp.concatenate([hi, pos - hi], axis=axis).astype(BF16)


def _for_each_piece(n_pieces, fn):
    def body(p, carry):
        fn(p)
        return carry

    lax.fori_loop(0, n_pieces, body, 0)


def _dispatch_kernel(dst_s, np_s, h_ref, gt_ref, offc_ref, offr_ref, cntr_ref, xs_hbm, sorted_scr, sems):
    i = pl.program_id(0)
    last = pl.num_programs(0) - 1
    slot = i % 2
    tm = h_ref.shape[0]
    routed = gt_ref[0] > 0.0
    t0 = lax.broadcasted_iota(jnp.int32, (tm, tm), 0)
    t1 = lax.broadcasted_iota(jnp.int32, (tm, tm), 1)
    earlier = jnp.where(t0 < t1, 1.0, 0.0).astype(BF16)
    rank_t = jnp.dot(jnp.where(routed, 1.0, 0.0).astype(BF16), earlier, preferred_element_type=F32)
    pos_t = jnp.where(routed, rank_t + offc_ref[0] + 1.0, 0.0)
    pos2 = _split_pos(pos_t, 0)
    x = h_ref[...]
    offr = offr_ref[0]
    endr = offr + cntr_ref[0]
    for c in range(MOE_LMAX // MOE_LC):
        r = (c * MOE_LC + lax.broadcasted_iota(jnp.int32, (MOE_LC, 1), 0)).astype(F32)
        owner = jnp.where(jnp.logical_and(r >= offr, r < endr), 1.0, 0.0).astype(BF16)
        possel = jnp.dot(jnp.concatenate([owner, owner], axis=1), pos2, preferred_element_type=F32)
        perm = jnp.where(possel == r + 1.0, 1.0, 0.0).astype(BF16)
        sorted_scr[slot, c * MOE_LC:(c + 1) * MOE_LC, :] = (
            jnp.dot(perm, x, preferred_element_type=F32).astype(BF16))

    def piece(sl, tile, p):
        src = sorted_scr.at[sl, pl.ds(pl.multiple_of(p * MOE_ALIGN, MOE_ALIGN), MOE_ALIGN)]
        dst = xs_hbm.at[pl.ds(pl.multiple_of(dst_s[tile * MOE_NP + p], MOE_ALIGN), MOE_ALIGN)]
        return pltpu.make_async_copy(src, dst, sems.at[sl])

    _for_each_piece(np_s[i], lambda p: piece(slot, i, p).start())

    @pl.when(i > 0)
    def _():
        _for_each_piece(np_s[i - 1], lambda p: piece(1 - slot, i - 1, p).wait())

    @pl.when(i == last)
    def _():
        _for_each_piece(np_s[i], lambda p: piece(slot, i, p).wait())


def _expert_ffn_kernel(be_s, nu_s, x_ref, wg_ref, wu_ref, wd_ref, y_ref):
    @pl.when(pl.program_id(0) < nu_s[0])
    def _():
        y_ref[...] = _ffn(x_ref[...], wg_ref[0], wu_ref[0], wd_ref[0]).astype(BF16)


def _combine_kernel(dst_s, np_s, ys_hbm, g_ref, h_ref, offr_ref, offc_ref, cntc_ref,
                    sg_ref, su_ref, sd_ref, x_ref, g2_ref, lng_ref, lnb_ref, o_ref, ys_scr, sems):
    i = pl.program_id(0)
    last = pl.num_programs(0) - 1
    slot = i % 2
    tm = h_ref.shape[0]

    def piece(sl, tile, p):
        src = ys_hbm.at[pl.ds(pl.multiple_of(dst_s[tile * MOE_NP + p], MOE_ALIGN), MOE_ALIGN)]
        dst = ys_scr.at[sl, pl.ds(pl.multiple_of(p * MOE_ALIGN, MOE_ALIGN), MOE_ALIGN)]
        return pltpu.make_async_copy(src, dst, sems.at[sl])

    @pl.when(i == 0)
    def _():
        ys_scr[...] = jnp.zeros(ys_scr.shape, BF16)
        _for_each_piece(np_s[0], lambda p: piece(0, 0, p).start())

    @pl.when(i < last)
    def _():
        _for_each_piece(np_s[i + 1], lambda p: piece(1 - slot, i + 1, p).start())

    acc = _ffn(h_ref[...], sg_ref[...], su_ref[...], sd_ref[...])
    g = g_ref[...]
    routed = g > 0.0
    t0 = lax.broadcasted_iota(jnp.int32, (tm, tm), 0)
    t1 = lax.broadcasted_iota(jnp.int32, (tm, tm), 1)
    earlier = jnp.where(t0 > t1, 1.0, 0.0).astype(BF16)
    rank = jnp.dot(earlier, jnp.where(routed, 1.0, 0.0).astype(BF16), preferred_element_type=F32)
    pos = jnp.where(routed, rank + offr_ref[0] + 1.0, 0.0)
    pos2 = _split_pos(pos, 1)
    g16 = g.astype(BF16)
    offc = offc_ref[0]
    endc = offc + cntc_ref[0]
    _for_each_piece(np_s[i], lambda p: piece(slot, i, p).wait())
    for c in range(MOE_LMAX // MOE_LC):
        r = (c * MOE_LC + lax.broadcasted_iota(jnp.int32, (1, MOE_LC), 1)).astype(F32)
        owner = jnp.where(jnp.logical_and(r >= offc, r < endc), 1.0, 0.0).astype(BF16)
        possel = jnp.dot(pos2, jnp.concatenate([owner, owner], axis=0), preferred_element_type=F32)
        gsel = jnp.dot(g16, owner, preferred_element_type=F32)
        w = jnp.where(possel == r + 1.0, gsel, 0.0).astype(BF16)
        acc = acc + jnp.dot(w, ys_scr[slot, c * MOE_LC:(c + 1) * MOE_LC, :], preferred_element_type=F32)
    z = DEEPNORM_ALPHA * x_ref[...] + g2_ref[0] * acc
    o_ref[...] = _layer_norm(z, lng_ref[...], lnb_ref[...])


def _ceil_to(v, m):
    return jnp.floor((v + (m - 1.0)) * (1.0 / m)) * m


def _moe_layout_kernel(cnt_ref, dst_ref, np_ref, off_ref, cntp_ref, be_ref, nu_ref):
    cnt = cnt_ref[...]
    ntiles, ne = cnt.shape
    cntp = _ceil_to(cnt, MOE_ALIGN)
    e0 = lax.broadcasted_iota(jnp.int32, (ne, ne), 0)
    e1 = lax.broadcasted_iota(jnp.int32, (ne, ne), 1)
    i0 = lax.broadcasted_iota(jnp.int32, (ntiles, ntiles), 0)
    i1 = lax.broadcasted_iota(jnp.int32, (ntiles, ntiles), 1)

    def mm(a, b):
        return jnp.dot(a, b, precision=HI, preferred_element_type=F32)

    off = mm(cntp, jnp.where(e0 < e1, 1.0, 0.0))
    before = mm(jnp.where(i0 > i1, 1.0, 0.0), cntp)
    tot = jnp.broadcast_to(jnp.sum(cntp, axis=0, keepdims=True), (8, ne))
    totr = _ceil_to(tot, MOE_R)
    base = mm(totr, jnp.where(e0 < e1, 1.0, 0.0))[0:1, :]
    delta = base + before - off
    step = mm(delta, jnp.where(e0 == e1, 1.0, 0.0) - jnp.where(e0 + 1 == e1, 1.0, 0.0))
    rowp = (lax.broadcasted_iota(jnp.int32, (1, MOE_NP), 1) * MOE_ALIGN).astype(F32)
    dst = jnp.broadcast_to(rowp, (ntiles, MOE_NP))
    for e in range(ne):
        dst = dst + jnp.where(off[:, e:e + 1] <= rowp, step[:, e:e + 1], 0.0)
    dst_ref[...] = dst.astype(jnp.int32)
    pieces = jnp.sum(cntp, axis=1, keepdims=True) * (1.0 / MOE_ALIGN)
    np_ref[...] = jnp.broadcast_to(pieces, np_ref.shape).astype(jnp.int32)
    off_ref[...] = off
    cntp_ref[...] = cntp
    tot_c = lax.dot_general(cntp, jnp.ones((ntiles, LANES), F32), TN, precision=HI, preferred_element_type=F32)
    end_c = mm(jnp.where(e0 >= e1, 1.0, 0.0), _ceil_to(tot_c, MOE_R))[:, 0:1]
    total = end_c[ne - 1:ne, :]
    first = jnp.minimum((lax.broadcasted_iota(jnp.int32, be_ref.shape, 1) * MOE_R).astype(F32), total - 1.0)
    be = jnp.sum(jnp.where(end_c <= first, 1.0, 0.0), axis=0, keepdims=True)
    be_ref[...] = jnp.minimum(be, ne - 1.0).astype(jnp.int32)
    nu_ref[...] = jnp.broadcast_to(total * (1.0 / MOE_R), nu_ref.shape).astype(jnp.int32)


def _moe_layout(cnt_raw, tn):
    ne = N_EXPERTS
    nsub = tn // MOE_TM
    cnt = jnp.transpose(cnt_raw[..., :nsub], (0, 1, 3, 2)).reshape(-1, ne)
    ntiles = cnt.shape[0]
    nblk = -(-(ntiles * (MOE_TM * TOP_K + ne * (MOE_ALIGN - 1)) + ne * (MOE_R - 1)) // MOE_R)
    nblk_pad = -(-nblk // LANES) * LANES
    dst, npieces, off, cntp, blk_exp, nused = pl.pallas_call(
        _moe_layout_kernel,
        out_shape=[jax.ShapeDtypeStruct((ntiles, MOE_NP), jnp.int32),
                   jax.ShapeDtypeStruct((ntiles, LANES), jnp.int32),
                   jax.ShapeDtypeStruct((ntiles, ne), F32),
                   jax.ShapeDtypeStruct((ntiles, ne), F32),
                   jax.ShapeDtypeStruct((1, nblk_pad), jnp.int32),
                   jax.ShapeDtypeStruct((1, LANES), jnp.int32)],
        compiler_params=pltpu.CompilerParams(vmem_limit_bytes=VMEM_LIMIT),
        name="moe_layout",
    )(cnt)
    return dst.reshape(-1), npieces[:, 0], off, cntp, nblk, nused[0, :1], blk_exp[0, :nblk]


def _moe(h2, gates, gates_t, cnt_raw, tn, wg, wu, wd, sg, su, sd, x, g2, ln_g, ln_b):
    bsz, s, d = x.shape
    t = bsz * s
    tm = MOE_TM
    ne, _, de = wg.shape
    per_b = s // tm
    ntiles = t // tm
    dst, npieces, off_f, cnt_f, nblk, nused, blk_exp = _moe_layout(cnt_raw, tn)
    scalars = (dst, npieces)
    row = pl.BlockSpec((1, 1, ne), lambda i, *_: (i, 0, 0))
    col = pl.BlockSpec((1, ne, 1), lambda i, *_: (i, 0, 0))
    tok = pl.BlockSpec((tm, d), lambda i, *_: (i, 0))
    par = pl.BlockSpec((1, d), lambda i, *_: (0, 0))
    rows = nblk * MOE_R

    xs = pl.pallas_call(
        _dispatch_kernel,
        grid_spec=pltpu.PrefetchScalarGridSpec(
            num_scalar_prefetch=2, grid=(ntiles,),
            in_specs=[tok,
                      pl.BlockSpec((1, ne, tm), lambda i, *_: (i // per_b, 0, i % per_b)),
                      col, row, row],
            out_specs=pl.BlockSpec(memory_space=pl.ANY),
            scratch_shapes=[pltpu.VMEM((2, MOE_LMAX, d), BF16), pltpu.SemaphoreType.DMA((2,))]),
        out_shape=jax.ShapeDtypeStruct((rows, d), BF16),
        compiler_params=_cparams(("arbitrary",)),
        name="moe_dispatch",
    )(*scalars, h2.reshape(t, d), gates_t, off_f.reshape(ntiles, ne, 1), off_f.reshape(ntiles, 1, ne),
      cnt_f.reshape(ntiles, 1, ne))

    blk = pl.BlockSpec((MOE_R, d), lambda j, be, nu: (jnp.maximum(jnp.minimum(j, nu[0] - 1), 0), 0))
    ys = pl.pallas_call(
        _expert_ffn_kernel,
        grid_spec=pltpu.PrefetchScalarGridSpec(
            num_scalar_prefetch=2, grid=(nblk,),
            in_specs=[blk,
                      pl.BlockSpec((1, d, de), lambda j, be, nu: (be[j], 0, 0)),
                      pl.BlockSpec((1, d, de), lambda j, be, nu: (be[j], 0, 0)),
                      pl.BlockSpec((1, de, d), lambda j, be, nu: (be[j], 0, 0))],
            out_specs=blk),
        out_shape=jax.ShapeDtypeStruct((rows, d), BF16),
        compiler_params=_cparams(("arbitrary",)),
        name="moe_expert_ffn",
    )(blk_exp, nused, xs, wg, wu, wd)

    out = pl.pallas_call(
        _combine_kernel,
        grid_spec=pltpu.PrefetchScalarGridSpec(
            num_scalar_prefetch=2, grid=(ntiles,),
            in_specs=[pl.BlockSpec(memory_space=pl.ANY),
                      pl.BlockSpec((tm, ne), lambda i, *_: (i, 0)),
                      tok, row, col, col,
                      pl.BlockSpec(sg.shape, lambda i, *_: (0, 0)),
                      pl.BlockSpec(su.shape, lambda i, *_: (0, 0)),
                      pl.BlockSpec(sd.shape, lambda i, *_: (0, 0)),
                      tok,
                      pl.BlockSpec((1, 1, d), lambda i, *_: (i // per_b, 0, 0)),
                      par, par],
            out_specs=tok,
            scratch_shapes=[pltpu.VMEM((2, MOE_LMAX, d), BF16), pltpu.SemaphoreType.DMA((2,))]),
        out_shape=jax.ShapeDtypeStruct((t, d), F32),
        compiler_params=_cparams(("arbitrary",)),
        name="moe_combine",
    )(*scalars, ys, gates.reshape(t, ne), h2.reshape(t, d), off_f.reshape(ntiles, 1, ne),
      off_f.reshape(ntiles, ne, 1), cnt_f.reshape(ntiles, ne, 1), sg, su, sd, x.reshape(t, d), g2, ln_g, ln_b)
    return out.reshape(bsz, s, d)


def _pad_cols(w, n):
    return jnp.pad(w, ((0, 0), (0, n - w.shape[1])))


def kernel(x, c, rpe_bias, ada_w, ada_b, ln_mix_g, ln_mix_b, ln_ffn_g, ln_ffn_b, ev_w_in, ev_gk_w2, ev_gk_b, ev_norm, ev_w_out, od_w_in, od_conv_w, od_a_log, od_dt_bias, od_norm, od_w_out, moe_router_w, moe_router_b, moe_w_gate, moe_w_up, moe_w_down, sh_w_gate, sh_w_up, sh_w_down):
    bsz, s, d = x.shape
    mod = _ada(c, ada_w, ada_b)
    tiles = _rpe_tiles(rpe_bias)

    for layer in range(DEPTH):
        sh1, sc1, g1, sh2, sc2, g2 = [mod[layer, :, u * d:(u + 1) * d].reshape(bsz, 1, d) for u in range(6)]
        i = layer // 2
        if layer % 2 == 0:
            n_main = 3 * MOBA_W + 2 * GLA_QK_W + 2 * GLA_V_W
            w_in = jnp.concatenate([ev_w_in[i][:, :n_main], _pad_cols(ev_w_in[i][:, n_main:], LANES)], axis=1)
            proj = _mod_matmul(x, sc1, sh1, w_in.astype(BF16))
            nb = MOBA_W // LANES
            o_a = _moba(proj, tiles, 0, nb, 2 * nb)
            gk_w2p = jnp.pad(ev_gk_w2[i], ((0, LANES - GLA_GATE_RANK), (0, 0)))
            gla0 = 3 * MOBA_W
            o_b = _gla(proj, gk_w2p, ev_gk_b[i].reshape(1, -1), ev_norm[i].reshape(1, -1),
                       gla0 // GLA_QK_W, gla0 // GLA_QK_W + 1,
                       (gla0 + 2 * GLA_QK_W) // GLA_V_W, (gla0 + 2 * GLA_QK_W) // GLA_V_W + 1,
                       n_main // LANES)
            w_out = ev_w_out[i].astype(BF16)
            acts, ws = [o_a, o_b], [w_out[:MOBA_W], w_out[MOBA_W:]]
        else:
            n_main = 4 * GDN_W
            w_in = jnp.concatenate([od_w_in[i][:, :n_main], _pad_cols(od_w_in[i][:, n_main:], LANES)], axis=1)
            proj = _mod_matmul(x, sc1, sh1, w_in.astype(BF16))
            pvec = jnp.zeros((2, LANES), F32)
            pvec = pvec.at[0, GDN_HEADS:2 * GDN_HEADS].set(od_a_log[i])
            pvec = pvec.at[1, GDN_HEADS:2 * GDN_HEADS].set(od_dt_bias[i])
            o = _gdn(proj, od_conv_w[i], pvec, od_norm[i].reshape(1, -1))
            acts, ws = [o], [od_w_out[i].astype(BF16)]

        x, h2, rl = _mixout(acts, ws, x, g1, ln_mix_g[layer].reshape(1, d), ln_mix_b[layer].reshape(1, d),
                            sc2, sh2, moe_router_w[layer].T)
        router_tn = min(1024, s)
        gates, gates_t, cnt_raw = _router(rl, moe_router_b[layer], router_tn)
        x = _moe(h2, gates, gates_t, cnt_raw, router_tn,
                 moe_w_gate[layer].astype(BF16), moe_w_up[layer].astype(BF16),
                 moe_w_down[layer].astype(BF16), sh_w_gate[layer].astype(BF16), sh_w_up[layer].astype(BF16),
                 sh_w_down[layer].astype(BF16), x, g2, ln_ffn_g[layer].reshape(1, d), ln_ffn_b[layer].reshape(1, d))
    return x
```

```python
import functools
import math

import numpy as np
import jax
import jax.numpy as jnp
from jax import lax
from jax.experimental import pallas as pl
from jax.experimental.pallas import tpu as pltpu

F32 = jnp.float32
BF16 = jnp.bfloat16
HI = lax.Precision.HIGHEST
NT = (((1,), (1,)), ((), ()))
TN = (((0,), (0,)), ((), ()))
NEG = -1e30

LANES = 128
VMEM_LIMIT = 56 * 1024 * 1024

DEPTH = 2
MOBA_HEAD_DIM = 128
MOBA_HEADS = 4
MOBA_BLOCK = 256
MOBA_TOPK = 3
MOBA_GROUP = 4
GLA_DV = 128
GLA_HEADS = 4
GLA_DK = 64
GLA_GATE_RANK = 16
GLA_GATE_NORM = 16.0
GLA_CHUNK = 64
GDN_DK = 128
GDN_DV = 128
GDN_HEADS = 8
GDN_CONV = 4
GDN_CHUNK = 64
GDN_GROUP = 4
RPE_BUCKETS = 32
RPE_MAX_DIST = 2048
RPE_TILES = 8
N_EXPERTS = 64
TOP_K = 6
N_GROUPS = 8
TOPK_GROUPS = 4
D_EXPERT = 256
ROUTED_SCALE = 2.5
DEEPNORM_ALPHA = float((2 * DEPTH) ** 0.25)
LN_EPS = 1e-5
NORM_EPS = 1e-6

MOBA_W = MOBA_HEADS * MOBA_HEAD_DIM
GLA_QK_W = GLA_HEADS * GLA_DK
GLA_V_W = GLA_HEADS * GLA_DV
GDN_W = GDN_HEADS * GDN_DK


def _cparams(sem):
    return pltpu.CompilerParams(dimension_semantics=sem, vmem_limit_bytes=VMEM_LIMIT)


def _sigmoid(x):
    return 1.0 / (1.0 + jnp.exp(-x))


def _silu(x):
    return x * _sigmoid(x)


def _softplus(x):
    return jnp.maximum(x, 0.0) + jnp.log(1.0 + jnp.exp(-jnp.abs(x)))


def _layer_norm(z, g, b):
    mu = jnp.mean(z, axis=-1, keepdims=True)
    zc = z - mu
    var = jnp.mean(zc * zc, axis=-1, keepdims=True)
    return zc * lax.rsqrt(var + LN_EPS) * g + b


def _ada_kernel(c_ref, w_ref, b_ref, o_ref):
    ca = _silu(c_ref[...])
    o_ref[0] = jnp.dot(ca, w_ref[0], precision=HI, preferred_element_type=F32) + b_ref[0]


def _ada(c, ada_w, ada_b):
    depth, d, n = ada_w.shape
    bsz = c.shape[0]
    tn = 6 * LANES
    return pl.pallas_call(
        _ada_kernel,
        grid=(depth, n // tn),
        in_specs=[pl.BlockSpec((bsz, d), lambda l, j: (0, 0)),
                  pl.BlockSpec((1, d, tn), lambda l, j: (l, 0, j)),
                  pl.BlockSpec((1, 1, tn), lambda l, j: (l, 0, j))],
        out_specs=pl.BlockSpec((1, bsz, tn), lambda l, j: (l, 0, j)),
        out_shape=jax.ShapeDtypeStruct((depth, bsz, n), F32),
        compiler_params=_cparams(("arbitrary", "arbitrary")),
        name="ada_mod",
    )(c, ada_w, ada_b.reshape(depth, 1, n))


def _modmm_kernel(x_ref, sc_ref, sh_ref, w_ref, o_ref, h_scr):
    @pl.when(pl.program_id(2) == 0)
    def _():
        h_scr[...] = (x_ref[0] * (1.0 + sc_ref[0]) + sh_ref[0]).astype(BF16)

    o_ref[0] = jnp.dot(h_scr[...], w_ref[...], preferred_element_type=F32).astype(o_ref.dtype)


def _col_tile(n, cap):
    best = LANES
    for t in range(LANES, cap + 1, LANES):
        if n % t == 0:
            best = t
    return best


def _mod_matmul(x, sc, sh, w, tm=1024, tn_cap=768):
    bsz, s, d = x.shape
    n = w.shape[1]
    tm = min(tm, s)
    tn = _col_tile(n, tn_cap)
    return pl.pallas_call(
        _modmm_kernel,
        grid=(bsz, s // tm, n // tn),
        in_specs=[pl.BlockSpec((1, tm, d), lambda b, i, j: (b, i, 0)),
                  pl.BlockSpec((1, 1, d), lambda b, i, j: (b, 0, 0)),
                  pl.BlockSpec((1, 1, d), lambda b, i, j: (b, 0, 0)),
                  pl.BlockSpec((d, tn), lambda b, i, j: (0, j))],
        out_specs=pl.BlockSpec((1, tm, tn), lambda b, i, j: (b, i, j)),
        out_shape=jax.ShapeDtypeStruct((bsz, s, n), BF16),
        scratch_shapes=[pltpu.VMEM((tm, d), BF16)],
        compiler_params=_cparams(("arbitrary", "arbitrary", "arbitrary")),
        name="mod_matmul",
    )(x, sc, sh, w)


def _rpe_lower_bounds():
    exact = RPE_BUCKETS // 2
    d = np.arange(0, 2 * RPE_MAX_DIST, dtype=np.int64)
    logd = np.log(np.maximum(d, 1).astype(np.float64) / exact)
    large = exact + (logd / math.log(RPE_MAX_DIST / exact) * (RPE_BUCKETS - exact)).astype(np.int64)
    large = np.minimum(large, RPE_BUCKETS - 1)
    bucket = np.where(d < exact, d, large)
    return [int(np.argmax(bucket >= k)) for k in range(RPE_BUCKETS)]


def _rpe_tiles_kernel(lo, rpe_ref, o_ref):
    h = pl.program_id(0)
    j = pl.program_id(1)
    blk = o_ref.shape[-1]
    key = lax.broadcasted_iota(jnp.int32, (blk, blk), 0)
    qry = lax.broadcasted_iota(jnp.int32, (blk, blk), 1)
    dist = j * blk + qry - key
    val = jnp.full((blk, blk), rpe_ref[0, h], F32)
    for k in range(1, RPE_BUCKETS):
        val = jnp.where(dist >= lo[k], rpe_ref[k, h], val)
    o_ref[0, 0] = jnp.where(dist >= 0, val, NEG)


def _rpe_tiles(rpe_bias):
    heads = rpe_bias.shape[1]
    return pl.pallas_call(
        functools.partial(_rpe_tiles_kernel, _rpe_lower_bounds()),
        grid=(heads, RPE_TILES),
        in_specs=[pl.BlockSpec(memory_space=pltpu.SMEM)],
        out_specs=pl.BlockSpec((1, 1, MOBA_BLOCK, MOBA_BLOCK), lambda h, j: (h, j, 0, 0)),
        out_shape=jax.ShapeDtypeStruct((heads, RPE_TILES, MOBA_BLOCK, MOBA_BLOCK), F32),
        compiler_params=_cparams(("arbitrary", "arbitrary")),
        name="rpe_tiles",
    )(rpe_bias)


def _moba_kernel(q_ref, k_ref, v_ref, t_ref, o_ref, kb_scr, vt_scr, km_scr, sel_scr):
    i = pl.program_id(2)
    nkb, blk, dh = kb_scr.shape

    @pl.when(i == 0)
    def _():
        for n in range(nkb):
            kn = k_ref[0, n * blk:(n + 1) * blk, :]
            kb_scr[n] = kn.astype(BF16)
            km_scr[n:n + 1, :] = jnp.mean(kn.astype(F32), axis=0, keepdims=True)
            vt_scr[n] = v_ref[0, n * blk:(n + 1) * blk, :].astype(F32).T.astype(BF16)

    q = q_ref[0].astype(F32)
    gate = lax.dot_general(km_scr[...], q, NT, precision=HI, preferred_element_type=F32)
    bidx = lax.broadcasted_iota(jnp.int32, gate.shape, 0)
    past = bidx < i
    g = jnp.where(past, gate, -jnp.inf)
    sel = None
    for _ in range(MOBA_TOPK):
        m = jnp.max(g, axis=0, keepdims=True)
        first = jnp.min(jnp.where(g == m, bidx, nkb), axis=0, keepdims=True)
        hit = bidx == first
        sel = hit if sel is None else jnp.logical_or(sel, hit)
        g = jnp.where(hit, -jnp.inf, g)
    sel_scr[...] = jnp.where(jnp.logical_and(sel, past), 0.0, NEG)

    qs = (q * dh ** -0.5).astype(BF16)

    s = lax.dot_general(kb_scr[i], qs, NT, preferred_element_type=F32) + t_ref[0, 0]
    m0 = jnp.max(s, axis=0, keepdims=True)
    p = jnp.exp(s - m0)
    l0 = jnp.sum(p, axis=0, keepdims=True)
    acc0 = jnp.dot(vt_scr[i], p.astype(BF16), preferred_element_type=F32)

    def body(g, carry):
        m, l, acc = carry
        ns = [jnp.minimum(g * MOBA_GROUP + u, nkb - 1) for u in range(MOBA_GROUP)]
        ss = []
        for n in ns:
            j = jnp.clip(i - n, 0, RPE_TILES - 1)
            ss.append(lax.dot_general(kb_scr[n], qs, NT, preferred_element_type=F32)
                      + t_ref[0, j] + sel_scr[pl.ds(n, 1), :])
        m_new = m
        for s in ss:
            m_new = jnp.maximum(m_new, jnp.max(s, axis=0, keepdims=True))
        alpha = jnp.exp(m - m_new)
        l = alpha * l
        acc = alpha * acc
        for n, s in zip(ns, ss):
            p = jnp.exp(s - m_new)
            l = l + jnp.sum(p, axis=0, keepdims=True)
            acc = acc + jnp.dot(vt_scr[n], p.astype(BF16), preferred_element_type=F32)
        return m_new, l, acc

    _, l, acc = lax.fori_loop(0, (i + MOBA_GROUP - 1) // MOBA_GROUP, body, (m0, l0, acc0))
    o_ref[0] = (acc / l).T.astype(o_ref.dtype)


def _moba(proj, tiles, q_col, k_col, v_col):
    bsz, s, _ = proj.shape
    dh, blk, heads = MOBA_HEAD_DIM, MOBA_BLOCK, MOBA_HEADS
    nkb = s // blk
    return pl.pallas_call(
        _moba_kernel,
        grid=(bsz, heads, nkb),
        in_specs=[pl.BlockSpec((1, blk, dh), lambda b, h, i: (b, i, q_col + h)),
                  pl.BlockSpec((1, s, dh), lambda b, h, i: (b, 0, k_col + h)),
                  pl.BlockSpec((1, s, dh), lambda b, h, i: (b, 0, v_col + h)),
                  pl.BlockSpec((1, RPE_TILES, blk, blk), lambda b, h, i: (h, 0, 0, 0))],
        out_specs=pl.BlockSpec((1, blk, dh), lambda b, h, i: (b, i, h)),
        out_shape=jax.ShapeDtypeStruct((bsz, s, heads * dh), BF16),
        scratch_shapes=[pltpu.VMEM((nkb, blk, dh), BF16),
                        pltpu.VMEM((nkb, dh, blk), BF16),
                        pltpu.VMEM((nkb, dh), F32),
                        pltpu.VMEM((nkb, blk), F32)],
        compiler_params=_cparams(("arbitrary", "arbitrary", "arbitrary")),
        name="moba_attention",
    )(proj, proj, proj, tiles)


def _gla_kernel(q_ref, k_ref, v_ref, gg_ref, glr_ref, w2_ref, gb_ref, nw_ref, o_ref, st_scr):
    @pl.when(pl.program_id(1) == 0)
    def _():
        st_scr[...] = jnp.zeros(st_scr.shape, F32)

    tg = q_ref.shape[1]
    c = GLA_CHUNK
    x = jnp.dot(glr_ref[0].astype(F32), w2_ref[...], precision=HI, preferred_element_type=F32) + gb_ref[...]
    lg = -_softplus(-x) * (1.0 / GLA_GATE_NORM)
    row = lax.broadcasted_iota(jnp.int32, (c, c), 0)
    col = lax.broadcasted_iota(jnp.int32, (c, c), 1)
    incl = row >= col
    tri = incl.astype(F32)
    nw = nw_ref[...]
    for ci in range(tg // c):
        rows = slice(ci * c, (ci + 1) * c)
        b = jnp.dot(tri, lg[rows], precision=HI, preferred_element_type=F32)
        bl = b[c - 1:c, :]
        q = q_ref[0, rows, :].astype(F32) * GLA_DK ** -0.5
        k = k_ref[0, rows, :].astype(F32)
        q_e = (q * jnp.exp(b)).astype(BF16)
        k_e = (k * jnp.exp(-b)).astype(BF16)
        k_end = (k * jnp.exp(bl - b)).astype(BF16)
        d = jnp.exp(bl)
        for h in range(GLA_HEADS):
            ks = slice(h * GLA_DK, (h + 1) * GLA_DK)
            vs = slice(h * GLA_DV, (h + 1) * GLA_DV)
            vh = v_ref[0, rows, vs].astype(BF16)
            a = lax.dot_general(q_e[:, ks], k_e[:, ks], NT, preferred_element_type=F32)
            a = jnp.where(incl, a, 0.0).astype(BF16)
            st = st_scr[h]
            o = (jnp.dot(a, vh, preferred_element_type=F32)
                 + lax.dot_general(q_e[:, ks], st.astype(BF16), NT, preferred_element_type=F32))
            st_scr[h] = st * d[:, ks] + lax.dot_general(vh, k_end[:, ks], TN, preferred_element_type=F32)
            o = o * lax.rsqrt(jnp.mean(o * o, axis=-1, keepdims=True) + NORM_EPS)
            o_ref[0, rows, vs] = (o * nw * _silu(gg_ref[0, rows, vs].astype(F32))).astype(o_ref.dtype)


def _gla(proj, gk_w2p, gk_b, o_norm, q_col, k_col, v_col, g_col, r_col, tg=256):
    bsz, s, _ = proj.shape
    tg = min(tg, s)
    qk, vw = GLA_QK_W, GLA_V_W
    return pl.pallas_call(
        _gla_kernel,
        grid=(bsz, s // tg),
        in_specs=[pl.BlockSpec((1, tg, qk), lambda b, t: (b, t, q_col)),
                  pl.BlockSpec((1, tg, qk), lambda b, t: (b, t, k_col)),
                  pl.BlockSpec((1, tg, vw), lambda b, t: (b, t, v_col)),
                  pl.BlockSpec((1, tg, vw), lambda b, t: (b, t, g_col)),
                  pl.BlockSpec((1, tg, LANES), lambda b, t: (b, t, r_col)),
                  pl.BlockSpec((LANES, qk), lambda b, t: (0, 0)),
                  pl.BlockSpec((1, qk), lambda b, t: (0, 0)),
                  pl.BlockSpec((1, GLA_DV), lambda b, t: (0, 0))],
        out_specs=pl.BlockSpec((1, tg, vw), lambda b, t: (b, t, 0)),
        out_shape=jax.ShapeDtypeStruct((bsz, s, vw), BF16),
        scratch_shapes=[pltpu.VMEM((GLA_HEADS, GLA_DV, GLA_DK), F32)],
        compiler_params=_cparams(("arbitrary", "arbitrary")),
        name="gla_mixer",
    )(proj, proj, proj, proj, proj, gk_w2p, gk_b, o_norm)


def _gdn_kernel(qkv_ref, gate_ref, ba_ref, cw_ref, pv_ref, nw_ref, o_ref, tail_scr, s_scr):
    @pl.when(pl.program_id(1) == 0)
    def _():
        tail_scr[...] = jnp.zeros(tail_scr.shape, F32)
        s_scr[...] = jnp.zeros(s_scr.shape, F32)

    tg = qkv_ref.shape[1]
    c, dk, grp = GDN_CHUNK, GDN_DK, GDN_GROUP
    gr = grp * c
    w = GDN_W

    x = qkv_ref[0].astype(F32)
    tail = tail_scr[...]
    tail_scr[...] = x[tg - 8:, :]
    r8 = lax.broadcasted_iota(jnp.int32, (8, 1), 0)
    y = x * cw_ref[GDN_CONV - 1:GDN_CONV, :]
    for sft in range(1, GDN_CONV):
        xs = pltpu.roll(x, sft, axis=0)
        head = jnp.where(r8 < sft, pltpu.roll(tail, sft, axis=0), xs[:8, :])
        xs = jnp.concatenate([head, xs[8:, :]], axis=0)
        y = y + xs * cw_ref[GDN_CONV - 1 - sft:GDN_CONV - sft, :]
    y = _silu(y)

    ba = ba_ref[0].astype(F32)
    beta_t = _sigmoid(ba)
    g_t = -jnp.exp(pv_ref[0:1, :]) * _softplus(ba + pv_ref[1:2, :])

    row = lax.broadcasted_iota(jnp.int32, (c, c), 0)
    col = lax.broadcasted_iota(jnp.int32, (c, c), 1)
    tri = (row >= col).astype(F32)
    rr = lax.broadcasted_iota(jnp.int32, (gr, gr), 0)
    cc = lax.broadcasted_iota(jnp.int32, (gr, gr), 1)
    same = (rr // c) == (cc // c)
    incl = jnp.logical_and(same, rr >= cc)
    strict = jnp.logical_and(same, rr > cc)
    eye = (rr == cc).astype(F32)
    halves = [(rr // sz) == (cc // sz) for sz in (2 ** e for e in range(1, int(math.log2(c)) + 1))]
    nw = nw_ref[...]

    probs = []
    for ci in range(tg // c):
        rows = slice(ci * c, (ci + 1) * c)
        gcum = jnp.dot(tri, g_t[rows], precision=HI, preferred_element_type=F32)
        gcum_t = gcum.T
        for gi in range(GDN_HEADS // grp):
            hs = [gi * grp + u for u in range(grp)]

            def stack(a, off):
                return jnp.concatenate([a[rows, off + h * dk: off + (h + 1) * dk] for h in hs], axis=0)

            q = stack(y, 0)
            k = stack(y, w)
            v = stack(y, 2 * w)
            q = q * lax.rsqrt(jnp.sum(q * q, axis=-1, keepdims=True) + NORM_EPS) * dk ** -0.5
            k = k * lax.rsqrt(jnp.sum(k * k, axis=-1, keepdims=True) + NORM_EPS)
            beta = jnp.concatenate([beta_t[rows, h:h + 1] for h in hs], axis=0)
            gc = jnp.concatenate([gcum[:, GDN_HEADS + h:GDN_HEADS + h + 1] for h in hs], axis=0)
            gc_row = jnp.concatenate([gcum_t[GDN_HEADS + h:GDN_HEADS + h + 1, :] for h in hs], axis=1)
            gl = jnp.concatenate([jnp.broadcast_to(gcum[c - 1:c, GDN_HEADS + h:GDN_HEADS + h + 1], (c, 1))
                                  for h in hs], axis=0)

            decay = jnp.where(incl, jnp.exp(jnp.where(incl, gc - gc_row, 0.0)), 0.0)
            kb = k * beta
            k16 = k.astype(BF16)
            a = lax.dot_general(kb.astype(BF16), k16, NT, preferred_element_type=F32)
            a = jnp.where(strict, a * decay, 0.0)
            eg = jnp.exp(gc)
            probs.append(dict(
                rows=rows, hs=hs, a16=a.astype(BF16),
                t=eye - jnp.where(halves[0], a, 0.0),
                rhs=jnp.concatenate([v * beta, kb * eg], axis=1).astype(BF16),
                attn=(lax.dot_general(q.astype(BF16), k16, NT, preferred_element_type=F32) * decay).astype(BF16),
                q_g=(q * eg).astype(BF16),
                k_end=(k * jnp.exp(gl - gc)).astype(BF16),
                d_last=[jnp.exp(gcum[c - 1:c, GDN_HEADS + h:GDN_HEADS + h + 1]) for h in hs]))

    for lvl in range(1, len(halves)):
        off16 = jnp.where(jnp.logical_and(halves[lvl], jnp.logical_not(halves[lvl - 1])), 1.0, 0.0).astype(BF16)
        t16s = [p["t"].astype(BF16) for p in probs]
        xs = [jnp.dot(p["a16"] * off16, t16, preferred_element_type=F32).astype(BF16)
              for p, t16 in zip(probs, t16s)]
        for p, t16, x in zip(probs, t16s, xs):
            p["t"] = p["t"] - jnp.dot(t16, x, preferred_element_type=F32)
    for p in probs:
        wk = jnp.dot(p["t"].astype(BF16), p["rhs"], preferred_element_type=F32)
        p["w_val"] = wk[:, :GDN_DV]
        p["k_cum"] = wk[:, GDN_DV:].astype(BF16)

    for p in probs:
        rows, hs = p["rows"], p["hs"]
        v_new = []
        for u, h in enumerate(hs):
            hr = slice(u * c, (u + 1) * c)
            st16 = s_scr[h].astype(BF16)
            v_new.append(p["w_val"][hr] - jnp.dot(p["k_cum"][hr], st16, preferred_element_type=F32))
        v16 = jnp.concatenate(v_new, axis=0).astype(BF16)
        o_intra = jnp.dot(p["attn"], v16, preferred_element_type=F32)
        for u, h in enumerate(hs):
            hr = slice(u * c, (u + 1) * c)
            st = s_scr[h]
            o = o_intra[hr] + jnp.dot(p["q_g"][hr], st.astype(BF16), preferred_element_type=F32)
            s_scr[h] = (st * p["d_last"][u]
                        + lax.dot_general(p["k_end"][hr], v16[hr], TN, preferred_element_type=F32))
            o = o * lax.rsqrt(jnp.mean(o * o, axis=-1, keepdims=True) + NORM_EPS)
            cs = slice(h * GDN_DV, (h + 1) * GDN_DV)
            o_ref[0, rows, cs] = (o * nw * _silu(gate_ref[0, rows, cs].astype(F32))).astype(o_ref.dtype)


def _gdn(proj, conv_w, pvec, o_norm, tg=256):
    bsz, s, _ = proj.shape
    tg = min(tg, s)
    w = GDN_W
    return pl.pallas_call(
        _gdn_kernel,
        grid=(bsz, s // tg),
        in_specs=[pl.BlockSpec((1, tg, 3 * w), lambda b, t: (b, t, 0)),
                  pl.BlockSpec((1, tg, w), lambda b, t: (b, t, 3)),
                  pl.BlockSpec((1, tg, LANES), lambda b, t: (b, t, 4 * w // LANES)),
                  pl.BlockSpec((GDN_CONV, 3 * w), lambda b, t: (0, 0)),
                  pl.BlockSpec((2, LANES), lambda b, t: (0, 0)),
                  pl.BlockSpec((1, GDN_DV), lambda b, t: (0, 0))],
        out_specs=pl.BlockSpec((1, tg, w), lambda b, t: (b, t, 0)),
        out_shape=jax.ShapeDtypeStruct((bsz, s, w), BF16),
        scratch_shapes=[pltpu.VMEM((8, 3 * w), F32),
                        pltpu.VMEM((GDN_HEADS, GDN_DK, GDN_DV), F32)],
        compiler_params=_cparams(("arbitrary", "arbitrary")),
        name="gdn_mixer",
    )(proj, proj, proj, conv_w, pvec, o_norm)


def _mixout_kernel(n_act, *refs):
    acts = refs[:n_act]
    ws = refs[n_act:2 * n_act]
    x_ref, g1_ref, lng_ref, lnb_ref, sc_ref, sh_ref, rw_ref = refs[2 * n_act:2 * n_act + 7]
    xo_ref, h_ref, rl_ref = refs[2 * n_act + 7:]
    y = None
    for a_ref, w_ref in zip(acts, ws):
        t = jnp.dot(a_ref[0].astype(BF16), w_ref[...], preferred_element_type=F32)
        y = t if y is None else y + t
    xn = _layer_norm(DEEPNORM_ALPHA * x_ref[0] + g1_ref[0] * y, lng_ref[...], lnb_ref[...])
    xo_ref[0] = xn
    h = xn * (1.0 + sc_ref[0]) + sh_ref[0]
    h_ref[0] = h.astype(BF16)
    rl_ref[0] = lax.dot_general(rw_ref[...], h, NT, precision=HI, preferred_element_type=F32)


def _mixout(acts, ws, x, g1, ln_g, ln_b, sc2, sh2, router_wt, tm=512):
    bsz, s, d = x.shape
    tm = min(tm, s)
    n_act = len(acts)
    ne = router_wt.shape[0]
    vec = pl.BlockSpec((1, 1, d), lambda b, i: (b, 0, 0))
    par = pl.BlockSpec((1, d), lambda b, i: (0, 0))
    in_specs = ([pl.BlockSpec((1, tm, a.shape[-1]), lambda b, i: (b, i, 0)) for a in acts]
                + [pl.BlockSpec(w.shape, lambda b, i: (0, 0)) for w in ws]
                + [pl.BlockSpec((1, tm, d), lambda b, i: (b, i, 0)), vec, par, par, vec, vec,
                   pl.BlockSpec((ne, d), lambda b, i: (0, 0))])
    return pl.pallas_call(
        functools.partial(_mixout_kernel, n_act),
        grid=(bsz, s // tm),
        in_specs=in_specs,
        out_specs=[pl.BlockSpec((1, tm, d), lambda b, i: (b, i, 0)),
                   pl.BlockSpec((1, tm, d), lambda b, i: (b, i, 0)),
                   pl.BlockSpec((1, ne, tm), lambda b, i: (b, 0, i))],
        out_shape=[jax.ShapeDtypeStruct((bsz, s, d), F32),
                   jax.ShapeDtypeStruct((bsz, s, d), BF16),
                   jax.ShapeDtypeStruct((bsz, ne, s), F32)],
        compiler_params=_cparams(("arbitrary", "arbitrary")),
        name="mix_out",
    )(*acts, *ws, x, g1, ln_g, ln_b, sc2, sh2, router_wt)


def _first_max(vals, idx, axis, sentinel):
    m = jnp.max(vals, axis=axis, keepdims=True)
    first = jnp.min(jnp.where(vals == m, idx, sentinel), axis=axis, keepdims=True)
    return m, idx == first


def _router_kernel(rl_ref, rb_ref, g_ref, gt_ref, cnt_ref):
    ne, tn = rl_ref.shape[1], rl_ref.shape[2]
    gsz = ne // N_GROUPS
    scores = _sigmoid(rl_ref[0])
    sel = scores + rb_ref[...]
    ridx = lax.broadcasted_iota(jnp.int32, (gsz, tn), 0)
    gidx = lax.broadcasted_iota(jnp.int32, (N_GROUPS, tn), 0)
    gs = jnp.zeros((N_GROUPS, tn), F32)
    for g in range(N_GROUPS):
        sg = sel[g * gsz:(g + 1) * gsz, :]
        m1, hit = _first_max(sg, ridx, 0, gsz)
        m2 = jnp.max(jnp.where(hit, -jnp.inf, sg), axis=0, keepdims=True)
        gs = jnp.where(gidx == g, m1 + m2, gs)
    gsel = None
    for _ in range(TOPK_GROUPS):
        _, hit = _first_max(gs, gidx, 0, N_GROUPS)
        gsel = hit if gsel is None else jnp.logical_or(gsel, hit)
        gs = jnp.where(hit, -jnp.inf, gs)
    gself = gsel.astype(F32)
    emask = jnp.concatenate([jnp.broadcast_to(gself[g:g + 1, :], (gsz, tn)) for g in range(N_GROUPS)], axis=0)
    cand = jnp.where(emask > 0.5, sel, -jnp.inf)
    eidx = lax.broadcasted_iota(jnp.int32, cand.shape, 0)
    chosen = None
    for _ in range(TOP_K):
        _, hit = _first_max(cand, eidx, 0, ne)
        chosen = hit if chosen is None else jnp.logical_or(chosen, hit)
        cand = jnp.where(hit, -jnp.inf, cand)
    wsel = jnp.where(chosen, scores, 0.0)
    tot = jnp.sum(wsel, axis=0, keepdims=True)
    gates = wsel / (tot + 1e-20) * ROUTED_SCALE
    g_ref[0] = gates.T
    gt_ref[0] = gates
    tile_of = lax.broadcasted_iota(jnp.int32, (tn, LANES), 0) // MOE_TM
    ind = (tile_of == lax.broadcasted_iota(jnp.int32, (tn, LANES), 1)).astype(BF16)
    routed = jnp.where(gates > 0.0, 1.0, 0.0).astype(BF16)
    cnt_ref[0, 0] = jnp.dot(routed, ind, preferred_element_type=F32)


def _router(rl, router_b, tn=1024):
    bsz, ne, s = rl.shape
    tn = min(tn, s)
    return pl.pallas_call(
        _router_kernel,
        grid=(bsz, s // tn),
        in_specs=[pl.BlockSpec((1, ne, tn), lambda b, i: (b, 0, i)),
                  pl.BlockSpec((ne, 1), lambda b, i: (0, 0))],
        out_specs=[pl.BlockSpec((1, tn, ne), lambda b, i: (b, i, 0)),
                   pl.BlockSpec((1, ne, tn), lambda b, i: (b, 0, i)),
                   pl.BlockSpec((1, 1, ne, LANES), lambda b, i: (b, i, 0, 0))],
        out_shape=[jax.ShapeDtypeStruct((bsz, s, ne), F32),
                   jax.ShapeDtypeStruct((bsz, ne, s), F32),
                   jax.ShapeDtypeStruct((bsz, s // tn, ne, LANES), F32)],
        compiler_params=_cparams(("arbitrary", "arbitrary")),
        name="moe_router",
    )(rl, router_b.reshape(ne, 1))


MOE_TM = 256
MOE_ALIGN = 16
MOE_R = 512
MOE_LC = 512
MOE_LMAX = -(-(MOE_TM * TOP_K + N_EXPERTS * (MOE_ALIGN - 1)) // MOE_LC) * MOE_LC
MOE_NP = MOE_LMAX // MOE_ALIGN
POS_SPLIT = 256.0


def _ffn(x, wg, wu, wd):
    a = jnp.dot(x, wg, preferred_element_type=F32)
    u = jnp.dot(x, wu, preferred_element_type=F32)
    return jnp.dot((_silu(a) * u).astype(BF16), wd, preferred_element_type=F32)


def _split_pos(pos, axis):
    hi = jnp.floor(pos * (1.0 / POS_SPLIT)) * POS_SPLIT
    return jnp.concatenate([hi, pos - hi], axis=axis).astype(BF16)


def _for_each_piece(n_pieces, fn):
    def body(p, carry):
        fn(p)
        return carry

    lax.fori_loop(0, n_pieces, body, 0)


def _dispatch_kernel(dst_s, np_s, h_ref, gt_ref, offc_ref, offr_ref, cntr_ref, xs_hbm, sorted_scr, sems):
    i = pl.program_id(0)
    last = pl.num_programs(0) - 1
    slot = i % 2
    tm = h_ref.shape[0]
    routed = gt_ref[0] > 0.0
    t0 = lax.broadcasted_iota(jnp.int32, (tm, tm), 0)
    t1 = lax.broadcasted_iota(jnp.int32, (tm, tm), 1)
    earlier = jnp.where(t0 < t1, 1.0, 0.0).astype(BF16)
    rank_t = jnp.dot(jnp.where(routed, 1.0, 0.0).astype(BF16), earlier, preferred_element_type=F32)
    pos_t = jnp.where(routed, rank_t + offc_ref[0] + 1.0, 0.0)
    pos2 = _split_pos(pos_t, 0)
    x = h_ref[...]
    offr = offr_ref[0]
    endr = offr + cntr_ref[0]
    for c in range(MOE_LMAX // MOE_LC):
        r = (c * MOE_LC + lax.broadcasted_iota(jnp.int32, (MOE_LC, 1), 0)).astype(F32)
        owner = jnp.where(jnp.logical_and(r >= offr, r < endr), 1.0, 0.0).astype(BF16)
        possel = jnp.dot(jnp.concatenate([owner, owner], axis=1), pos2, preferred_element_type=F32)
        perm = jnp.where(possel == r + 1.0, 1.0, 0.0).astype(BF16)
        sorted_scr[slot, c * MOE_LC:(c + 1) * MOE_LC, :] = (
            jnp.dot(perm, x, preferred_element_type=F32).astype(BF16))

    def piece(sl, tile, p):
        src = sorted_scr.at[sl, pl.ds(pl.multiple_of(p * MOE_ALIGN, MOE_ALIGN), MOE_ALIGN)]
        dst = xs_hbm.at[pl.ds(pl.multiple_of(dst_s[tile * MOE_NP + p], MOE_ALIGN), MOE_ALIGN)]
        return pltpu.make_async_copy(src, dst, sems.at[sl])

    _for_each_piece(np_s[i], lambda p: piece(slot, i, p).start())

    @pl.when(i > 0)
    def _():
        _for_each_piece(np_s[i - 1], lambda p: piece(1 - slot, i - 1, p).wait())

    @pl.when(i == last)
    def _():
        _for_each_piece(np_s[i], lambda p: piece(slot, i, p).wait())


def _expert_ffn_kernel(be_s, nu_s, x_ref, wg_ref, wu_ref, wd_ref, y_ref):
    @pl.when(pl.program_id(0) < nu_s[0])
    def _():
        y_ref[...] = _ffn(x_ref[...], wg_ref[0, 0].astype(BF16), wu_ref[0, 0].astype(BF16),
                          wd_ref[0, 0].astype(BF16)).astype(BF16)


def _combine_kernel(dst_s, np_s, ys_hbm, g_ref, h_ref, offr_ref, offc_ref, cntc_ref,
                    sg_ref, su_ref, sd_ref, x_ref, g2_ref, lng_ref, lnb_ref, o_ref, ys_scr, sems):
    i = pl.program_id(0)
    last = pl.num_programs(0) - 1
    slot = i % 2
    tm = h_ref.shape[0]

    def piece(sl, tile, p):
        src = ys_hbm.at[pl.ds(pl.multiple_of(dst_s[tile * MOE_NP + p], MOE_ALIGN), MOE_ALIGN)]
        dst = ys_scr.at[sl, pl.ds(pl.multiple_of(p * MOE_ALIGN, MOE_ALIGN), MOE_ALIGN)]
        return pltpu.make_async_copy(src, dst, sems.at[sl])

    @pl.when(i == 0)
    def _():
        ys_scr[...] = jnp.zeros(ys_scr.shape, BF16)
        _for_each_piece(np_s[0], lambda p: piece(0, 0, p).start())

    @pl.when(i < last)
    def _():
        _for_each_piece(np_s[i + 1], lambda p: piece(1 - slot, i + 1, p).start())

    acc = _ffn(h_ref[...], sg_ref[...], su_ref[...], sd_ref[...])
    g = g_ref[...]
    routed = g > 0.0
    t0 = lax.broadcasted_iota(jnp.int32, (tm, tm), 0)
    t1 = lax.broadcasted_iota(jnp.int32, (tm, tm), 1)
    earlier = jnp.where(t0 > t1, 1.0, 0.0).astype(BF16)
    rank = jnp.dot(earlier, jnp.where(routed, 1.0, 0.0).astype(BF16), preferred_element_type=F32)
    pos = jnp.where(routed, rank + offr_ref[0] + 1.0, 0.0)
    pos2 = _split_pos(pos, 1)
    g16 = g.astype(BF16)
    offc = offc_ref[0]
    endc = offc + cntc_ref[0]
    _for_each_piece(np_s[i], lambda p: piece(slot, i, p).wait())
    for c in range(MOE_LMAX // MOE_LC):
        r = (c * MOE_LC + lax.broadcasted_iota(jnp.int32, (1, MOE_LC), 1)).astype(F32)
        owner = jnp.where(jnp.logical_and(r >= offc, r < endc), 1.0, 0.0).astype(BF16)
        possel = jnp.dot(pos2, jnp.concatenate([owner, owner], axis=0), preferred_element_type=F32)
        gsel = jnp.dot(g16, owner, preferred_element_type=F32)
        w = jnp.where(possel == r + 1.0, gsel, 0.0).astype(BF16)
        acc = acc + jnp.dot(w, ys_scr[slot, c * MOE_LC:(c + 1) * MOE_LC, :], preferred_element_type=F32)
    z = DEEPNORM_ALPHA * x_ref[...] + g2_ref[0] * acc
    o_ref[...] = _layer_norm(z, lng_ref[...], lnb_ref[...])


def _ceil_to(v, m):
    return jnp.floor((v + (m - 1.0)) * (1.0 / m)) * m


def _moe_layout_kernel(cnt_ref, dst_ref, np_ref, off_ref, cntp_ref, be_ref, nu_ref):
    cnt = cnt_ref[...]
    ntiles, ne = cnt.shape
    cntp = _ceil_to(cnt, MOE_ALIGN)
    e0 = lax.broadcasted_iota(jnp.int32, (ne, ne), 0)
    e1 = lax.broadcasted_iota(jnp.int32, (ne, ne), 1)
    i0 = lax.broadcasted_iota(jnp.int32, (ntiles, ntiles), 0)
    i1 = lax.broadcasted_iota(jnp.int32, (ntiles, ntiles), 1)

    def mm(a, b):
        return jnp.dot(a, b, precision=HI, preferred_element_type=F32)

    off = mm(cntp, jnp.where(e0 < e1, 1.0, 0.0))
    before = mm(jnp.where(i0 > i1, 1.0, 0.0), cntp)
    tot = jnp.broadcast_to(jnp.sum(cntp, axis=0, keepdims=True), (8, ne))
    totr = _ceil_to(tot, MOE_R)
    base = mm(totr, jnp.where(e0 < e1, 1.0, 0.0))[0:1, :]
    delta = base + before - off
    step = mm(delta, jnp.where(e0 == e1, 1.0, 0.0) - jnp.where(e0 + 1 == e1, 1.0, 0.0))
    rowp = (lax.broadcasted_iota(jnp.int32, (1, MOE_NP), 1) * MOE_ALIGN).astype(F32)
    dst = jnp.broadcast_to(rowp, (ntiles, MOE_NP))
    for e in range(ne):
        dst = dst + jnp.where(off[:, e:e + 1] <= rowp, step[:, e:e + 1], 0.0)
    dst_ref[...] = dst.astype(jnp.int32)
    pieces = jnp.sum(cntp, axis=1, keepdims=True) * (1.0 / MOE_ALIGN)
    np_ref[...] = jnp.broadcast_to(pieces, np_ref.shape).astype(jnp.int32)
    off_ref[...] = off
    cntp_ref[...] = cntp
    tot_c = lax.dot_general(cntp, jnp.ones((ntiles, LANES), F32), TN, precision=HI, preferred_element_type=F32)
    end_c = mm(jnp.where(e0 >= e1, 1.0, 0.0), _ceil_to(tot_c, MOE_R))[:, 0:1]
    total = end_c[ne - 1:ne, :]
    first = jnp.minimum((lax.broadcasted_iota(jnp.int32, be_ref.shape, 1) * MOE_R).astype(F32), total - 1.0)
    be = jnp.sum(jnp.where(end_c <= first, 1.0, 0.0), axis=0, keepdims=True)
    be_ref[...] = jnp.minimum(be, ne - 1.0).astype(jnp.int32)
    nu_ref[...] = jnp.broadcast_to(total * (1.0 / MOE_R), nu_ref.shape).astype(jnp.int32)


def _moe_layout(cnt_raw, tn):
    ne = N_EXPERTS
    nsub = tn // MOE_TM
    cnt = jnp.transpose(cnt_raw[..., :nsub], (0, 1, 3, 2)).reshape(-1, ne)
    ntiles = cnt.shape[0]
    nblk = -(-(ntiles * (MOE_TM * TOP_K + ne * (MOE_ALIGN - 1)) + ne * (MOE_R - 1)) // MOE_R)
    nblk_pad = -(-nblk // LANES) * LANES
    dst, npieces, off, cntp, blk_exp, nused = pl.pallas_call(
        _moe_layout_kernel,
        out_shape=[jax.ShapeDtypeStruct((ntiles, MOE_NP), jnp.int32),
                   jax.ShapeDtypeStruct((ntiles, LANES), jnp.int32),
                   jax.ShapeDtypeStruct((ntiles, ne), F32),
                   jax.ShapeDtypeStruct((ntiles, ne), F32),
                   jax.ShapeDtypeStruct((1, nblk_pad), jnp.int32),
                   jax.ShapeDtypeStruct((1, LANES), jnp.int32)],
        compiler_params=pltpu.CompilerParams(vmem_limit_bytes=VMEM_LIMIT),
        name="moe_layout",
    )(cnt)
    return dst.reshape(-1), npieces[:, 0], off, cntp, nblk, nused[0, :1], blk_exp[0, :nblk]


def _moe(h2, gates, gates_t, cnt_raw, tn, layer, wg, wu, wd, sg, su, sd, x, g2, ln_g, ln_b):
    bsz, s, d = x.shape
    t = bsz * s
    tm = MOE_TM
    _, ne, _, de = wg.shape
    per_b = s // tm
    ntiles = t // tm
    dst, npieces, off_f, cnt_f, nblk, nused, blk_exp = _moe_layout(cnt_raw, tn)
    scalars = (dst, npieces)
    row = pl.BlockSpec((1, 1, ne), lambda i, *_: (i, 0, 0))
    col = pl.BlockSpec((1, ne, 1), lambda i, *_: (i, 0, 0))
    tok = pl.BlockSpec((tm, d), lambda i, *_: (i, 0))
    par = pl.BlockSpec((1, d), lambda i, *_: (0, 0))
    rows = nblk * MOE_R

    xs = pl.pallas_call(
        _dispatch_kernel,
        grid_spec=pltpu.PrefetchScalarGridSpec(
            num_scalar_prefetch=2, grid=(ntiles,),
            in_specs=[tok,
                      pl.BlockSpec((1, ne, tm), lambda i, *_: (i // per_b, 0, i % per_b)),
                      col, row, row],
            out_specs=pl.BlockSpec(memory_space=pl.ANY),
            scratch_shapes=[pltpu.VMEM((2, MOE_LMAX, d), BF16), pltpu.SemaphoreType.DMA((2,))]),
        out_shape=jax.ShapeDtypeStruct((rows, d), BF16),
        compiler_params=_cparams(("arbitrary",)),
        name="moe_dispatch",
    )(*scalars, h2.reshape(t, d), gates_t, off_f.reshape(ntiles, ne, 1), off_f.reshape(ntiles, 1, ne),
      cnt_f.reshape(ntiles, 1, ne))

    blk = pl.BlockSpec((MOE_R, d), lambda j, be, nu: (jnp.maximum(jnp.minimum(j, nu[0] - 1), 0), 0))
    ys = pl.pallas_call(
        _expert_ffn_kernel,
        grid_spec=pltpu.PrefetchScalarGridSpec(
            num_scalar_prefetch=2, grid=(nblk,),
            in_specs=[blk,
                      pl.BlockSpec((1, 1, d, de), lambda j, be, nu: (layer, be[j], 0, 0)),
                      pl.BlockSpec((1, 1, d, de), lambda j, be, nu: (layer, be[j], 0, 0)),
                      pl.BlockSpec((1, 1, de, d), lambda j, be, nu: (layer, be[j], 0, 0))],
            out_specs=blk),
        out_shape=jax.ShapeDtypeStruct((rows, d), BF16),
        compiler_params=_cparams(("arbitrary",)),
        name="moe_expert_ffn",
    )(blk_exp, nused, xs, wg, wu, wd)

    out = pl.pallas_call(
        _combine_kernel,
        grid_spec=pltpu.PrefetchScalarGridSpec(
            num_scalar_prefetch=2, grid=(ntiles,),
            in_specs=[pl.BlockSpec(memory_space=pl.ANY),
                      pl.BlockSpec((tm, ne), lambda i, *_: (i, 0)),
                      tok, row, col, col,
                      pl.BlockSpec(sg.shape, lambda i, *_: (0, 0)),
                      pl.BlockSpec(su.shape, lambda i, *_: (0, 0)),
                      pl.BlockSpec(sd.shape, lambda i, *_: (0, 0)),
                      tok,
                      pl.BlockSpec((1, 1, d), lambda i, *_: (i // per_b, 0, 0)),
                      par, par],
            out_specs=tok,
            scratch_shapes=[pltpu.VMEM((2, MOE_LMAX, d), BF16), pltpu.SemaphoreType.DMA((2,))]),
        out_shape=jax.ShapeDtypeStruct((t, d), F32),
        compiler_params=_cparams(("arbitrary",)),
        name="moe_combine",
    )(*scalars, ys, gates.reshape(t, ne), h2.reshape(t, d), off_f.reshape(ntiles, 1, ne),
      off_f.reshape(ntiles, ne, 1), cnt_f.reshape(ntiles, ne, 1), sg, su, sd, x.reshape(t, d), g2, ln_g, ln_b)
    return out.reshape(bsz, s, d)


def _pad_cols(w, n):
    return jnp.pad(w, ((0, 0), (0, n - w.shape[1])))


def kernel(x, c, rpe_bias, ada_w, ada_b, ln_mix_g, ln_mix_b, ln_ffn_g, ln_ffn_b, ev_w_in, ev_gk_w2, ev_gk_b, ev_norm, ev_w_out, od_w_in, od_conv_w, od_a_log, od_dt_bias, od_norm, od_w_out, moe_router_w, moe_router_b, moe_w_gate, moe_w_up, moe_w_down, sh_w_gate, sh_w_up, sh_w_down):
    bsz, s, d = x.shape
    mod = _ada(c, ada_w, ada_b)
    tiles = _rpe_tiles(rpe_bias)

    for layer in range(DEPTH):
        sh1, sc1, g1, sh2, sc2, g2 = [mod[layer, :, u * d:(u + 1) * d].reshape(bsz, 1, d) for u in range(6)]
        i = layer // 2
        if layer % 2 == 0:
            n_main = 3 * MOBA_W + 2 * GLA_QK_W + 2 * GLA_V_W
            w_in = jnp.concatenate([ev_w_in[i][:, :n_main], _pad_cols(ev_w_in[i][:, n_main:], LANES)], axis=1)
            proj = _mod_matmul(x, sc1, sh1, w_in.astype(BF16))
            nb = MOBA_W // LANES
            o_a = _moba(proj, tiles, 0, nb, 2 * nb)
            gk_w2p = jnp.pad(ev_gk_w2[i], ((0, LANES - GLA_GATE_RANK), (0, 0)))
            gla0 = 3 * MOBA_W
            o_b = _gla(proj, gk_w2p, ev_gk_b[i].reshape(1, -1), ev_norm[i].reshape(1, -1),
                       gla0 // GLA_QK_W, gla0 // GLA_QK_W + 1,
                       (gla0 + 2 * GLA_QK_W) // GLA_V_W, (gla0 + 2 * GLA_QK_W) // GLA_V_W + 1,
                       n_main // LANES)
            w_out = ev_w_out[i].astype(BF16)
            acts, ws = [o_a, o_b], [w_out[:MOBA_W], w_out[MOBA_W:]]
        else:
            n_main = 4 * GDN_W
            w_in = jnp.concatenate([od_w_in[i][:, :n_main], _pad_cols(od_w_in[i][:, n_main:], LANES)], axis=1)
            proj = _mod_matmul(x, sc1, sh1, w_in.astype(BF16))
            pvec = jnp.zeros((2, LANES), F32)
            pvec = pvec.at[0, GDN_HEADS:2 * GDN_HEADS].set(od_a_log[i])
            pvec = pvec.at[1, GDN_HEADS:2 * GDN_HEADS].set(od_dt_bias[i])
            o = _gdn(proj, od_conv_w[i], pvec, od_norm[i].reshape(1, -1))
            acts, ws = [o], [od_w_out[i].astype(BF16)]

        x, h2, rl = _mixout(acts, ws, x, g1, ln_mix_g[layer].reshape(1, d), ln_mix_b[layer].reshape(1, d),
                            sc2, sh2, moe_router_w[layer].T)
        router_tn = min(1024, s)
        gates, gates_t, cnt_raw = _router(rl, moe_router_b[layer], router_tn)
        x = _moe(h2, gates, gates_t, cnt_raw, router_tn, layer, moe_w_gate, moe_w_up, moe_w_down,
                 sh_w_gate[layer].astype(BF16), sh_w_up[layer].astype(BF16),
                 sh_w_down[layer].astype(BF16), x, g2, ln_ffn_g[layer].reshape(1, d), ln_ffn_b[layer].reshape(1, d))
    return x
```

```python
import functools
import math

import numpy as np
import jax
import jax.numpy as jnp
from jax import lax
from jax.experimental import pallas as pl
from jax.experimental.pallas import tpu as pltpu

F32 = jnp.float32
BF16 = jnp.bfloat16
HI = lax.Precision.HIGHEST
NT = (((1,), (1,)), ((), ()))
TN = (((0,), (0,)), ((), ()))
NEG = -1e30
LOG2E = math.log2(math.e)

LANES = 128
VMEM_LIMIT = 56 * 1024 * 1024

DEPTH = 2
MOBA_HEAD_DIM = 128
MOBA_HEADS = 4
MOBA_BLOCK = 256
MOBA_TOPK = 3
MOBA_GROUP = 8
MOBA_SUB = 4
GLA_DV = 128
GLA_HEADS = 4
GLA_DK = 64
GLA_GATE_RANK = 16
GLA_GATE_NORM = 16.0
GLA_CHUNK = 64
GDN_DK = 128
GDN_DV = 128
GDN_HEADS = 8
GDN_CONV = 4
GDN_CHUNK = 64
GDN_GROUP = 4
RPE_BUCKETS = 32
RPE_MAX_DIST = 2048
RPE_TILES = 8
N_EXPERTS = 64
TOP_K = 6
N_GROUPS = 8
TOPK_GROUPS = 4
D_EXPERT = 256
ROUTED_SCALE = 2.5
DEEPNORM_ALPHA = float((2 * DEPTH) ** 0.25)
LN_EPS = 1e-5
NORM_EPS = 1e-6

MOBA_W = MOBA_HEADS * MOBA_HEAD_DIM
GLA_QK_W = GLA_HEADS * GLA_DK
GLA_V_W = GLA_HEADS * GLA_DV
GDN_W = GDN_HEADS * GDN_DK


def _cparams(sem):
    return pltpu.CompilerParams(dimension_semantics=sem, vmem_limit_bytes=VMEM_LIMIT)


def _sigmoid(x):
    return 1.0 / (1.0 + jnp.exp(-x))


def _silu(x):
    return x * _sigmoid(x)


def _softplus(x):
    return jnp.maximum(x, 0.0) + jnp.log(1.0 + jnp.exp(-jnp.abs(x)))


def _layer_norm(z, g, b):
    mu = jnp.mean(z, axis=-1, keepdims=True)
    zc = z - mu
    var = jnp.mean(zc * zc, axis=-1, keepdims=True)
    return zc * lax.rsqrt(var + LN_EPS) * g + b


def _ada_kernel(c_ref, w_ref, b_ref, o_ref):
    ca = _silu(c_ref[...])
    o_ref[0] = jnp.dot(ca, w_ref[0], precision=HI, preferred_element_type=F32) + b_ref[0]


def _ada(c, ada_w, ada_b):
    depth, d, n = ada_w.shape
    bsz = c.shape[0]
    tn = 6 * LANES
    return pl.pallas_call(
        _ada_kernel,
        grid=(depth, n // tn),
        in_specs=[pl.BlockSpec((bsz, d), lambda l, j: (0, 0)),
                  pl.BlockSpec((1, d, tn), lambda l, j: (l, 0, j)),
                  pl.BlockSpec((1, 1, tn), lambda l, j: (l, 0, j))],
        out_specs=pl.BlockSpec((1, bsz, tn), lambda l, j: (l, 0, j)),
        out_shape=jax.ShapeDtypeStruct((depth, bsz, n), F32),
        compiler_params=_cparams(("arbitrary", "arbitrary")),
        name="ada_mod",
    )(c, ada_w, ada_b.reshape(depth, 1, n))


def _modmm_kernel(x_ref, sc_ref, sh_ref, w_ref, o_ref, h_scr):
    @pl.when(pl.program_id(2) == 0)
    def _():
        h_scr[...] = (x_ref[0] * (1.0 + sc_ref[0]) + sh_ref[0]).astype(BF16)

    o_ref[0] = jnp.dot(h_scr[...], w_ref[...], preferred_element_type=F32).astype(o_ref.dtype)


def _col_tile(n, cap):
    best = LANES
    for t in range(LANES, cap + 1, LANES):
        if n % t == 0:
            best = t
    return best


def _mod_matmul(x, sc, sh, w, tm=1024, tn_cap=768):
    bsz, s, d = x.shape
    n = w.shape[1]
    tm = min(tm, s)
    tn = _col_tile(n, tn_cap)
    return pl.pallas_call(
        _modmm_kernel,
        grid=(bsz, s // tm, n // tn),
        in_specs=[pl.BlockSpec((1, tm, d), lambda b, i, j: (b, i, 0)),
                  pl.BlockSpec((1, 1, d), lambda b, i, j: (b, 0, 0)),
                  pl.BlockSpec((1, 1, d), lambda b, i, j: (b, 0, 0)),
                  pl.BlockSpec((d, tn), lambda b, i, j: (0, j))],
        out_specs=pl.BlockSpec((1, tm, tn), lambda b, i, j: (b, i, j)),
        out_shape=jax.ShapeDtypeStruct((bsz, s, n), BF16),
        scratch_shapes=[pltpu.VMEM((tm, d), BF16)],
        compiler_params=_cparams(("arbitrary", "arbitrary", "arbitrary")),
        name="mod_matmul",
    )(x, sc, sh, w)


def _rpe_lower_bounds():
    exact = RPE_BUCKETS // 2
    d = np.arange(0, 2 * RPE_MAX_DIST, dtype=np.int64)
    logd = np.log(np.maximum(d, 1).astype(np.float64) / exact)
    large = exact + (logd / math.log(RPE_MAX_DIST / exact) * (RPE_BUCKETS - exact)).astype(np.int64)
    large = np.minimum(large, RPE_BUCKETS - 1)
    bucket = np.where(d < exact, d, large)
    return [int(np.argmax(bucket >= k)) for k in range(RPE_BUCKETS)]


def _rpe_tiles_kernel(lo, rpe_ref, o_ref):
    h = pl.program_id(0)
    j = pl.program_id(1)
    blk = o_ref.shape[-1]
    key = lax.broadcasted_iota(jnp.int32, (blk, blk), 0)
    qry = lax.broadcasted_iota(jnp.int32, (blk, blk), 1)
    dist = j * blk + qry - key
    val = jnp.full((blk, blk), rpe_ref[0, h], F32)
    for k in range(1, RPE_BUCKETS):
        val = jnp.where(dist >= lo[k], rpe_ref[k, h], val)
    o_ref[0, 0] = jnp.where(dist >= 0, val * LOG2E, NEG)


def _rpe_tiles(rpe_bias):
    heads = rpe_bias.shape[1]
    lo = _rpe_lower_bounds()
    assert lo[-1] <= (RPE_TILES - 1) * MOBA_BLOCK - (MOBA_BLOCK - 1)
    return pl.pallas_call(
        functools.partial(_rpe_tiles_kernel, lo),
        grid=(heads, RPE_TILES),
        in_specs=[pl.BlockSpec(memory_space=pltpu.SMEM)],
        out_specs=pl.BlockSpec((1, 1, MOBA_BLOCK, MOBA_BLOCK), lambda h, j: (h, j, 0, 0)),
        out_shape=jax.ShapeDtypeStruct((heads, RPE_TILES, MOBA_BLOCK, MOBA_BLOCK), F32),
        compiler_params=_cparams(("arbitrary", "arbitrary")),
        name="rpe_tiles",
    )(rpe_bias)


def _moba_kernel(q_ref, k_ref, v_ref, t_ref, o_ref, kb_scr, vt_scr, km_scr, sel_scr, far_scr):
    i = pl.program_id(2)
    nkb, blk, dh = kb_scr.shape

    @pl.when(i == 0)
    def _():
        for n in range(nkb):
            kn = k_ref[0, n * blk:(n + 1) * blk, :]
            kb_scr[n] = kn.astype(BF16)
            km_scr[n:n + 1, :] = jnp.mean(kn.astype(F32), axis=0, keepdims=True)
            vt_scr[n] = v_ref[0, n * blk:(n + 1) * blk, :].astype(F32).T.astype(BF16)

    q = q_ref[0].astype(F32)
    gate = lax.dot_general(km_scr[...], q, NT, precision=HI, preferred_element_type=F32)
    bidx = lax.broadcasted_iota(jnp.int32, gate.shape, 0)
    past = bidx < i
    g = jnp.where(past, gate, -jnp.inf)
    sel = None
    for _ in range(MOBA_TOPK):
        m = jnp.max(g, axis=0, keepdims=True)
        first = jnp.min(jnp.where(g == m, bidx, nkb), axis=0, keepdims=True)
        hit = bidx == first
        sel = hit if sel is None else jnp.logical_or(sel, hit)
        g = jnp.where(hit, -jnp.inf, g)
    mask = jnp.where(jnp.logical_and(sel, past), 0.0, NEG)
    sel_scr[...] = mask
    far_bias = t_ref[0, RPE_TILES - 1, 0:1, :]
    far_scr[...] = mask + jnp.where(i - bidx >= RPE_TILES - 1, far_bias, 0.0)

    qs = (q * (dh ** -0.5 * LOG2E)).astype(BF16)

    s = lax.dot_general(kb_scr[i], qs, NT, preferred_element_type=F32) + t_ref[0, 0]
    m0 = jnp.max(s, axis=0, keepdims=True)
    p = jnp.exp2(s - m0)
    l0 = jnp.sum(p, axis=0, keepdims=True)
    acc0 = jnp.dot(vt_scr[i], p.astype(BF16), preferred_element_type=F32)

    def body(far, g, carry):
        m, l, acc = carry
        nsub = MOBA_GROUP // MOBA_SUB
        blocks = [[jnp.minimum(g * MOBA_GROUP + k * MOBA_SUB + u, nkb - 1) for u in range(MOBA_SUB)]
                  for k in range(nsub)]
        scores, probs, alphas = {}, {}, {}

        def emit_scores(k):
            out = []
            for n in blocks[k]:
                s = lax.dot_general(kb_scr[n], qs, NT, preferred_element_type=F32)
                if far:
                    out.append(s + far_scr[pl.ds(n, 1), :])
                else:
                    out.append(s + t_ref[0, jnp.clip(i - n, 0, RPE_TILES - 1)] + sel_scr[pl.ds(n, 1), :])
            scores[k] = out

        def emit_softmax(k, m, l):
            m_new = m
            for s in scores[k]:
                m_new = jnp.maximum(m_new, jnp.max(s, axis=0, keepdims=True))
            alphas[k] = jnp.exp2(m - m_new)
            l = alphas[k] * l
            probs[k] = []
            for s in scores[k]:
                p = jnp.exp2(s - m_new)
                l = l + jnp.sum(p, axis=0, keepdims=True)
                probs[k].append(p.astype(BF16))
            return m_new, l

        def emit_values(k, acc):
            acc = alphas[k] * acc
            for n, p in zip(blocks[k], probs[k]):
                acc = acc + jnp.dot(vt_scr[n], p, preferred_element_type=F32)
            return acc

        emit_scores(0)
        for k in range(nsub):
            if k + 1 < nsub:
                emit_scores(k + 1)
            if k >= 1:
                acc = emit_values(k - 1, acc)
            m, l = emit_softmax(k, m, l)
        acc = emit_values(nsub - 1, acc)
        return m, l, acc

    n_far = jnp.maximum(i - (RPE_TILES - 2), 0) // MOBA_GROUP
    n_all = (i + MOBA_GROUP - 1) // MOBA_GROUP
    carry = lax.fori_loop(0, n_far, functools.partial(body, True), (m0, l0, acc0))
    _, l, acc = lax.fori_loop(n_far, n_all, functools.partial(body, False), carry)
    o_ref[0] = (acc / l).T.astype(o_ref.dtype)


def _moba(proj, tiles, q_col, k_col, v_col):
    bsz, s, _ = proj.shape
    dh, blk, heads = MOBA_HEAD_DIM, MOBA_BLOCK, MOBA_HEADS
    nkb = s // blk
    return pl.pallas_call(
        _moba_kernel,
        grid=(bsz, heads, nkb),
        in_specs=[pl.BlockSpec((1, blk, dh), lambda b, h, i: (b, i, q_col + h)),
                  pl.BlockSpec((1, s, dh), lambda b, h, i: (b, 0, k_col + h)),
                  pl.BlockSpec((1, s, dh), lambda b, h, i: (b, 0, v_col + h)),
                  pl.BlockSpec((1, RPE_TILES, blk, blk), lambda b, h, i: (h, 0, 0, 0))],
        out_specs=pl.BlockSpec((1, blk, dh), lambda b, h, i: (b, i, h)),
        out_shape=jax.ShapeDtypeStruct((bsz, s, heads * dh), BF16),
        scratch_shapes=[pltpu.VMEM((nkb, blk, dh), BF16),
                        pltpu.VMEM((nkb, dh, blk), BF16),
                        pltpu.VMEM((nkb, dh), F32),
                        pltpu.VMEM((nkb, blk), F32),
                        pltpu.VMEM((nkb, blk), F32)],
        compiler_params=_cparams(("arbitrary", "arbitrary", "arbitrary")),
        name="moba_attention",
    )(proj, proj, proj, tiles)


def _gla_kernel(q_ref, k_ref, v_ref, gg_ref, glr_ref, w2_ref, gb_ref, nw_ref, o_ref, st_scr):
    @pl.when(pl.program_id(1) == 0)
    def _():
        st_scr[...] = jnp.zeros(st_scr.shape, F32)

    tg = q_ref.shape[1]
    c = GLA_CHUNK
    x = jnp.dot(glr_ref[0].astype(F32), w2_ref[...], precision=HI, preferred_element_type=F32) + gb_ref[...]
    lg = -_softplus(-x) * (1.0 / GLA_GATE_NORM)
    row = lax.broadcasted_iota(jnp.int32, (c, c), 0)
    col = lax.broadcasted_iota(jnp.int32, (c, c), 1)
    incl = row >= col
    tri = incl.astype(F32)
    nw = nw_ref[...]
    for ci in range(tg // c):
        rows = slice(ci * c, (ci + 1) * c)
        b = jnp.dot(tri, lg[rows], precision=HI, preferred_element_type=F32)
        bl = b[c - 1:c, :]
        q = q_ref[0, rows, :].astype(F32) * GLA_DK ** -0.5
        k = k_ref[0, rows, :].astype(F32)
        q_e = (q * jnp.exp(b)).astype(BF16)
        k_e = (k * jnp.exp(-b)).astype(BF16)
        k_end = (k * jnp.exp(bl - b)).astype(BF16)
        d = jnp.exp(bl)
        for h in range(GLA_HEADS):
            ks = slice(h * GLA_DK, (h + 1) * GLA_DK)
            vs = slice(h * GLA_DV, (h + 1) * GLA_DV)
            vh = v_ref[0, rows, vs].astype(BF16)
            a = lax.dot_general(q_e[:, ks], k_e[:, ks], NT, preferred_element_type=F32)
            a = jnp.where(incl, a, 0.0).astype(BF16)
            st = st_scr[h]
            o = (jnp.dot(a, vh, preferred_element_type=F32)
                 + lax.dot_general(q_e[:, ks], st.astype(BF16), NT, preferred_element_type=F32))
            st_scr[h] = st * d[:, ks] + lax.dot_general(vh, k_end[:, ks], TN, preferred_element_type=F32)
            o = o * lax.rsqrt(jnp.mean(o * o, axis=-1, keepdims=True) + NORM_EPS)
            o_ref[0, rows, vs] = (o * nw * _silu(gg_ref[0, rows, vs].astype(F32))).astype(o_ref.dtype)


def _gla(proj, gk_w2p, gk_b, o_norm, q_col, k_col, v_col, g_col, r_col, tg=256):
    bsz, s, _ = proj.shape
    tg = min(tg, s)
    qk, vw = GLA_QK_W, GLA_V_W
    return pl.pallas_call(
        _gla_kernel,
        grid=(bsz, s // tg),
        in_specs=[pl.BlockSpec((1, tg, qk), lambda b, t: (b, t, q_col)),
                  pl.BlockSpec((1, tg, qk), lambda b, t: (b, t, k_col)),
                  pl.BlockSpec((1, tg, vw), lambda b, t: (b, t, v_col)),
                  pl.BlockSpec((1, tg, vw), lambda b, t: (b, t, g_col)),
                  pl.BlockSpec((1, tg, LANES), lambda b, t: (b, t, r_col)),
                  pl.BlockSpec((LANES, qk), lambda b, t: (0, 0)),
                  pl.BlockSpec((1, qk), lambda b, t: (0, 0)),
                  pl.BlockSpec((1, GLA_DV), lambda b, t: (0, 0))],
        out_specs=pl.BlockSpec((1, tg, vw), lambda b, t: (b, t, 0)),
        out_shape=jax.ShapeDtypeStruct((bsz, s, vw), BF16),
        scratch_shapes=[pltpu.VMEM((GLA_HEADS, GLA_DV, GLA_DK), F32)],
        compiler_params=_cparams(("arbitrary", "arbitrary")),
        name="gla_mixer",
    )(proj, proj, proj, proj, proj, gk_w2p, gk_b, o_norm)


def _gdn_kernel(qkv_ref, gate_ref, ba_ref, cw_ref, pv_ref, nw_ref, o_ref, tail_scr, s_scr):
    @pl.when(pl.program_id(1) == 0)
    def _():
        tail_scr[...] = jnp.zeros(tail_scr.shape, F32)
        s_scr[...] = jnp.zeros(s_scr.shape, F32)

    tg = qkv_ref.shape[1]
    c, dk, grp = GDN_CHUNK, GDN_DK, GDN_GROUP
    gr = grp * c
    w = GDN_W

    x = qkv_ref[0].astype(F32)
    tail = tail_scr[...]
    tail_scr[...] = x[tg - 8:, :]
    r8 = lax.broadcasted_iota(jnp.int32, (8, 1), 0)
    y = x * cw_ref[GDN_CONV - 1:GDN_CONV, :]
    for sft in range(1, GDN_CONV):
        xs = pltpu.roll(x, sft, axis=0)
        head = jnp.where(r8 < sft, pltpu.roll(tail, sft, axis=0), xs[:8, :])
        xs = jnp.concatenate([head, xs[8:, :]], axis=0)
        y = y + xs * cw_ref[GDN_CONV - 1 - sft:GDN_CONV - sft, :]
    y = _silu(y)

    ba = ba_ref[0].astype(F32)
    beta_t = _sigmoid(ba)
    g_t = -jnp.exp(pv_ref[0:1, :]) * _softplus(ba + pv_ref[1:2, :])

    row = lax.broadcasted_iota(jnp.int32, (c, c), 0)
    col = lax.broadcasted_iota(jnp.int32, (c, c), 1)
    tri = (row >= col).astype(F32)
    rr = lax.broadcasted_iota(jnp.int32, (gr, gr), 0)
    cc = lax.broadcasted_iota(jnp.int32, (gr, gr), 1)
    same = (rr // c) == (cc // c)
    incl = jnp.logical_and(same, rr >= cc)
    strict = jnp.logical_and(same, rr > cc)
    eye = (rr == cc).astype(F32)
    halves = [(rr // sz) == (cc // sz) for sz in (2 ** e for e in range(1, int(math.log2(c)) + 1))]
    nw = nw_ref[...]

    probs = []
    for ci in range(tg // c):
        rows = slice(ci * c, (ci + 1) * c)
        gcum = jnp.dot(tri, g_t[rows], precision=HI, preferred_element_type=F32)
        gcum_t = gcum.T
        for gi in range(GDN_HEADS // grp):
            hs = [gi * grp + u for u in range(grp)]

            def stack(a, off):
                return jnp.concatenate([a[rows, off + h * dk: off + (h + 1) * dk] for h in hs], axis=0)

            q = stack(y, 0)
            k = stack(y, w)
            v = stack(y, 2 * w)
            q = q * lax.rsqrt(jnp.sum(q * q, axis=-1, keepdims=True) + NORM_EPS) * dk ** -0.5
            k = k * lax.rsqrt(jnp.sum(k * k, axis=-1, keepdims=True) + NORM_EPS)
            beta = jnp.concatenate([beta_t[rows, h:h + 1] for h in hs], axis=0)
            gc = jnp.concatenate([gcum[:, GDN_HEADS + h:GDN_HEADS + h + 1] for h in hs], axis=0)
            gc_row = jnp.concatenate([gcum_t[GDN_HEADS + h:GDN_HEADS + h + 1, :] for h in hs], axis=1)
            gl = jnp.concatenate([jnp.broadcast_to(gcum[c - 1:c, GDN_HEADS + h:GDN_HEADS + h + 1], (c, 1))
                                  for h in hs], axis=0)

            decay = jnp.where(incl, jnp.exp(jnp.where(incl, gc - gc_row, 0.0)), 0.0)
            kb = k * beta
            k16 = k.astype(BF16)
            a = lax.dot_general(kb.astype(BF16), k16, NT, preferred_element_type=F32)
            a = jnp.where(strict, a * decay, 0.0)
            eg = jnp.exp(gc)
            probs.append(dict(
                rows=rows, hs=hs, a16=a.astype(BF16),
                t=eye - jnp.where(halves[0], a, 0.0),
                rhs=jnp.concatenate([v * beta, kb * eg], axis=1).astype(BF16),
                attn=(lax.dot_general(q.astype(BF16), k16, NT, preferred_element_type=F32) * decay).astype(BF16),
                q_g=(q * eg).astype(BF16),
                k_end=(k * jnp.exp(gl - gc)).astype(BF16),
                d_last=[jnp.exp(gcum[c - 1:c, GDN_HEADS + h:GDN_HEADS + h + 1]) for h in hs]))

    for lvl in range(1, len(halves)):
        off16 = jnp.where(jnp.logical_and(halves[lvl], jnp.logical_not(halves[lvl - 1])), 1.0, 0.0).astype(BF16)
        t16s = [p["t"].astype(BF16) for p in probs]
        xs = [jnp.dot(p["a16"] * off16, t16, preferred_element_type=F32).astype(BF16)
              for p, t16 in zip(probs, t16s)]
        for p, t16, x in zip(probs, t16s, xs):
            p["t"] = p["t"] - jnp.dot(t16, x, preferred_element_type=F32)
    for p in probs:
        wk = jnp.dot(p["t"].astype(BF16), p["rhs"], preferred_element_type=F32)
        p["w_val"] = wk[:, :GDN_DV]
        p["k_cum"] = wk[:, GDN_DV:].astype(BF16)

    for p in probs:
        rows, hs = p["rows"], p["hs"]
        v_new = []
        for u, h in enumerate(hs):
            hr = slice(u * c, (u + 1) * c)
            st16 = s_scr[h].astype(BF16)
            v_new.append(p["w_val"][hr] - jnp.dot(p["k_cum"][hr], st16, preferred_element_type=F32))
        v16 = jnp.concatenate(v_new, axis=0).astype(BF16)
        o_intra = jnp.dot(p["attn"], v16, preferred_element_type=F32)
        for u, h in enumerate(hs):
            hr = slice(u * c, (u + 1) * c)
            st = s_scr[h]
            o = o_intra[hr] + jnp.dot(p["q_g"][hr], st.astype(BF16), preferred_element_type=F32)
            s_scr[h] = (st * p["d_last"][u]
                        + lax.dot_general(p["k_end"][hr], v16[hr], TN, preferred_element_type=F32))
            o = o * lax.rsqrt(jnp.mean(o * o, axis=-1, keepdims=True) + NORM_EPS)
            cs = slice(h * GDN_DV, (h + 1) * GDN_DV)
            o_ref[0, rows, cs] = (o * nw * _silu(gate_ref[0, rows, cs].astype(F32))).astype(o_ref.dtype)


def _gdn(proj, conv_w, pvec, o_norm, tg=256):
    bsz, s, _ = proj.shape
    tg = min(tg, s)
    w = GDN_W
    return pl.pallas_call(
        _gdn_kernel,
        grid=(bsz, s // tg),
        in_specs=[pl.BlockSpec((1, tg, 3 * w), lambda b, t: (b, t, 0)),
                  pl.BlockSpec((1, tg, w), lambda b, t: (b, t, 3)),
                  pl.BlockSpec((1, tg, LANES), lambda b, t: (b, t, 4 * w // LANES)),
                  pl.BlockSpec((GDN_CONV, 3 * w), lambda b, t: (0, 0)),
                  pl.BlockSpec((2, LANES), lambda b, t: (0, 0)),
                  pl.BlockSpec((1, GDN_DV), lambda b, t: (0, 0))],
        out_specs=pl.BlockSpec((1, tg, w), lambda b, t: (b, t, 0)),
        out_shape=jax.ShapeDtypeStruct((bsz, s, w), BF16),
        scratch_shapes=[pltpu.VMEM((8, 3 * w), F32),
                        pltpu.VMEM((GDN_HEADS, GDN_DK, GDN_DV), F32)],
        compiler_params=_cparams(("arbitrary", "arbitrary")),
        name="gdn_mixer",
    )(proj, proj, proj, conv_w, pvec, o_norm)


def _mixout_kernel(n_act, *refs):
    acts = refs[:n_act]
    ws = refs[n_act:2 * n_act]
    x_ref, g1_ref, lng_ref, lnb_ref, sc_ref, sh_ref, rw_ref = refs[2 * n_act:2 * n_act + 7]
    xo_ref, h_ref, rl_ref = refs[2 * n_act + 7:]
    y = None
    for a_ref, w_ref in zip(acts, ws):
        t = jnp.dot(a_ref[0].astype(BF16), w_ref[...], preferred_element_type=F32)
        y = t if y is None else y + t
    xn = _layer_norm(DEEPNORM_ALPHA * x_ref[0] + g1_ref[0] * y, lng_ref[...], lnb_ref[...])
    xo_ref[0] = xn
    h = xn * (1.0 + sc_ref[0]) + sh_ref[0]
    h_ref[0] = h.astype(BF16)
    rl_ref[0] = lax.dot_general(rw_ref[...], h, NT, precision=HI, preferred_element_type=F32)


def _mixout(acts, ws, x, g1, ln_g, ln_b, sc2, sh2, router_wt, tm=512):
    bsz, s, d = x.shape
    tm = min(tm, s)
    n_act = len(acts)
    ne = router_wt.shape[0]
    vec = pl.BlockSpec((1, 1, d), lambda b, i: (b, 0, 0))
    par = pl.BlockSpec((1, d), lambda b, i: (0, 0))
    in_specs = ([pl.BlockSpec((1, tm, a.shape[-1]), lambda b, i: (b, i, 0)) for a in acts]
                + [pl.BlockSpec(w.shape, lambda b, i: (0, 0)) for w in ws]
                + [pl.BlockSpec((1, tm, d), lambda b, i: (b, i, 0)), vec, par, par, vec, vec,
                   pl.BlockSpec((ne, d), lambda b, i: (0, 0))])
    return pl.pallas_call(
        functools.partial(_mixout_kernel, n_act),
        grid=(bsz, s // tm),
        in_specs=in_specs,
        out_specs=[pl.BlockSpec((1, tm, d), lambda b, i: (b, i, 0)),
                   pl.BlockSpec((1, tm, d), lambda b, i: (b, i, 0)),
                   pl.BlockSpec((1, ne, tm), lambda b, i: (b, 0, i))],
        out_shape=[jax.ShapeDtypeStruct((bsz, s, d), F32),
                   jax.ShapeDtypeStruct((bsz, s, d), BF16),
                   jax.ShapeDtypeStruct((bsz, ne, s), F32)],
        compiler_params=_cparams(("arbitrary", "arbitrary")),
        name="mix_out",
    )(*acts, *ws, x, g1, ln_g, ln_b, sc2, sh2, router_wt)


def _first_max(vals, idx, axis, sentinel):
    m = jnp.max(vals, axis=axis, keepdims=True)
    first = jnp.min(jnp.where(vals == m, idx, sentinel), axis=axis, keepdims=True)
    return m, idx == first


def _router_kernel(rl_ref, rb_ref, g_ref, gt_ref, cnt_ref):
    ne, tn = rl_ref.shape[1], rl_ref.shape[2]
    gsz = ne // N_GROUPS
    scores = _sigmoid(rl_ref[0])
    sel = scores + rb_ref[...]
    ridx = lax.broadcasted_iota(jnp.int32, (gsz, tn), 0)
    gidx = lax.broadcasted_iota(jnp.int32, (N_GROUPS, tn), 0)
    gs = jnp.zeros((N_GROUPS, tn), F32)
    for g in range(N_GROUPS):
        sg = sel[g * gsz:(g + 1) * gsz, :]
        m1, hit = _first_max(sg, ridx, 0, gsz)
        m2 = jnp.max(jnp.where(hit, -jnp.inf, sg), axis=0, keepdims=True)
        gs = jnp.where(gidx == g, m1 + m2, gs)
    gsel = None
    for _ in range(TOPK_GROUPS):
        _, hit = _first_max(gs, gidx, 0, N_GROUPS)
        gsel = hit if gsel is None else jnp.logical_or(gsel, hit)
        gs = jnp.where(hit, -jnp.inf, gs)
    gself = gsel.astype(F32)
    emask = jnp.concatenate([jnp.broadcast_to(gself[g:g + 1, :], (gsz, tn)) for g in range(N_GROUPS)], axis=0)
    cand = jnp.where(emask > 0.5, sel, -jnp.inf)
    eidx = lax.broadcasted_iota(jnp.int32, cand.shape, 0)
    chosen = None
    for _ in range(TOP_K):
        _, hit = _first_max(cand, eidx, 0, ne)
        chosen = hit if chosen is None else jnp.logical_or(chosen, hit)
        cand = jnp.where(hit, -jnp.inf, cand)
    wsel = jnp.where(chosen, scores, 0.0)
    tot = jnp.sum(wsel, axis=0, keepdims=True)
    gates = wsel / (tot + 1e-20) * ROUTED_SCALE
    g_ref[0] = gates.T
    gt_ref[0] = gates
    tile_of = lax.broadcasted_iota(jnp.int32, (tn, LANES), 0) // MOE_TM
    ind = (tile_of == lax.broadcasted_iota(jnp.int32, (tn, LANES), 1)).astype(BF16)
    routed = jnp.where(gates > 0.0, 1.0, 0.0).astype(BF16)
    cnt_ref[0, 0] = jnp.dot(routed, ind, preferred_element_type=F32)


def _router(rl, router_b, tn=1024):
    bsz, ne, s = rl.shape
    tn = min(tn, s)
    return pl.pallas_call(
        _router_kernel,
        grid=(bsz, s // tn),
        in_specs=[pl.BlockSpec((1, ne, tn), lambda b, i: (b, 0, i)),
                  pl.BlockSpec((ne, 1), lambda b, i: (0, 0))],
        out_specs=[pl.BlockSpec((1, tn, ne), lambda b, i: (b, i, 0)),
                   pl.BlockSpec((1, ne, tn), lambda b, i: (b, 0, i)),
                   pl.BlockSpec((1, 1, ne, LANES), lambda b, i: (b, i, 0, 0))],
        out_shape=[jax.ShapeDtypeStruct((bsz, s, ne), F32),
                   jax.ShapeDtypeStruct((bsz, ne, s), F32),
                   jax.ShapeDtypeStruct((bsz, s // tn, ne, LANES), F32)],
        compiler_params=_cparams(("arbitrary", "arbitrary")),
        name="moe_router",
    )(rl, router_b.reshape(ne, 1))


MOE_TM = 256
MOE_ALIGN = 16
MOE_R = 512
MOE_LC = 512
MOE_LMAX = -(-(MOE_TM * TOP_K + N_EXPERTS * (MOE_ALIGN - 1)) // MOE_LC) * MOE_LC
MOE_NP = MOE_LMAX // MOE_ALIGN
POS_SPLIT = 256.0


def _ffn(x, wg, wu, wd):
    a = jnp.dot(x, wg, preferred_element_type=F32)
    u = jnp.dot(x, wu, preferred_element_type=F32)
    return jnp.dot((_silu(a) * u).astype(BF16), wd, preferred_element_type=F32)


def _split_pos(pos, axis):
    hi = jnp.floor(pos * (1.0 / POS_SPLIT)) * POS_SPLIT
    return jnp.concatenate([hi, pos - hi], axis=axis).astype(BF16)


def _for_each_piece(n_pieces, fn):
    def body(p, carry):
        fn(p)
        return carry

    lax.fori_loop(0, n_pieces, body, 0)


def _dispatch_kernel(dst_s, np_s, h_ref, gt_ref, offc_ref, offr_ref, cntr_ref, xs_hbm, sorted_scr, sems):
    i = pl.program_id(0)
    last = pl.num_programs(0) - 1
    slot = i % 2
    tm = h_ref.shape[0]
    routed = gt_ref[0] > 0.0
    t0 = lax.broadcasted_iota(jnp.int32, (tm, tm), 0)
    t1 = lax.broadcasted_iota(jnp.int32, (tm, tm), 1)
    earlier = jnp.where(t0 < t1, 1.0, 0.0).astype(BF16)
    rank_t = jnp.dot(jnp.where(routed, 1.0, 0.0).astype(BF16), earlier, preferred_element_type=F32)
    pos_t = jnp.where(routed, rank_t + offc_ref[0] + 1.0, 0.0)
    pos2 = _split_pos(pos_t, 0)
    x = h_ref[...]
    offr = offr_ref[0]
    endr = offr + cntr_ref[0]
    for c in range(MOE_LMAX // MOE_LC):
        r = (c * MOE_LC + lax.broadcasted_iota(jnp.int32, (MOE_LC, 1), 0)).astype(F32)
        owner = jnp.where(jnp.logical_and(r >= offr, r < endr), 1.0, 0.0).astype(BF16)
        possel = jnp.dot(jnp.concatenate([owner, owner], axis=1), pos2, preferred_element_type=F32)
        perm = jnp.where(possel == r + 1.0, 1.0, 0.0).astype(BF16)
        sorted_scr[slot, c * MOE_LC:(c + 1) * MOE_LC, :] = (
            jnp.dot(perm, x, preferred_element_type=F32).astype(BF16))

    def piece(sl, tile, p):
        src = sorted_scr.at[sl, pl.ds(pl.multiple_of(p * MOE_ALIGN, MOE_ALIGN), MOE_ALIGN)]
        dst = xs_hbm.at[pl.ds(pl.multiple_of(dst_s[tile * MOE_NP + p], MOE_ALIGN), MOE_ALIGN)]
        return pltpu.make_async_copy(src, dst, sems.at[sl])

    _for_each_piece(np_s[i], lambda p: piece(slot, i, p).start())

    @pl.when(i > 0)
    def _():
        _for_each_piece(np_s[i - 1], lambda p: piece(1 - slot, i - 1, p).wait())

    @pl.when(i == last)
    def _():
        _for_each_piece(np_s[i], lambda p: piece(slot, i, p).wait())


def _expert_ffn_kernel(be_s, nu_s, x_ref, wg_ref, wu_ref, wd_ref, y_ref):
    @pl.when(pl.program_id(0) < nu_s[0])
    def _():
        y_ref[...] = _ffn(x_ref[...], wg_ref[0, 0].astype(BF16), wu_ref[0, 0].astype(BF16),
                          wd_ref[0, 0].astype(BF16)).astype(BF16)


def _combine_kernel(dst_s, np_s, ys_hbm, g_ref, h_ref, offr_ref, offc_ref, cntc_ref,
                    sg_ref, su_ref, sd_ref, x_ref, g2_ref, lng_ref, lnb_ref, o_ref, ys_scr, sems):
    i = pl.program_id(0)
    last = pl.num_programs(0) - 1
    slot = i % 2
    tm = h_ref.shape[0]

    def piece(sl, tile, p):
        src = ys_hbm.at[pl.ds(pl.multiple_of(dst_s[tile * MOE_NP + p], MOE_ALIGN), MOE_ALIGN)]
        dst = ys_scr.at[sl, pl.ds(pl.multiple_of(p * MOE_ALIGN, MOE_ALIGN), MOE_ALIGN)]
        return pltpu.make_async_copy(src, dst, sems.at[sl])

    @pl.when(i == 0)
    def _():
        ys_scr[...] = jnp.zeros(ys_scr.shape, BF16)
        _for_each_piece(np_s[0], lambda p: piece(0, 0, p).start())

    @pl.when(i < last)
    def _():
        _for_each_piece(np_s[i + 1], lambda p: piece(1 - slot, i + 1, p).start())

    acc = _ffn(h_ref[...], sg_ref[...], su_ref[...], sd_ref[...])
    g = g_ref[...]
    routed = g > 0.0
    t0 = lax.broadcasted_iota(jnp.int32, (tm, tm), 0)
    t1 = lax.broadcasted_iota(jnp.int32, (tm, tm), 1)
    earlier = jnp.where(t0 > t1, 1.0, 0.0).astype(BF16)
    rank = jnp.dot(earlier, jnp.where(routed, 1.0, 0.0).astype(BF16), preferred_element_type=F32)
    pos = jnp.where(routed, rank + offr_ref[0] + 1.0, 0.0)
    pos2 = _split_pos(pos, 1)
    g16 = g.astype(BF16)
    offc = offc_ref[0]
    endc = offc + cntc_ref[0]
    _for_each_piece(np_s[i], lambda p: piece(slot, i, p).wait())
    for c in range(MOE_LMAX // MOE_LC):
        r = (c * MOE_LC + lax.broadcasted_iota(jnp.int32, (1, MOE_LC), 1)).astype(F32)
        owner = jnp.where(jnp.logical_and(r >= offc, r < endc), 1.0, 0.0).astype(BF16)
        possel = jnp.dot(pos2, jnp.concatenate([owner, owner], axis=0), preferred_element_type=F32)
        gsel = jnp.dot(g16, owner, preferred_element_type=F32)
        w = jnp.where(possel == r + 1.0, gsel, 0.0).astype(BF16)
        acc = acc + jnp.dot(w, ys_scr[slot, c * MOE_LC:(c + 1) * MOE_LC, :], preferred_element_type=F32)
    z = DEEPNORM_ALPHA * x_ref[...] + g2_ref[0] * acc
    o_ref[...] = _layer_norm(z, lng_ref[...], lnb_ref[...])


def _ceil_to(v, m):
    return jnp.floor((v + (m - 1.0)) * (1.0 / m)) * m


def _moe_layout_kernel(cnt_ref, dst_ref, np_ref, off_ref, cntp_ref, be_ref, nu_ref):
    cnt = cnt_ref[...]
    ntiles, ne = cnt.shape
    cntp = _ceil_to(cnt, MOE_ALIGN)
    e0 = lax.broadcasted_iota(jnp.int32, (ne, ne), 0)
    e1 = lax.broadcasted_iota(jnp.int32, (ne, ne), 1)
    i0 = lax.broadcasted_iota(jnp.int32, (ntiles, ntiles), 0)
    i1 = lax.broadcasted_iota(jnp.int32, (ntiles, ntiles), 1)

    def mm(a, b):
        return jnp.dot(a, b, precision=HI, preferred_element_type=F32)

    off = mm(cntp, jnp.where(e0 < e1, 1.0, 0.0))
    before = mm(jnp.where(i0 > i1, 1.0, 0.0), cntp)
    tot = jnp.broadcast_to(jnp.sum(cntp, axis=0, keepdims=True), (8, ne))
    totr = _ceil_to(tot, MOE_R)
    base = mm(totr, jnp.where(e0 < e1, 1.0, 0.0))[0:1, :]
    delta = base + before - off
    step = mm(delta, jnp.where(e0 == e1, 1.0, 0.0) - jnp.where(e0 + 1 == e1, 1.0, 0.0))
    rowp = (lax.broadcasted_iota(jnp.int32, (1, MOE_NP), 1) * MOE_ALIGN).astype(F32)
    dst = jnp.broadcast_to(rowp, (ntiles, MOE_NP))
    for e in range(ne):
        dst = dst + jnp.where(off[:, e:e + 1] <= rowp, step[:, e:e + 1], 0.0)
    dst_ref[...] = dst.astype(jnp.int32)
    pieces = jnp.sum(cntp, axis=1, keepdims=True) * (1.0 / MOE_ALIGN)
    np_ref[...] = jnp.broadcast_to(pieces, np_ref.shape).astype(jnp.int32)
    off_ref[...] = off
    cntp_ref[...] = cntp
    tot_c = lax.dot_general(cntp, jnp.ones((ntiles, LANES), F32), TN, precision=HI, preferred_element_type=F32)
    end_c = mm(jnp.where(e0 >= e1, 1.0, 0.0), _ceil_to(tot_c, MOE_R))[:, 0:1]
    total = end_c[ne - 1:ne, :]
    first = jnp.minimum((lax.broadcasted_iota(jnp.int32, be_ref.shape, 1) * MOE_R).astype(F32), total - 1.0)
    be = jnp.sum(jnp.where(end_c <= first, 1.0, 0.0), axis=0, keepdims=True)
    be_ref[...] = jnp.minimum(be, ne - 1.0).astype(jnp.int32)
    nu_ref[...] = jnp.broadcast_to(total * (1.0 / MOE_R), nu_ref.shape).astype(jnp.int32)


def _moe_layout(cnt_raw, tn):
    ne = N_EXPERTS
    nsub = tn // MOE_TM
    cnt = jnp.transpose(cnt_raw[..., :nsub], (0, 1, 3, 2)).reshape(-1, ne)
    ntiles = cnt.shape[0]
    nblk = -(-(ntiles * (MOE_TM * TOP_K + ne * (MOE_ALIGN - 1)) + ne * (MOE_R - 1)) // MOE_R)
    nblk_pad = -(-nblk // LANES) * LANES
    dst, npieces, off, cntp, blk_exp, nused = pl.pallas_call(
        _moe_layout_kernel,
        out_shape=[jax.ShapeDtypeStruct((ntiles, MOE_NP), jnp.int32),
                   jax.ShapeDtypeStruct((ntiles, LANES), jnp.int32),
                   jax.ShapeDtypeStruct((ntiles, ne), F32),
                   jax.ShapeDtypeStruct((ntiles, ne), F32),
                   jax.ShapeDtypeStruct((1, nblk_pad), jnp.int32),
                   jax.ShapeDtypeStruct((1, LANES), jnp.int32)],
        compiler_params=pltpu.CompilerParams(vmem_limit_bytes=VMEM_LIMIT),
        name="moe_layout",
    )(cnt)
    return dst.reshape(-1), npieces[:, 0], off, cntp, nblk, nused[0, :1], blk_exp[0, :nblk]


def _moe(h2, gates, gates_t, cnt_raw, tn, layer, wg, wu, wd, sg, su, sd, x, g2, ln_g, ln_b):
    bsz, s, d = x.shape
    t = bsz * s
    tm = MOE_TM
    _, ne, _, de = wg.shape
    per_b = s // tm
    ntiles = t // tm
    dst, npieces, off_f, cnt_f, nblk, nused, blk_exp = _moe_layout(cnt_raw, tn)
    scalars = (dst, npieces)
    row = pl.BlockSpec((1, 1, ne), lambda i, *_: (i, 0, 0))
    col = pl.BlockSpec((1, ne, 1), lambda i, *_: (i, 0, 0))
    tok = pl.BlockSpec((tm, d), lambda i, *_: (i, 0))
    par = pl.BlockSpec((1, d), lambda i, *_: (0, 0))
    rows = nblk * MOE_R

    xs = pl.pallas_call(
        _dispatch_kernel,
        grid_spec=pltpu.PrefetchScalarGridSpec(
            num_scalar_prefetch=2, grid=(ntiles,),
            in_specs=[tok,
                      pl.BlockSpec((1, ne, tm), lambda i, *_: (i // per_b, 0, i % per_b)),
                      col, row, row],
            out_specs=pl.BlockSpec(memory_space=pl.ANY),
            scratch_shapes=[pltpu.VMEM((2, MOE_LMAX, d), BF16), pltpu.SemaphoreType.DMA((2,))]),
        out_shape=jax.ShapeDtypeStruct((rows, d), BF16),
        compiler_params=_cparams(("arbitrary",)),
        name="moe_dispatch",
    )(*scalars, h2.reshape(t, d), gates_t, off_f.reshape(ntiles, ne, 1), off_f.reshape(ntiles, 1, ne),
      cnt_f.reshape(ntiles, 1, ne))

    blk = pl.BlockSpec((MOE_R, d), lambda j, be, nu: (jnp.maximum(jnp.minimum(j, nu[0] - 1), 0), 0))
    ys = pl.pallas_call(
        _expert_ffn_kernel,
        grid_spec=pltpu.PrefetchScalarGridSpec(
            num_scalar_prefetch=2, grid=(nblk,),
            in_specs=[blk,
                      pl.BlockSpec((1, 1, d, de), lambda j, be, nu: (layer, be[j], 0, 0)),
                      pl.BlockSpec((1, 1, d, de), lambda j, be, nu: (layer, be[j], 0, 0)),
                      pl.BlockSpec((1, 1, de, d), lambda j, be, nu: (layer, be[j], 0, 0))],
            out_specs=blk),
        out_shape=jax.ShapeDtypeStruct((rows, d), BF16),
        compiler_params=_cparams(("arbitrary",)),
        name="moe_expert_ffn",
    )(blk_exp, nused, xs, wg, wu, wd)

    out = pl.pallas_call(
        _combine_kernel,
        grid_spec=pltpu.PrefetchScalarGridSpec(
            num_scalar_prefetch=2, grid=(ntiles,),
            in_specs=[pl.BlockSpec(memory_space=pl.ANY),
                      pl.BlockSpec((tm, ne), lambda i, *_: (i, 0)),
                      tok, row, col, col,
                      pl.BlockSpec(sg.shape, lambda i, *_: (0, 0)),
                      pl.BlockSpec(su.shape, lambda i, *_: (0, 0)),
                      pl.BlockSpec(sd.shape, lambda i, *_: (0, 0)),
                      tok,
                      pl.BlockSpec((1, 1, d), lambda i, *_: (i // per_b, 0, 0)),
                      par, par],
            out_specs=tok,
            scratch_shapes=[pltpu.VMEM((2, MOE_LMAX, d), BF16), pltpu.SemaphoreType.DMA((2,))]),
        out_shape=jax.ShapeDtypeStruct((t, d), F32),
        compiler_params=_cparams(("arbitrary",)),
        name="moe_combine",
    )(*scalars, ys, gates.reshape(t, ne), h2.reshape(t, d), off_f.reshape(ntiles, 1, ne),
      off_f.reshape(ntiles, ne, 1), cnt_f.reshape(ntiles, ne, 1), sg, su, sd, x.reshape(t, d), g2, ln_g, ln_b)
    return out.reshape(bsz, s, d)


def _pad_cols(w, n):
    return jnp.pad(w, ((0, 0), (0, n - w.shape[1])))


def kernel(x, c, rpe_bias, ada_w, ada_b, ln_mix_g, ln_mix_b, ln_ffn_g, ln_ffn_b, ev_w_in, ev_gk_w2, ev_gk_b, ev_norm, ev_w_out, od_w_in, od_conv_w, od_a_log, od_dt_bias, od_norm, od_w_out, moe_router_w, moe_router_b, moe_w_gate, moe_w_up, moe_w_down, sh_w_gate, sh_w_up, sh_w_down):
    bsz, s, d = x.shape
    mod = _ada(c, ada_w, ada_b)
    tiles = _rpe_tiles(rpe_bias)

    for layer in range(DEPTH):
        sh1, sc1, g1, sh2, sc2, g2 = [mod[layer, :, u * d:(u + 1) * d].reshape(bsz, 1, d) for u in range(6)]
        i = layer // 2
        if layer % 2 == 0:
            n_main = 3 * MOBA_W + 2 * GLA_QK_W + 2 * GLA_V_W
            w_in = jnp.concatenate([ev_w_in[i][:, :n_main], _pad_cols(ev_w_in[i][:, n_main:], LANES)], axis=1)
            proj = _mod_matmul(x, sc1, sh1, w_in.astype(BF16))
            nb = MOBA_W // LANES
            o_a = _moba(proj, tiles, 0, nb, 2 * nb)
            gk_w2p = jnp.pad(ev_gk_w2[i], ((0, LANES - GLA_GATE_RANK), (0, 0)))
            gla0 = 3 * MOBA_W
            o_b = _gla(proj, gk_w2p, ev_gk_b[i].reshape(1, -1), ev_norm[i].reshape(1, -1),
                       gla0 // GLA_QK_W, gla0 // GLA_QK_W + 1,
                       (gla0 + 2 * GLA_QK_W) // GLA_V_W, (gla0 + 2 * GLA_QK_W) // GLA_V_W + 1,
                       n_main // LANES)
            w_out = ev_w_out[i].astype(BF16)
            acts, ws = [o_a, o_b], [w_out[:MOBA_W], w_out[MOBA_W:]]
        else:
            n_main = 4 * GDN_W
            w_in = jnp.concatenate([od_w_in[i][:, :n_main], _pad_cols(od_w_in[i][:, n_main:], LANES)], axis=1)
            proj = _mod_matmul(x, sc1, sh1, w_in.astype(BF16))
            pvec = jnp.zeros((2, LANES), F32)
            pvec = pvec.at[0, GDN_HEADS:2 * GDN_HEADS].set(od_a_log[i])
            pvec = pvec.at[1, GDN_HEADS:2 * GDN_HEADS].set(od_dt_bias[i])
            o = _gdn(proj, od_conv_w[i], pvec, od_norm[i].reshape(1, -1))
            acts, ws = [o], [od_w_out[i].astype(BF16)]

        x, h2, rl = _mixout(acts, ws, x, g1, ln_mix_g[layer].reshape(1, d), ln_mix_b[layer].reshape(1, d),
                            sc2, sh2, moe_router_w[layer].T)
        router_tn = min(1024, s)
        gates, gates_t, cnt_raw = _router(rl, moe_router_b[layer], router_tn)
        x = _moe(h2, gates, gates_t, cnt_raw, router_tn, layer, moe_w_gate, moe_w_up, moe_w_down,
                 sh_w_gate[layer].astype(BF16), sh_w_up[layer].astype(BF16),
                 sh_w_down[layer].astype(BF16), x, g2, ln_ffn_g[layer].reshape(1, d), ln_ffn_b[layer].reshape(1, d))
    return x
```

```python
import functools
import math

import numpy as np
import jax
import jax.numpy as jnp
from jax import lax
from jax.experimental import pallas as pl
from jax.experimental.pallas import tpu as pltpu

F32 = jnp.float32
BF16 = jnp.bfloat16
HI = lax.Precision.HIGHEST
NT = (((1,), (1,)), ((), ()))
TN = (((0,), (0,)), ((), ()))
NEG = -1e30
LOG2E = math.log2(math.e)

LANES = 128
VMEM_LIMIT = 56 * 1024 * 1024

DEPTH = 2
MOBA_HEAD_DIM = 128
MOBA_HEADS = 4
MOBA_BLOCK = 256
MOBA_TOPK = 3
MOBA_GROUP = 8
MOBA_SUB = 4
GLA_DV = 128
GLA_HEADS = 4
GLA_DK = 64
GLA_GATE_RANK = 16
GLA_GATE_NORM = 16.0
GLA_CHUNK = 64
GDN_DK = 128
GDN_DV = 128
GDN_HEADS = 8
GDN_CONV = 4
GDN_CHUNK = 64
GDN_GROUP = 4
RPE_BUCKETS = 32
RPE_MAX_DIST = 2048
RPE_TILES = 8
N_EXPERTS = 64
TOP_K = 6
N_GROUPS = 8
TOPK_GROUPS = 4
D_EXPERT = 256
ROUTED_SCALE = 2.5
DEEPNORM_ALPHA = float((2 * DEPTH) ** 0.25)
LN_EPS = 1e-5
NORM_EPS = 1e-6

MOBA_W = MOBA_HEADS * MOBA_HEAD_DIM
GLA_QK_W = GLA_HEADS * GLA_DK
GLA_V_W = GLA_HEADS * GLA_DV
GDN_W = GDN_HEADS * GDN_DK


def _cparams(sem):
    return pltpu.CompilerParams(dimension_semantics=sem, vmem_limit_bytes=VMEM_LIMIT)


def _sigmoid(x):
    return 1.0 / (1.0 + jnp.exp(-x))


def _silu(x):
    return x * _sigmoid(x)


def _softplus(x):
    return jnp.maximum(x, 0.0) + jnp.log(1.0 + jnp.exp(-jnp.abs(x)))


def _layer_norm(z, g, b):
    mu = jnp.mean(z, axis=-1, keepdims=True)
    zc = z - mu
    var = jnp.mean(zc * zc, axis=-1, keepdims=True)
    return zc * lax.rsqrt(var + LN_EPS) * g + b


def _ada_kernel(c_ref, w_ref, b_ref, o_ref):
    ca = _silu(c_ref[...])
    o_ref[0] = jnp.dot(ca, w_ref[0], precision=HI, preferred_element_type=F32) + b_ref[0]


def _ada(c, ada_w, ada_b):
    depth, d, n = ada_w.shape
    bsz = c.shape[0]
    tn = 6 * LANES
    return pl.pallas_call(
        _ada_kernel,
        grid=(depth, n // tn),
        in_specs=[pl.BlockSpec((bsz, d), lambda l, j: (0, 0)),
                  pl.BlockSpec((1, d, tn), lambda l, j: (l, 0, j)),
                  pl.BlockSpec((1, 1, tn), lambda l, j: (l, 0, j))],
        out_specs=pl.BlockSpec((1, bsz, tn), lambda l, j: (l, 0, j)),
        out_shape=jax.ShapeDtypeStruct((depth, bsz, n), F32),
        compiler_params=_cparams(("arbitrary", "arbitrary")),
        name="ada_mod",
    )(c, ada_w, ada_b.reshape(depth, 1, n))


def _modmm_kernel(x_ref, sc_ref, sh_ref, w_ref, o_ref, h_scr):
    @pl.when(pl.program_id(2) == 0)
    def _():
        h_scr[...] = (x_ref[0] * (1.0 + sc_ref[0]) + sh_ref[0]).astype(BF16)

    o_ref[0] = jnp.dot(h_scr[...], w_ref[...], preferred_element_type=F32).astype(o_ref.dtype)


def _col_tile(n, cap):
    best = LANES
    for t in range(LANES, cap + 1, LANES):
        if n % t == 0:
            best = t
    return best


def _mod_matmul(x, sc, sh, w, tm=512, tn_cap=4608):
    bsz, s, d = x.shape
    n = w.shape[1]
    tm = min(tm, s)
    tn = _col_tile(n, tn_cap)
    return pl.pallas_call(
        _modmm_kernel,
        grid=(bsz, s // tm, n // tn),
        in_specs=[pl.BlockSpec((1, tm, d), lambda b, i, j: (b, i, 0)),
                  pl.BlockSpec((1, 1, d), lambda b, i, j: (b, 0, 0)),
                  pl.BlockSpec((1, 1, d), lambda b, i, j: (b, 0, 0)),
                  pl.BlockSpec((d, tn), lambda b, i, j: (0, j))],
        out_specs=pl.BlockSpec((1, tm, tn), lambda b, i, j: (b, i, j)),
        out_shape=jax.ShapeDtypeStruct((bsz, s, n), BF16),
        scratch_shapes=[pltpu.VMEM((tm, d), BF16)],
        compiler_params=_cparams(("arbitrary", "arbitrary", "arbitrary")),
        name="mod_matmul",
    )(x, sc, sh, w)


def _rpe_lower_bounds():
    exact = RPE_BUCKETS // 2
    d = np.arange(0, 2 * RPE_MAX_DIST, dtype=np.int64)
    logd = np.log(np.maximum(d, 1).astype(np.float64) / exact)
    large = exact + (logd / math.log(RPE_MAX_DIST / exact) * (RPE_BUCKETS - exact)).astype(np.int64)
    large = np.minimum(large, RPE_BUCKETS - 1)
    bucket = np.where(d < exact, d, large)
    return [int(np.argmax(bucket >= k)) for k in range(RPE_BUCKETS)]


def _rpe_tiles_kernel(lo, rpe_ref, o_ref):
    h = pl.program_id(0)
    j = pl.program_id(1)
    blk = o_ref.shape[-1]
    key = lax.broadcasted_iota(jnp.int32, (blk, blk), 0)
    qry = lax.broadcasted_iota(jnp.int32, (blk, blk), 1)
    dist = j * blk + qry - key
    val = jnp.full((blk, blk), rpe_ref[0, h], F32)
    for k in range(1, RPE_BUCKETS):
        val = jnp.where(dist >= lo[k], rpe_ref[k, h], val)
    o_ref[0, 0] = jnp.where(dist >= 0, val * LOG2E, NEG)


def _rpe_tiles(rpe_bias):
    heads = rpe_bias.shape[1]
    lo = _rpe_lower_bounds()
    assert lo[-1] <= (RPE_TILES - 1) * MOBA_BLOCK - (MOBA_BLOCK - 1)
    return pl.pallas_call(
        functools.partial(_rpe_tiles_kernel, lo),
        grid=(heads, RPE_TILES),
        in_specs=[pl.BlockSpec(memory_space=pltpu.SMEM)],
        out_specs=pl.BlockSpec((1, 1, MOBA_BLOCK, MOBA_BLOCK), lambda h, j: (h, j, 0, 0)),
        out_shape=jax.ShapeDtypeStruct((heads, RPE_TILES, MOBA_BLOCK, MOBA_BLOCK), F32),
        compiler_params=_cparams(("arbitrary", "arbitrary")),
        name="rpe_tiles",
    )(rpe_bias)


def _moba_kernel(q_ref, k_ref, v_ref, t_ref, o_ref, kb_scr, vt_scr, km_scr, sel_scr, far_scr):
    i = pl.program_id(2)
    nkb, blk, dh = kb_scr.shape

    @pl.when(i == 0)
    def _():
        for n in range(nkb):
            kn = k_ref[0, n * blk:(n + 1) * blk, :]
            kb_scr[n] = kn.astype(BF16)
            km_scr[n:n + 1, :] = jnp.mean(kn.astype(F32), axis=0, keepdims=True)
            vt_scr[n] = v_ref[0, n * blk:(n + 1) * blk, :].astype(F32).T.astype(BF16)

    q = q_ref[0].astype(F32)
    gate = lax.dot_general(km_scr[...], q, NT, precision=HI, preferred_element_type=F32)
    bidx = lax.broadcasted_iota(jnp.int32, gate.shape, 0)
    past = bidx < i
    g = jnp.where(past, gate, -jnp.inf)
    sel = None
    for _ in range(MOBA_TOPK):
        m = jnp.max(g, axis=0, keepdims=True)
        first = jnp.min(jnp.where(g == m, bidx, nkb), axis=0, keepdims=True)
        hit = bidx == first
        sel = hit if sel is None else jnp.logical_or(sel, hit)
        g = jnp.where(hit, -jnp.inf, g)
    mask = jnp.where(jnp.logical_and(sel, past), 0.0, NEG)
    sel_scr[...] = mask
    far_bias = t_ref[0, RPE_TILES - 1, 0:1, :]
    far_scr[...] = mask + jnp.where(i - bidx >= RPE_TILES - 1, far_bias, 0.0)

    qs = (q * (dh ** -0.5 * LOG2E)).astype(BF16)

    s = lax.dot_general(kb_scr[i], qs, NT, preferred_element_type=F32) + t_ref[0, 0]
    m0 = jnp.max(s, axis=0, keepdims=True)
    p = jnp.exp2(s - m0)
    l0 = jnp.sum(p, axis=0, keepdims=True)
    acc0 = jnp.dot(vt_scr[i], p.astype(BF16), preferred_element_type=F32)

    def body(far, g, carry):
        m, l, acc = carry
        nsub = MOBA_GROUP // MOBA_SUB
        blocks = [[jnp.minimum(g * MOBA_GROUP + k * MOBA_SUB + u, nkb - 1) for u in range(MOBA_SUB)]
                  for k in range(nsub)]
        scores, probs, alphas = {}, {}, {}

        def emit_scores(k):
            out = []
            for n in blocks[k]:
                s = lax.dot_general(kb_scr[n], qs, NT, preferred_element_type=F32)
                if far:
                    out.append(s + far_scr[pl.ds(n, 1), :])
                else:
                    out.append(s + t_ref[0, jnp.clip(i - n, 0, RPE_TILES - 1)] + sel_scr[pl.ds(n, 1), :])
            scores[k] = out

        def emit_softmax(k, m, l):
            m_new = m
            for s in scores[k]:
                m_new = jnp.maximum(m_new, jnp.max(s, axis=0, keepdims=True))
            alphas[k] = jnp.exp2(m - m_new)
            l = alphas[k] * l
            probs[k] = []
            for s in scores[k]:
                p = jnp.exp2(s - m_new)
                l = l + jnp.sum(p, axis=0, keepdims=True)
                probs[k].append(p.astype(BF16))
            return m_new, l

        def emit_values(k, acc):
            acc = alphas[k] * acc
            for n, p in zip(blocks[k], probs[k]):
                acc = acc + jnp.dot(vt_scr[n], p, preferred_element_type=F32)
            return acc

        emit_scores(0)
        for k in range(nsub):
            if k + 1 < nsub:
                emit_scores(k + 1)
            if k >= 1:
                acc = emit_values(k - 1, acc)
            m, l = emit_softmax(k, m, l)
        acc = emit_values(nsub - 1, acc)
        return m, l, acc

    n_far = jnp.maximum(i - (RPE_TILES - 2), 0) // MOBA_GROUP
    n_all = (i + MOBA_GROUP - 1) // MOBA_GROUP
    carry = lax.fori_loop(0, n_far, functools.partial(body, True), (m0, l0, acc0))
    _, l, acc = lax.fori_loop(n_far, n_all, functools.partial(body, False), carry)
    o_ref[0] = (acc / l).T.astype(o_ref.dtype)


def _moba(proj, tiles, q_col, k_col, v_col):
    bsz, s, _ = proj.shape
    dh, blk, heads = MOBA_HEAD_DIM, MOBA_BLOCK, MOBA_HEADS
    nkb = s // blk
    return pl.pallas_call(
        _moba_kernel,
        grid=(bsz, heads, nkb),
        in_specs=[pl.BlockSpec((1, blk, dh), lambda b, h, i: (b, i, q_col + h)),
                  pl.BlockSpec((1, s, dh), lambda b, h, i: (b, 0, k_col + h)),
                  pl.BlockSpec((1, s, dh), lambda b, h, i: (b, 0, v_col + h)),
                  pl.BlockSpec((1, RPE_TILES, blk, blk), lambda b, h, i: (h, 0, 0, 0))],
        out_specs=pl.BlockSpec((1, blk, dh), lambda b, h, i: (b, i, h)),
        out_shape=jax.ShapeDtypeStruct((bsz, s, heads * dh), BF16),
        scratch_shapes=[pltpu.VMEM((nkb, blk, dh), BF16),
                        pltpu.VMEM((nkb, dh, blk), BF16),
                        pltpu.VMEM((nkb, dh), F32),
                        pltpu.VMEM((nkb, blk), F32),
                        pltpu.VMEM((nkb, blk), F32)],
        compiler_params=_cparams(("arbitrary", "arbitrary", "arbitrary")),
        name="moba_attention",
    )(proj, proj, proj, tiles)


def _gla_kernel(q_ref, k_ref, v_ref, gg_ref, glr_ref, w2_ref, gb_ref, nw_ref, o_ref, st_scr):
    @pl.when(pl.program_id(1) == 0)
    def _():
        st_scr[...] = jnp.zeros(st_scr.shape, F32)

    tg = q_ref.shape[1]
    c = GLA_CHUNK
    x = jnp.dot(glr_ref[0].astype(F32), w2_ref[...], precision=HI, preferred_element_type=F32) + gb_ref[...]
    lg = -_softplus(-x) * (1.0 / GLA_GATE_NORM)
    row = lax.broadcasted_iota(jnp.int32, (c, c), 0)
    col = lax.broadcasted_iota(jnp.int32, (c, c), 1)
    tri = (row >= col).astype(F32)
    hc = GLA_HEADS * c
    rr = lax.broadcasted_iota(jnp.int32, (hc, hc), 0)
    cc = lax.broadcasted_iota(jnp.int32, (hc, hc), 1)
    incl = jnp.logical_and(rr // c == cc // c, rr >= cc)
    nw = nw_ref[...]
    heads = range(GLA_HEADS)

    chunks = []
    for ci in range(tg // c):
        rows = slice(ci * c, (ci + 1) * c)
        b = jnp.dot(tri, lg[rows], precision=HI, preferred_element_type=F32)
        bl = b[c - 1:c, :]
        q = q_ref[0, rows, :].astype(F32) * GLA_DK ** -0.5
        k = k_ref[0, rows, :].astype(F32)
        q_e = (q * jnp.exp(b)).astype(BF16)
        k_e = (k * jnp.exp(-b)).astype(BF16)
        chunks.append(dict(rows=rows, q_e=q_e, k_e=k_e, k_end=(k * jnp.exp(bl - b)).astype(BF16), d=jnp.exp(bl)))
    for p in chunks:
        qs = jnp.concatenate([p["q_e"][:, h * GLA_DK:(h + 1) * GLA_DK] for h in heads], axis=0)
        ks = jnp.concatenate([p["k_e"][:, h * GLA_DK:(h + 1) * GLA_DK] for h in heads], axis=0)
        p["v"] = jnp.concatenate([v_ref[0, p["rows"], h * GLA_DV:(h + 1) * GLA_DV] for h in heads],
                                 axis=0).astype(BF16)
        a = lax.dot_general(qs, ks, NT, preferred_element_type=F32)
        p["a"] = jnp.where(incl, a, 0.0).astype(BF16)
    for p in chunks:
        p["o"] = jnp.dot(p["a"], p["v"], preferred_element_type=F32)

    for p in chunks:
        for h in heads:
            ks = slice(h * GLA_DK, (h + 1) * GLA_DK)
            vs = slice(h * GLA_DV, (h + 1) * GLA_DV)
            hr = slice(h * c, (h + 1) * c)
            st = st_scr[h]
            o = p["o"][hr] + lax.dot_general(p["q_e"][:, ks], st.astype(BF16), NT, preferred_element_type=F32)
            st_scr[h] = (st * p["d"][:, ks]
                         + lax.dot_general(p["v"][hr], p["k_end"][:, ks], TN, preferred_element_type=F32))
            o = o * lax.rsqrt(jnp.mean(o * o, axis=-1, keepdims=True) + NORM_EPS)
            o_ref[0, p["rows"], vs] = (o * nw * _silu(gg_ref[0, p["rows"], vs].astype(F32))).astype(o_ref.dtype)


def _gla(proj, gk_w2p, gk_b, o_norm, q_col, k_col, v_col, g_col, r_col, tg=256):
    bsz, s, _ = proj.shape
    tg = min(tg, s)
    qk, vw = GLA_QK_W, GLA_V_W
    return pl.pallas_call(
        _gla_kernel,
        grid=(bsz, s // tg),
        in_specs=[pl.BlockSpec((1, tg, qk), lambda b, t: (b, t, q_col)),
                  pl.BlockSpec((1, tg, qk), lambda b, t: (b, t, k_col)),
                  pl.BlockSpec((1, tg, vw), lambda b, t: (b, t, v_col)),
                  pl.BlockSpec((1, tg, vw), lambda b, t: (b, t, g_col)),
                  pl.BlockSpec((1, tg, LANES), lambda b, t: (b, t, r_col)),
                  pl.BlockSpec((LANES, qk), lambda b, t: (0, 0)),
                  pl.BlockSpec((1, qk), lambda b, t: (0, 0)),
                  pl.BlockSpec((1, GLA_DV), lambda b, t: (0, 0))],
        out_specs=pl.BlockSpec((1, tg, vw), lambda b, t: (b, t, 0)),
        out_shape=jax.ShapeDtypeStruct((bsz, s, vw), BF16),
        scratch_shapes=[pltpu.VMEM((GLA_HEADS, GLA_DV, GLA_DK), F32)],
        compiler_params=_cparams(("arbitrary", "arbitrary")),
        name="gla_mixer",
    )(proj, proj, proj, proj, proj, gk_w2p, gk_b, o_norm)


def _gdn_kernel(qkv_ref, gate_ref, ba_ref, cw_ref, pv_ref, nw_ref, o_ref, tail_scr, s_scr):
    @pl.when(pl.program_id(1) == 0)
    def _():
        tail_scr[...] = jnp.zeros(tail_scr.shape, F32)
        s_scr[...] = jnp.zeros(s_scr.shape, F32)

    tg = qkv_ref.shape[1]
    c, dk, grp = GDN_CHUNK, GDN_DK, GDN_GROUP
    gr = grp * c
    w = GDN_W

    x = qkv_ref[0].astype(F32)
    tail = tail_scr[...]
    tail_scr[...] = x[tg - 8:, :]
    r8 = lax.broadcasted_iota(jnp.int32, (8, 1), 0)
    y = x * cw_ref[GDN_CONV - 1:GDN_CONV, :]
    for sft in range(1, GDN_CONV):
        xs = pltpu.roll(x, sft, axis=0)
        head = jnp.where(r8 < sft, pltpu.roll(tail, sft, axis=0), xs[:8, :])
        xs = jnp.concatenate([head, xs[8:, :]], axis=0)
        y = y + xs * cw_ref[GDN_CONV - 1 - sft:GDN_CONV - sft, :]
    y = _silu(y)

    ba = ba_ref[0].astype(F32)
    beta_t = _sigmoid(ba)
    g_t = -jnp.exp(pv_ref[0:1, :]) * _softplus(ba + pv_ref[1:2, :])

    row = lax.broadcasted_iota(jnp.int32, (c, c), 0)
    col = lax.broadcasted_iota(jnp.int32, (c, c), 1)
    tri = (row >= col).astype(F32)
    rr = lax.broadcasted_iota(jnp.int32, (gr, gr), 0)
    cc = lax.broadcasted_iota(jnp.int32, (gr, gr), 1)
    same = (rr // c) == (cc // c)
    incl = jnp.logical_and(same, rr >= cc)
    strict = jnp.logical_and(same, rr > cc)
    eye = (rr == cc).astype(F32)
    halves = [(rr // sz) == (cc // sz) for sz in (2 ** e for e in range(1, int(math.log2(c)) + 1))]
    nw = nw_ref[...]

    probs = []
    for ci in range(tg // c):
        rows = slice(ci * c, (ci + 1) * c)
        gcum = jnp.dot(tri, g_t[rows], precision=HI, preferred_element_type=F32)
        gcum_t = gcum.T
        for gi in range(GDN_HEADS // grp):
            hs = [gi * grp + u for u in range(grp)]

            def stack(a, off):
                return jnp.concatenate([a[rows, off + h * dk: off + (h + 1) * dk] for h in hs], axis=0)

            q = stack(y, 0)
            k = stack(y, w)
            v = stack(y, 2 * w)
            q = q * lax.rsqrt(jnp.sum(q * q, axis=-1, keepdims=True) + NORM_EPS) * dk ** -0.5
            k = k * lax.rsqrt(jnp.sum(k * k, axis=-1, keepdims=True) + NORM_EPS)
            beta = jnp.concatenate([beta_t[rows, h:h + 1] for h in hs], axis=0)
            gc = jnp.concatenate([gcum[:, GDN_HEADS + h:GDN_HEADS + h + 1] for h in hs], axis=0)
            gc_row = jnp.concatenate([gcum_t[GDN_HEADS + h:GDN_HEADS + h + 1, :] for h in hs], axis=1)
            gl = jnp.concatenate([jnp.broadcast_to(gcum[c - 1:c, GDN_HEADS + h:GDN_HEADS + h + 1], (c, 1))
                                  for h in hs], axis=0)

            decay = jnp.where(incl, jnp.exp(jnp.where(incl, gc - gc_row, 0.0)), 0.0)
            kb = k * beta
            k16 = k.astype(BF16)
            a = lax.dot_general(kb.astype(BF16), k16, NT, preferred_element_type=F32)
            a = jnp.where(strict, a * decay, 0.0)
            eg = jnp.exp(gc)
            probs.append(dict(
                rows=rows, hs=hs, a16=a.astype(BF16),
                t=eye - jnp.where(halves[0], a, 0.0),
                rhs=jnp.concatenate([v * beta, kb * eg], axis=1).astype(BF16),
                attn=(lax.dot_general(q.astype(BF16), k16, NT, preferred_element_type=F32) * decay).astype(BF16),
                q_g=(q * eg).astype(BF16),
                k_end=(k * jnp.exp(gl - gc)).astype(BF16),
                d_last=[jnp.exp(gcum[c - 1:c, GDN_HEADS + h:GDN_HEADS + h + 1]) for h in hs]))

    for lvl in range(1, len(halves)):
        off16 = jnp.where(jnp.logical_and(halves[lvl], jnp.logical_not(halves[lvl - 1])), 1.0, 0.0).astype(BF16)
        t16s = [p["t"].astype(BF16) for p in probs]
        xs = [jnp.dot(p["a16"] * off16, t16, preferred_element_type=F32).astype(BF16)
              for p, t16 in zip(probs, t16s)]
        for p, t16, x in zip(probs, t16s, xs):
            p["t"] = p["t"] - jnp.dot(t16, x, preferred_element_type=F32)
    for p in probs:
        wk = jnp.dot(p["t"].astype(BF16), p["rhs"], preferred_element_type=F32)
        p["w_val"] = wk[:, :GDN_DV]
        p["k_cum"] = wk[:, GDN_DV:].astype(BF16)

    for p in probs:
        rows, hs = p["rows"], p["hs"]
        v_new = []
        for u, h in enumerate(hs):
            hr = slice(u * c, (u + 1) * c)
            st16 = s_scr[h].astype(BF16)
            v_new.append(p["w_val"][hr] - jnp.dot(p["k_cum"][hr], st16, preferred_element_type=F32))
        v16 = jnp.concatenate(v_new, axis=0).astype(BF16)
        o_intra = jnp.dot(p["attn"], v16, preferred_element_type=F32)
        for u, h in enumerate(hs):
            hr = slice(u * c, (u + 1) * c)
            st = s_scr[h]
            o = o_intra[hr] + jnp.dot(p["q_g"][hr], st.astype(BF16), preferred_element_type=F32)
            s_scr[h] = (st * p["d_last"][u]
                        + lax.dot_general(p["k_end"][hr], v16[hr], TN, preferred_element_type=F32))
            o = o * lax.rsqrt(jnp.mean(o * o, axis=-1, keepdims=True) + NORM_EPS)
            cs = slice(h * GDN_DV, (h + 1) * GDN_DV)
            o_ref[0, rows, cs] = (o * nw * _silu(gate_ref[0, rows, cs].astype(F32))).astype(o_ref.dtype)


def _gdn(proj, conv_w, pvec, o_norm, tg=256):
    bsz, s, _ = proj.shape
    tg = min(tg, s)
    w = GDN_W
    return pl.pallas_call(
        _gdn_kernel,
        grid=(bsz, s // tg),
        in_specs=[pl.BlockSpec((1, tg, 3 * w), lambda b, t: (b, t, 0)),
                  pl.BlockSpec((1, tg, w), lambda b, t: (b, t, 3)),
                  pl.BlockSpec((1, tg, LANES), lambda b, t: (b, t, 4 * w // LANES)),
                  pl.BlockSpec((GDN_CONV, 3 * w), lambda b, t: (0, 0)),
                  pl.BlockSpec((2, LANES), lambda b, t: (0, 0)),
                  pl.BlockSpec((1, GDN_DV), lambda b, t: (0, 0))],
        out_specs=pl.BlockSpec((1, tg, w), lambda b, t: (b, t, 0)),
        out_shape=jax.ShapeDtypeStruct((bsz, s, w), BF16),
        scratch_shapes=[pltpu.VMEM((8, 3 * w), F32),
                        pltpu.VMEM((GDN_HEADS, GDN_DK, GDN_DV), F32)],
        compiler_params=_cparams(("arbitrary", "arbitrary")),
        name="gdn_mixer",
    )(proj, proj, proj, conv_w, pvec, o_norm)


def _mixout_kernel(n_act, *refs):
    acts = refs[:n_act]
    ws = refs[n_act:2 * n_act]
    x_ref, g1_ref, lng_ref, lnb_ref, sc_ref, sh_ref, rw_ref = refs[2 * n_act:2 * n_act + 7]
    xo_ref, h_ref, rl_ref = refs[2 * n_act + 7:]
    y = None
    for a_ref, w_ref in zip(acts, ws):
        t = jnp.dot(a_ref[0].astype(BF16), w_ref[...], preferred_element_type=F32)
        y = t if y is None else y + t
    xn = _layer_norm(DEEPNORM_ALPHA * x_ref[0] + g1_ref[0] * y, lng_ref[...], lnb_ref[...])
    xo_ref[0] = xn
    h = xn * (1.0 + sc_ref[0]) + sh_ref[0]
    h_ref[0] = h.astype(BF16)
    rl_ref[0] = lax.dot_general(rw_ref[...], h, NT, precision=HI, preferred_element_type=F32)


def _mixout(acts, ws, x, g1, ln_g, ln_b, sc2, sh2, router_wt, tm=512):
    bsz, s, d = x.shape
    tm = min(tm, s)
    n_act = len(acts)
    ne = router_wt.shape[0]
    vec = pl.BlockSpec((1, 1, d), lambda b, i: (b, 0, 0))
    par = pl.BlockSpec((1, d), lambda b, i: (0, 0))
    in_specs = ([pl.BlockSpec((1, tm, a.shape[-1]), lambda b, i: (b, i, 0)) for a in acts]
                + [pl.BlockSpec(w.shape, lambda b, i: (0, 0)) for w in ws]
                + [pl.BlockSpec((1, tm, d), lambda b, i: (b, i, 0)), vec, par, par, vec, vec,
                   pl.BlockSpec((ne, d), lambda b, i: (0, 0))])
    return pl.pallas_call(
        functools.partial(_mixout_kernel, n_act),
        grid=(bsz, s // tm),
        in_specs=in_specs,
        out_specs=[pl.BlockSpec((1, tm, d), lambda b, i: (b, i, 0)),
                   pl.BlockSpec((1, tm, d), lambda b, i: (b, i, 0)),
                   pl.BlockSpec((1, ne, tm), lambda b, i: (b, 0, i))],
        out_shape=[jax.ShapeDtypeStruct((bsz, s, d), F32),
                   jax.ShapeDtypeStruct((bsz, s, d), BF16),
                   jax.ShapeDtypeStruct((bsz, ne, s), F32)],
        compiler_params=_cparams(("arbitrary", "arbitrary")),
        name="mix_out",
    )(*acts, *ws, x, g1, ln_g, ln_b, sc2, sh2, router_wt)


def _first_max(vals, idx, axis, sentinel):
    m = jnp.max(vals, axis=axis, keepdims=True)
    first = jnp.min(jnp.where(vals == m, idx, sentinel), axis=axis, keepdims=True)
    return m, idx == first


def _router_kernel(rl_ref, rb_ref, g_ref, gt_ref, cnt_ref):
    ne, tn = rl_ref.shape[1], rl_ref.shape[2]
    gsz = ne // N_GROUPS
    scores = _sigmoid(rl_ref[0])
    sel = scores + rb_ref[...]
    ridx = lax.broadcasted_iota(jnp.int32, (gsz, tn), 0)
    gidx = lax.broadcasted_iota(jnp.int32, (N_GROUPS, tn), 0)
    gs = jnp.zeros((N_GROUPS, tn), F32)
    for g in range(N_GROUPS):
        sg = sel[g * gsz:(g + 1) * gsz, :]
        m1, hit = _first_max(sg, ridx, 0, gsz)
        m2 = jnp.max(jnp.where(hit, -jnp.inf, sg), axis=0, keepdims=True)
        gs = jnp.where(gidx == g, m1 + m2, gs)
    gsel = None
    for _ in range(TOPK_GROUPS):
        _, hit = _first_max(gs, gidx, 0, N_GROUPS)
        gsel = hit if gsel is None else jnp.logical_or(gsel, hit)
        gs = jnp.where(hit, -jnp.inf, gs)
    gself = gsel.astype(F32)
    emask = jnp.concatenate([jnp.broadcast_to(gself[g:g + 1, :], (gsz, tn)) for g in range(N_GROUPS)], axis=0)
    cand = jnp.where(emask > 0.5, sel, -jnp.inf)
    eidx = lax.broadcasted_iota(jnp.int32, cand.shape, 0)
    chosen = None
    for _ in range(TOP_K):
        _, hit = _first_max(cand, eidx, 0, ne)
        chosen = hit if chosen is None else jnp.logical_or(chosen, hit)
        cand = jnp.where(hit, -jnp.inf, cand)
    wsel = jnp.where(chosen, scores, 0.0)
    tot = jnp.sum(wsel, axis=0, keepdims=True)
    gates = wsel / (tot + 1e-20) * ROUTED_SCALE
    g_ref[0] = gates.T
    gt_ref[0] = gates
    tile_of = lax.broadcasted_iota(jnp.int32, (tn, LANES), 0) // MOE_TM
    ind = (tile_of == lax.broadcasted_iota(jnp.int32, (tn, LANES), 1)).astype(BF16)
    routed = jnp.where(gates > 0.0, 1.0, 0.0).astype(BF16)
    cnt_ref[0, 0] = jnp.dot(routed, ind, preferred_element_type=F32)


def _router(rl, router_b, tn=1024):
    bsz, ne, s = rl.shape
    tn = min(tn, s)
    return pl.pallas_call(
        _router_kernel,
        grid=(bsz, s // tn),
        in_specs=[pl.BlockSpec((1, ne, tn), lambda b, i: (b, 0, i)),
                  pl.BlockSpec((ne, 1), lambda b, i: (0, 0))],
        out_specs=[pl.BlockSpec((1, tn, ne), lambda b, i: (b, i, 0)),
                   pl.BlockSpec((1, ne, tn), lambda b, i: (b, 0, i)),
                   pl.BlockSpec((1, 1, ne, LANES), lambda b, i: (b, i, 0, 0))],
        out_shape=[jax.ShapeDtypeStruct((bsz, s, ne), F32),
                   jax.ShapeDtypeStruct((bsz, ne, s), F32),
                   jax.ShapeDtypeStruct((bsz, s // tn, ne, LANES), F32)],
        compiler_params=_cparams(("arbitrary", "arbitrary")),
        name="moe_router",
    )(rl, router_b.reshape(ne, 1))


MOE_TM = 256
MOE_ALIGN = 16
MOE_R = 512
MOE_LC = 512
MOE_LMAX = -(-(MOE_TM * TOP_K + N_EXPERTS * (MOE_ALIGN - 1)) // MOE_LC) * MOE_LC
MOE_NP = MOE_LMAX // MOE_ALIGN
MOE_LP = MOE_LC // MOE_ALIGN
POS_SPLIT = 256.0


def _ffn(x, wg, wu, wd):
    a = jnp.dot(x, wg, preferred_element_type=F32)
    u = jnp.dot(x, wu, preferred_element_type=F32)
    return jnp.dot((_silu(a) * u).astype(BF16), wd, preferred_element_type=F32)


def _split_pos(pos, axis):
    hi = jnp.floor(pos * (1.0 / POS_SPLIT)) * POS_SPLIT
    return jnp.concatenate([hi, pos - hi], axis=axis).astype(BF16)


def _for_each_piece(n_pieces, fn):
    unroll = 4

    def body(q, carry):
        for u in range(unroll):
            fn(q * unroll + u)
        return carry

    full = n_pieces // unroll
    lax.fori_loop(0, full, body, 0)

    def tail(p, carry):
        fn(p)
        return carry

    lax.fori_loop(full * unroll, n_pieces, tail, 0)


def _dispatch_kernel(dst_s, np_s, h_ref, gt_ref, offc_ref, offr_ref, cntr_ref, xs_hbm, sorted_scr, sems):
    i = pl.program_id(0)
    last = pl.num_programs(0) - 1
    slot = i % 2
    tm = h_ref.shape[0]
    routed = gt_ref[0] > 0.0
    t0 = lax.broadcasted_iota(jnp.int32, (tm, tm), 0)
    t1 = lax.broadcasted_iota(jnp.int32, (tm, tm), 1)
    earlier = jnp.where(t0 < t1, 1.0, 0.0).astype(BF16)
    rank_t = jnp.dot(jnp.where(routed, 1.0, 0.0).astype(BF16), earlier, preferred_element_type=F32)
    pos_t = jnp.where(routed, rank_t + offc_ref[0] + 1.0, 0.0)
    pos2 = _split_pos(pos_t, 0)
    x = h_ref[...]
    offr = offr_ref[0]
    endr = offr + cntr_ref[0]
    for c in range(MOE_LMAX // MOE_LC):
        r = (c * MOE_LC + lax.broadcasted_iota(jnp.int32, (MOE_LC, 1), 0)).astype(F32)
        owner = jnp.where(jnp.logical_and(r >= offr, r < endr), 1.0, 0.0).astype(BF16)
        possel = jnp.dot(jnp.concatenate([owner, owner], axis=1), pos2, preferred_element_type=F32)
        perm = jnp.where(possel == r + 1.0, 1.0, 0.0).astype(BF16)
        srt = jnp.dot(perm, x, preferred_element_type=F32).astype(BF16)
        sorted_scr[slot, c * MOE_LP:(c + 1) * MOE_LP] = srt.reshape(MOE_LP, MOE_ALIGN, srt.shape[-1])

    def piece(sl, tile, p):
        return pltpu.make_async_copy(sorted_scr.at[sl, p], xs_hbm.at[dst_s[tile * MOE_NP + p]], sems.at[sl])

    _for_each_piece(np_s[i], lambda p: piece(slot, i, p).start())

    @pl.when(i > 0)
    def _():
        _for_each_piece(np_s[i - 1], lambda p: piece(1 - slot, i - 1, p).wait())

    @pl.when(i == last)
    def _():
        _for_each_piece(np_s[i], lambda p: piece(slot, i, p).wait())


def _expert_ffn_kernel(be_s, nu_s, x_ref, wg_ref, wu_ref, wd_ref, y_ref):
    @pl.when(pl.program_id(0) < nu_s[0])
    def _():
        y_ref[...] = _ffn(x_ref[...], wg_ref[0, 0].astype(BF16), wu_ref[0, 0].astype(BF16),
                          wd_ref[0, 0].astype(BF16)).astype(BF16)


def _combine_kernel(dst_s, np_s, ys_hbm, g_ref, h_ref, offr_ref, offc_ref, cntc_ref,
                    sg_ref, su_ref, sd_ref, x_ref, g2_ref, lng_ref, lnb_ref, o_ref, ys_scr, sems):
    i = pl.program_id(0)
    last = pl.num_programs(0) - 1
    slot = i % 2
    tm = h_ref.shape[0]

    def piece(sl, tile, p):
        return pltpu.make_async_copy(ys_hbm.at[dst_s[tile * MOE_NP + p]], ys_scr.at[sl, p], sems.at[sl])

    @pl.when(i == 0)
    def _():
        ys_scr[...] = jnp.zeros(ys_scr.shape, BF16)
        _for_each_piece(np_s[0], lambda p: piece(0, 0, p).start())

    @pl.when(i < last)
    def _():
        _for_each_piece(np_s[i + 1], lambda p: piece(1 - slot, i + 1, p).start())

    acc = _ffn(h_ref[...], sg_ref[...], su_ref[...], sd_ref[...])
    g = g_ref[...]
    routed = g > 0.0
    t0 = lax.broadcasted_iota(jnp.int32, (tm, tm), 0)
    t1 = lax.broadcasted_iota(jnp.int32, (tm, tm), 1)
    earlier = jnp.where(t0 > t1, 1.0, 0.0).astype(BF16)
    rank = jnp.dot(earlier, jnp.where(routed, 1.0, 0.0).astype(BF16), preferred_element_type=F32)
    pos = jnp.where(routed, rank + offr_ref[0] + 1.0, 0.0)
    pos2 = _split_pos(pos, 1)
    g16 = g.astype(BF16)
    offc = offc_ref[0]
    endc = offc + cntc_ref[0]
    _for_each_piece(np_s[i], lambda p: piece(slot, i, p).wait())
    for c in range(MOE_LMAX // MOE_LC):
        r = (c * MOE_LC + lax.broadcasted_iota(jnp.int32, (1, MOE_LC), 1)).astype(F32)
        owner = jnp.where(jnp.logical_and(r >= offc, r < endc), 1.0, 0.0).astype(BF16)
        possel = jnp.dot(pos2, jnp.concatenate([owner, owner], axis=0), preferred_element_type=F32)
        gsel = jnp.dot(g16, owner, preferred_element_type=F32)
        w = jnp.where(possel == r + 1.0, gsel, 0.0).astype(BF16)
        ys = ys_scr[slot, c * MOE_LP:(c + 1) * MOE_LP].reshape(MOE_LC, ys_scr.shape[-1])
        acc = acc + jnp.dot(w, ys, preferred_element_type=F32)
    z = DEEPNORM_ALPHA * x_ref[...] + g2_ref[0] * acc
    o_ref[...] = _layer_norm(z, lng_ref[...], lnb_ref[...])


def _ceil_to(v, m):
    return jnp.floor((v + (m - 1.0)) * (1.0 / m)) * m


def _moe_layout_kernel(cnt_ref, dst_ref, np_ref, off_ref, cntp_ref, be_ref, nu_ref):
    cnt = cnt_ref[...]
    ntiles, ne = cnt.shape
    cntp = _ceil_to(cnt, MOE_ALIGN)
    e0 = lax.broadcasted_iota(jnp.int32, (ne, ne), 0)
    e1 = lax.broadcasted_iota(jnp.int32, (ne, ne), 1)
    i0 = lax.broadcasted_iota(jnp.int32, (ntiles, ntiles), 0)
    i1 = lax.broadcasted_iota(jnp.int32, (ntiles, ntiles), 1)

    def mm(a, b):
        return jnp.dot(a, b, precision=HI, preferred_element_type=F32)

    off = mm(cntp, jnp.where(e0 < e1, 1.0, 0.0))
    before = mm(jnp.where(i0 > i1, 1.0, 0.0), cntp)
    tot = jnp.broadcast_to(jnp.sum(cntp, axis=0, keepdims=True), (8, ne))
    totr = _ceil_to(tot, MOE_R)
    base = mm(totr, jnp.where(e0 < e1, 1.0, 0.0))[0:1, :]
    delta = base + before - off
    step = mm(delta, jnp.where(e0 == e1, 1.0, 0.0) - jnp.where(e0 + 1 == e1, 1.0, 0.0))
    rowp = (lax.broadcasted_iota(jnp.int32, (1, MOE_NP), 1) * MOE_ALIGN).astype(F32)
    dst = jnp.broadcast_to(rowp, (ntiles, MOE_NP))
    for e in range(ne):
        dst = dst + jnp.where(off[:, e:e + 1] <= rowp, step[:, e:e + 1], 0.0)
    dst_ref[...] = (dst * (1.0 / MOE_ALIGN)).astype(jnp.int32)
    pieces = jnp.sum(cntp, axis=1, keepdims=True) * (1.0 / MOE_ALIGN)
    np_ref[...] = jnp.broadcast_to(pieces, np_ref.shape).astype(jnp.int32)
    off_ref[...] = off
    cntp_ref[...] = cntp
    tot_c = lax.dot_general(cntp, jnp.ones((ntiles, LANES), F32), TN, precision=HI, preferred_element_type=F32)
    end_c = mm(jnp.where(e0 >= e1, 1.0, 0.0), _ceil_to(tot_c, MOE_R))[:, 0:1]
    total = end_c[ne - 1:ne, :]
    first = jnp.minimum((lax.broadcasted_iota(jnp.int32, be_ref.shape, 1) * MOE_R).astype(F32), total - 1.0)
    be = jnp.sum(jnp.where(end_c <= first, 1.0, 0.0), axis=0, keepdims=True)
    be_ref[...] = jnp.minimum(be, ne - 1.0).astype(jnp.int32)
    nu_ref[...] = jnp.broadcast_to(total * (1.0 / MOE_R), nu_ref.shape).astype(jnp.int32)


def _moe_layout(cnt_raw, tn):
    ne = N_EXPERTS
    nsub = tn // MOE_TM
    cnt = jnp.transpose(cnt_raw[..., :nsub], (0, 1, 3, 2)).reshape(-1, ne)
    ntiles = cnt.shape[0]
    nblk = -(-(ntiles * (MOE_TM * TOP_K + ne * (MOE_ALIGN - 1)) + ne * (MOE_R - 1)) // MOE_R)
    nblk_pad = -(-nblk // LANES) * LANES
    dst, npieces, off, cntp, blk_exp, nused = pl.pallas_call(
        _moe_layout_kernel,
        out_shape=[jax.ShapeDtypeStruct((ntiles, MOE_NP), jnp.int32),
                   jax.ShapeDtypeStruct((ntiles, LANES), jnp.int32),
                   jax.ShapeDtypeStruct((ntiles, ne), F32),
                   jax.ShapeDtypeStruct((ntiles, ne), F32),
                   jax.ShapeDtypeStruct((1, nblk_pad), jnp.int32),
                   jax.ShapeDtypeStruct((1, LANES), jnp.int32)],
        compiler_params=pltpu.CompilerParams(vmem_limit_bytes=VMEM_LIMIT),
        name="moe_layout",
    )(cnt)
    return dst.reshape(-1), npieces[:, 0], off, cntp, nblk, nused[0, :1], blk_exp[0, :nblk]


def _moe(h2, gates, gates_t, cnt_raw, tn, layer, wg, wu, wd, sg, su, sd, x, g2, ln_g, ln_b):
    bsz, s, d = x.shape
    t = bsz * s
    tm = MOE_TM
    _, ne, _, de = wg.shape
    per_b = s // tm
    ntiles = t // tm
    dst, npieces, off_f, cnt_f, nblk, nused, blk_exp = _moe_layout(cnt_raw, tn)
    scalars = (dst, npieces)
    row = pl.BlockSpec((1, 1, ne), lambda i, *_: (i, 0, 0))
    col = pl.BlockSpec((1, ne, 1), lambda i, *_: (i, 0, 0))
    tok = pl.BlockSpec((tm, d), lambda i, *_: (i, 0))
    par = pl.BlockSpec((1, d), lambda i, *_: (0, 0))
    rows = nblk * MOE_R

    xs = pl.pallas_call(
        _dispatch_kernel,
        grid_spec=pltpu.PrefetchScalarGridSpec(
            num_scalar_prefetch=2, grid=(ntiles,),
            in_specs=[tok,
                      pl.BlockSpec((1, ne, tm), lambda i, *_: (i // per_b, 0, i % per_b)),
                      col, row, row],
            out_specs=pl.BlockSpec(memory_space=pl.ANY),
            scratch_shapes=[pltpu.VMEM((2, MOE_NP, MOE_ALIGN, d), BF16), pltpu.SemaphoreType.DMA((2,))]),
        out_shape=jax.ShapeDtypeStruct((rows // MOE_ALIGN, MOE_ALIGN, d), BF16),
        compiler_params=_cparams(("arbitrary",)),
        name="moe_dispatch",
    )(*scalars, h2.reshape(t, d), gates_t, off_f.reshape(ntiles, ne, 1), off_f.reshape(ntiles, 1, ne),
      cnt_f.reshape(ntiles, 1, ne))

    def blk_index(j, be, nu):
        return jnp.maximum(jnp.minimum(j, nu[0] - 1), 0), 0

    blk = pl.BlockSpec((MOE_R, d), blk_index)
    ys = pl.pallas_call(
        _expert_ffn_kernel,
        grid_spec=pltpu.PrefetchScalarGridSpec(
            num_scalar_prefetch=2, grid=(nblk,),
            in_specs=[blk,
                      pl.BlockSpec((1, 1, d, de), lambda j, be, nu: (layer, be[j], 0, 0)),
                      pl.BlockSpec((1, 1, d, de), lambda j, be, nu: (layer, be[j], 0, 0)),
                      pl.BlockSpec((1, 1, de, d), lambda j, be, nu: (layer, be[j], 0, 0))],
            out_specs=blk),
        out_shape=jax.ShapeDtypeStruct((rows, d), BF16),
        compiler_params=_cparams(("arbitrary",)),
        name="moe_expert_ffn",
    )(blk_exp, nused, xs.reshape(rows, d), wg, wu, wd)

    out = pl.pallas_call(
        _combine_kernel,
        grid_spec=pltpu.PrefetchScalarGridSpec(
            num_scalar_prefetch=2, grid=(ntiles,),
            in_specs=[pl.BlockSpec(memory_space=pl.ANY),
                      pl.BlockSpec((tm, ne), lambda i, *_: (i, 0)),
                      tok, row, col, col,
                      pl.BlockSpec(sg.shape, lambda i, *_: (0, 0)),
                      pl.BlockSpec(su.shape, lambda i, *_: (0, 0)),
                      pl.BlockSpec(sd.shape, lambda i, *_: (0, 0)),
                      tok,
                      pl.BlockSpec((1, 1, d), lambda i, *_: (i // per_b, 0, 0)),
                      par, par],
            out_specs=tok,
            scratch_shapes=[pltpu.VMEM((2, MOE_NP, MOE_ALIGN, d), BF16), pltpu.SemaphoreType.DMA((2,))]),
        out_shape=jax.ShapeDtypeStruct((t, d), F32),
        compiler_params=_cparams(("arbitrary",)),
        name="moe_combine",
    )(*scalars, ys.reshape(rows // MOE_ALIGN, MOE_ALIGN, d), gates.reshape(t, ne), h2.reshape(t, d),
      off_f.reshape(ntiles, 1, ne),
      off_f.reshape(ntiles, ne, 1), cnt_f.reshape(ntiles, ne, 1), sg, su, sd, x.reshape(t, d), g2, ln_g, ln_b)
    return out.reshape(bsz, s, d)


def _pad_cols(w, n):
    return jnp.pad(w, ((0, 0), (0, n - w.shape[1])))


def kernel(x, c, rpe_bias, ada_w, ada_b, ln_mix_g, ln_mix_b, ln_ffn_g, ln_ffn_b, ev_w_in, ev_gk_w2, ev_gk_b, ev_norm, ev_w_out, od_w_in, od_conv_w, od_a_log, od_dt_bias, od_norm, od_w_out, moe_router_w, moe_router_b, moe_w_gate, moe_w_up, moe_w_down, sh_w_gate, sh_w_up, sh_w_down):
    bsz, s, d = x.shape
    mod = _ada(c, ada_w, ada_b)
    tiles = _rpe_tiles(rpe_bias)

    for layer in range(DEPTH):
        sh1, sc1, g1, sh2, sc2, g2 = [mod[layer, :, u * d:(u + 1) * d].reshape(bsz, 1, d) for u in range(6)]
        i = layer // 2
        if layer % 2 == 0:
            n_main = 3 * MOBA_W + 2 * GLA_QK_W + 2 * GLA_V_W
            w_in = jnp.concatenate([ev_w_in[i][:, :n_main], _pad_cols(ev_w_in[i][:, n_main:], LANES)], axis=1)
            proj = _mod_matmul(x, sc1, sh1, w_in.astype(BF16))
            nb = MOBA_W // LANES
            o_a = _moba(proj, tiles, 0, nb, 2 * nb)
            gk_w2p = jnp.pad(ev_gk_w2[i], ((0, LANES - GLA_GATE_RANK), (0, 0)))
            gla0 = 3 * MOBA_W
            o_b = _gla(proj, gk_w2p, ev_gk_b[i].reshape(1, -1), ev_norm[i].reshape(1, -1),
                       gla0 // GLA_QK_W, gla0 // GLA_QK_W + 1,
                       (gla0 + 2 * GLA_QK_W) // GLA_V_W, (gla0 + 2 * GLA_QK_W) // GLA_V_W + 1,
                       n_main // LANES)
            w_out = ev_w_out[i].astype(BF16)
            acts, ws = [o_a, o_b], [w_out[:MOBA_W], w_out[MOBA_W:]]
        else:
            n_main = 4 * GDN_W
            w_in = jnp.concatenate([od_w_in[i][:, :n_main], _pad_cols(od_w_in[i][:, n_main:], LANES)], axis=1)
            proj = _mod_matmul(x, sc1, sh1, w_in.astype(BF16))
            pvec = jnp.zeros((2, LANES), F32)
            pvec = pvec.at[0, GDN_HEADS:2 * GDN_HEADS].set(od_a_log[i])
            pvec = pvec.at[1, GDN_HEADS:2 * GDN_HEADS].set(od_dt_bias[i])
            o = _gdn(proj, od_conv_w[i], pvec, od_norm[i].reshape(1, -1))
            acts, ws = [o], [od_w_out[i].astype(BF16)]

        x, h2, rl = _mixout(acts, ws, x, g1, ln_mix_g[layer].reshape(1, d), ln_mix_b[layer].reshape(1, d),
                            sc2, sh2, moe_router_w[layer].T)
        router_tn = min(1024, s)
        gates, gates_t, cnt_raw = _router(rl, moe_router_b[layer], router_tn)
        x = _moe(h2, gates, gates_t, cnt_raw, router_tn, layer, moe_w_gate, moe_w_up, moe_w_down,
                 sh_w_gate[layer].astype(BF16), sh_w_up[layer].astype(BF16),
                 sh_w_down[layer].astype(BF16), x, g2, ln_ffn_g[layer].reshape(1, d), ln_ffn_b[layer].reshape(1, d))
    return x
```

```python
import functools
import math

import numpy as np
import jax
import jax.numpy as jnp
from jax import lax
from jax.experimental import pallas as pl
from jax.experimental.pallas import tpu as pltpu

F32 = jnp.float32
BF16 = jnp.bfloat16
HI = lax.Precision.HIGHEST
NT = (((1,), (1,)), ((), ()))
TN = (((0,), (0,)), ((), ()))
NEG = -1e30
LOG2E = math.log2(math.e)

LANES = 128
VMEM_LIMIT = 56 * 1024 * 1024

DEPTH = 2
MOBA_HEAD_DIM = 128
MOBA_HEADS = 4
MOBA_BLOCK = 256
MOBA_TOPK = 3
MOBA_GROUP = 8
MOBA_SUB = 4
MOBA_DEN_ROWS = 16
GLA_DV = 128
GLA_HEADS = 4
GLA_DK = 64
GLA_GATE_RANK = 16
GLA_GATE_NORM = 16.0
GLA_CHUNK = 64
GDN_DK = 128
GDN_DV = 128
GDN_HEADS = 8
GDN_CONV = 4
GDN_CHUNK = 64
GDN_GROUP = 4
RPE_BUCKETS = 32
RPE_MAX_DIST = 2048
RPE_TILES = 8
N_EXPERTS = 64
TOP_K = 6
N_GROUPS = 8
TOPK_GROUPS = 4
D_EXPERT = 256
ROUTED_SCALE = 2.5
DEEPNORM_ALPHA = float((2 * DEPTH) ** 0.25)
LN_EPS = 1e-5
NORM_EPS = 1e-6

MOBA_W = MOBA_HEADS * MOBA_HEAD_DIM
GLA_QK_W = GLA_HEADS * GLA_DK
GLA_V_W = GLA_HEADS * GLA_DV
GDN_W = GDN_HEADS * GDN_DK


def _cparams(sem):
    return pltpu.CompilerParams(dimension_semantics=sem, vmem_limit_bytes=VMEM_LIMIT)


def _sigmoid(x):
    return 1.0 / (1.0 + jnp.exp(-x))


def _silu(x):
    return x * _sigmoid(x)


def _softplus(x):
    return jnp.maximum(x, 0.0) + jnp.log(1.0 + jnp.exp(-jnp.abs(x)))


def _layer_norm(z, g, b):
    mu = jnp.mean(z, axis=-1, keepdims=True)
    zc = z - mu
    var = jnp.mean(zc * zc, axis=-1, keepdims=True)
    return zc * lax.rsqrt(var + LN_EPS) * g + b


def _ada_kernel(c_ref, w_ref, b_ref, o_ref):
    ca = _silu(c_ref[...])
    o_ref[0] = jnp.dot(ca, w_ref[0], precision=HI, preferred_element_type=F32) + b_ref[0]


def _ada(c, ada_w, ada_b):
    depth, d, n = ada_w.shape
    bsz = c.shape[0]
    tn = 6 * LANES
    return pl.pallas_call(
        _ada_kernel,
        grid=(depth, n // tn),
        in_specs=[pl.BlockSpec((bsz, d), lambda l, j: (0, 0)),
                  pl.BlockSpec((1, d, tn), lambda l, j: (l, 0, j)),
                  pl.BlockSpec((1, 1, tn), lambda l, j: (l, 0, j))],
        out_specs=pl.BlockSpec((1, bsz, tn), lambda l, j: (l, 0, j)),
        out_shape=jax.ShapeDtypeStruct((depth, bsz, n), F32),
        compiler_params=_cparams(("arbitrary", "arbitrary")),
        name="ada_mod",
    )(c, ada_w, ada_b.reshape(depth, 1, n))


def _modmm_kernel(x_ref, sc_ref, sh_ref, w_ref, o_ref, h_scr):
    @pl.when(pl.program_id(2) == 0)
    def _():
        h_scr[...] = (x_ref[0] * (1.0 + sc_ref[0]) + sh_ref[0]).astype(BF16)

    o_ref[0] = jnp.dot(h_scr[...], w_ref[...], preferred_element_type=F32).astype(o_ref.dtype)


def _col_tile(n, cap):
    best = LANES
    for t in range(LANES, cap + 1, LANES):
        if n % t == 0:
            best = t
    return best


def _mod_matmul(x, sc, sh, w, tm=512, tn_cap=4608):
    bsz, s, d = x.shape
    n = w.shape[1]
    tm = min(tm, s)
    tn = _col_tile(n, tn_cap)
    return pl.pallas_call(
        _modmm_kernel,
        grid=(bsz, s // tm, n // tn),
        in_specs=[pl.BlockSpec((1, tm, d), lambda b, i, j: (b, i, 0)),
                  pl.BlockSpec((1, 1, d), lambda b, i, j: (b, 0, 0)),
                  pl.BlockSpec((1, 1, d), lambda b, i, j: (b, 0, 0)),
                  pl.BlockSpec((d, tn), lambda b, i, j: (0, j))],
        out_specs=pl.BlockSpec((1, tm, tn), lambda b, i, j: (b, i, j)),
        out_shape=jax.ShapeDtypeStruct((bsz, s, n), BF16),
        scratch_shapes=[pltpu.VMEM((tm, d), BF16)],
        compiler_params=_cparams(("arbitrary", "arbitrary", "arbitrary")),
        name="mod_matmul",
    )(x, sc, sh, w)


def _rpe_lower_bounds():
    exact = RPE_BUCKETS // 2
    d = np.arange(0, 2 * RPE_MAX_DIST, dtype=np.int64)
    logd = np.log(np.maximum(d, 1).astype(np.float64) / exact)
    large = exact + (logd / math.log(RPE_MAX_DIST / exact) * (RPE_BUCKETS - exact)).astype(np.int64)
    large = np.minimum(large, RPE_BUCKETS - 1)
    bucket = np.where(d < exact, d, large)
    return [int(np.argmax(bucket >= k)) for k in range(RPE_BUCKETS)]


def _rpe_tiles_kernel(lo, rpe_ref, o_ref):
    h = pl.program_id(0)
    j = pl.program_id(1)
    blk = o_ref.shape[-1]
    key = lax.broadcasted_iota(jnp.int32, (blk, blk), 0)
    qry = lax.broadcasted_iota(jnp.int32, (blk, blk), 1)
    dist = j * blk + qry - key
    val = jnp.full((blk, blk), rpe_ref[0, h], F32)
    for k in range(1, RPE_BUCKETS):
        val = jnp.where(dist >= lo[k], rpe_ref[k, h], val)
    o_ref[0, 0] = jnp.where(dist >= 0, val * LOG2E, NEG)


def _rpe_tiles(rpe_bias):
    heads = rpe_bias.shape[1]
    lo = _rpe_lower_bounds()
    assert lo[-1] <= (RPE_TILES - 1) * MOBA_BLOCK - (MOBA_BLOCK - 1)
    return pl.pallas_call(
        functools.partial(_rpe_tiles_kernel, lo),
        grid=(heads, RPE_TILES),
        in_specs=[pl.BlockSpec(memory_space=pltpu.SMEM)],
        out_specs=pl.BlockSpec((1, 1, MOBA_BLOCK, MOBA_BLOCK), lambda h, j: (h, j, 0, 0)),
        out_shape=jax.ShapeDtypeStruct((heads, RPE_TILES, MOBA_BLOCK, MOBA_BLOCK), F32),
        compiler_params=_cparams(("arbitrary", "arbitrary")),
        name="rpe_tiles",
    )(rpe_bias)


def _moba_kernel(q_ref, k_ref, v_ref, t_ref, o_ref, kb_scr, vt_scr, km_scr, sel_scr, far_scr):
    i = pl.program_id(2)
    nkb, blk, dh = kb_scr.shape

    @pl.when(i == 0)
    def _():
        for n in range(nkb):
            kn = k_ref[0, n * blk:(n + 1) * blk, :]
            kb_scr[n] = kn.astype(BF16)
            km_scr[n:n + 1, :] = jnp.mean(kn.astype(F32), axis=0, keepdims=True)
            vt_scr[n, :dh] = v_ref[0, n * blk:(n + 1) * blk, :].astype(F32).T.astype(BF16)
            ones_row = lax.broadcasted_iota(jnp.int32, (MOBA_DEN_ROWS, blk), 0) == 0
            vt_scr[n, dh:] = jnp.where(ones_row, 1.0, 0.0).astype(BF16)

    q = q_ref[0].astype(F32)
    gate = lax.dot_general(km_scr[...], q, NT, precision=HI, preferred_element_type=F32)
    bidx = lax.broadcasted_iota(jnp.int32, gate.shape, 0)
    past = bidx < i
    g = jnp.where(past, gate, -jnp.inf)
    sel = None
    for _ in range(MOBA_TOPK):
        m = jnp.max(g, axis=0, keepdims=True)
        first = jnp.min(jnp.where(g == m, bidx, nkb), axis=0, keepdims=True)
        hit = bidx == first
        sel = hit if sel is None else jnp.logical_or(sel, hit)
        g = jnp.where(hit, -jnp.inf, g)
    mask = jnp.where(jnp.logical_or(jnp.logical_and(sel, past), bidx == i), 0.0, NEG)
    sel_scr[...] = mask
    far_bias = t_ref[0, RPE_TILES - 1, 0:1, :]
    far_scr[...] = mask + jnp.where(i - bidx >= RPE_TILES - 1, far_bias, 0.0)

    qs = (q * (dh ** -0.5 * LOG2E)).astype(BF16)

    m0 = jnp.full((1, blk), NEG, F32)
    acc0 = jnp.zeros((dh + MOBA_DEN_ROWS, blk), F32)
    def body(far, g, carry):
        m, acc = carry
        nsub = MOBA_GROUP // MOBA_SUB
        blocks = [[jnp.minimum(g * MOBA_GROUP + k * MOBA_SUB + u, nkb - 1) for u in range(MOBA_SUB)]
                  for k in range(nsub)]
        scores, probs, alphas = {}, {}, {}

        def emit_scores(k):
            out = []
            for n in blocks[k]:
                s = lax.dot_general(kb_scr[n], qs, NT, preferred_element_type=F32)
                if far:
                    out.append(s + far_scr[pl.ds(n, 1), :])
                else:
                    out.append(s + t_ref[0, jnp.clip(i - n, 0, RPE_TILES - 1)] + sel_scr[pl.ds(n, 1), :])
            scores[k] = out

        def emit_softmax(k, m):
            m_new = m
            for s in scores[k]:
                m_new = jnp.maximum(m_new, jnp.max(s, axis=0, keepdims=True))
            alphas[k] = jnp.exp2(m - m_new)
            probs[k] = [jnp.exp2(s - m_new).astype(BF16) for s in scores[k]]
            return m_new

        def emit_values(k, acc):
            acc = alphas[k] * acc
            for n, p in zip(blocks[k], probs[k]):
                acc = acc + jnp.dot(vt_scr[n], p, preferred_element_type=F32)
            return acc

        emit_scores(0)
        for k in range(nsub):
            if k + 1 < nsub:
                emit_scores(k + 1)
            if k >= 1:
                acc = emit_values(k - 1, acc)
            m = emit_softmax(k, m)
        acc = emit_values(nsub - 1, acc)
        return m, acc

    n_far = jnp.maximum(i - (RPE_TILES - 2), 0) // MOBA_GROUP
    n_all = i // MOBA_GROUP + 1
    carry = lax.fori_loop(0, n_far, functools.partial(body, True), (m0, acc0))
    _, acc = lax.fori_loop(n_far, n_all, functools.partial(body, False), carry)
    o_ref[0] = (acc[:dh] / acc[dh:dh + 1]).T.astype(o_ref.dtype)


def _moba(proj, tiles, q_col, k_col, v_col):
    bsz, s, _ = proj.shape
    dh, blk, heads = MOBA_HEAD_DIM, MOBA_BLOCK, MOBA_HEADS
    nkb = s // blk
    return pl.pallas_call(
        _moba_kernel,
        grid=(bsz, heads, nkb),
        in_specs=[pl.BlockSpec((1, blk, dh), lambda b, h, i: (b, i, q_col + h)),
                  pl.BlockSpec((1, s, dh), lambda b, h, i: (b, 0, k_col + h)),
                  pl.BlockSpec((1, s, dh), lambda b, h, i: (b, 0, v_col + h)),
                  pl.BlockSpec((1, RPE_TILES, blk, blk), lambda b, h, i: (h, 0, 0, 0))],
        out_specs=pl.BlockSpec((1, blk, dh), lambda b, h, i: (b, i, h)),
        out_shape=jax.ShapeDtypeStruct((bsz, s, heads * dh), BF16),
        scratch_shapes=[pltpu.VMEM((nkb, blk, dh), BF16),
                        pltpu.VMEM((nkb, dh + MOBA_DEN_ROWS, blk), BF16),
                        pltpu.VMEM((nkb, dh), F32),
                        pltpu.VMEM((nkb, blk), F32),
                        pltpu.VMEM((nkb, blk), F32)],
        compiler_params=_cparams(("arbitrary", "arbitrary", "arbitrary")),
        name="moba_attention",
    )(proj, proj, proj, tiles)


def _gla_kernel(q_ref, k_ref, v_ref, gg_ref, glr_ref, w2_ref, gb_ref, nw_ref, o_ref, st_scr):
    @pl.when(pl.program_id(1) == 0)
    def _():
        st_scr[...] = jnp.zeros(st_scr.shape, F32)

    tg = q_ref.shape[1]
    c = GLA_CHUNK
    x = jnp.dot(glr_ref[0].astype(F32), w2_ref[...], precision=HI, preferred_element_type=F32) + gb_ref[...]
    lg = -_softplus(-x) * (1.0 / GLA_GATE_NORM)
    row = lax.broadcasted_iota(jnp.int32, (c, c), 0)
    col = lax.broadcasted_iota(jnp.int32, (c, c), 1)
    tri = (row >= col).astype(F32)
    hc = GLA_HEADS * c
    rr = lax.broadcasted_iota(jnp.int32, (hc, hc), 0)
    cc = lax.broadcasted_iota(jnp.int32, (hc, hc), 1)
    incl = jnp.logical_and(rr // c == cc // c, rr >= cc)
    nw = nw_ref[...]
    heads = range(GLA_HEADS)

    chunks = []
    for ci in range(tg // c):
        rows = slice(ci * c, (ci + 1) * c)
        b = jnp.dot(tri, lg[rows], precision=HI, preferred_element_type=F32)
        bl = b[c - 1:c, :]
        q = q_ref[0, rows, :].astype(F32) * GLA_DK ** -0.5
        k = k_ref[0, rows, :].astype(F32)
        q_e = (q * jnp.exp(b)).astype(BF16)
        k_e = (k * jnp.exp(-b)).astype(BF16)
        chunks.append(dict(rows=rows, q_e=q_e, k_e=k_e, k_end=(k * jnp.exp(bl - b)).astype(BF16), d=jnp.exp(bl)))
    for p in chunks:
        qs = jnp.concatenate([p["q_e"][:, h * GLA_DK:(h + 1) * GLA_DK] for h in heads], axis=0)
        ks = jnp.concatenate([p["k_e"][:, h * GLA_DK:(h + 1) * GLA_DK] for h in heads], axis=0)
        p["v"] = jnp.concatenate([v_ref[0, p["rows"], h * GLA_DV:(h + 1) * GLA_DV] for h in heads],
                                 axis=0).astype(BF16)
        a = lax.dot_general(qs, ks, NT, preferred_element_type=F32)
        p["a"] = jnp.where(incl, a, 0.0).astype(BF16)
    for p in chunks:
        p["o"] = jnp.dot(p["a"], p["v"], preferred_element_type=F32)

    for p in chunks:
        for h in heads:
            ks = slice(h * GLA_DK, (h + 1) * GLA_DK)
            vs = slice(h * GLA_DV, (h + 1) * GLA_DV)
            hr = slice(h * c, (h + 1) * c)
            st = st_scr[h]
            o = p["o"][hr] + lax.dot_general(p["q_e"][:, ks], st.astype(BF16), NT, preferred_element_type=F32)
            st_scr[h] = (st * p["d"][:, ks]
                         + lax.dot_general(p["v"][hr], p["k_end"][:, ks], TN, preferred_element_type=F32))
            o = o * lax.rsqrt(jnp.mean(o * o, axis=-1, keepdims=True) + NORM_EPS)
            o_ref[0, p["rows"], vs] = (o * nw * _silu(gg_ref[0, p["rows"], vs].astype(F32))).astype(o_ref.dtype)


def _gla(proj, gk_w2p, gk_b, o_norm, q_col, k_col, v_col, g_col, r_col, tg=256):
    bsz, s, _ = proj.shape
    tg = min(tg, s)
    qk, vw = GLA_QK_W, GLA_V_W
    return pl.pallas_call(
        _gla_kernel,
        grid=(bsz, s // tg),
        in_specs=[pl.BlockSpec((1, tg, qk), lambda b, t: (b, t, q_col)),
                  pl.BlockSpec((1, tg, qk), lambda b, t: (b, t, k_col)),
                  pl.BlockSpec((1, tg, vw), lambda b, t: (b, t, v_col)),
                  pl.BlockSpec((1, tg, vw), lambda b, t: (b, t, g_col)),
                  pl.BlockSpec((1, tg, LANES), lambda b, t: (b, t, r_col)),
                  pl.BlockSpec((LANES, qk), lambda b, t: (0, 0)),
                  pl.BlockSpec((1, qk), lambda b, t: (0, 0)),
                  pl.BlockSpec((1, GLA_DV), lambda b, t: (0, 0))],
        out_specs=pl.BlockSpec((1, tg, vw), lambda b, t: (b, t, 0)),
        out_shape=jax.ShapeDtypeStruct((bsz, s, vw), BF16),
        scratch_shapes=[pltpu.VMEM((GLA_HEADS, GLA_DV, GLA_DK), F32)],
        compiler_params=_cparams(("arbitrary", "arbitrary")),
        name="gla_mixer",
    )(proj, proj, proj, proj, proj, gk_w2p, gk_b, o_norm)


def _gdn_kernel(qkv_ref, gate_ref, ba_ref, cw_ref, pv_ref, nw_ref, o_ref, tail_scr, s_scr):
    @pl.when(pl.program_id(1) == 0)
    def _():
        tail_scr[...] = jnp.zeros(tail_scr.shape, F32)
        s_scr[...] = jnp.zeros(s_scr.shape, F32)

    tg = qkv_ref.shape[1]
    c, dk, grp = GDN_CHUNK, GDN_DK, GDN_GROUP
    gr = grp * c
    w = GDN_W

    x = qkv_ref[0].astype(F32)
    tail = tail_scr[...]
    tail_scr[...] = x[tg - 8:, :]
    r8 = lax.broadcasted_iota(jnp.int32, (8, 1), 0)
    y = x * cw_ref[GDN_CONV - 1:GDN_CONV, :]
    for sft in range(1, GDN_CONV):
        xs = pltpu.roll(x, sft, axis=0)
        head = jnp.where(r8 < sft, pltpu.roll(tail, sft, axis=0), xs[:8, :])
        xs = jnp.concatenate([head, xs[8:, :]], axis=0)
        y = y + xs * cw_ref[GDN_CONV - 1 - sft:GDN_CONV - sft, :]
    y = _silu(y)

    ba = ba_ref[0].astype(F32)
    beta_t = _sigmoid(ba)
    g_t = -jnp.exp(pv_ref[0:1, :]) * _softplus(ba + pv_ref[1:2, :])

    row = lax.broadcasted_iota(jnp.int32, (c, c), 0)
    col = lax.broadcasted_iota(jnp.int32, (c, c), 1)
    tri = (row >= col).astype(F32)
    rr = lax.broadcasted_iota(jnp.int32, (gr, gr), 0)
    cc = lax.broadcasted_iota(jnp.int32, (gr, gr), 1)
    same = (rr // c) == (cc // c)
    incl = jnp.logical_and(same, rr >= cc)
    strict = jnp.logical_and(same, rr > cc)
    eye = (rr == cc).astype(F32)
    halves = [(rr // sz) == (cc // sz) for sz in (2 ** e for e in range(1, int(math.log2(c)) + 1))]
    nw = nw_ref[...]

    probs = []
    for ci in range(tg // c):
        rows = slice(ci * c, (ci + 1) * c)
        gcum = jnp.dot(tri, g_t[rows], precision=HI, preferred_element_type=F32)
        gcum_t = gcum.T
        for gi in range(GDN_HEADS // grp):
            hs = [gi * grp + u for u in range(grp)]

            def stack(a, off):
                return jnp.concatenate([a[rows, off + h * dk: off + (h + 1) * dk] for h in hs], axis=0)

            q = stack(y, 0)
            k = stack(y, w)
            v = stack(y, 2 * w)
            q = q * lax.rsqrt(jnp.sum(q * q, axis=-1, keepdims=True) + NORM_EPS) * dk ** -0.5
            k = k * lax.rsqrt(jnp.sum(k * k, axis=-1, keepdims=True) + NORM_EPS)
            beta = jnp.concatenate([beta_t[rows, h:h + 1] for h in hs], axis=0)
            gc = jnp.concatenate([gcum[:, GDN_HEADS + h:GDN_HEADS + h + 1] for h in hs], axis=0)
            gc_row = jnp.concatenate([gcum_t[GDN_HEADS + h:GDN_HEADS + h + 1, :] for h in hs], axis=1)
            gl = jnp.concatenate([jnp.broadcast_to(gcum[c - 1:c, GDN_HEADS + h:GDN_HEADS + h + 1], (c, 1))
                                  for h in hs], axis=0)

            decay = jnp.where(incl, jnp.exp(jnp.where(incl, gc - gc_row, 0.0)), 0.0)
            kb = k * beta
            k16 = k.astype(BF16)
            a = lax.dot_general(kb.astype(BF16), k16, NT, preferred_element_type=F32)
            a = jnp.where(strict, a * decay, 0.0)
            eg = jnp.exp(gc)
            probs.append(dict(
                rows=rows, hs=hs, a16=a.astype(BF16),
                t=eye - jnp.where(halves[0], a, 0.0),
                rhs=jnp.concatenate([v * beta, kb * eg], axis=1).astype(BF16),
                attn=(lax.dot_general(q.astype(BF16), k16, NT, preferred_element_type=F32) * decay).astype(BF16),
                q_g=(q * eg).astype(BF16),
                k_end=(k * jnp.exp(gl - gc)).astype(BF16),
                d_last=[jnp.exp(gcum[c - 1:c, GDN_HEADS + h:GDN_HEADS + h + 1]) for h in hs]))

    for lvl in range(1, len(halves)):
        off16 = jnp.where(jnp.logical_and(halves[lvl], jnp.logical_not(halves[lvl - 1])), 1.0, 0.0).astype(BF16)
        t16s = [p["t"].astype(BF16) for p in probs]
        xs = [jnp.dot(p["a16"] * off16, t16, preferred_element_type=F32).astype(BF16)
              for p, t16 in zip(probs, t16s)]
        for p, t16, x in zip(probs, t16s, xs):
            p["t"] = p["t"] - jnp.dot(t16, x, preferred_element_type=F32)
    for p in probs:
        wk = jnp.dot(p["t"].astype(BF16), p["rhs"], preferred_element_type=F32)
        p["w_val"] = wk[:, :GDN_DV]
        p["k_cum"] = wk[:, GDN_DV:].astype(BF16)

    for p in probs:
        rows, hs = p["rows"], p["hs"]
        v_new = []
        for u, h in enumerate(hs):
            hr = slice(u * c, (u + 1) * c)
            st16 = s_scr[h].astype(BF16)
            v_new.append(p["w_val"][hr] - jnp.dot(p["k_cum"][hr], st16, preferred_element_type=F32))
        v16 = jnp.concatenate(v_new, axis=0).astype(BF16)
        o_intra = jnp.dot(p["attn"], v16, preferred_element_type=F32)
        for u, h in enumerate(hs):
            hr = slice(u * c, (u + 1) * c)
            st = s_scr[h]
            o = o_intra[hr] + jnp.dot(p["q_g"][hr], st.astype(BF16), preferred_element_type=F32)
            s_scr[h] = (st * p["d_last"][u]
                        + lax.dot_general(p["k_end"][hr], v16[hr], TN, preferred_element_type=F32))
            o = o * lax.rsqrt(jnp.mean(o * o, axis=-1, keepdims=True) + NORM_EPS)
            cs = slice(h * GDN_DV, (h + 1) * GDN_DV)
            o_ref[0, rows, cs] = (o * nw * _silu(gate_ref[0, rows, cs].astype(F32))).astype(o_ref.dtype)


def _gdn(proj, conv_w, pvec, o_norm, tg=256):
    bsz, s, _ = proj.shape
    tg = min(tg, s)
    w = GDN_W
    return pl.pallas_call(
        _gdn_kernel,
        grid=(bsz, s // tg),
        in_specs=[pl.BlockSpec((1, tg, 3 * w), lambda b, t: (b, t, 0)),
                  pl.BlockSpec((1, tg, w), lambda b, t: (b, t, 3)),
                  pl.BlockSpec((1, tg, LANES), lambda b, t: (b, t, 4 * w // LANES)),
                  pl.BlockSpec((GDN_CONV, 3 * w), lambda b, t: (0, 0)),
                  pl.BlockSpec((2, LANES), lambda b, t: (0, 0)),
                  pl.BlockSpec((1, GDN_DV), lambda b, t: (0, 0))],
        out_specs=pl.BlockSpec((1, tg, w), lambda b, t: (b, t, 0)),
        out_shape=jax.ShapeDtypeStruct((bsz, s, w), BF16),
        scratch_shapes=[pltpu.VMEM((8, 3 * w), F32),
                        pltpu.VMEM((GDN_HEADS, GDN_DK, GDN_DV), F32)],
        compiler_params=_cparams(("arbitrary", "arbitrary")),
        name="gdn_mixer",
    )(proj, proj, proj, conv_w, pvec, o_norm)


def _mixout_kernel(n_act, *refs):
    acts = refs[:n_act]
    ws = refs[n_act:2 * n_act]
    x_ref, g1_ref, lng_ref, lnb_ref, sc_ref, sh_ref, rw_ref = refs[2 * n_act:2 * n_act + 7]
    xo_ref, h_ref, rl_ref = refs[2 * n_act + 7:]
    y = None
    for a_ref, w_ref in zip(acts, ws):
        t = jnp.dot(a_ref[0].astype(BF16), w_ref[...], preferred_element_type=F32)
        y = t if y is None else y + t
    xn = _layer_norm(DEEPNORM_ALPHA * x_ref[0] + g1_ref[0] * y, lng_ref[...], lnb_ref[...])
    xo_ref[0] = xn
    h = xn * (1.0 + sc_ref[0]) + sh_ref[0]
    h_hi = h.astype(BF16)
    h_ref[0] = h_hi
    h_lo = (h - h_hi.astype(F32)).astype(BF16)
    rw = rw_ref[...]
    rw_hi = rw.astype(BF16)
    rw_lo = (rw - rw_hi.astype(F32)).astype(BF16)
    rl_ref[0] = (lax.dot_general(rw_hi, h_hi, NT, preferred_element_type=F32)
                 + lax.dot_general(rw_hi, h_lo, NT, preferred_element_type=F32)
                 + lax.dot_general(rw_lo, h_hi, NT, preferred_element_type=F32))


def _mixout(acts, ws, x, g1, ln_g, ln_b, sc2, sh2, router_wt, tm=512):
    bsz, s, d = x.shape
    tm = min(tm, s)
    n_act = len(acts)
    ne = router_wt.shape[0]
    vec = pl.BlockSpec((1, 1, d), lambda b, i: (b, 0, 0))
    par = pl.BlockSpec((1, d), lambda b, i: (0, 0))
    in_specs = ([pl.BlockSpec((1, tm, a.shape[-1]), lambda b, i: (b, i, 0)) for a in acts]
                + [pl.BlockSpec(w.shape, lambda b, i: (0, 0)) for w in ws]
                + [pl.BlockSpec((1, tm, d), lambda b, i: (b, i, 0)), vec, par, par, vec, vec,
                   pl.BlockSpec((ne, d), lambda b, i: (0, 0))])
    return pl.pallas_call(
        functools.partial(_mixout_kernel, n_act),
        grid=(bsz, s // tm),
        in_specs=in_specs,
        out_specs=[pl.BlockSpec((1, tm, d), lambda b, i: (b, i, 0)),
                   pl.BlockSpec((1, tm, d), lambda b, i: (b, i, 0)),
                   pl.BlockSpec((1, ne, tm), lambda b, i: (b, 0, i))],
        out_shape=[jax.ShapeDtypeStruct((bsz, s, d), F32),
                   jax.ShapeDtypeStruct((bsz, s, d), BF16),
                   jax.ShapeDtypeStruct((bsz, ne, s), F32)],
        compiler_params=_cparams(("arbitrary", "arbitrary")),
        name="mix_out",
    )(*acts, *ws, x, g1, ln_g, ln_b, sc2, sh2, router_wt)


def _first_max(vals, idx, axis, sentinel):
    m = jnp.max(vals, axis=axis, keepdims=True)
    first = jnp.min(jnp.where(vals == m, idx, sentinel), axis=axis, keepdims=True)
    return m, idx == first


def _router_kernel(rl_ref, rb_ref, g_ref, gt_ref, cnt_ref):
    ne, tn = rl_ref.shape[1], rl_ref.shape[2]
    gsz = ne // N_GROUPS
    scores = _sigmoid(rl_ref[0])
    sel = scores + rb_ref[...]
    ridx = lax.broadcasted_iota(jnp.int32, (gsz, tn), 0)
    gidx = lax.broadcasted_iota(jnp.int32, (N_GROUPS, tn), 0)
    gs = jnp.zeros((N_GROUPS, tn), F32)
    for g in range(N_GROUPS):
        sg = sel[g * gsz:(g + 1) * gsz, :]
        m1, hit = _first_max(sg, ridx, 0, gsz)
        m2 = jnp.max(jnp.where(hit, -jnp.inf, sg), axis=0, keepdims=True)
        gs = jnp.where(gidx == g, m1 + m2, gs)
    gsel = None
    for _ in range(TOPK_GROUPS):
        _, hit = _first_max(gs, gidx, 0, N_GROUPS)
        gsel = hit if gsel is None else jnp.logical_or(gsel, hit)
        gs = jnp.where(hit, -jnp.inf, gs)
    gself = gsel.astype(F32)
    emask = jnp.concatenate([jnp.broadcast_to(gself[g:g + 1, :], (gsz, tn)) for g in range(N_GROUPS)], axis=0)
    cand = jnp.where(emask > 0.5, sel, -jnp.inf)
    eidx = lax.broadcasted_iota(jnp.int32, cand.shape, 0)
    chosen = None
    for _ in range(TOP_K):
        _, hit = _first_max(cand, eidx, 0, ne)
        chosen = hit if chosen is None else jnp.logical_or(chosen, hit)
        cand = jnp.where(hit, -jnp.inf, cand)
    wsel = jnp.where(chosen, scores, 0.0)
    tot = jnp.sum(wsel, axis=0, keepdims=True)
    gates = wsel / (tot + 1e-20) * ROUTED_SCALE
    g_ref[0] = gates.T
    gt_ref[0] = gates
    tile_of = lax.broadcasted_iota(jnp.int32, (tn, LANES), 0) // MOE_TM
    ind = (tile_of == lax.broadcasted_iota(jnp.int32, (tn, LANES), 1)).astype(BF16)
    routed = jnp.where(gates > 0.0, 1.0, 0.0).astype(BF16)
    cnt_ref[0, 0] = jnp.dot(routed, ind, preferred_element_type=F32)


def _router(rl, router_b, tn=1024):
    bsz, ne, s = rl.shape
    tn = min(tn, s)
    return pl.pallas_call(
        _router_kernel,
        grid=(bsz, s // tn),
        in_specs=[pl.BlockSpec((1, ne, tn), lambda b, i: (b, 0, i)),
                  pl.BlockSpec((ne, 1), lambda b, i: (0, 0))],
        out_specs=[pl.BlockSpec((1, tn, ne), lambda b, i: (b, i, 0)),
                   pl.BlockSpec((1, ne, tn), lambda b, i: (b, 0, i)),
                   pl.BlockSpec((1, 1, ne, LANES), lambda b, i: (b, i, 0, 0))],
        out_shape=[jax.ShapeDtypeStruct((bsz, s, ne), F32),
                   jax.ShapeDtypeStruct((bsz, ne, s), F32),
                   jax.ShapeDtypeStruct((bsz, s // tn, ne, LANES), F32)],
        compiler_params=_cparams(("arbitrary", "arbitrary")),
        name="moe_router",
    )(rl, router_b.reshape(ne, 1))


MOE_TM = 256
MOE_ALIGN = 16
MOE_R = 512
MOE_LC = 512
MOE_LMAX = -(-(MOE_TM * TOP_K + N_EXPERTS * (MOE_ALIGN - 1)) // MOE_LC) * MOE_LC
MOE_NP = MOE_LMAX // MOE_ALIGN
MOE_LP = MOE_LC // MOE_ALIGN
POS_SPLIT = 256.0


def _ffn(x, wg, wu, wd):
    a = jnp.dot(x, wg, preferred_element_type=F32)
    u = jnp.dot(x, wu, preferred_element_type=F32)
    return jnp.dot((_silu(a) * u).astype(BF16), wd, preferred_element_type=F32)


def _split_pos(pos, axis):
    hi = jnp.floor(pos * (1.0 / POS_SPLIT)) * POS_SPLIT
    return jnp.concatenate([hi, pos - hi], axis=axis).astype(BF16)


def _for_each_piece(n_pieces, fn):
    unroll = 4

    def body(q, carry):
        for u in range(unroll):
            fn(q * unroll + u)
        return carry

    full = n_pieces // unroll
    lax.fori_loop(0, full, body, 0)

    def tail(p, carry):
        fn(p)
        return carry

    lax.fori_loop(full * unroll, n_pieces, tail, 0)


def _dispatch_kernel(dst_s, np_s, h_ref, gt_ref, offc_ref, offr_ref, cntr_ref, xs_hbm, sorted_scr, sems):
    i = pl.program_id(0)
    last = pl.num_programs(0) - 1
    slot = i % 2
    tm = h_ref.shape[0]
    routed = gt_ref[0] > 0.0
    t0 = lax.broadcasted_iota(jnp.int32, (tm, tm), 0)
    t1 = lax.broadcasted_iota(jnp.int32, (tm, tm), 1)
    earlier = jnp.where(t0 < t1, 1.0, 0.0).astype(BF16)
    rank_t = jnp.dot(jnp.where(routed, 1.0, 0.0).astype(BF16), earlier, preferred_element_type=F32)
    pos_t = jnp.where(routed, rank_t + offc_ref[0] + 1.0, 0.0)
    pos2 = _split_pos(pos_t, 0)
    x = h_ref[...]
    offr = offr_ref[0]
    endr = offr + cntr_ref[0]
    for c in range(MOE_LMAX // MOE_LC):
        r = (c * MOE_LC + lax.broadcasted_iota(jnp.int32, (MOE_LC, 1), 0)).astype(F32)
        owner = jnp.where(jnp.logical_and(r >= offr, r < endr), 1.0, 0.0).astype(BF16)
        possel = jnp.dot(jnp.concatenate([owner, owner], axis=1), pos2, preferred_element_type=F32)
        perm = jnp.where(possel == r + 1.0, 1.0, 0.0).astype(BF16)
        srt = jnp.dot(perm, x, preferred_element_type=F32).astype(BF16)
        sorted_scr[slot, c * MOE_LP:(c + 1) * MOE_LP] = srt.reshape(MOE_LP, MOE_ALIGN, srt.shape[-1])

    def piece(sl, tile, p):
        return pltpu.make_async_copy(sorted_scr.at[sl, p], xs_hbm.at[dst_s[tile * MOE_NP + p]], sems.at[sl])

    _for_each_piece(np_s[i], lambda p: piece(slot, i, p).start())

    @pl.when(i > 0)
    def _():
        _for_each_piece(np_s[i - 1], lambda p: piece(1 - slot, i - 1, p).wait())

    @pl.when(i == last)
    def _():
        _for_each_piece(np_s[i], lambda p: piece(slot, i, p).wait())


def _expert_ffn_kernel(be_s, nu_s, x_ref, wg_ref, wu_ref, wd_ref, y_ref):
    @pl.when(pl.program_id(0) < nu_s[0])
    def _():
        y_ref[...] = _ffn(x_ref[...], wg_ref[0, 0].astype(BF16), wu_ref[0, 0].astype(BF16),
                          wd_ref[0, 0].astype(BF16)).astype(BF16)


def _combine_kernel(dst_s, np_s, ys_hbm, g_ref, h_ref, offr_ref, offc_ref, cntc_ref,
                    sg_ref, su_ref, sd_ref, x_ref, g2_ref, lng_ref, lnb_ref, o_ref, ys_scr, sems):
    i = pl.program_id(0)
    last = pl.num_programs(0) - 1
    slot = i % 2
    tm = h_ref.shape[0]

    def piece(sl, tile, p):
        return pltpu.make_async_copy(ys_hbm.at[dst_s[tile * MOE_NP + p]], ys_scr.at[sl, p], sems.at[sl])

    @pl.when(i == 0)
    def _():
        ys_scr[...] = jnp.zeros(ys_scr.shape, BF16)
        _for_each_piece(np_s[0], lambda p: piece(0, 0, p).start())

    @pl.when(i < last)
    def _():
        _for_each_piece(np_s[i + 1], lambda p: piece(1 - slot, i + 1, p).start())

    acc = _ffn(h_ref[...], sg_ref[...], su_ref[...], sd_ref[...])
    g = g_ref[...]
    routed = g > 0.0
    t0 = lax.broadcasted_iota(jnp.int32, (tm, tm), 0)
    t1 = lax.broadcasted_iota(jnp.int32, (tm, tm), 1)
    earlier = jnp.where(t0 > t1, 1.0, 0.0).astype(BF16)
    rank = jnp.dot(earlier, jnp.where(routed, 1.0, 0.0).astype(BF16), preferred_element_type=F32)
    pos = jnp.where(routed, rank + offr_ref[0] + 1.0, 0.0)
    pos2 = _split_pos(pos, 1)
    g16 = g.astype(BF16)
    offc = offc_ref[0]
    endc = offc + cntc_ref[0]
    _for_each_piece(np_s[i], lambda p: piece(slot, i, p).wait())
    for c in range(MOE_LMAX // MOE_LC):
        r = (c * MOE_LC + lax.broadcasted_iota(jnp.int32, (1, MOE_LC), 1)).astype(F32)
        owner = jnp.where(jnp.logical_and(r >= offc, r < endc), 1.0, 0.0).astype(BF16)
        possel = jnp.dot(pos2, jnp.concatenate([owner, owner], axis=0), preferred_element_type=F32)
        gsel = jnp.dot(g16, owner, preferred_element_type=F32)
        w = jnp.where(possel == r + 1.0, gsel, 0.0).astype(BF16)
        ys = ys_scr[slot, c * MOE_LP:(c + 1) * MOE_LP].reshape(MOE_LC, ys_scr.shape[-1])
        acc = acc + jnp.dot(w, ys, preferred_element_type=F32)
    z = DEEPNORM_ALPHA * x_ref[...] + g2_ref[0] * acc
    o_ref[...] = _layer_norm(z, lng_ref[...], lnb_ref[...])


def _ceil_to(v, m):
    return jnp.floor((v + (m - 1.0)) * (1.0 / m)) * m


def _moe_layout_kernel(cnt_ref, dst_ref, np_ref, off_ref, cntp_ref, be_ref, nu_ref):
    cnt = cnt_ref[...]
    ntiles, ne = cnt.shape
    cntp = _ceil_to(cnt, MOE_ALIGN)
    e0 = lax.broadcasted_iota(jnp.int32, (ne, ne), 0)
    e1 = lax.broadcasted_iota(jnp.int32, (ne, ne), 1)
    i0 = lax.broadcasted_iota(jnp.int32, (ntiles, ntiles), 0)
    i1 = lax.broadcasted_iota(jnp.int32, (ntiles, ntiles), 1)

    def mm(a, b):
        return jnp.dot(a, b, precision=HI, preferred_element_type=F32)

    off = mm(cntp, jnp.where(e0 < e1, 1.0, 0.0))
    before = mm(jnp.where(i0 > i1, 1.0, 0.0), cntp)
    tot = jnp.broadcast_to(jnp.sum(cntp, axis=0, keepdims=True), (8, ne))
    totr = _ceil_to(tot, MOE_R)
    base = mm(totr, jnp.where(e0 < e1, 1.0, 0.0))[0:1, :]
    delta = base + before - off
    step = mm(delta, jnp.where(e0 == e1, 1.0, 0.0) - jnp.where(e0 + 1 == e1, 1.0, 0.0))
    rowp = (lax.broadcasted_iota(jnp.int32, (1, MOE_NP), 1) * MOE_ALIGN).astype(F32)
    dst = jnp.broadcast_to(rowp, (ntiles, MOE_NP))
    for e in range(ne):
        dst = dst + jnp.where(off[:, e:e + 1] <= rowp, step[:, e:e + 1], 0.0)
    dst_ref[...] = (dst * (1.0 / MOE_ALIGN)).astype(jnp.int32)
    pieces = jnp.sum(cntp, axis=1, keepdims=True) * (1.0 / MOE_ALIGN)
    np_ref[...] = jnp.broadcast_to(pieces, np_ref.shape).astype(jnp.int32)
    off_ref[...] = off
    cntp_ref[...] = cntp
    tot_c = lax.dot_general(cntp, jnp.ones((ntiles, LANES), F32), TN, precision=HI, preferred_element_type=F32)
    end_c = mm(jnp.where(e0 >= e1, 1.0, 0.0), _ceil_to(tot_c, MOE_R))[:, 0:1]
    total = end_c[ne - 1:ne, :]
    first = jnp.minimum((lax.broadcasted_iota(jnp.int32, be_ref.shape, 1) * MOE_R).astype(F32), total - 1.0)
    be = jnp.sum(jnp.where(end_c <= first, 1.0, 0.0), axis=0, keepdims=True)
    be_ref[...] = jnp.minimum(be, ne - 1.0).astype(jnp.int32)
    nu_ref[...] = jnp.broadcast_to(total * (1.0 / MOE_R), nu_ref.shape).astype(jnp.int32)


def _moe_layout(cnt_raw, tn):
    ne = N_EXPERTS
    nsub = tn // MOE_TM
    cnt = jnp.transpose(cnt_raw[..., :nsub], (0, 1, 3, 2)).reshape(-1, ne)
    ntiles = cnt.shape[0]
    nblk = -(-(ntiles * (MOE_TM * TOP_K + ne * (MOE_ALIGN - 1)) + ne * (MOE_R - 1)) // MOE_R)
    nblk_pad = -(-nblk // LANES) * LANES
    dst, npieces, off, cntp, blk_exp, nused = pl.pallas_call(
        _moe_layout_kernel,
        out_shape=[jax.ShapeDtypeStruct((ntiles, MOE_NP), jnp.int32),
                   jax.ShapeDtypeStruct((ntiles, LANES), jnp.int32),
                   jax.ShapeDtypeStruct((ntiles, ne), F32),
                   jax.ShapeDtypeStruct((ntiles, ne), F32),
                   jax.ShapeDtypeStruct((1, nblk_pad), jnp.int32),
                   jax.ShapeDtypeStruct((1, LANES), jnp.int32)],
        compiler_params=pltpu.CompilerParams(vmem_limit_bytes=VMEM_LIMIT),
        name="moe_layout",
    )(cnt)
    return dst.reshape(-1), npieces[:, 0], off, cntp, nblk, nused[0, :1], blk_exp[0, :nblk]


def _moe(h2, gates, gates_t, cnt_raw, tn, layer, wg, wu, wd, sg, su, sd, x, g2, ln_g, ln_b):
    bsz, s, d = x.shape
    t = bsz * s
    tm = MOE_TM
    _, ne, _, de = wg.shape
    per_b = s // tm
    ntiles = t // tm
    dst, npieces, off_f, cnt_f, nblk, nused, blk_exp = _moe_layout(cnt_raw, tn)
    scalars = (dst, npieces)
    row = pl.BlockSpec((1, 1, ne), lambda i, *_: (i, 0, 0))
    col = pl.BlockSpec((1, ne, 1), lambda i, *_: (i, 0, 0))
    tok = pl.BlockSpec((tm, d), lambda i, *_: (i, 0))
    par = pl.BlockSpec((1, d), lambda i, *_: (0, 0))
    rows = nblk * MOE_R

    xs = pl.pallas_call(
        _dispatch_kernel,
        grid_spec=pltpu.PrefetchScalarGridSpec(
            num_scalar_prefetch=2, grid=(ntiles,),
            in_specs=[tok,
                      pl.BlockSpec((1, ne, tm), lambda i, *_: (i // per_b, 0, i % per_b)),
                      col, row, row],
            out_specs=pl.BlockSpec(memory_space=pl.ANY),
            scratch_shapes=[pltpu.VMEM((2, MOE_NP, MOE_ALIGN, d), BF16), pltpu.SemaphoreType.DMA((2,))]),
        out_shape=jax.ShapeDtypeStruct((rows // MOE_ALIGN, MOE_ALIGN, d), BF16),
        compiler_params=_cparams(("arbitrary",)),
        name="moe_dispatch",
    )(*scalars, h2.reshape(t, d), gates_t, off_f.reshape(ntiles, ne, 1), off_f.reshape(ntiles, 1, ne),
      cnt_f.reshape(ntiles, 1, ne))

    def blk_index(j, be, nu):
        return jnp.maximum(jnp.minimum(j, nu[0] - 1), 0), 0

    blk = pl.BlockSpec((MOE_R, d), blk_index)
    ys = pl.pallas_call(
        _expert_ffn_kernel,
        grid_spec=pltpu.PrefetchScalarGridSpec(
            num_scalar_prefetch=2, grid=(nblk,),
            in_specs=[blk,
                      pl.BlockSpec((1, 1, d, de), lambda j, be, nu: (layer, be[j], 0, 0)),
                      pl.BlockSpec((1, 1, d, de), lambda j, be, nu: (layer, be[j], 0, 0)),
                      pl.BlockSpec((1, 1, de, d), lambda j, be, nu: (layer, be[j], 0, 0))],
            out_specs=blk),
        out_shape=jax.ShapeDtypeStruct((rows, d), BF16),
        compiler_params=_cparams(("arbitrary",)),
        name="moe_expert_ffn",
    )(blk_exp, nused, xs.reshape(rows, d), wg, wu, wd)

    out = pl.pallas_call(
        _combine_kernel,
        grid_spec=pltpu.PrefetchScalarGridSpec(
            num_scalar_prefetch=2, grid=(ntiles,),
            in_specs=[pl.BlockSpec(memory_space=pl.ANY),
                      pl.BlockSpec((tm, ne), lambda i, *_: (i, 0)),
                      tok, row, col, col,
                      pl.BlockSpec(sg.shape, lambda i, *_: (0, 0)),
                      pl.BlockSpec(su.shape, lambda i, *_: (0, 0)),
                      pl.BlockSpec(sd.shape, lambda i, *_: (0, 0)),
                      tok,
                      pl.BlockSpec((1, 1, d), lambda i, *_: (i // per_b, 0, 0)),
                      par, par],
            out_specs=tok,
            scratch_shapes=[pltpu.VMEM((2, MOE_NP, MOE_ALIGN, d), BF16), pltpu.SemaphoreType.DMA((2,))]),
        out_shape=jax.ShapeDtypeStruct((t, d), F32),
        compiler_params=_cparams(("arbitrary",)),
        name="moe_combine",
    )(*scalars, ys.reshape(rows // MOE_ALIGN, MOE_ALIGN, d), gates.reshape(t, ne), h2.reshape(t, d),
      off_f.reshape(ntiles, 1, ne),
      off_f.reshape(ntiles, ne, 1), cnt_f.reshape(ntiles, ne, 1), sg, su, sd, x.reshape(t, d), g2, ln_g, ln_b)
    return out.reshape(bsz, s, d)


def _pad_cols(w, n):
    return jnp.pad(w, ((0, 0), (0, n - w.shape[1])))


def kernel(x, c, rpe_bias, ada_w, ada_b, ln_mix_g, ln_mix_b, ln_ffn_g, ln_ffn_b, ev_w_in, ev_gk_w2, ev_gk_b, ev_norm, ev_w_out, od_w_in, od_conv_w, od_a_log, od_dt_bias, od_norm, od_w_out, moe_router_w, moe_router_b, moe_w_gate, moe_w_up, moe_w_down, sh_w_gate, sh_w_up, sh_w_down):
    bsz, s, d = x.shape
    mod = _ada(c, ada_w, ada_b)
    tiles = _rpe_tiles(rpe_bias)

    for layer in range(DEPTH):
        sh1, sc1, g1, sh2, sc2, g2 = [mod[layer, :, u * d:(u + 1) * d].reshape(bsz, 1, d) for u in range(6)]
        i = layer // 2
        if layer % 2 == 0:
            n_main = 3 * MOBA_W + 2 * GLA_QK_W + 2 * GLA_V_W
            w_in = jnp.concatenate([ev_w_in[i][:, :n_main], _pad_cols(ev_w_in[i][:, n_main:], LANES)], axis=1)
            proj = _mod_matmul(x, sc1, sh1, w_in.astype(BF16))
            nb = MOBA_W // LANES
            o_a = _moba(proj, tiles, 0, nb, 2 * nb)
            gk_w2p = jnp.pad(ev_gk_w2[i], ((0, LANES - GLA_GATE_RANK), (0, 0)))
            gla0 = 3 * MOBA_W
            o_b = _gla(proj, gk_w2p, ev_gk_b[i].reshape(1, -1), ev_norm[i].reshape(1, -1),
                       gla0 // GLA_QK_W, gla0 // GLA_QK_W + 1,
                       (gla0 + 2 * GLA_QK_W) // GLA_V_W, (gla0 + 2 * GLA_QK_W) // GLA_V_W + 1,
                       n_main // LANES)
            w_out = ev_w_out[i].astype(BF16)
            acts, ws = [o_a, o_b], [w_out[:MOBA_W], w_out[MOBA_W:]]
        else:
            n_main = 4 * GDN_W
            w_in = jnp.concatenate([od_w_in[i][:, :n_main], _pad_cols(od_w_in[i][:, n_main:], LANES)], axis=1)
            proj = _mod_matmul(x, sc1, sh1, w_in.astype(BF16))
            pvec = jnp.zeros((2, LANES), F32)
            pvec = pvec.at[0, GDN_HEADS:2 * GDN_HEADS].set(od_a_log[i])
            pvec = pvec.at[1, GDN_HEADS:2 * GDN_HEADS].set(od_dt_bias[i])
            o = _gdn(proj, od_conv_w[i], pvec, od_norm[i].reshape(1, -1))
            acts, ws = [o], [od_w_out[i].astype(BF16)]

        x, h2, rl = _mixout(acts, ws, x, g1, ln_mix_g[layer].reshape(1, d), ln_mix_b[layer].reshape(1, d),
                            sc2, sh2, moe_router_w[layer].T)
        router_tn = min(1024, s)
        gates, gates_t, cnt_raw = _router(rl, moe_router_b[layer], router_tn)
        x = _moe(h2, gates, gates_t, cnt_raw, router_tn, layer, moe_w_gate, moe_w_up, moe_w_down,
                 sh_w_gate[layer].astype(BF16), sh_w_up[layer].astype(BF16),
                 sh_w_down[layer].astype(BF16), x, g2, ln_ffn_g[layer].reshape(1, d), ln_ffn_b[layer].reshape(1, d))
    return x
```

```python
import functools
import math

import numpy as np
import jax
import jax.numpy as jnp
from jax import lax
from jax.experimental import pallas as pl
from jax.experimental.pallas import tpu as pltpu

F32 = jnp.float32
BF16 = jnp.bfloat16
HI = lax.Precision.HIGHEST
NT = (((1,), (1,)), ((), ()))
TN = (((0,), (0,)), ((), ()))
NEG = -1e30
LOG2E = math.log2(math.e)

LANES = 128
VMEM_LIMIT = 56 * 1024 * 1024

DEPTH = 2
MOBA_HEAD_DIM = 128
MOBA_HEADS = 4
MOBA_BLOCK = 256
MOBA_TOPK = 3
MOBA_GROUP = 8
MOBA_SUB = 4
MOBA_DEN_ROWS = 16
GLA_DV = 128
GLA_HEADS = 4
GLA_DK = 64
GLA_GATE_RANK = 16
GLA_GATE_NORM = 16.0
GLA_CHUNK = 64
GDN_DK = 128
GDN_DV = 128
GDN_HEADS = 8
GDN_CONV = 4
GDN_CHUNK = 64
GDN_GROUP = 4
RPE_BUCKETS = 32
RPE_MAX_DIST = 2048
RPE_TILES = 8
N_EXPERTS = 64
TOP_K = 6
N_GROUPS = 8
TOPK_GROUPS = 4
D_EXPERT = 256
ROUTED_SCALE = 2.5
DEEPNORM_ALPHA = float((2 * DEPTH) ** 0.25)
LN_EPS = 1e-5
NORM_EPS = 1e-6

MOBA_W = MOBA_HEADS * MOBA_HEAD_DIM
GLA_QK_W = GLA_HEADS * GLA_DK
GLA_V_W = GLA_HEADS * GLA_DV
GDN_W = GDN_HEADS * GDN_DK


def _cparams(sem):
    return pltpu.CompilerParams(dimension_semantics=sem, vmem_limit_bytes=VMEM_LIMIT)


def _sigmoid(x):
    return 1.0 / (1.0 + jnp.exp(-x))


def _silu(x):
    return x * _sigmoid(x)


def _softplus(x):
    return jnp.maximum(x, 0.0) + jnp.log(1.0 + jnp.exp(-jnp.abs(x)))


def _layer_norm(z, g, b):
    mu = jnp.mean(z, axis=-1, keepdims=True)
    zc = z - mu
    var = jnp.mean(zc * zc, axis=-1, keepdims=True)
    return zc * lax.rsqrt(var + LN_EPS) * g + b


def _ada_kernel(c_ref, w_ref, b_ref, o_ref):
    ca = _silu(c_ref[...])
    o_ref[0] = jnp.dot(ca, w_ref[0], precision=HI, preferred_element_type=F32) + b_ref[0]


def _ada(c, ada_w, ada_b):
    depth, d, n = ada_w.shape
    bsz = c.shape[0]
    tn = 6 * LANES
    return pl.pallas_call(
        _ada_kernel,
        grid=(depth, n // tn),
        in_specs=[pl.BlockSpec((bsz, d), lambda l, j: (0, 0)),
                  pl.BlockSpec((1, d, tn), lambda l, j: (l, 0, j)),
                  pl.BlockSpec((1, 1, tn), lambda l, j: (l, 0, j))],
        out_specs=pl.BlockSpec((1, bsz, tn), lambda l, j: (l, 0, j)),
        out_shape=jax.ShapeDtypeStruct((depth, bsz, n), F32),
        compiler_params=_cparams(("arbitrary", "arbitrary")),
        name="ada_mod",
    )(c, ada_w, ada_b.reshape(depth, 1, n))


def _modmm_kernel(x_ref, sc_ref, sh_ref, w_ref, o_ref, h_scr):
    @pl.when(pl.program_id(2) == 0)
    def _():
        h_scr[...] = (x_ref[0] * (1.0 + sc_ref[0]) + sh_ref[0]).astype(BF16)

    o_ref[0] = jnp.dot(h_scr[...], w_ref[...], preferred_element_type=F32).astype(o_ref.dtype)


def _col_tile(n, cap):
    best = LANES
    for t in range(LANES, cap + 1, LANES):
        if n % t == 0:
            best = t
    return best


def _mod_matmul(x, sc, sh, w, tm=512, tn_cap=4608):
    bsz, s, d = x.shape
    n = w.shape[1]
    tm = min(tm, s)
    tn = _col_tile(n, tn_cap)
    return pl.pallas_call(
        _modmm_kernel,
        grid=(bsz, s // tm, n // tn),
        in_specs=[pl.BlockSpec((1, tm, d), lambda b, i, j: (b, i, 0)),
                  pl.BlockSpec((1, 1, d), lambda b, i, j: (b, 0, 0)),
                  pl.BlockSpec((1, 1, d), lambda b, i, j: (b, 0, 0)),
                  pl.BlockSpec((d, tn), lambda b, i, j: (0, j))],
        out_specs=pl.BlockSpec((1, tm, tn), lambda b, i, j: (b, i, j)),
        out_shape=jax.ShapeDtypeStruct((bsz, s, n), BF16),
        scratch_shapes=[pltpu.VMEM((tm, d), BF16)],
        compiler_params=_cparams(("arbitrary", "arbitrary", "arbitrary")),
        name="mod_matmul",
    )(x, sc, sh, w)


def _rpe_lower_bounds():
    exact = RPE_BUCKETS // 2
    d = np.arange(0, 2 * RPE_MAX_DIST, dtype=np.int64)
    logd = np.log(np.maximum(d, 1).astype(np.float64) / exact)
    large = exact + (logd / math.log(RPE_MAX_DIST / exact) * (RPE_BUCKETS - exact)).astype(np.int64)
    large = np.minimum(large, RPE_BUCKETS - 1)
    bucket = np.where(d < exact, d, large)
    return [int(np.argmax(bucket >= k)) for k in range(RPE_BUCKETS)]


def _rpe_tiles_kernel(lo, rpe_ref, o_ref):
    h = pl.program_id(0)
    j = pl.program_id(1)
    blk = o_ref.shape[-1]
    key = lax.broadcasted_iota(jnp.int32, (blk, blk), 0)
    qry = lax.broadcasted_iota(jnp.int32, (blk, blk), 1)
    dist = j * blk + qry - key
    val = jnp.full((blk, blk), rpe_ref[0, h], F32)
    for k in range(1, RPE_BUCKETS):
        val = jnp.where(dist >= lo[k], rpe_ref[k, h], val)
    o_ref[0, 0] = jnp.where(dist >= 0, val * LOG2E, NEG)


def _rpe_tiles(rpe_bias):
    heads = rpe_bias.shape[1]
    lo = _rpe_lower_bounds()
    assert lo[-1] <= (RPE_TILES - 1) * MOBA_BLOCK - (MOBA_BLOCK - 1)
    return pl.pallas_call(
        functools.partial(_rpe_tiles_kernel, lo),
        grid=(heads, RPE_TILES),
        in_specs=[pl.BlockSpec(memory_space=pltpu.SMEM)],
        out_specs=pl.BlockSpec((1, 1, MOBA_BLOCK, MOBA_BLOCK), lambda h, j: (h, j, 0, 0)),
        out_shape=jax.ShapeDtypeStruct((heads, RPE_TILES, MOBA_BLOCK, MOBA_BLOCK), F32),
        compiler_params=_cparams(("arbitrary", "arbitrary")),
        name="rpe_tiles",
    )(rpe_bias)


def _moba_kernel(q_ref, k_ref, v_ref, t_ref, o_ref, kb_scr, vt_scr, km_scr, sel_scr, far_scr):
    i = pl.program_id(2)
    nkb, blk, dh = kb_scr.shape

    @pl.when(i == 0)
    def _():
        for n in range(nkb):
            kn = k_ref[0, n * blk:(n + 1) * blk, :]
            kb_scr[n] = kn.astype(BF16)
            km_scr[n:n + 1, :] = jnp.mean(kn.astype(F32), axis=0, keepdims=True)
            vt_scr[n, :dh] = v_ref[0, n * blk:(n + 1) * blk, :].astype(F32).T.astype(BF16)
            ones_row = lax.broadcasted_iota(jnp.int32, (MOBA_DEN_ROWS, blk), 0) == 0
            vt_scr[n, dh:] = jnp.where(ones_row, 1.0, 0.0).astype(BF16)

    q = q_ref[0].astype(F32)
    gate = lax.dot_general(km_scr[...], q, NT, precision=HI, preferred_element_type=F32)
    bidx = lax.broadcasted_iota(jnp.int32, gate.shape, 0)
    past = bidx < i
    g = jnp.where(past, gate, -jnp.inf)
    sel = None
    for _ in range(MOBA_TOPK):
        m = jnp.max(g, axis=0, keepdims=True)
        first = jnp.min(jnp.where(g == m, bidx, nkb), axis=0, keepdims=True)
        hit = bidx == first
        sel = hit if sel is None else jnp.logical_or(sel, hit)
        g = jnp.where(hit, -jnp.inf, g)
    mask = jnp.where(jnp.logical_or(jnp.logical_and(sel, past), bidx == i), 0.0, NEG)
    sel_scr[...] = mask
    far_bias = t_ref[0, RPE_TILES - 1, 0:1, :]
    far_scr[...] = mask + jnp.where(i - bidx >= RPE_TILES - 1, far_bias, 0.0)

    qs = (q * (dh ** -0.5 * LOG2E)).astype(BF16)

    m0 = jnp.full((1, blk), NEG, F32)
    acc0 = jnp.zeros((dh + MOBA_DEN_ROWS, blk), F32)
    def body(far, g, carry):
        m, acc = carry
        nsub = MOBA_GROUP // MOBA_SUB
        blocks = [[jnp.minimum(g * MOBA_GROUP + k * MOBA_SUB + u, nkb - 1) for u in range(MOBA_SUB)]
                  for k in range(nsub)]
        scores, probs, alphas = {}, {}, {}

        def emit_scores(k):
            out = []
            for n in blocks[k]:
                s = lax.dot_general(kb_scr[n], qs, NT, preferred_element_type=F32)
                if far:
                    out.append(s + far_scr[pl.ds(n, 1), :])
                else:
                    out.append(s + t_ref[0, jnp.clip(i - n, 0, RPE_TILES - 1)] + sel_scr[pl.ds(n, 1), :])
            scores[k] = out

        def emit_softmax(k, m):
            m_new = m
            for s in scores[k]:
                m_new = jnp.maximum(m_new, jnp.max(s, axis=0, keepdims=True))
            alphas[k] = jnp.exp2(m - m_new)
            probs[k] = [jnp.exp2(s - m_new).astype(BF16) for s in scores[k]]
            return m_new

        def emit_values(k, acc):
            acc = alphas[k] * acc
            for n, p in zip(blocks[k], probs[k]):
                acc = acc + jnp.dot(vt_scr[n], p, preferred_element_type=F32)
            return acc

        emit_scores(0)
        for k in range(nsub):
            if k + 1 < nsub:
                emit_scores(k + 1)
            if k >= 1:
                acc = emit_values(k - 1, acc)
            m = emit_softmax(k, m)
        acc = emit_values(nsub - 1, acc)
        return m, acc

    n_far = jnp.maximum(i - (RPE_TILES - 2), 0) // MOBA_GROUP
    n_all = i // MOBA_GROUP + 1
    carry = lax.fori_loop(0, n_far, functools.partial(body, True), (m0, acc0))
    _, acc = lax.fori_loop(n_far, n_all, functools.partial(body, False), carry)
    o_ref[0] = (acc[:dh] / acc[dh:dh + 1]).T.astype(o_ref.dtype)


def _moba(proj, tiles, q_col, k_col, v_col):
    bsz, s, _ = proj.shape
    dh, blk, heads = MOBA_HEAD_DIM, MOBA_BLOCK, MOBA_HEADS
    nkb = s // blk
    return pl.pallas_call(
        _moba_kernel,
        grid=(bsz, heads, nkb),
        in_specs=[pl.BlockSpec((1, blk, dh), lambda b, h, i: (b, i, q_col + h)),
                  pl.BlockSpec((1, s, dh), lambda b, h, i: (b, 0, k_col + h)),
                  pl.BlockSpec((1, s, dh), lambda b, h, i: (b, 0, v_col + h)),
                  pl.BlockSpec((1, RPE_TILES, blk, blk), lambda b, h, i: (h, 0, 0, 0))],
        out_specs=pl.BlockSpec((1, blk, dh), lambda b, h, i: (b, i, h)),
        out_shape=jax.ShapeDtypeStruct((bsz, s, heads * dh), BF16),
        scratch_shapes=[pltpu.VMEM((nkb, blk, dh), BF16),
                        pltpu.VMEM((nkb, dh + MOBA_DEN_ROWS, blk), BF16),
                        pltpu.VMEM((nkb, dh), F32),
                        pltpu.VMEM((nkb, blk), F32),
                        pltpu.VMEM((nkb, blk), F32)],
        compiler_params=_cparams(("arbitrary", "arbitrary", "arbitrary")),
        name="moba_attention",
    )(proj, proj, proj, tiles)


def _gla_kernel(q_ref, k_ref, v_ref, gg_ref, glr_ref, w2_ref, gb_ref, nw_ref, o_ref, st_scr):
    @pl.when(pl.program_id(1) == 0)
    def _():
        st_scr[...] = jnp.zeros(st_scr.shape, F32)

    tg = q_ref.shape[1]
    c = GLA_CHUNK
    x = jnp.dot(glr_ref[0].astype(F32), w2_ref[...], precision=HI, preferred_element_type=F32) + gb_ref[...]
    lg = -_softplus(-x) * (1.0 / GLA_GATE_NORM)
    row = lax.broadcasted_iota(jnp.int32, (c, c), 0)
    col = lax.broadcasted_iota(jnp.int32, (c, c), 1)
    tri = (row >= col).astype(F32)
    hc = GLA_HEADS * c
    rr = lax.broadcasted_iota(jnp.int32, (hc, hc), 0)
    cc = lax.broadcasted_iota(jnp.int32, (hc, hc), 1)
    incl = jnp.logical_and(rr // c == cc // c, rr >= cc)
    nw = nw_ref[...]
    heads = range(GLA_HEADS)

    chunks = []
    for ci in range(tg // c):
        rows = slice(ci * c, (ci + 1) * c)
        b = jnp.dot(tri, lg[rows], precision=HI, preferred_element_type=F32)
        bl = b[c - 1:c, :]
        q = q_ref[0, rows, :].astype(F32) * GLA_DK ** -0.5
        k = k_ref[0, rows, :].astype(F32)
        q_e = (q * jnp.exp(b)).astype(BF16)
        k_e = (k * jnp.exp(-b)).astype(BF16)
        chunks.append(dict(rows=rows, q_e=q_e, k_e=k_e, k_end=(k * jnp.exp(bl - b)).astype(BF16), d=jnp.exp(bl)))
    for p in chunks:
        qs = jnp.concatenate([p["q_e"][:, h * GLA_DK:(h + 1) * GLA_DK] for h in heads], axis=0)
        ks = jnp.concatenate([p["k_e"][:, h * GLA_DK:(h + 1) * GLA_DK] for h in heads], axis=0)
        p["v"] = jnp.concatenate([v_ref[0, p["rows"], h * GLA_DV:(h + 1) * GLA_DV] for h in heads],
                                 axis=0).astype(BF16)
        a = lax.dot_general(qs, ks, NT, preferred_element_type=F32)
        p["a"] = jnp.where(incl, a, 0.0).astype(BF16)
    for p in chunks:
        p["o"] = jnp.dot(p["a"], p["v"], preferred_element_type=F32)

    for p in chunks:
        for h in heads:
            ks = slice(h * GLA_DK, (h + 1) * GLA_DK)
            vs = slice(h * GLA_DV, (h + 1) * GLA_DV)
            hr = slice(h * c, (h + 1) * c)
            st = st_scr[h]
            o = p["o"][hr] + lax.dot_general(p["q_e"][:, ks], st.astype(BF16), NT, preferred_element_type=F32)
            st_scr[h] = (st * p["d"][:, ks]
                         + lax.dot_general(p["v"][hr], p["k_end"][:, ks], TN, preferred_element_type=F32))
            o = o * lax.rsqrt(jnp.mean(o * o, axis=-1, keepdims=True) + NORM_EPS)
            o_ref[0, p["rows"], vs] = (o * nw * _silu(gg_ref[0, p["rows"], vs].astype(F32))).astype(o_ref.dtype)


def _gla(proj, gk_w2p, gk_b, o_norm, q_col, k_col, v_col, g_col, r_col, tg=512):
    bsz, s, _ = proj.shape
    tg = min(tg, s)
    qk, vw = GLA_QK_W, GLA_V_W
    return pl.pallas_call(
        _gla_kernel,
        grid=(bsz, s // tg),
        in_specs=[pl.BlockSpec((1, tg, qk), lambda b, t: (b, t, q_col)),
                  pl.BlockSpec((1, tg, qk), lambda b, t: (b, t, k_col)),
                  pl.BlockSpec((1, tg, vw), lambda b, t: (b, t, v_col)),
                  pl.BlockSpec((1, tg, vw), lambda b, t: (b, t, g_col)),
                  pl.BlockSpec((1, tg, LANES), lambda b, t: (b, t, r_col)),
                  pl.BlockSpec((LANES, qk), lambda b, t: (0, 0)),
                  pl.BlockSpec((1, qk), lambda b, t: (0, 0)),
                  pl.BlockSpec((1, GLA_DV), lambda b, t: (0, 0))],
        out_specs=pl.BlockSpec((1, tg, vw), lambda b, t: (b, t, 0)),
        out_shape=jax.ShapeDtypeStruct((bsz, s, vw), BF16),
        scratch_shapes=[pltpu.VMEM((GLA_HEADS, GLA_DV, GLA_DK), F32)],
        compiler_params=_cparams(("arbitrary", "arbitrary")),
        name="gla_mixer",
    )(proj, proj, proj, proj, proj, gk_w2p, gk_b, o_norm)


def _gdn_kernel(qkv_ref, gate_ref, ba_ref, cw_ref, pv_ref, nw_ref, o_ref, tail_scr, s_scr):
    @pl.when(pl.program_id(1) == 0)
    def _():
        tail_scr[...] = jnp.zeros(tail_scr.shape, F32)
        s_scr[...] = jnp.zeros(s_scr.shape, F32)

    tg = qkv_ref.shape[1]
    c, dk, grp = GDN_CHUNK, GDN_DK, GDN_GROUP
    gr = grp * c
    w = GDN_W

    x = qkv_ref[0].astype(F32)
    tail = tail_scr[...]
    tail_scr[...] = x[tg - 8:, :]
    r8 = lax.broadcasted_iota(jnp.int32, (8, 1), 0)
    y = x * cw_ref[GDN_CONV - 1:GDN_CONV, :]
    for sft in range(1, GDN_CONV):
        xs = pltpu.roll(x, sft, axis=0)
        head = jnp.where(r8 < sft, pltpu.roll(tail, sft, axis=0), xs[:8, :])
        xs = jnp.concatenate([head, xs[8:, :]], axis=0)
        y = y + xs * cw_ref[GDN_CONV - 1 - sft:GDN_CONV - sft, :]
    y = _silu(y)

    ba = ba_ref[0].astype(F32)
    beta_t = _sigmoid(ba)
    g_t = -jnp.exp(pv_ref[0:1, :]) * _softplus(ba + pv_ref[1:2, :])

    row = lax.broadcasted_iota(jnp.int32, (c, c), 0)
    col = lax.broadcasted_iota(jnp.int32, (c, c), 1)
    tri = (row >= col).astype(F32)
    rr = lax.broadcasted_iota(jnp.int32, (gr, gr), 0)
    cc = lax.broadcasted_iota(jnp.int32, (gr, gr), 1)
    same = (rr // c) == (cc // c)
    incl = jnp.logical_and(same, rr >= cc)
    strict = jnp.logical_and(same, rr > cc)
    eye = (rr == cc).astype(F32)
    halves = [(rr // sz) == (cc // sz) for sz in (2 ** e for e in range(1, int(math.log2(c)) + 1))]
    nw = nw_ref[...]

    probs = []
    for ci in range(tg // c):
        rows = slice(ci * c, (ci + 1) * c)
        gcum = jnp.dot(tri, g_t[rows], precision=HI, preferred_element_type=F32)
        gcum_t = gcum.T
        for gi in range(GDN_HEADS // grp):
            hs = [gi * grp + u for u in range(grp)]

            def stack(a, off):
                return jnp.concatenate([a[rows, off + h * dk: off + (h + 1) * dk] for h in hs], axis=0)

            q = stack(y, 0)
            k = stack(y, w)
            v = stack(y, 2 * w)
            q = q * lax.rsqrt(jnp.sum(q * q, axis=-1, keepdims=True) + NORM_EPS) * dk ** -0.5
            k = k * lax.rsqrt(jnp.sum(k * k, axis=-1, keepdims=True) + NORM_EPS)
            beta = jnp.concatenate([beta_t[rows, h:h + 1] for h in hs], axis=0)
            gc = jnp.concatenate([gcum[:, GDN_HEADS + h:GDN_HEADS + h + 1] for h in hs], axis=0)
            gc_row = jnp.concatenate([gcum_t[GDN_HEADS + h:GDN_HEADS + h + 1, :] for h in hs], axis=1)
            gl = jnp.concatenate([jnp.broadcast_to(gcum[c - 1:c, GDN_HEADS + h:GDN_HEADS + h + 1], (c, 1))
                                  for h in hs], axis=0)

            decay = jnp.where(incl, jnp.exp(jnp.where(incl, gc - gc_row, 0.0)), 0.0)
            kb = k * beta
            k16 = k.astype(BF16)
            a = lax.dot_general(kb.astype(BF16), k16, NT, preferred_element_type=F32)
            a = jnp.where(strict, a * decay, 0.0)
            eg = jnp.exp(gc)
            probs.append(dict(
                rows=rows, hs=hs, a16=a.astype(BF16),
                t=eye - jnp.where(halves[0], a, 0.0),
                rhs=jnp.concatenate([v * beta, kb * eg], axis=1).astype(BF16),
                attn=(lax.dot_general(q.astype(BF16), k16, NT, preferred_element_type=F32) * decay).astype(BF16),
                q_g=(q * eg).astype(BF16),
                k_end=(k * jnp.exp(gl - gc)).astype(BF16),
                d_last=[jnp.exp(gcum[c - 1:c, GDN_HEADS + h:GDN_HEADS + h + 1]) for h in hs]))

    for lvl in range(1, len(halves)):
        off16 = jnp.where(jnp.logical_and(halves[lvl], jnp.logical_not(halves[lvl - 1])), 1.0, 0.0).astype(BF16)
        t16s = [p["t"].astype(BF16) for p in probs]
        xs = [jnp.dot(p["a16"] * off16, t16, preferred_element_type=F32).astype(BF16)
              for p, t16 in zip(probs, t16s)]
        for p, t16, x in zip(probs, t16s, xs):
            p["t"] = p["t"] - jnp.dot(t16, x, preferred_element_type=F32)
    for p in probs:
        wk = jnp.dot(p["t"].astype(BF16), p["rhs"], preferred_element_type=F32)
        p["w_val"] = wk[:, :GDN_DV]
        p["k_cum"] = wk[:, GDN_DV:].astype(BF16)

    for p in probs:
        rows, hs = p["rows"], p["hs"]
        v_new = []
        for u, h in enumerate(hs):
            hr = slice(u * c, (u + 1) * c)
            st16 = s_scr[h].astype(BF16)
            v_new.append(p["w_val"][hr] - jnp.dot(p["k_cum"][hr], st16, preferred_element_type=F32))
        v16 = jnp.concatenate(v_new, axis=0).astype(BF16)
        o_intra = jnp.dot(p["attn"], v16, preferred_element_type=F32)
        for u, h in enumerate(hs):
            hr = slice(u * c, (u + 1) * c)
            st = s_scr[h]
            o = o_intra[hr] + jnp.dot(p["q_g"][hr], st.astype(BF16), preferred_element_type=F32)
            s_scr[h] = (st * p["d_last"][u]
                        + lax.dot_general(p["k_end"][hr], v16[hr], TN, preferred_element_type=F32))
            o = o * lax.rsqrt(jnp.mean(o * o, axis=-1, keepdims=True) + NORM_EPS)
            cs = slice(h * GDN_DV, (h + 1) * GDN_DV)
            o_ref[0, rows, cs] = (o * nw * _silu(gate_ref[0, rows, cs].astype(F32))).astype(o_ref.dtype)


def _gdn(proj, conv_w, pvec, o_norm, tg=512):
    bsz, s, _ = proj.shape
    tg = min(tg, s)
    w = GDN_W
    return pl.pallas_call(
        _gdn_kernel,
        grid=(bsz, s // tg),
        in_specs=[pl.BlockSpec((1, tg, 3 * w), lambda b, t: (b, t, 0)),
                  pl.BlockSpec((1, tg, w), lambda b, t: (b, t, 3)),
                  pl.BlockSpec((1, tg, LANES), lambda b, t: (b, t, 4 * w // LANES)),
                  pl.BlockSpec((GDN_CONV, 3 * w), lambda b, t: (0, 0)),
                  pl.BlockSpec((2, LANES), lambda b, t: (0, 0)),
                  pl.BlockSpec((1, GDN_DV), lambda b, t: (0, 0))],
        out_specs=pl.BlockSpec((1, tg, w), lambda b, t: (b, t, 0)),
        out_shape=jax.ShapeDtypeStruct((bsz, s, w), BF16),
        scratch_shapes=[pltpu.VMEM((8, 3 * w), F32),
                        pltpu.VMEM((GDN_HEADS, GDN_DK, GDN_DV), F32)],
        compiler_params=_cparams(("arbitrary", "arbitrary")),
        name="gdn_mixer",
    )(proj, proj, proj, conv_w, pvec, o_norm)


def _mixout_kernel(n_act, *refs):
    acts = refs[:n_act]
    ws = refs[n_act:2 * n_act]
    x_ref, g1_ref, lng_ref, lnb_ref, sc_ref, sh_ref, rw_ref = refs[2 * n_act:2 * n_act + 7]
    xo_ref, h_ref, rl_ref = refs[2 * n_act + 7:]
    y = None
    for a_ref, w_ref in zip(acts, ws):
        t = jnp.dot(a_ref[0].astype(BF16), w_ref[...], preferred_element_type=F32)
        y = t if y is None else y + t
    xn = _layer_norm(DEEPNORM_ALPHA * x_ref[0] + g1_ref[0] * y, lng_ref[...], lnb_ref[...])
    xo_ref[0] = xn
    h = xn * (1.0 + sc_ref[0]) + sh_ref[0]
    h_hi = h.astype(BF16)
    h_ref[0] = h_hi
    h_lo = (h - h_hi.astype(F32)).astype(BF16)
    rw = rw_ref[...]
    rw_hi = rw.astype(BF16)
    rw_lo = (rw - rw_hi.astype(F32)).astype(BF16)
    rl_ref[0] = (lax.dot_general(rw_hi, h_hi, NT, preferred_element_type=F32)
                 + lax.dot_general(rw_hi, h_lo, NT, preferred_element_type=F32)
                 + lax.dot_general(rw_lo, h_hi, NT, preferred_element_type=F32))


def _mixout(acts, ws, x, g1, ln_g, ln_b, sc2, sh2, router_wt, tm=512):
    bsz, s, d = x.shape
    tm = min(tm, s)
    n_act = len(acts)
    ne = router_wt.shape[0]
    vec = pl.BlockSpec((1, 1, d), lambda b, i: (b, 0, 0))
    par = pl.BlockSpec((1, d), lambda b, i: (0, 0))
    in_specs = ([pl.BlockSpec((1, tm, a.shape[-1]), lambda b, i: (b, i, 0)) for a in acts]
                + [pl.BlockSpec(w.shape, lambda b, i: (0, 0)) for w in ws]
                + [pl.BlockSpec((1, tm, d), lambda b, i: (b, i, 0)), vec, par, par, vec, vec,
                   pl.BlockSpec((ne, d), lambda b, i: (0, 0))])
    return pl.pallas_call(
        functools.partial(_mixout_kernel, n_act),
        grid=(bsz, s // tm),
        in_specs=in_specs,
        out_specs=[pl.BlockSpec((1, tm, d), lambda b, i: (b, i, 0)),
                   pl.BlockSpec((1, tm, d), lambda b, i: (b, i, 0)),
                   pl.BlockSpec((1, ne, tm), lambda b, i: (b, 0, i))],
        out_shape=[jax.ShapeDtypeStruct((bsz, s, d), F32),
                   jax.ShapeDtypeStruct((bsz, s, d), BF16),
                   jax.ShapeDtypeStruct((bsz, ne, s), F32)],
        compiler_params=_cparams(("arbitrary", "arbitrary")),
        name="mix_out",
    )(*acts, *ws, x, g1, ln_g, ln_b, sc2, sh2, router_wt)


def _first_max(vals, idx, axis, sentinel):
    m = jnp.max(vals, axis=axis, keepdims=True)
    first = jnp.min(jnp.where(vals == m, idx, sentinel), axis=axis, keepdims=True)
    return m, idx == first


def _router_kernel(rl_ref, rb_ref, g_ref, gt_ref, cnt_ref):
    ne, tn = rl_ref.shape[1], rl_ref.shape[2]
    gsz = ne // N_GROUPS
    scores = _sigmoid(rl_ref[0])
    sel = scores + rb_ref[...]
    ridx = lax.broadcasted_iota(jnp.int32, (gsz, tn), 0)
    gidx = lax.broadcasted_iota(jnp.int32, (N_GROUPS, tn), 0)
    gs = jnp.zeros((N_GROUPS, tn), F32)
    for g in range(N_GROUPS):
        sg = sel[g * gsz:(g + 1) * gsz, :]
        m1, hit = _first_max(sg, ridx, 0, gsz)
        m2 = jnp.max(jnp.where(hit, -jnp.inf, sg), axis=0, keepdims=True)
        gs = jnp.where(gidx == g, m1 + m2, gs)
    gsel = None
    for _ in range(TOPK_GROUPS):
        _, hit = _first_max(gs, gidx, 0, N_GROUPS)
        gsel = hit if gsel is None else jnp.logical_or(gsel, hit)
        gs = jnp.where(hit, -jnp.inf, gs)
    gself = gsel.astype(F32)
    emask = jnp.concatenate([jnp.broadcast_to(gself[g:g + 1, :], (gsz, tn)) for g in range(N_GROUPS)], axis=0)
    cand = jnp.where(emask > 0.5, sel, -jnp.inf)
    eidx = lax.broadcasted_iota(jnp.int32, cand.shape, 0)
    chosen = None
    for _ in range(TOP_K):
        _, hit = _first_max(cand, eidx, 0, ne)
        chosen = hit if chosen is None else jnp.logical_or(chosen, hit)
        cand = jnp.where(hit, -jnp.inf, cand)
    wsel = jnp.where(chosen, scores, 0.0)
    tot = jnp.sum(wsel, axis=0, keepdims=True)
    gates = wsel / (tot + 1e-20) * ROUTED_SCALE
    g_ref[0] = gates.T
    gt_ref[0] = gates
    tile_of = lax.broadcasted_iota(jnp.int32, (tn, LANES), 0) // MOE_TM
    ind = (tile_of == lax.broadcasted_iota(jnp.int32, (tn, LANES), 1)).astype(BF16)
    routed = jnp.where(gates > 0.0, 1.0, 0.0).astype(BF16)
    cnt_ref[0, 0] = jnp.dot(routed, ind, preferred_element_type=F32)


def _router(rl, router_b, tn=1024):
    bsz, ne, s = rl.shape
    tn = min(tn, s)
    return pl.pallas_call(
        _router_kernel,
        grid=(bsz, s // tn),
        in_specs=[pl.BlockSpec((1, ne, tn), lambda b, i: (b, 0, i)),
                  pl.BlockSpec((ne, 1), lambda b, i: (0, 0))],
        out_specs=[pl.BlockSpec((1, tn, ne), lambda b, i: (b, i, 0)),
                   pl.BlockSpec((1, ne, tn), lambda b, i: (b, 0, i)),
                   pl.BlockSpec((1, 1, ne, LANES), lambda b, i: (b, i, 0, 0))],
        out_shape=[jax.ShapeDtypeStruct((bsz, s, ne), F32),
                   jax.ShapeDtypeStruct((bsz, ne, s), F32),
                   jax.ShapeDtypeStruct((bsz, s // tn, ne, LANES), F32)],
        compiler_params=_cparams(("arbitrary", "arbitrary")),
        name="moe_router",
    )(rl, router_b.reshape(ne, 1))


MOE_TM = 256
MOE_ALIGN = 16
MOE_R = 512
MOE_LC = 512
MOE_LMAX = -(-(MOE_TM * TOP_K + N_EXPERTS * (MOE_ALIGN - 1)) // MOE_LC) * MOE_LC
MOE_NP = MOE_LMAX // MOE_ALIGN
MOE_LP = MOE_LC // MOE_ALIGN
POS_SPLIT = 256.0


def _ffn(x, wg, wu, wd):
    a = jnp.dot(x, wg, preferred_element_type=F32)
    u = jnp.dot(x, wu, preferred_element_type=F32)
    return jnp.dot((_silu(a) * u).astype(BF16), wd, preferred_element_type=F32)


def _split_pos(pos, axis):
    hi = jnp.floor(pos * (1.0 / POS_SPLIT)) * POS_SPLIT
    return jnp.concatenate([hi, pos - hi], axis=axis).astype(BF16)


def _for_each_piece(n_pieces, fn):
    unroll = 4

    def body(q, carry):
        for u in range(unroll):
            fn(q * unroll + u)
        return carry

    full = n_pieces // unroll
    lax.fori_loop(0, full, body, 0)

    def tail(p, carry):
        fn(p)
        return carry

    lax.fori_loop(full * unroll, n_pieces, tail, 0)


def _dispatch_kernel(dst_s, np_s, h_ref, gt_ref, offc_ref, offr_ref, cntr_ref, xs_hbm, sorted_scr, sems):
    i = pl.program_id(0)
    last = pl.num_programs(0) - 1
    slot = i % 2
    tm = h_ref.shape[0]
    routed = gt_ref[0] > 0.0
    t0 = lax.broadcasted_iota(jnp.int32, (tm, tm), 0)
    t1 = lax.broadcasted_iota(jnp.int32, (tm, tm), 1)
    earlier = jnp.where(t0 < t1, 1.0, 0.0).astype(BF16)
    rank_t = jnp.dot(jnp.where(routed, 1.0, 0.0).astype(BF16), earlier, preferred_element_type=F32)
    pos_t = jnp.where(routed, rank_t + offc_ref[0] + 1.0, 0.0)
    pos2 = _split_pos(pos_t, 0)
    x = h_ref[...]
    offr = offr_ref[0]
    endr = offr + cntr_ref[0]
    for c in range(MOE_LMAX // MOE_LC):
        r = (c * MOE_LC + lax.broadcasted_iota(jnp.int32, (MOE_LC, 1), 0)).astype(F32)
        owner = jnp.where(jnp.logical_and(r >= offr, r < endr), 1.0, 0.0).astype(BF16)
        possel = jnp.dot(jnp.concatenate([owner, owner], axis=1), pos2, preferred_element_type=F32)
        perm = jnp.where(possel == r + 1.0, 1.0, 0.0).astype(BF16)
        srt = jnp.dot(perm, x, preferred_element_type=F32).astype(BF16)
        sorted_scr[slot, c * MOE_LP:(c + 1) * MOE_LP] = srt.reshape(MOE_LP, MOE_ALIGN, srt.shape[-1])

    def piece(sl, tile, p):
        return pltpu.make_async_copy(sorted_scr.at[sl, p], xs_hbm.at[dst_s[tile * MOE_NP + p]], sems.at[sl])

    _for_each_piece(np_s[i], lambda p: piece(slot, i, p).start())

    @pl.when(i > 0)
    def _():
        _for_each_piece(np_s[i - 1], lambda p: piece(1 - slot, i - 1, p).wait())

    @pl.when(i == last)
    def _():
        _for_each_piece(np_s[i], lambda p: piece(slot, i, p).wait())


def _expert_ffn_kernel(be_s, nu_s, x_ref, wg_ref, wu_ref, wd_ref, y_ref, wg_scr, wu_scr, wd_scr):
    j = pl.program_id(0)

    @pl.when(jnp.logical_or(j == 0, be_s[j] != be_s[jnp.maximum(j - 1, 0)]))
    def _():
        wg_scr[...] = wg_ref[0, 0].astype(BF16)
        wu_scr[...] = wu_ref[0, 0].astype(BF16)
        wd_scr[...] = wd_ref[0, 0].astype(BF16)

    @pl.when(j < nu_s[0])
    def _():
        y_ref[...] = _ffn(x_ref[...], wg_scr[...], wu_scr[...], wd_scr[...]).astype(BF16)


def _combine_kernel(dst_s, np_s, ys_hbm, g_ref, h_ref, offr_ref, offc_ref, cntc_ref,
                    sg_ref, su_ref, sd_ref, x_ref, g2_ref, lng_ref, lnb_ref, o_ref, ys_scr, sems):
    i = pl.program_id(0)
    last = pl.num_programs(0) - 1
    slot = i % 2
    tm = h_ref.shape[0]

    def piece(sl, tile, p):
        return pltpu.make_async_copy(ys_hbm.at[dst_s[tile * MOE_NP + p]], ys_scr.at[sl, p], sems.at[sl])

    @pl.when(i == 0)
    def _():
        ys_scr[...] = jnp.zeros(ys_scr.shape, BF16)
        _for_each_piece(np_s[0], lambda p: piece(0, 0, p).start())

    @pl.when(i < last)
    def _():
        _for_each_piece(np_s[i + 1], lambda p: piece(1 - slot, i + 1, p).start())

    acc = _ffn(h_ref[...], sg_ref[...], su_ref[...], sd_ref[...])
    g = g_ref[...]
    routed = g > 0.0
    t0 = lax.broadcasted_iota(jnp.int32, (tm, tm), 0)
    t1 = lax.broadcasted_iota(jnp.int32, (tm, tm), 1)
    earlier = jnp.where(t0 > t1, 1.0, 0.0).astype(BF16)
    rank = jnp.dot(earlier, jnp.where(routed, 1.0, 0.0).astype(BF16), preferred_element_type=F32)
    pos = jnp.where(routed, rank + offr_ref[0] + 1.0, 0.0)
    pos2 = _split_pos(pos, 1)
    g16 = g.astype(BF16)
    offc = offc_ref[0]
    endc = offc + cntc_ref[0]
    _for_each_piece(np_s[i], lambda p: piece(slot, i, p).wait())
    for c in range(MOE_LMAX // MOE_LC):
        r = (c * MOE_LC + lax.broadcasted_iota(jnp.int32, (1, MOE_LC), 1)).astype(F32)
        owner = jnp.where(jnp.logical_and(r >= offc, r < endc), 1.0, 0.0).astype(BF16)
        possel = jnp.dot(pos2, jnp.concatenate([owner, owner], axis=0), preferred_element_type=F32)
        gsel = jnp.dot(g16, owner, preferred_element_type=F32)
        w = jnp.where(possel == r + 1.0, gsel, 0.0).astype(BF16)
        ys = ys_scr[slot, c * MOE_LP:(c + 1) * MOE_LP].reshape(MOE_LC, ys_scr.shape[-1])
        acc = acc + jnp.dot(w, ys, preferred_element_type=F32)
    z = DEEPNORM_ALPHA * x_ref[...] + g2_ref[0] * acc
    o_ref[...] = _layer_norm(z, lng_ref[...], lnb_ref[...])


def _ceil_to(v, m):
    return jnp.floor((v + (m - 1.0)) * (1.0 / m)) * m


def _moe_layout_kernel(cnt_ref, dst_ref, np_ref, off_ref, cntp_ref, be_ref, nu_ref):
    cnt = cnt_ref[...]
    ntiles, ne = cnt.shape
    cntp = _ceil_to(cnt, MOE_ALIGN)
    e0 = lax.broadcasted_iota(jnp.int32, (ne, ne), 0)
    e1 = lax.broadcasted_iota(jnp.int32, (ne, ne), 1)
    i0 = lax.broadcasted_iota(jnp.int32, (ntiles, ntiles), 0)
    i1 = lax.broadcasted_iota(jnp.int32, (ntiles, ntiles), 1)

    def mm(a, b):
        return jnp.dot(a, b, precision=HI, preferred_element_type=F32)

    off = mm(cntp, jnp.where(e0 < e1, 1.0, 0.0))
    before = mm(jnp.where(i0 > i1, 1.0, 0.0), cntp)
    tot = jnp.broadcast_to(jnp.sum(cntp, axis=0, keepdims=True), (8, ne))
    totr = _ceil_to(tot, MOE_R)
    base = mm(totr, jnp.where(e0 < e1, 1.0, 0.0))[0:1, :]
    delta = base + before - off
    step = mm(delta, jnp.where(e0 == e1, 1.0, 0.0) - jnp.where(e0 + 1 == e1, 1.0, 0.0))
    rowp = (lax.broadcasted_iota(jnp.int32, (1, MOE_NP), 1) * MOE_ALIGN).astype(F32)
    dst = jnp.broadcast_to(rowp, (ntiles, MOE_NP))
    for e in range(ne):
        dst = dst + jnp.where(off[:, e:e + 1] <= rowp, step[:, e:e + 1], 0.0)
    dst_ref[...] = (dst * (1.0 / MOE_ALIGN)).astype(jnp.int32)
    pieces = jnp.sum(cntp, axis=1, keepdims=True) * (1.0 / MOE_ALIGN)
    np_ref[...] = jnp.broadcast_to(pieces, np_ref.shape).astype(jnp.int32)
    off_ref[...] = off
    cntp_ref[...] = cntp
    tot_c = lax.dot_general(cntp, jnp.ones((ntiles, LANES), F32), TN, precision=HI, preferred_element_type=F32)
    end_c = mm(jnp.where(e0 >= e1, 1.0, 0.0), _ceil_to(tot_c, MOE_R))[:, 0:1]
    total = end_c[ne - 1:ne, :]
    first = jnp.minimum((lax.broadcasted_iota(jnp.int32, be_ref.shape, 1) * MOE_R).astype(F32), total - 1.0)
    be = jnp.sum(jnp.where(end_c <= first, 1.0, 0.0), axis=0, keepdims=True)
    be_ref[...] = jnp.minimum(be, ne - 1.0).astype(jnp.int32)
    nu_ref[...] = jnp.broadcast_to(total * (1.0 / MOE_R), nu_ref.shape).astype(jnp.int32)


def _moe_layout(cnt_raw, tn):
    ne = N_EXPERTS
    nsub = tn // MOE_TM
    cnt = jnp.transpose(cnt_raw[..., :nsub], (0, 1, 3, 2)).reshape(-1, ne)
    ntiles = cnt.shape[0]
    nblk = -(-(ntiles * (MOE_TM * TOP_K + ne * (MOE_ALIGN - 1)) + ne * (MOE_R - 1)) // MOE_R)
    nblk_pad = -(-nblk // LANES) * LANES
    dst, npieces, off, cntp, blk_exp, nused = pl.pallas_call(
        _moe_layout_kernel,
        out_shape=[jax.ShapeDtypeStruct((ntiles, MOE_NP), jnp.int32),
                   jax.ShapeDtypeStruct((ntiles, LANES), jnp.int32),
                   jax.ShapeDtypeStruct((ntiles, ne), F32),
                   jax.ShapeDtypeStruct((ntiles, ne), F32),
                   jax.ShapeDtypeStruct((1, nblk_pad), jnp.int32),
                   jax.ShapeDtypeStruct((1, LANES), jnp.int32)],
        compiler_params=pltpu.CompilerParams(vmem_limit_bytes=VMEM_LIMIT),
        name="moe_layout",
    )(cnt)
    return dst.reshape(-1), npieces[:, 0], off, cntp, nblk, nused[0, :1], blk_exp[0, :nblk]


def _moe(h2, gates, gates_t, cnt_raw, tn, layer, wg, wu, wd, sg, su, sd, x, g2, ln_g, ln_b):
    bsz, s, d = x.shape
    t = bsz * s
    tm = MOE_TM
    _, ne, _, de = wg.shape
    per_b = s // tm
    ntiles = t // tm
    dst, npieces, off_f, cnt_f, nblk, nused, blk_exp = _moe_layout(cnt_raw, tn)
    scalars = (dst, npieces)
    row = pl.BlockSpec((1, 1, ne), lambda i, *_: (i, 0, 0))
    col = pl.BlockSpec((1, ne, 1), lambda i, *_: (i, 0, 0))
    tok = pl.BlockSpec((tm, d), lambda i, *_: (i, 0))
    par = pl.BlockSpec((1, d), lambda i, *_: (0, 0))
    rows = nblk * MOE_R

    xs = pl.pallas_call(
        _dispatch_kernel,
        grid_spec=pltpu.PrefetchScalarGridSpec(
            num_scalar_prefetch=2, grid=(ntiles,),
            in_specs=[tok,
                      pl.BlockSpec((1, ne, tm), lambda i, *_: (i // per_b, 0, i % per_b)),
                      col, row, row],
            out_specs=pl.BlockSpec(memory_space=pl.ANY),
            scratch_shapes=[pltpu.VMEM((2, MOE_NP, MOE_ALIGN, d), BF16), pltpu.SemaphoreType.DMA((2,))]),
        out_shape=jax.ShapeDtypeStruct((rows // MOE_ALIGN, MOE_ALIGN, d), BF16),
        compiler_params=_cparams(("arbitrary",)),
        name="moe_dispatch",
    )(*scalars, h2.reshape(t, d), gates_t, off_f.reshape(ntiles, ne, 1), off_f.reshape(ntiles, 1, ne),
      cnt_f.reshape(ntiles, 1, ne))

    def blk_index(j, be, nu):
        return jnp.maximum(jnp.minimum(j, nu[0] - 1), 0), 0

    blk = pl.BlockSpec((MOE_R, d), blk_index)
    ys = pl.pallas_call(
        _expert_ffn_kernel,
        grid_spec=pltpu.PrefetchScalarGridSpec(
            num_scalar_prefetch=2, grid=(nblk,),
            in_specs=[blk,
                      pl.BlockSpec((1, 1, d, de), lambda j, be, nu: (layer, be[j], 0, 0)),
                      pl.BlockSpec((1, 1, d, de), lambda j, be, nu: (layer, be[j], 0, 0)),
                      pl.BlockSpec((1, 1, de, d), lambda j, be, nu: (layer, be[j], 0, 0))],
            out_specs=blk,
            scratch_shapes=[pltpu.VMEM((d, de), BF16), pltpu.VMEM((d, de), BF16), pltpu.VMEM((de, d), BF16)]),
        out_shape=jax.ShapeDtypeStruct((rows, d), BF16),
        compiler_params=_cparams(("arbitrary",)),
        name="moe_expert_ffn",
    )(blk_exp, nused, xs.reshape(rows, d), wg, wu, wd)

    out = pl.pallas_call(
        _combine_kernel,
        grid_spec=pltpu.PrefetchScalarGridSpec(
            num_scalar_prefetch=2, grid=(ntiles,),
            in_specs=[pl.BlockSpec(memory_space=pl.ANY),
                      pl.BlockSpec((tm, ne), lambda i, *_: (i, 0)),
                      tok, row, col, col,
                      pl.BlockSpec(sg.shape, lambda i, *_: (0, 0)),
                      pl.BlockSpec(su.shape, lambda i, *_: (0, 0)),
                      pl.BlockSpec(sd.shape, lambda i, *_: (0, 0)),
                      tok,
                      pl.BlockSpec((1, 1, d), lambda i, *_: (i // per_b, 0, 0)),
                      par, par],
            out_specs=tok,
            scratch_shapes=[pltpu.VMEM((2, MOE_NP, MOE_ALIGN, d), BF16), pltpu.SemaphoreType.DMA((2,))]),
        out_shape=jax.ShapeDtypeStruct((t, d), F32),
        compiler_params=_cparams(("arbitrary",)),
        name="moe_combine",
    )(*scalars, ys.reshape(rows // MOE_ALIGN, MOE_ALIGN, d), gates.reshape(t, ne), h2.reshape(t, d),
      off_f.reshape(ntiles, 1, ne),
      off_f.reshape(ntiles, ne, 1), cnt_f.reshape(ntiles, ne, 1), sg, su, sd, x.reshape(t, d), g2, ln_g, ln_b)
    return out.reshape(bsz, s, d)


def _pad_cols(w, n):
    return jnp.pad(w, ((0, 0), (0, n - w.shape[1])))


def kernel(x, c, rpe_bias, ada_w, ada_b, ln_mix_g, ln_mix_b, ln_ffn_g, ln_ffn_b, ev_w_in, ev_gk_w2, ev_gk_b, ev_norm, ev_w_out, od_w_in, od_conv_w, od_a_log, od_dt_bias, od_norm, od_w_out, moe_router_w, moe_router_b, moe_w_gate, moe_w_up, moe_w_down, sh_w_gate, sh_w_up, sh_w_down):
    bsz, s, d = x.shape
    mod = _ada(c, ada_w, ada_b)
    tiles = _rpe_tiles(rpe_bias)

    for layer in range(DEPTH):
        sh1, sc1, g1, sh2, sc2, g2 = [mod[layer, :, u * d:(u + 1) * d].reshape(bsz, 1, d) for u in range(6)]
        i = layer // 2
        if layer % 2 == 0:
            n_main = 3 * MOBA_W + 2 * GLA_QK_W + 2 * GLA_V_W
            w_in = jnp.concatenate([ev_w_in[i][:, :n_main], _pad_cols(ev_w_in[i][:, n_main:], LANES)], axis=1)
            proj = _mod_matmul(x, sc1, sh1, w_in.astype(BF16))
            nb = MOBA_W // LANES
            o_a = _moba(proj, tiles, 0, nb, 2 * nb)
            gk_w2p = jnp.pad(ev_gk_w2[i], ((0, LANES - GLA_GATE_RANK), (0, 0)))
            gla0 = 3 * MOBA_W
            o_b = _gla(proj, gk_w2p, ev_gk_b[i].reshape(1, -1), ev_norm[i].reshape(1, -1),
                       gla0 // GLA_QK_W, gla0 // GLA_QK_W + 1,
                       (gla0 + 2 * GLA_QK_W) // GLA_V_W, (gla0 + 2 * GLA_QK_W) // GLA_V_W + 1,
                       n_main // LANES)
            w_out = ev_w_out[i].astype(BF16)
            acts, ws = [o_a, o_b], [w_out[:MOBA_W], w_out[MOBA_W:]]
        else:
            n_main = 4 * GDN_W
            w_in = jnp.concatenate([od_w_in[i][:, :n_main], _pad_cols(od_w_in[i][:, n_main:], LANES)], axis=1)
            proj = _mod_matmul(x, sc1, sh1, w_in.astype(BF16))
            pvec = jnp.zeros((2, LANES), F32)
            pvec = pvec.at[0, GDN_HEADS:2 * GDN_HEADS].set(od_a_log[i])
            pvec = pvec.at[1, GDN_HEADS:2 * GDN_HEADS].set(od_dt_bias[i])
            o = _gdn(proj, od_conv_w[i], pvec, od_norm[i].reshape(1, -1))
            acts, ws = [o], [od_w_out[i].astype(BF16)]

        x, h2, rl = _mixout(acts, ws, x, g1, ln_mix_g[layer].reshape(1, d), ln_mix_b[layer].reshape(1, d),
                            sc2, sh2, moe_router_w[layer].T)
        router_tn = min(1024, s)
        gates, gates_t, cnt_raw = _router(rl, moe_router_b[layer], router_tn)
        x = _moe(h2, gates, gates_t, cnt_raw, router_tn, layer, moe_w_gate, moe_w_up, moe_w_down,
                 sh_w_gate[layer].astype(BF16), sh_w_up[layer].astype(BF16),
                 sh_w_down[layer].astype(BF16), x, g2, ln_ffn_g[layer].reshape(1, d), ln_ffn_b[layer].reshape(1, d))
    return x
```

```python
import functools
import math

import numpy as np
import jax
import jax.numpy as jnp
from jax import lax
from jax.experimental import pallas as pl
from jax.experimental.pallas import tpu as pltpu

F32 = jnp.float32
BF16 = jnp.bfloat16
HI = lax.Precision.HIGHEST
NT = (((1,), (1,)), ((), ()))
TN = (((0,), (0,)), ((), ()))
NEG = -1e30
LOG2E = math.log2(math.e)

LANES = 128
VMEM_LIMIT = 56 * 1024 * 1024

DEPTH = 2
MOBA_HEAD_DIM = 128
MOBA_HEADS = 4
MOBA_BLOCK = 256
MOBA_TOPK = 3
MOBA_GROUP = 8
MOBA_SUB = 4
MOBA_DEN_ROWS = 16
GLA_DV = 128
GLA_HEADS = 4
GLA_DK = 64
GLA_GATE_RANK = 16
GLA_GATE_NORM = 16.0
GLA_CHUNK = 64
GDN_DK = 128
GDN_DV = 128
GDN_HEADS = 8
GDN_CONV = 4
GDN_CHUNK = 64
GDN_GROUP = 4
RPE_BUCKETS = 32
RPE_MAX_DIST = 2048
RPE_TILES = 8
N_EXPERTS = 64
TOP_K = 6
N_GROUPS = 8
TOPK_GROUPS = 4
D_EXPERT = 256
ROUTED_SCALE = 2.5
DEEPNORM_ALPHA = float((2 * DEPTH) ** 0.25)
LN_EPS = 1e-5
NORM_EPS = 1e-6

MOBA_W = MOBA_HEADS * MOBA_HEAD_DIM
GLA_QK_W = GLA_HEADS * GLA_DK
GLA_V_W = GLA_HEADS * GLA_DV
GDN_W = GDN_HEADS * GDN_DK


def _cparams(sem):
    return pltpu.CompilerParams(dimension_semantics=sem, vmem_limit_bytes=VMEM_LIMIT)


def _sigmoid(x):
    return 1.0 / (1.0 + jnp.exp(-x))


def _silu(x):
    return x * _sigmoid(x)


def _softplus(x):
    return jnp.maximum(x, 0.0) + jnp.log(1.0 + jnp.exp(-jnp.abs(x)))


def _layer_norm(z, g, b):
    mu = jnp.mean(z, axis=-1, keepdims=True)
    zc = z - mu
    var = jnp.mean(zc * zc, axis=-1, keepdims=True)
    return zc * lax.rsqrt(var + LN_EPS) * g + b


def _ada_kernel(c_ref, w_ref, b_ref, o_ref):
    ca = _silu(c_ref[...])
    o_ref[0] = jnp.dot(ca, w_ref[0], precision=HI, preferred_element_type=F32) + b_ref[0]


def _ada(c, ada_w, ada_b):
    depth, d, n = ada_w.shape
    bsz = c.shape[0]
    tn = 6 * LANES
    return pl.pallas_call(
        _ada_kernel,
        grid=(depth, n // tn),
        in_specs=[pl.BlockSpec((bsz, d), lambda l, j: (0, 0)),
                  pl.BlockSpec((1, d, tn), lambda l, j: (l, 0, j)),
                  pl.BlockSpec((1, 1, tn), lambda l, j: (l, 0, j))],
        out_specs=pl.BlockSpec((1, bsz, tn), lambda l, j: (l, 0, j)),
        out_shape=jax.ShapeDtypeStruct((depth, bsz, n), F32),
        compiler_params=_cparams(("arbitrary", "arbitrary")),
        name="ada_mod",
    )(c, ada_w, ada_b.reshape(depth, 1, n))


def _modmm_kernel(x_ref, sc_ref, sh_ref, w_ref, o_ref, h_scr):
    @pl.when(pl.program_id(2) == 0)
    def _():
        h_scr[...] = (x_ref[0] * (1.0 + sc_ref[0]) + sh_ref[0]).astype(BF16)

    o_ref[0] = jnp.dot(h_scr[...], w_ref[...], preferred_element_type=F32).astype(o_ref.dtype)


def _col_tile(n, cap):
    best = LANES
    for t in range(LANES, cap + 1, LANES):
        if n % t == 0:
            best = t
    return best


def _mod_matmul(x, sc, sh, w, tm=512, tn_cap=4608):
    bsz, s, d = x.shape
    n = w.shape[1]
    tm = min(tm, s)
    tn = _col_tile(n, tn_cap)
    return pl.pallas_call(
        _modmm_kernel,
        grid=(bsz, s // tm, n // tn),
        in_specs=[pl.BlockSpec((1, tm, d), lambda b, i, j: (b, i, 0)),
                  pl.BlockSpec((1, 1, d), lambda b, i, j: (b, 0, 0)),
                  pl.BlockSpec((1, 1, d), lambda b, i, j: (b, 0, 0)),
                  pl.BlockSpec((d, tn), lambda b, i, j: (0, j))],
        out_specs=pl.BlockSpec((1, tm, tn), lambda b, i, j: (b, i, j)),
        out_shape=jax.ShapeDtypeStruct((bsz, s, n), BF16),
        scratch_shapes=[pltpu.VMEM((tm, d), BF16)],
        compiler_params=_cparams(("arbitrary", "arbitrary", "arbitrary")),
        name="mod_matmul",
    )(x, sc, sh, w)


def _rpe_lower_bounds():
    exact = RPE_BUCKETS // 2
    d = np.arange(0, 2 * RPE_MAX_DIST, dtype=np.int64)
    logd = np.log(np.maximum(d, 1).astype(np.float64) / exact)
    large = exact + (logd / math.log(RPE_MAX_DIST / exact) * (RPE_BUCKETS - exact)).astype(np.int64)
    large = np.minimum(large, RPE_BUCKETS - 1)
    bucket = np.where(d < exact, d, large)
    return [int(np.argmax(bucket >= k)) for k in range(RPE_BUCKETS)]


def _rpe_tiles_kernel(lo, rpe_ref, o_ref):
    h = pl.program_id(0)
    j = pl.program_id(1)
    blk = o_ref.shape[-1]
    key = lax.broadcasted_iota(jnp.int32, (blk, blk), 0)
    qry = lax.broadcasted_iota(jnp.int32, (blk, blk), 1)
    dist = j * blk + qry - key
    val = jnp.full((blk, blk), rpe_ref[0, h], F32)
    for k in range(1, RPE_BUCKETS):
        val = jnp.where(dist >= lo[k], rpe_ref[k, h], val)
    o_ref[0, 0] = jnp.where(dist >= 0, val * LOG2E, NEG)


def _rpe_tiles(rpe_bias):
    heads = rpe_bias.shape[1]
    lo = _rpe_lower_bounds()
    assert lo[-1] <= (RPE_TILES - 1) * MOBA_BLOCK - (MOBA_BLOCK - 1)
    return pl.pallas_call(
        functools.partial(_rpe_tiles_kernel, lo),
        grid=(heads, RPE_TILES),
        in_specs=[pl.BlockSpec(memory_space=pltpu.SMEM)],
        out_specs=pl.BlockSpec((1, 1, MOBA_BLOCK, MOBA_BLOCK), lambda h, j: (h, j, 0, 0)),
        out_shape=jax.ShapeDtypeStruct((heads, RPE_TILES, MOBA_BLOCK, MOBA_BLOCK), F32),
        compiler_params=_cparams(("arbitrary", "arbitrary")),
        name="rpe_tiles",
    )(rpe_bias)


def _moba_kernel(q_ref, k_ref, v_ref, t_ref, o_ref, kb_scr, vt_scr, km_scr, sel_scr, far_scr):
    i = pl.program_id(2)
    nkb, blk, dh = kb_scr.shape

    @pl.when(i == 0)
    def _():
        for n in range(nkb):
            kn = k_ref[0, n * blk:(n + 1) * blk, :]
            kb_scr[n] = kn.astype(BF16)
            km_scr[n:n + 1, :] = jnp.mean(kn.astype(F32), axis=0, keepdims=True)
            vt_scr[n, :dh] = v_ref[0, n * blk:(n + 1) * blk, :].astype(F32).T.astype(BF16)
            ones_row = lax.broadcasted_iota(jnp.int32, (MOBA_DEN_ROWS, blk), 0) == 0
            vt_scr[n, dh:] = jnp.where(ones_row, 1.0, 0.0).astype(BF16)

    q = q_ref[0].astype(F32)
    gate = lax.dot_general(km_scr[...], q, NT, precision=HI, preferred_element_type=F32)
    bidx = lax.broadcasted_iota(jnp.int32, gate.shape, 0)
    past = bidx < i
    g = jnp.where(past, gate, -jnp.inf)
    sel = None
    for _ in range(MOBA_TOPK):
        m = jnp.max(g, axis=0, keepdims=True)
        first = jnp.min(jnp.where(g == m, bidx, nkb), axis=0, keepdims=True)
        hit = bidx == first
        sel = hit if sel is None else jnp.logical_or(sel, hit)
        g = jnp.where(hit, -jnp.inf, g)
    mask = jnp.where(jnp.logical_or(jnp.logical_and(sel, past), bidx == i), 0.0, NEG)
    sel_scr[...] = mask
    far_bias = t_ref[0, RPE_TILES - 1, 0:1, :]
    far_scr[...] = mask + jnp.where(i - bidx >= RPE_TILES - 1, far_bias, 0.0)

    qs = (q * (dh ** -0.5 * LOG2E)).astype(BF16)

    m0 = jnp.full((1, blk), NEG, F32)
    acc0 = jnp.zeros((dh + MOBA_DEN_ROWS, blk), F32)
    def body(far, g, carry):
        m, acc = carry
        nsub = MOBA_GROUP // MOBA_SUB
        blocks = [[jnp.minimum(g * MOBA_GROUP + k * MOBA_SUB + u, nkb - 1) for u in range(MOBA_SUB)]
                  for k in range(nsub)]
        scores, probs, alphas = {}, {}, {}

        def emit_scores(k):
            out = []
            for n in blocks[k]:
                s = lax.dot_general(kb_scr[n], qs, NT, preferred_element_type=F32)
                if far:
                    out.append(s + far_scr[pl.ds(n, 1), :])
                else:
                    out.append(s + t_ref[0, jnp.clip(i - n, 0, RPE_TILES - 1)] + sel_scr[pl.ds(n, 1), :])
            scores[k] = out

        def emit_softmax(k, m):
            m_new = m
            for s in scores[k]:
                m_new = jnp.maximum(m_new, jnp.max(s, axis=0, keepdims=True))
            alphas[k] = jnp.exp2(m - m_new)
            probs[k] = [jnp.exp2(s - m_new).astype(BF16) for s in scores[k]]
            return m_new

        def emit_values(k, acc):
            acc = alphas[k] * acc
            for n, p in zip(blocks[k], probs[k]):
                acc = acc + jnp.dot(vt_scr[n], p, preferred_element_type=F32)
            return acc

        emit_scores(0)
        for k in range(nsub):
            if k + 1 < nsub:
                emit_scores(k + 1)
            if k >= 1:
                acc = emit_values(k - 1, acc)
            m = emit_softmax(k, m)
        acc = emit_values(nsub - 1, acc)
        return m, acc

    n_far = jnp.maximum(i - (RPE_TILES - 2), 0) // MOBA_GROUP
    n_all = i // MOBA_GROUP + 1
    carry = lax.fori_loop(0, n_far, functools.partial(body, True), (m0, acc0))
    _, acc = lax.fori_loop(n_far, n_all, functools.partial(body, False), carry)
    o_ref[0] = (acc[:dh] / acc[dh:dh + 1]).T.astype(o_ref.dtype)


def _moba(proj, tiles, q_col, k_col, v_col):
    bsz, s, _ = proj.shape
    dh, blk, heads = MOBA_HEAD_DIM, MOBA_BLOCK, MOBA_HEADS
    nkb = s // blk
    return pl.pallas_call(
        _moba_kernel,
        grid=(bsz, heads, nkb),
        in_specs=[pl.BlockSpec((1, blk, dh), lambda b, h, i: (b, i, q_col + h)),
                  pl.BlockSpec((1, s, dh), lambda b, h, i: (b, 0, k_col + h)),
                  pl.BlockSpec((1, s, dh), lambda b, h, i: (b, 0, v_col + h)),
                  pl.BlockSpec((1, RPE_TILES, blk, blk), lambda b, h, i: (h, 0, 0, 0))],
        out_specs=pl.BlockSpec((1, blk, dh), lambda b, h, i: (b, i, h)),
        out_shape=jax.ShapeDtypeStruct((bsz, s, heads * dh), BF16),
        scratch_shapes=[pltpu.VMEM((nkb, blk, dh), BF16),
                        pltpu.VMEM((nkb, dh + MOBA_DEN_ROWS, blk), BF16),
                        pltpu.VMEM((nkb, dh), F32),
                        pltpu.VMEM((nkb, blk), F32),
                        pltpu.VMEM((nkb, blk), F32)],
        compiler_params=_cparams(("arbitrary", "arbitrary", "arbitrary")),
        name="moba_attention",
    )(proj, proj, proj, tiles)


def _gla_kernel(q_ref, k_ref, v_ref, gg_ref, glr_ref, w2_ref, gb_ref, nw_ref, o_ref, st_scr):
    @pl.when(pl.program_id(1) == 0)
    def _():
        st_scr[...] = jnp.zeros(st_scr.shape, F32)

    tg = q_ref.shape[1]
    c = GLA_CHUNK
    x = jnp.dot(glr_ref[0].astype(F32), w2_ref[...], precision=HI, preferred_element_type=F32) + gb_ref[...]
    lg = -_softplus(-x) * (1.0 / GLA_GATE_NORM)
    row = lax.broadcasted_iota(jnp.int32, (c, c), 0)
    col = lax.broadcasted_iota(jnp.int32, (c, c), 1)
    tri = (row >= col).astype(F32)
    hc = GLA_HEADS * c
    rr = lax.broadcasted_iota(jnp.int32, (hc, hc), 0)
    cc = lax.broadcasted_iota(jnp.int32, (hc, hc), 1)
    incl = jnp.logical_and(rr // c == cc // c, rr >= cc)
    nw = nw_ref[...]
    heads = range(GLA_HEADS)

    chunks = []
    for ci in range(tg // c):
        rows = slice(ci * c, (ci + 1) * c)
        b = jnp.dot(tri, lg[rows], precision=HI, preferred_element_type=F32)
        bl = b[c - 1:c, :]
        q = q_ref[0, rows, :].astype(F32) * GLA_DK ** -0.5
        k = k_ref[0, rows, :].astype(F32)
        q_e = (q * jnp.exp(b)).astype(BF16)
        k_e = (k * jnp.exp(-b)).astype(BF16)
        chunks.append(dict(rows=rows, q_e=q_e, k_e=k_e, k_end=(k * jnp.exp(bl - b)).astype(BF16), d=jnp.exp(bl)))
    for p in chunks:
        qs = jnp.concatenate([p["q_e"][:, h * GLA_DK:(h + 1) * GLA_DK] for h in heads], axis=0)
        ks = jnp.concatenate([p["k_e"][:, h * GLA_DK:(h + 1) * GLA_DK] for h in heads], axis=0)
        p["v"] = jnp.concatenate([v_ref[0, p["rows"], h * GLA_DV:(h + 1) * GLA_DV] for h in heads],
                                 axis=0).astype(BF16)
        a = lax.dot_general(qs, ks, NT, preferred_element_type=F32)
        p["a"] = jnp.where(incl, a, 0.0).astype(BF16)
    for p in chunks:
        p["o"] = jnp.dot(p["a"], p["v"], preferred_element_type=F32)

    for p in chunks:
        for h in heads:
            ks = slice(h * GLA_DK, (h + 1) * GLA_DK)
            vs = slice(h * GLA_DV, (h + 1) * GLA_DV)
            hr = slice(h * c, (h + 1) * c)
            st = st_scr[h]
            o = p["o"][hr] + lax.dot_general(p["q_e"][:, ks], st.astype(BF16), NT, preferred_element_type=F32)
            st_scr[h] = (st * p["d"][:, ks]
                         + lax.dot_general(p["v"][hr], p["k_end"][:, ks], TN, preferred_element_type=F32))
            o = o * lax.rsqrt(jnp.mean(o * o, axis=-1, keepdims=True) + NORM_EPS)
            o_ref[0, p["rows"], vs] = (o * nw * _silu(gg_ref[0, p["rows"], vs].astype(F32))).astype(o_ref.dtype)


def _gla(proj, gk_w2p, gk_b, o_norm, q_col, k_col, v_col, g_col, r_col, tg=512):
    bsz, s, _ = proj.shape
    tg = min(tg, s)
    qk, vw = GLA_QK_W, GLA_V_W
    return pl.pallas_call(
        _gla_kernel,
        grid=(bsz, s // tg),
        in_specs=[pl.BlockSpec((1, tg, qk), lambda b, t: (b, t, q_col)),
                  pl.BlockSpec((1, tg, qk), lambda b, t: (b, t, k_col)),
                  pl.BlockSpec((1, tg, vw), lambda b, t: (b, t, v_col)),
                  pl.BlockSpec((1, tg, vw), lambda b, t: (b, t, g_col)),
                  pl.BlockSpec((1, tg, LANES), lambda b, t: (b, t, r_col)),
                  pl.BlockSpec((LANES, qk), lambda b, t: (0, 0)),
                  pl.BlockSpec((1, qk), lambda b, t: (0, 0)),
                  pl.BlockSpec((1, GLA_DV), lambda b, t: (0, 0))],
        out_specs=pl.BlockSpec((1, tg, vw), lambda b, t: (b, t, 0)),
        out_shape=jax.ShapeDtypeStruct((bsz, s, vw), BF16),
        scratch_shapes=[pltpu.VMEM((GLA_HEADS, GLA_DV, GLA_DK), F32)],
        compiler_params=_cparams(("arbitrary", "arbitrary")),
        name="gla_mixer",
    )(proj, proj, proj, proj, proj, gk_w2p, gk_b, o_norm)


def _gdn_kernel(qkv_ref, gate_ref, ba_ref, cw_ref, pv_ref, nw_ref, o_ref, tail_scr, s_scr):
    @pl.when(pl.program_id(1) == 0)
    def _():
        tail_scr[...] = jnp.zeros(tail_scr.shape, F32)
        s_scr[...] = jnp.zeros(s_scr.shape, F32)

    tg = qkv_ref.shape[1]
    c, dk, grp = GDN_CHUNK, GDN_DK, GDN_GROUP
    gr = grp * c
    w = GDN_W

    x = qkv_ref[0].astype(F32)
    tail = tail_scr[...]
    tail_scr[...] = x[tg - 8:, :]
    r8 = lax.broadcasted_iota(jnp.int32, (8, 1), 0)
    y = x * cw_ref[GDN_CONV - 1:GDN_CONV, :]
    for sft in range(1, GDN_CONV):
        xs = pltpu.roll(x, sft, axis=0)
        head = jnp.where(r8 < sft, pltpu.roll(tail, sft, axis=0), xs[:8, :])
        xs = jnp.concatenate([head, xs[8:, :]], axis=0)
        y = y + xs * cw_ref[GDN_CONV - 1 - sft:GDN_CONV - sft, :]
    y = _silu(y)

    ba = ba_ref[0].astype(F32)
    beta_t = _sigmoid(ba)
    g_t = -jnp.exp(pv_ref[0:1, :]) * _softplus(ba + pv_ref[1:2, :])

    row = lax.broadcasted_iota(jnp.int32, (c, c), 0)
    col = lax.broadcasted_iota(jnp.int32, (c, c), 1)
    tri = (row >= col).astype(F32)
    rr = lax.broadcasted_iota(jnp.int32, (gr, gr), 0)
    cc = lax.broadcasted_iota(jnp.int32, (gr, gr), 1)
    same = (rr // c) == (cc // c)
    incl = jnp.logical_and(same, rr >= cc)
    strict = jnp.logical_and(same, rr > cc)
    eye = (rr == cc).astype(F32)
    halves = [(rr // sz) == (cc // sz) for sz in (2 ** e for e in range(1, int(math.log2(c)) + 1))]
    nw = nw_ref[...]

    probs = []
    for ci in range(tg // c):
        rows = slice(ci * c, (ci + 1) * c)
        gcum = jnp.dot(tri, g_t[rows], precision=HI, preferred_element_type=F32)
        gcum_t = gcum.T
        for gi in range(GDN_HEADS // grp):
            hs = [gi * grp + u for u in range(grp)]

            def stack(a, off):
                return jnp.concatenate([a[rows, off + h * dk: off + (h + 1) * dk] for h in hs], axis=0)

            q = stack(y, 0)
            k = stack(y, w)
            v = stack(y, 2 * w)
            q = q * lax.rsqrt(jnp.sum(q * q, axis=-1, keepdims=True) + NORM_EPS) * dk ** -0.5
            k = k * lax.rsqrt(jnp.sum(k * k, axis=-1, keepdims=True) + NORM_EPS)
            beta = jnp.concatenate([beta_t[rows, h:h + 1] for h in hs], axis=0)
            gc = jnp.concatenate([gcum[:, GDN_HEADS + h:GDN_HEADS + h + 1] for h in hs], axis=0)
            gc_row = jnp.concatenate([gcum_t[GDN_HEADS + h:GDN_HEADS + h + 1, :] for h in hs], axis=1)
            gl = jnp.concatenate([jnp.broadcast_to(gcum[c - 1:c, GDN_HEADS + h:GDN_HEADS + h + 1], (c, 1))
                                  for h in hs], axis=0)

            decay = jnp.where(incl, jnp.exp(jnp.where(incl, gc - gc_row, 0.0)), 0.0)
            kb = k * beta
            k16 = k.astype(BF16)
            a = lax.dot_general(kb.astype(BF16), k16, NT, preferred_element_type=F32)
            a = jnp.where(strict, a * decay, 0.0)
            eg = jnp.exp(gc)
            probs.append(dict(
                rows=rows, hs=hs, a16=a.astype(BF16),
                t=eye - jnp.where(halves[0], a, 0.0),
                rhs=jnp.concatenate([v * beta, kb * eg], axis=1).astype(BF16),
                attn=(lax.dot_general(q.astype(BF16), k16, NT, preferred_element_type=F32) * decay).astype(BF16),
                q_g=(q * eg).astype(BF16),
                k_end=(k * jnp.exp(gl - gc)).astype(BF16),
                d_last=[jnp.exp(gcum[c - 1:c, GDN_HEADS + h:GDN_HEADS + h + 1]) for h in hs]))

    for lvl in range(1, len(halves)):
        off16 = jnp.where(jnp.logical_and(halves[lvl], jnp.logical_not(halves[lvl - 1])), 1.0, 0.0).astype(BF16)
        t16s = [p["t"].astype(BF16) for p in probs]
        xs = [jnp.dot(p["a16"] * off16, t16, preferred_element_type=F32).astype(BF16)
              for p, t16 in zip(probs, t16s)]
        for p, t16, x in zip(probs, t16s, xs):
            p["t"] = p["t"] - jnp.dot(t16, x, preferred_element_type=F32)
    for p in probs:
        wk = jnp.dot(p["t"].astype(BF16), p["rhs"], preferred_element_type=F32)
        p["w_val"] = wk[:, :GDN_DV]
        p["k_cum"] = wk[:, GDN_DV:].astype(BF16)

    for p in probs:
        rows, hs = p["rows"], p["hs"]
        v_new = []
        for u, h in enumerate(hs):
            hr = slice(u * c, (u + 1) * c)
            st16 = s_scr[h].astype(BF16)
            v_new.append(p["w_val"][hr] - jnp.dot(p["k_cum"][hr], st16, preferred_element_type=F32))
        v16 = jnp.concatenate(v_new, axis=0).astype(BF16)
        o_intra = jnp.dot(p["attn"], v16, preferred_element_type=F32)
        for u, h in enumerate(hs):
            hr = slice(u * c, (u + 1) * c)
            st = s_scr[h]
            o = o_intra[hr] + jnp.dot(p["q_g"][hr], st.astype(BF16), preferred_element_type=F32)
            s_scr[h] = (st * p["d_last"][u]
                        + lax.dot_general(p["k_end"][hr], v16[hr], TN, preferred_element_type=F32))
            o = o * lax.rsqrt(jnp.mean(o * o, axis=-1, keepdims=True) + NORM_EPS)
            cs = slice(h * GDN_DV, (h + 1) * GDN_DV)
            o_ref[0, rows, cs] = (o * nw * _silu(gate_ref[0, rows, cs].astype(F32))).astype(o_ref.dtype)


def _gdn(proj, conv_w, pvec, o_norm, tg=512):
    bsz, s, _ = proj.shape
    tg = min(tg, s)
    w = GDN_W
    return pl.pallas_call(
        _gdn_kernel,
        grid=(bsz, s // tg),
        in_specs=[pl.BlockSpec((1, tg, 3 * w), lambda b, t: (b, t, 0)),
                  pl.BlockSpec((1, tg, w), lambda b, t: (b, t, 3)),
                  pl.BlockSpec((1, tg, LANES), lambda b, t: (b, t, 4 * w // LANES)),
                  pl.BlockSpec((GDN_CONV, 3 * w), lambda b, t: (0, 0)),
                  pl.BlockSpec((2, LANES), lambda b, t: (0, 0)),
                  pl.BlockSpec((1, GDN_DV), lambda b, t: (0, 0))],
        out_specs=pl.BlockSpec((1, tg, w), lambda b, t: (b, t, 0)),
        out_shape=jax.ShapeDtypeStruct((bsz, s, w), BF16),
        scratch_shapes=[pltpu.VMEM((8, 3 * w), F32),
                        pltpu.VMEM((GDN_HEADS, GDN_DK, GDN_DV), F32)],
        compiler_params=_cparams(("arbitrary", "arbitrary")),
        name="gdn_mixer",
    )(proj, proj, proj, conv_w, pvec, o_norm)


def _mixout_kernel(n_act, *refs):
    acts = refs[:n_act]
    ws = refs[n_act:2 * n_act]
    x_ref, g1_ref, lng_ref, lnb_ref, sc_ref, sh_ref, rw_ref = refs[2 * n_act:2 * n_act + 7]
    xo_ref, h_ref, rl_ref = refs[2 * n_act + 7:]
    y = None
    for a_ref, w_ref in zip(acts, ws):
        t = jnp.dot(a_ref[0].astype(BF16), w_ref[...], preferred_element_type=F32)
        y = t if y is None else y + t
    xn = _layer_norm(DEEPNORM_ALPHA * x_ref[0] + g1_ref[0] * y, lng_ref[...], lnb_ref[...])
    xo_ref[0] = xn
    h = xn * (1.0 + sc_ref[0]) + sh_ref[0]
    h_hi = h.astype(BF16)
    h_ref[0] = h_hi
    h_lo = (h - h_hi.astype(F32)).astype(BF16)
    rw = rw_ref[...]
    rw_hi = rw.astype(BF16)
    rw_lo = (rw - rw_hi.astype(F32)).astype(BF16)
    rl_ref[0] = (lax.dot_general(rw_hi, h_hi, NT, preferred_element_type=F32)
                 + lax.dot_general(rw_hi, h_lo, NT, preferred_element_type=F32)
                 + lax.dot_general(rw_lo, h_hi, NT, preferred_element_type=F32))


def _mixout(acts, ws, x, g1, ln_g, ln_b, sc2, sh2, router_wt, tm=512):
    bsz, s, d = x.shape
    tm = min(tm, s)
    n_act = len(acts)
    ne = router_wt.shape[0]
    vec = pl.BlockSpec((1, 1, d), lambda b, i: (b, 0, 0))
    par = pl.BlockSpec((1, d), lambda b, i: (0, 0))
    in_specs = ([pl.BlockSpec((1, tm, a.shape[-1]), lambda b, i: (b, i, 0)) for a in acts]
                + [pl.BlockSpec(w.shape, lambda b, i: (0, 0)) for w in ws]
                + [pl.BlockSpec((1, tm, d), lambda b, i: (b, i, 0)), vec, par, par, vec, vec,
                   pl.BlockSpec((ne, d), lambda b, i: (0, 0))])
    return pl.pallas_call(
        functools.partial(_mixout_kernel, n_act),
        grid=(bsz, s // tm),
        in_specs=in_specs,
        out_specs=[pl.BlockSpec((1, tm, d), lambda b, i: (b, i, 0)),
                   pl.BlockSpec((1, tm, d), lambda b, i: (b, i, 0)),
                   pl.BlockSpec((1, ne, tm), lambda b, i: (b, 0, i))],
        out_shape=[jax.ShapeDtypeStruct((bsz, s, d), F32),
                   jax.ShapeDtypeStruct((bsz, s, d), BF16),
                   jax.ShapeDtypeStruct((bsz, ne, s), F32)],
        compiler_params=_cparams(("arbitrary", "arbitrary")),
        name="mix_out",
    )(*acts, *ws, x, g1, ln_g, ln_b, sc2, sh2, router_wt)


def _first_max(vals, idx, axis, sentinel):
    m = jnp.max(vals, axis=axis, keepdims=True)
    first = jnp.min(jnp.where(vals == m, idx, sentinel), axis=axis, keepdims=True)
    return m, idx == first


def _router_kernel(rl_ref, rb_ref, g_ref, gt_ref, cnt_ref):
    ne, tn = rl_ref.shape[1], rl_ref.shape[2]
    gsz = ne // N_GROUPS
    scores = _sigmoid(rl_ref[0])
    sel = scores + rb_ref[...]
    ridx = lax.broadcasted_iota(jnp.int32, (gsz, tn), 0)
    gidx = lax.broadcasted_iota(jnp.int32, (N_GROUPS, tn), 0)
    gs = jnp.zeros((N_GROUPS, tn), F32)
    for g in range(N_GROUPS):
        sg = sel[g * gsz:(g + 1) * gsz, :]
        m1, hit = _first_max(sg, ridx, 0, gsz)
        m2 = jnp.max(jnp.where(hit, -jnp.inf, sg), axis=0, keepdims=True)
        gs = jnp.where(gidx == g, m1 + m2, gs)
    gsel = None
    for _ in range(TOPK_GROUPS):
        _, hit = _first_max(gs, gidx, 0, N_GROUPS)
        gsel = hit if gsel is None else jnp.logical_or(gsel, hit)
        gs = jnp.where(hit, -jnp.inf, gs)
    gself = gsel.astype(F32)
    emask = jnp.concatenate([jnp.broadcast_to(gself[g:g + 1, :], (gsz, tn)) for g in range(N_GROUPS)], axis=0)
    cand = jnp.where(emask > 0.5, sel, -jnp.inf)
    eidx = lax.broadcasted_iota(jnp.int32, cand.shape, 0)
    chosen = None
    for _ in range(TOP_K):
        _, hit = _first_max(cand, eidx, 0, ne)
        chosen = hit if chosen is None else jnp.logical_or(chosen, hit)
        cand = jnp.where(hit, -jnp.inf, cand)
    wsel = jnp.where(chosen, scores, 0.0)
    tot = jnp.sum(wsel, axis=0, keepdims=True)
    gates = wsel / (tot + 1e-20) * ROUTED_SCALE
    g_ref[0] = gates.T
    gt_ref[0] = gates
    tile_of = lax.broadcasted_iota(jnp.int32, (tn, LANES), 0) // MOE_TM
    ind = (tile_of == lax.broadcasted_iota(jnp.int32, (tn, LANES), 1)).astype(BF16)
    routed = jnp.where(gates > 0.0, 1.0, 0.0).astype(BF16)
    cnt_ref[0, 0] = jnp.dot(routed, ind, preferred_element_type=F32)


def _router(rl, router_b, tn=1024):
    bsz, ne, s = rl.shape
    tn = min(tn, s)
    return pl.pallas_call(
        _router_kernel,
        grid=(bsz, s // tn),
        in_specs=[pl.BlockSpec((1, ne, tn), lambda b, i: (b, 0, i)),
                  pl.BlockSpec((ne, 1), lambda b, i: (0, 0))],
        out_specs=[pl.BlockSpec((1, tn, ne), lambda b, i: (b, i, 0)),
                   pl.BlockSpec((1, ne, tn), lambda b, i: (b, 0, i)),
                   pl.BlockSpec((1, 1, ne, LANES), lambda b, i: (b, i, 0, 0))],
        out_shape=[jax.ShapeDtypeStruct((bsz, s, ne), F32),
                   jax.ShapeDtypeStruct((bsz, ne, s), F32),
                   jax.ShapeDtypeStruct((bsz, s // tn, ne, LANES), F32)],
        compiler_params=_cparams(("arbitrary", "arbitrary")),
        name="moe_router",
    )(rl, router_b.reshape(ne, 1))


MOE_TM = 256
MOE_ALIGN = 16
MOE_R = 1024
MOE_LC = 512
MOE_LMAX = -(-(MOE_TM * TOP_K + N_EXPERTS * (MOE_ALIGN - 1)) // MOE_LC) * MOE_LC
MOE_NP = MOE_LMAX // MOE_ALIGN
MOE_LP = MOE_LC // MOE_ALIGN
POS_SPLIT = 256.0


def _ffn(x, wg, wu, wd):
    a = jnp.dot(x, wg, preferred_element_type=F32)
    u = jnp.dot(x, wu, preferred_element_type=F32)
    return jnp.dot((_silu(a) * u).astype(BF16), wd, preferred_element_type=F32)


def _split_pos(pos, axis):
    hi = jnp.floor(pos * (1.0 / POS_SPLIT)) * POS_SPLIT
    return jnp.concatenate([hi, pos - hi], axis=axis).astype(BF16)


def _for_each_piece(n_pieces, fn):
    unroll = 8

    def body(q, carry):
        for u in range(unroll):
            fn(q * unroll + u)
        return carry

    full = n_pieces // unroll
    lax.fori_loop(0, full, body, 0)

    def tail(p, carry):
        fn(p)
        return carry

    lax.fori_loop(full * unroll, n_pieces, tail, 0)


def _dispatch_kernel(dst_s, np_s, h_ref, gt_ref, offc_ref, offr_ref, cntr_ref, xs_hbm, sorted_scr, sems):
    i = pl.program_id(0)
    last = pl.num_programs(0) - 1
    slot = i % 2
    tm = h_ref.shape[0]
    routed = gt_ref[0] > 0.0
    t0 = lax.broadcasted_iota(jnp.int32, (tm, tm), 0)
    t1 = lax.broadcasted_iota(jnp.int32, (tm, tm), 1)
    earlier = jnp.where(t0 < t1, 1.0, 0.0).astype(BF16)
    rank_t = jnp.dot(jnp.where(routed, 1.0, 0.0).astype(BF16), earlier, preferred_element_type=F32)
    pos_t = jnp.where(routed, rank_t + offc_ref[0] + 1.0, 0.0)
    pos2 = _split_pos(pos_t, 0)
    x = h_ref[...]
    offr = offr_ref[0]
    endr = offr + cntr_ref[0]
    for c in range(MOE_LMAX // MOE_LC):
        r = (c * MOE_LC + lax.broadcasted_iota(jnp.int32, (MOE_LC, 1), 0)).astype(F32)
        owner = jnp.where(jnp.logical_and(r >= offr, r < endr), 1.0, 0.0).astype(BF16)
        possel = jnp.dot(jnp.concatenate([owner, owner], axis=1), pos2, preferred_element_type=F32)
        perm = jnp.where(possel == r + 1.0, 1.0, 0.0).astype(BF16)
        srt = jnp.dot(perm, x, preferred_element_type=F32).astype(BF16)
        sorted_scr[slot, c * MOE_LP:(c + 1) * MOE_LP] = srt.reshape(MOE_LP, MOE_ALIGN, srt.shape[-1])

    def piece(sl, tile, p):
        return pltpu.make_async_copy(sorted_scr.at[sl, p], xs_hbm.at[dst_s[tile * MOE_NP + p]], sems.at[sl])

    _for_each_piece(np_s[i], lambda p: piece(slot, i, p).start())

    @pl.when(i > 0)
    def _():
        _for_each_piece(np_s[i - 1], lambda p: piece(1 - slot, i - 1, p).wait())

    @pl.when(i == last)
    def _():
        _for_each_piece(np_s[i], lambda p: piece(slot, i, p).wait())


def _expert_ffn_kernel(be_s, nu_s, x_ref, wg_ref, wu_ref, wd_ref, y_ref):
    @pl.when(pl.program_id(0) < nu_s[0])
    def _():
        y_ref[...] = _ffn(x_ref[...], wg_ref[0, 0].astype(BF16), wu_ref[0, 0].astype(BF16),
                          wd_ref[0, 0].astype(BF16)).astype(BF16)


def _combine_kernel(dst_s, np_s, ys_hbm, g_ref, h_ref, offr_ref, offc_ref, cntc_ref,
                    sg_ref, su_ref, sd_ref, x_ref, g2_ref, lng_ref, lnb_ref, o_ref, ys_scr, sems):
    i = pl.program_id(0)
    last = pl.num_programs(0) - 1
    slot = i % 2
    tm = h_ref.shape[0]

    def piece(sl, tile, p):
        return pltpu.make_async_copy(ys_hbm.at[dst_s[tile * MOE_NP + p]], ys_scr.at[sl, p], sems.at[sl])

    @pl.when(i == 0)
    def _():
        ys_scr[...] = jnp.zeros(ys_scr.shape, BF16)
        _for_each_piece(np_s[0], lambda p: piece(0, 0, p).start())

    @pl.when(i < last)
    def _():
        _for_each_piece(np_s[i + 1], lambda p: piece(1 - slot, i + 1, p).start())

    acc = _ffn(h_ref[...], sg_ref[...], su_ref[...], sd_ref[...])
    g = g_ref[...]
    routed = g > 0.0
    t0 = lax.broadcasted_iota(jnp.int32, (tm, tm), 0)
    t1 = lax.broadcasted_iota(jnp.int32, (tm, tm), 1)
    earlier = jnp.where(t0 > t1, 1.0, 0.0).astype(BF16)
    rank = jnp.dot(earlier, jnp.where(routed, 1.0, 0.0).astype(BF16), preferred_element_type=F32)
    pos = jnp.where(routed, rank + offr_ref[0] + 1.0, 0.0)
    pos2 = _split_pos(pos, 1)
    g16 = g.astype(BF16)
    offc = offc_ref[0]
    endc = offc + cntc_ref[0]
    _for_each_piece(np_s[i], lambda p: piece(slot, i, p).wait())
    for c in range(MOE_LMAX // MOE_LC):
        r = (c * MOE_LC + lax.broadcasted_iota(jnp.int32, (1, MOE_LC), 1)).astype(F32)
        owner = jnp.where(jnp.logical_and(r >= offc, r < endc), 1.0, 0.0).astype(BF16)
        possel = jnp.dot(pos2, jnp.concatenate([owner, owner], axis=0), preferred_element_type=F32)
        gsel = jnp.dot(g16, owner, preferred_element_type=F32)
        w = jnp.where(possel == r + 1.0, gsel, 0.0).astype(BF16)
        ys = ys_scr[slot, c * MOE_LP:(c + 1) * MOE_LP].reshape(MOE_LC, ys_scr.shape[-1])
        acc = acc + jnp.dot(w, ys, preferred_element_type=F32)
    z = DEEPNORM_ALPHA * x_ref[...] + g2_ref[0] * acc
    o_ref[...] = _layer_norm(z, lng_ref[...], lnb_ref[...])


def _ceil_to(v, m):
    return jnp.floor((v + (m - 1.0)) * (1.0 / m)) * m


def _moe_layout_kernel(cnt_ref, dst_ref, np_ref, off_ref, cntp_ref, be_ref, nu_ref):
    cnt = cnt_ref[...]
    ntiles, ne = cnt.shape
    cntp = _ceil_to(cnt, MOE_ALIGN)
    e0 = lax.broadcasted_iota(jnp.int32, (ne, ne), 0)
    e1 = lax.broadcasted_iota(jnp.int32, (ne, ne), 1)
    i0 = lax.broadcasted_iota(jnp.int32, (ntiles, ntiles), 0)
    i1 = lax.broadcasted_iota(jnp.int32, (ntiles, ntiles), 1)

    def mm(a, b):
        return jnp.dot(a, b, precision=HI, preferred_element_type=F32)

    off = mm(cntp, jnp.where(e0 < e1, 1.0, 0.0))
    before = mm(jnp.where(i0 > i1, 1.0, 0.0), cntp)
    tot = jnp.broadcast_to(jnp.sum(cntp, axis=0, keepdims=True), (8, ne))
    totr = _ceil_to(tot, MOE_R)
    base = mm(totr, jnp.where(e0 < e1, 1.0, 0.0))[0:1, :]
    delta = base + before - off
    step = mm(delta, jnp.where(e0 == e1, 1.0, 0.0) - jnp.where(e0 + 1 == e1, 1.0, 0.0))
    rowp = (lax.broadcasted_iota(jnp.int32, (1, MOE_NP), 1) * MOE_ALIGN).astype(F32)
    dst = jnp.broadcast_to(rowp, (ntiles, MOE_NP))
    for e in range(ne):
        dst = dst + jnp.where(off[:, e:e + 1] <= rowp, step[:, e:e + 1], 0.0)
    dst_ref[...] = (dst * (1.0 / MOE_ALIGN)).astype(jnp.int32)
    pieces = jnp.sum(cntp, axis=1, keepdims=True) * (1.0 / MOE_ALIGN)
    np_ref[...] = jnp.broadcast_to(pieces, np_ref.shape).astype(jnp.int32)
    off_ref[...] = off
    cntp_ref[...] = cntp
    tot_c = lax.dot_general(cntp, jnp.ones((ntiles, LANES), F32), TN, precision=HI, preferred_element_type=F32)
    end_c = mm(jnp.where(e0 >= e1, 1.0, 0.0), _ceil_to(tot_c, MOE_R))[:, 0:1]
    total = end_c[ne - 1:ne, :]
    first = jnp.minimum((lax.broadcasted_iota(jnp.int32, be_ref.shape, 1) * MOE_R).astype(F32), total - 1.0)
    be = jnp.sum(jnp.where(end_c <= first, 1.0, 0.0), axis=0, keepdims=True)
    be_ref[...] = jnp.minimum(be, ne - 1.0).astype(jnp.int32)
    nu_ref[...] = jnp.broadcast_to(total * (1.0 / MOE_R), nu_ref.shape).astype(jnp.int32)


def _moe_layout(cnt_raw, tn):
    ne = N_EXPERTS
    nsub = tn // MOE_TM
    cnt = jnp.transpose(cnt_raw[..., :nsub], (0, 1, 3, 2)).reshape(-1, ne)
    ntiles = cnt.shape[0]
    nblk = -(-(ntiles * (MOE_TM * TOP_K + ne * (MOE_ALIGN - 1)) + ne * (MOE_R - 1)) // MOE_R)
    nblk_pad = -(-nblk // LANES) * LANES
    dst, npieces, off, cntp, blk_exp, nused = pl.pallas_call(
        _moe_layout_kernel,
        out_shape=[jax.ShapeDtypeStruct((ntiles, MOE_NP), jnp.int32),
                   jax.ShapeDtypeStruct((ntiles, LANES), jnp.int32),
                   jax.ShapeDtypeStruct((ntiles, ne), F32),
                   jax.ShapeDtypeStruct((ntiles, ne), F32),
                   jax.ShapeDtypeStruct((1, nblk_pad), jnp.int32),
                   jax.ShapeDtypeStruct((1, LANES), jnp.int32)],
        compiler_params=pltpu.CompilerParams(vmem_limit_bytes=VMEM_LIMIT),
        name="moe_layout",
    )(cnt)
    return dst.reshape(-1), npieces[:, 0], off, cntp, nblk, nused[0, :1], blk_exp[0, :nblk]


def _moe(h2, gates, gates_t, cnt_raw, tn, layer, wg, wu, wd, sg, su, sd, x, g2, ln_g, ln_b):
    bsz, s, d = x.shape
    t = bsz * s
    tm = MOE_TM
    _, ne, _, de = wg.shape
    per_b = s // tm
    ntiles = t // tm
    dst, npieces, off_f, cnt_f, nblk, nused, blk_exp = _moe_layout(cnt_raw, tn)
    scalars = (dst, npieces)
    row = pl.BlockSpec((1, 1, ne), lambda i, *_: (i, 0, 0))
    col = pl.BlockSpec((1, ne, 1), lambda i, *_: (i, 0, 0))
    tok = pl.BlockSpec((tm, d), lambda i, *_: (i, 0))
    par = pl.BlockSpec((1, d), lambda i, *_: (0, 0))
    rows = nblk * MOE_R

    xs = pl.pallas_call(
        _dispatch_kernel,
        grid_spec=pltpu.PrefetchScalarGridSpec(
            num_scalar_prefetch=2, grid=(ntiles,),
            in_specs=[tok,
                      pl.BlockSpec((1, ne, tm), lambda i, *_: (i // per_b, 0, i % per_b)),
                      col, row, row],
            out_specs=pl.BlockSpec(memory_space=pl.ANY),
            scratch_shapes=[pltpu.VMEM((2, MOE_NP, MOE_ALIGN, d), BF16), pltpu.SemaphoreType.DMA((2,))]),
        out_shape=jax.ShapeDtypeStruct((rows // MOE_ALIGN, MOE_ALIGN, d), BF16),
        compiler_params=_cparams(("arbitrary",)),
        name="moe_dispatch",
    )(*scalars, h2.reshape(t, d), gates_t, off_f.reshape(ntiles, ne, 1), off_f.reshape(ntiles, 1, ne),
      cnt_f.reshape(ntiles, 1, ne))

    def blk_index(j, be, nu):
        return jnp.maximum(jnp.minimum(j, nu[0] - 1), 0), 0

    blk = pl.BlockSpec((MOE_R, d), blk_index)
    ys = pl.pallas_call(
        _expert_ffn_kernel,
        grid_spec=pltpu.PrefetchScalarGridSpec(
            num_scalar_prefetch=2, grid=(nblk,),
            in_specs=[blk,
                      pl.BlockSpec((1, 1, d, de), lambda j, be, nu: (layer, be[j], 0, 0)),
                      pl.BlockSpec((1, 1, d, de), lambda j, be, nu: (layer, be[j], 0, 0)),
                      pl.BlockSpec((1, 1, de, d), lambda j, be, nu: (layer, be[j], 0, 0))],
            out_specs=blk),
        out_shape=jax.ShapeDtypeStruct((rows, d), BF16),
        compiler_params=_cparams(("arbitrary",)),
        name="moe_expert_ffn",
    )(blk_exp, nused, xs.reshape(rows, d), wg, wu, wd)

    out = pl.pallas_call(
        _combine_kernel,
        grid_spec=pltpu.PrefetchScalarGridSpec(
            num_scalar_prefetch=2, grid=(ntiles,),
            in_specs=[pl.BlockSpec(memory_space=pl.ANY),
                      pl.BlockSpec((tm, ne), lambda i, *_: (i, 0)),
                      tok, row, col, col,
                      pl.BlockSpec(sg.shape, lambda i, *_: (0, 0)),
                      pl.BlockSpec(su.shape, lambda i, *_: (0, 0)),
                      pl.BlockSpec(sd.shape, lambda i, *_: (0, 0)),
                      tok,
                      pl.BlockSpec((1, 1, d), lambda i, *_: (i // per_b, 0, 0)),
                      par, par],
            out_specs=tok,
            scratch_shapes=[pltpu.VMEM((2, MOE_NP, MOE_ALIGN, d), BF16), pltpu.SemaphoreType.DMA((2,))]),
        out_shape=jax.ShapeDtypeStruct((t, d), F32),
        compiler_params=_cparams(("arbitrary",)),
        name="moe_combine",
    )(*scalars, ys.reshape(rows // MOE_ALIGN, MOE_ALIGN, d), gates.reshape(t, ne), h2.reshape(t, d),
      off_f.reshape(ntiles, 1, ne),
      off_f.reshape(ntiles, ne, 1), cnt_f.reshape(ntiles, ne, 1), sg, su, sd, x.reshape(t, d), g2, ln_g, ln_b)
    return out.reshape(bsz, s, d)


def _pad_cols(w, n):
    return jnp.pad(w, ((0, 0), (0, n - w.shape[1])))


def kernel(x, c, rpe_bias, ada_w, ada_b, ln_mix_g, ln_mix_b, ln_ffn_g, ln_ffn_b, ev_w_in, ev_gk_w2, ev_gk_b, ev_norm, ev_w_out, od_w_in, od_conv_w, od_a_log, od_dt_bias, od_norm, od_w_out, moe_router_w, moe_router_b, moe_w_gate, moe_w_up, moe_w_down, sh_w_gate, sh_w_up, sh_w_down):
    bsz, s, d = x.shape
    mod = _ada(c, ada_w, ada_b)
    tiles = _rpe_tiles(rpe_bias)

    for layer in range(DEPTH):
        sh1, sc1, g1, sh2, sc2, g2 = [mod[layer, :, u * d:(u + 1) * d].reshape(bsz, 1, d) for u in range(6)]
        i = layer // 2
        if layer % 2 == 0:
            n_main = 3 * MOBA_W + 2 * GLA_QK_W + 2 * GLA_V_W
            w_in = jnp.concatenate([ev_w_in[i][:, :n_main], _pad_cols(ev_w_in[i][:, n_main:], LANES)], axis=1)
            proj = _mod_matmul(x, sc1, sh1, w_in.astype(BF16))
            nb = MOBA_W // LANES
            o_a = _moba(proj, tiles, 0, nb, 2 * nb)
            gk_w2p = jnp.pad(ev_gk_w2[i], ((0, LANES - GLA_GATE_RANK), (0, 0)))
            gla0 = 3 * MOBA_W
            o_b = _gla(proj, gk_w2p, ev_gk_b[i].reshape(1, -1), ev_norm[i].reshape(1, -1),
                       gla0 // GLA_QK_W, gla0 // GLA_QK_W + 1,
                       (gla0 + 2 * GLA_QK_W) // GLA_V_W, (gla0 + 2 * GLA_QK_W) // GLA_V_W + 1,
                       n_main // LANES)
            w_out = ev_w_out[i].astype(BF16)
            acts, ws = [o_a, o_b], [w_out[:MOBA_W], w_out[MOBA_W:]]
        else:
            n_main = 4 * GDN_W
            w_in = jnp.concatenate([od_w_in[i][:, :n_main], _pad_cols(od_w_in[i][:, n_main:], LANES)], axis=1)
            proj = _mod_matmul(x, sc1, sh1, w_in.astype(BF16))
            pvec = jnp.zeros((2, LANES), F32)
            pvec = pvec.at[0, GDN_HEADS:2 * GDN_HEADS].set(od_a_log[i])
            pvec = pvec.at[1, GDN_HEADS:2 * GDN_HEADS].set(od_dt_bias[i])
            o = _gdn(proj, od_conv_w[i], pvec, od_norm[i].reshape(1, -1))
            acts, ws = [o], [od_w_out[i].astype(BF16)]

        x, h2, rl = _mixout(acts, ws, x, g1, ln_mix_g[layer].reshape(1, d), ln_mix_b[layer].reshape(1, d),
                            sc2, sh2, moe_router_w[layer].T)
        router_tn = min(1024, s)
        gates, gates_t, cnt_raw = _router(rl, moe_router_b[layer], router_tn)
        x = _moe(h2, gates, gates_t, cnt_raw, router_tn, layer, moe_w_gate, moe_w_up, moe_w_down,
                 sh_w_gate[layer].astype(BF16), sh_w_up[layer].astype(BF16),
                 sh_w_down[layer].astype(BF16), x, g2, ln_ffn_g[layer].reshape(1, d), ln_ffn_b[layer].reshape(1, d))
    return x
```

```python
import functools
import math

import numpy as np
import jax
import jax.numpy as jnp
from jax import lax
from jax.experimental import pallas as pl
from jax.experimental.pallas import tpu as pltpu

F32 = jnp.float32
BF16 = jnp.bfloat16
HI = lax.Precision.HIGHEST
NT = (((1,), (1,)), ((), ()))
TN = (((0,), (0,)), ((), ()))
NEG = -1e30
LOG2E = math.log2(math.e)

LANES = 128
VMEM_LIMIT = 56 * 1024 * 1024

DEPTH = 2
MOBA_HEAD_DIM = 128
MOBA_HEADS = 4
MOBA_BLOCK = 256
MOBA_TOPK = 3
MOBA_GROUP = 8
MOBA_SUB = 4
MOBA_QBLOCKS = 2
MOBA_DEN_ROWS = 16
GLA_DV = 128
GLA_HEADS = 4
GLA_DK = 64
GLA_GATE_RANK = 16
GLA_GATE_NORM = 16.0
GLA_CHUNK = 64
GDN_DK = 128
GDN_DV = 128
GDN_HEADS = 8
GDN_CONV = 4
GDN_CHUNK = 64
GDN_GROUP = 4
RPE_BUCKETS = 32
RPE_MAX_DIST = 2048
RPE_TILES = 8
N_EXPERTS = 64
TOP_K = 6
N_GROUPS = 8
TOPK_GROUPS = 4
D_EXPERT = 256
ROUTED_SCALE = 2.5
DEEPNORM_ALPHA = float((2 * DEPTH) ** 0.25)
LN_EPS = 1e-5
NORM_EPS = 1e-6

MOBA_W = MOBA_HEADS * MOBA_HEAD_DIM
GLA_QK_W = GLA_HEADS * GLA_DK
GLA_V_W = GLA_HEADS * GLA_DV
GDN_W = GDN_HEADS * GDN_DK


def _cparams(sem):
    return pltpu.CompilerParams(dimension_semantics=sem, vmem_limit_bytes=VMEM_LIMIT)


def _sigmoid(x):
    return 1.0 / (1.0 + jnp.exp(-x))


def _silu(x):
    return x * _sigmoid(x)


def _softplus(x):
    return jnp.maximum(x, 0.0) + jnp.log(1.0 + jnp.exp(-jnp.abs(x)))


def _layer_norm(z, g, b):
    mu = jnp.mean(z, axis=-1, keepdims=True)
    zc = z - mu
    var = jnp.mean(zc * zc, axis=-1, keepdims=True)
    return zc * lax.rsqrt(var + LN_EPS) * g + b


def _ada_kernel(c_ref, w_ref, b_ref, o_ref):
    ca = _silu(c_ref[...])
    o_ref[0] = jnp.dot(ca, w_ref[0], precision=HI, preferred_element_type=F32) + b_ref[0]


def _ada(c, ada_w, ada_b):
    depth, d, n = ada_w.shape
    bsz = c.shape[0]
    tn = 6 * LANES
    return pl.pallas_call(
        _ada_kernel,
        grid=(depth, n // tn),
        in_specs=[pl.BlockSpec((bsz, d), lambda l, j: (0, 0)),
                  pl.BlockSpec((1, d, tn), lambda l, j: (l, 0, j)),
                  pl.BlockSpec((1, 1, tn), lambda l, j: (l, 0, j))],
        out_specs=pl.BlockSpec((1, bsz, tn), lambda l, j: (l, 0, j)),
        out_shape=jax.ShapeDtypeStruct((depth, bsz, n), F32),
        compiler_params=_cparams(("arbitrary", "arbitrary")),
        name="ada_mod",
    )(c, ada_w, ada_b.reshape(depth, 1, n))


def _modmm_kernel(x_ref, sc_ref, sh_ref, w_ref, o_ref, h_scr):
    @pl.when(pl.program_id(2) == 0)
    def _():
        h_scr[...] = (x_ref[0] * (1.0 + sc_ref[0]) + sh_ref[0]).astype(BF16)

    o_ref[0] = jnp.dot(h_scr[...], w_ref[...], preferred_element_type=F32).astype(o_ref.dtype)


def _col_tile(n, cap):
    best = LANES
    for t in range(LANES, cap + 1, LANES):
        if n % t == 0:
            best = t
    return best


def _mod_matmul(x, sc, sh, w, tm=512, tn_cap=4608):
    bsz, s, d = x.shape
    n = w.shape[1]
    tm = min(tm, s)
    tn = _col_tile(n, tn_cap)
    return pl.pallas_call(
        _modmm_kernel,
        grid=(bsz, s // tm, n // tn),
        in_specs=[pl.BlockSpec((1, tm, d), lambda b, i, j: (b, i, 0)),
                  pl.BlockSpec((1, 1, d), lambda b, i, j: (b, 0, 0)),
                  pl.BlockSpec((1, 1, d), lambda b, i, j: (b, 0, 0)),
                  pl.BlockSpec((d, tn), lambda b, i, j: (0, j))],
        out_specs=pl.BlockSpec((1, tm, tn), lambda b, i, j: (b, i, j)),
        out_shape=jax.ShapeDtypeStruct((bsz, s, n), BF16),
        scratch_shapes=[pltpu.VMEM((tm, d), BF16)],
        compiler_params=_cparams(("arbitrary", "arbitrary", "arbitrary")),
        name="mod_matmul",
    )(x, sc, sh, w)


def _rpe_lower_bounds():
    exact = RPE_BUCKETS // 2
    d = np.arange(0, 2 * RPE_MAX_DIST, dtype=np.int64)
    logd = np.log(np.maximum(d, 1).astype(np.float64) / exact)
    large = exact + (logd / math.log(RPE_MAX_DIST / exact) * (RPE_BUCKETS - exact)).astype(np.int64)
    large = np.minimum(large, RPE_BUCKETS - 1)
    bucket = np.where(d < exact, d, large)
    return [int(np.argmax(bucket >= k)) for k in range(RPE_BUCKETS)]


def _rpe_tiles_kernel(lo, rpe_ref, o_ref):
    h = pl.program_id(0)
    j = pl.program_id(1)
    blk = o_ref.shape[-1]
    key = lax.broadcasted_iota(jnp.int32, (blk, blk), 0)
    qry = lax.broadcasted_iota(jnp.int32, (blk, blk), 1)
    dist = j * blk + qry - key
    val = jnp.full((blk, blk), rpe_ref[0, h], F32)
    for k in range(1, RPE_BUCKETS):
        val = jnp.where(dist >= lo[k], rpe_ref[k, h], val)
    o_ref[0, 0] = jnp.where(dist >= 0, val * LOG2E, NEG)


def _rpe_tiles(rpe_bias):
    heads = rpe_bias.shape[1]
    lo = _rpe_lower_bounds()
    assert lo[-1] <= (RPE_TILES - 1) * MOBA_BLOCK - (MOBA_BLOCK - 1)
    return pl.pallas_call(
        functools.partial(_rpe_tiles_kernel, lo),
        grid=(heads, RPE_TILES),
        in_specs=[pl.BlockSpec(memory_space=pltpu.SMEM)],
        out_specs=pl.BlockSpec((1, 1, MOBA_BLOCK, MOBA_BLOCK), lambda h, j: (h, j, 0, 0)),
        out_shape=jax.ShapeDtypeStruct((heads, RPE_TILES, MOBA_BLOCK, MOBA_BLOCK), F32),
        compiler_params=_cparams(("arbitrary", "arbitrary")),
        name="rpe_tiles",
    )(rpe_bias)


def _moba_kernel(q_ref, k_ref, v_ref, t_ref, o_ref, kb_scr, vt_scr, km_scr, sel_scr, far_scr):
    step = pl.program_id(2)
    nkb, blk, dh = kb_scr.shape

    @pl.when(step == 0)
    def _():
        for n in range(nkb):
            kn = k_ref[0, n * blk:(n + 1) * blk, :]
            kb_scr[n] = kn.astype(BF16)
            km_scr[n:n + 1, :] = jnp.mean(kn.astype(F32), axis=0, keepdims=True)
            vt_scr[n, :dh] = v_ref[0, n * blk:(n + 1) * blk, :].astype(F32).T.astype(BF16)
            ones_row = lax.broadcasted_iota(jnp.int32, (MOBA_DEN_ROWS, blk), 0) == 0
            vt_scr[n, dh:] = jnp.where(ones_row, 1.0, 0.0).astype(BF16)

    for sub in range(MOBA_QBLOCKS):
        _moba_query_block(step * MOBA_QBLOCKS + sub, slice(sub * blk, (sub + 1) * blk),
                          q_ref, t_ref, o_ref, kb_scr, vt_scr, km_scr, sel_scr, far_scr)


def _moba_query_block(i, rows, q_ref, t_ref, o_ref, kb_scr, vt_scr, km_scr, sel_scr, far_scr):
    nkb, blk, dh = kb_scr.shape
    q = q_ref[0, rows, :].astype(F32)
    gate = lax.dot_general(km_scr[...], q, NT, precision=HI, preferred_element_type=F32)
    bidx = lax.broadcasted_iota(jnp.int32, gate.shape, 0)
    past = bidx < i
    g = jnp.where(past, gate, -jnp.inf)
    sel = None
    for _ in range(MOBA_TOPK):
        m = jnp.max(g, axis=0, keepdims=True)
        first = jnp.min(jnp.where(g == m, bidx, nkb), axis=0, keepdims=True)
        hit = bidx == first
        sel = hit if sel is None else jnp.logical_or(sel, hit)
        g = jnp.where(hit, -jnp.inf, g)
    mask = jnp.where(jnp.logical_or(jnp.logical_and(sel, past), bidx == i), 0.0, NEG)
    sel_scr[...] = mask
    far_bias = t_ref[0, RPE_TILES - 1, 0:1, :]
    far_scr[...] = mask + jnp.where(i - bidx >= RPE_TILES - 1, far_bias, 0.0)

    qs = (q * (dh ** -0.5 * LOG2E)).astype(BF16)

    m0 = jnp.full((1, blk), NEG, F32)
    acc0 = jnp.zeros((dh + MOBA_DEN_ROWS, blk), F32)
    def body(far, g, carry):
        m, acc = carry
        nsub = MOBA_GROUP // MOBA_SUB
        blocks = [[jnp.minimum(g * MOBA_GROUP + k * MOBA_SUB + u, nkb - 1) for u in range(MOBA_SUB)]
                  for k in range(nsub)]
        scores, probs, alphas = {}, {}, {}

        def emit_scores(k):
            out = []
            for n in blocks[k]:
                s = lax.dot_general(kb_scr[n], qs, NT, preferred_element_type=F32)
                if far:
                    out.append(s + far_scr[pl.ds(n, 1), :])
                else:
                    out.append(s + t_ref[0, jnp.clip(i - n, 0, RPE_TILES - 1)] + sel_scr[pl.ds(n, 1), :])
            scores[k] = out

        def emit_softmax(k, m):
            m_new = m
            for s in scores[k]:
                m_new = jnp.maximum(m_new, jnp.max(s, axis=0, keepdims=True))
            alphas[k] = jnp.exp2(m - m_new)
            probs[k] = [jnp.exp2(s - m_new).astype(BF16) for s in scores[k]]
            return m_new

        def emit_values(k, acc):
            acc = alphas[k] * acc
            for n, p in zip(blocks[k], probs[k]):
                acc = acc + jnp.dot(vt_scr[n], p, preferred_element_type=F32)
            return acc

        emit_scores(0)
        for k in range(nsub):
            if k + 1 < nsub:
                emit_scores(k + 1)
            if k >= 1:
                acc = emit_values(k - 1, acc)
            m = emit_softmax(k, m)
        acc = emit_values(nsub - 1, acc)
        return m, acc

    n_far = jnp.maximum(i - (RPE_TILES - 2), 0) // MOBA_GROUP
    n_all = i // MOBA_GROUP + 1
    carry = lax.fori_loop(0, n_far, functools.partial(body, True), (m0, acc0))
    _, acc = lax.fori_loop(n_far, n_all, functools.partial(body, False), carry)
    o_ref[0, rows, :] = (acc[:dh] / acc[dh:dh + 1]).T.astype(o_ref.dtype)


def _moba(proj, tiles, q_col, k_col, v_col):
    bsz, s, _ = proj.shape
    dh, blk, heads = MOBA_HEAD_DIM, MOBA_BLOCK, MOBA_HEADS
    nkb = s // blk
    qrows = blk * MOBA_QBLOCKS
    return pl.pallas_call(
        _moba_kernel,
        grid=(bsz, heads, nkb // MOBA_QBLOCKS),
        in_specs=[pl.BlockSpec((1, qrows, dh), lambda b, h, i: (b, i, q_col + h)),
                  pl.BlockSpec((1, s, dh), lambda b, h, i: (b, 0, k_col + h)),
                  pl.BlockSpec((1, s, dh), lambda b, h, i: (b, 0, v_col + h)),
                  pl.BlockSpec((1, RPE_TILES, blk, blk), lambda b, h, i: (h, 0, 0, 0))],
        out_specs=pl.BlockSpec((1, qrows, dh), lambda b, h, i: (b, i, h)),
        out_shape=jax.ShapeDtypeStruct((bsz, s, heads * dh), BF16),
        scratch_shapes=[pltpu.VMEM((nkb, blk, dh), BF16),
                        pltpu.VMEM((nkb, dh + MOBA_DEN_ROWS, blk), BF16),
                        pltpu.VMEM((nkb, dh), F32),
                        pltpu.VMEM((nkb, blk), F32),
                        pltpu.VMEM((nkb, blk), F32)],
        compiler_params=_cparams(("arbitrary", "arbitrary", "arbitrary")),
        name="moba_attention",
    )(proj, proj, proj, tiles)


def _gla_kernel(q_ref, k_ref, v_ref, gg_ref, glr_ref, w2_ref, gb_ref, nw_ref, o_ref, st_scr):
    @pl.when(pl.program_id(1) == 0)
    def _():
        st_scr[...] = jnp.zeros(st_scr.shape, F32)

    tg = q_ref.shape[1]
    c = GLA_CHUNK
    x = jnp.dot(glr_ref[0].astype(F32), w2_ref[...], precision=HI, preferred_element_type=F32) + gb_ref[...]
    lg = -_softplus(-x) * (1.0 / GLA_GATE_NORM)
    row = lax.broadcasted_iota(jnp.int32, (c, c), 0)
    col = lax.broadcasted_iota(jnp.int32, (c, c), 1)
    tri = (row >= col).astype(F32)
    hc = GLA_HEADS * c
    rr = lax.broadcasted_iota(jnp.int32, (hc, hc), 0)
    cc = lax.broadcasted_iota(jnp.int32, (hc, hc), 1)
    incl = jnp.logical_and(rr // c == cc // c, rr >= cc)
    nw = nw_ref[...]
    heads = range(GLA_HEADS)

    chunks = []
    for ci in range(tg // c):
        rows = slice(ci * c, (ci + 1) * c)
        b = jnp.dot(tri, lg[rows], precision=HI, preferred_element_type=F32)
        bl = b[c - 1:c, :]
        q = q_ref[0, rows, :].astype(F32) * GLA_DK ** -0.5
        k = k_ref[0, rows, :].astype(F32)
        q_e = (q * jnp.exp(b)).astype(BF16)
        k_e = (k * jnp.exp(-b)).astype(BF16)
        chunks.append(dict(rows=rows, q_e=q_e, k_e=k_e, k_end=(k * jnp.exp(bl - b)).astype(BF16), d=jnp.exp(bl)))
    for p in chunks:
        qs = jnp.concatenate([p["q_e"][:, h * GLA_DK:(h + 1) * GLA_DK] for h in heads], axis=0)
        ks = jnp.concatenate([p["k_e"][:, h * GLA_DK:(h + 1) * GLA_DK] for h in heads], axis=0)
        p["v"] = jnp.concatenate([v_ref[0, p["rows"], h * GLA_DV:(h + 1) * GLA_DV] for h in heads],
                                 axis=0).astype(BF16)
        a = lax.dot_general(qs, ks, NT, preferred_element_type=F32)
        p["a"] = jnp.where(incl, a, 0.0).astype(BF16)
    for p in chunks:
        p["o"] = jnp.dot(p["a"], p["v"], preferred_element_type=F32)

    for p in chunks:
        for h in heads:
            ks = slice(h * GLA_DK, (h + 1) * GLA_DK)
            vs = slice(h * GLA_DV, (h + 1) * GLA_DV)
            hr = slice(h * c, (h + 1) * c)
            st = st_scr[h]
            o = p["o"][hr] + lax.dot_general(p["q_e"][:, ks], st.astype(BF16), NT, preferred_element_type=F32)
            st_scr[h] = (st * p["d"][:, ks]
                         + lax.dot_general(p["v"][hr], p["k_end"][:, ks], TN, preferred_element_type=F32))
            o = o * lax.rsqrt(jnp.mean(o * o, axis=-1, keepdims=True) + NORM_EPS)
            o_ref[0, p["rows"], vs] = (o * nw * _silu(gg_ref[0, p["rows"], vs].astype(F32))).astype(o_ref.dtype)


def _gla(proj, gk_w2p, gk_b, o_norm, q_col, k_col, v_col, g_col, r_col, tg=512):
    bsz, s, _ = proj.shape
    tg = min(tg, s)
    qk, vw = GLA_QK_W, GLA_V_W
    return pl.pallas_call(
        _gla_kernel,
        grid=(bsz, s // tg),
        in_specs=[pl.BlockSpec((1, tg, qk), lambda b, t: (b, t, q_col)),
                  pl.BlockSpec((1, tg, qk), lambda b, t: (b, t, k_col)),
                  pl.BlockSpec((1, tg, vw), lambda b, t: (b, t, v_col)),
                  pl.BlockSpec((1, tg, vw), lambda b, t: (b, t, g_col)),
                  pl.BlockSpec((1, tg, LANES), lambda b, t: (b, t, r_col)),
                  pl.BlockSpec((LANES, qk), lambda b, t: (0, 0)),
                  pl.BlockSpec((1, qk), lambda b, t: (0, 0)),
                  pl.BlockSpec((1, GLA_DV), lambda b, t: (0, 0))],
        out_specs=pl.BlockSpec((1, tg, vw), lambda b, t: (b, t, 0)),
        out_shape=jax.ShapeDtypeStruct((bsz, s, vw), BF16),
        scratch_shapes=[pltpu.VMEM((GLA_HEADS, GLA_DV, GLA_DK), F32)],
        compiler_params=_cparams(("arbitrary", "arbitrary")),
        name="gla_mixer",
    )(proj, proj, proj, proj, proj, gk_w2p, gk_b, o_norm)


def _gdn_kernel(qkv_ref, gate_ref, ba_ref, cw_ref, pv_ref, nw_ref, o_ref, tail_scr, s_scr):
    @pl.when(pl.program_id(1) == 0)
    def _():
        tail_scr[...] = jnp.zeros(tail_scr.shape, F32)
        s_scr[...] = jnp.zeros(s_scr.shape, F32)

    tg = qkv_ref.shape[1]
    c, dk, grp = GDN_CHUNK, GDN_DK, GDN_GROUP
    gr = grp * c
    w = GDN_W

    x = qkv_ref[0].astype(F32)
    tail = tail_scr[...]
    tail_scr[...] = x[tg - 8:, :]
    r8 = lax.broadcasted_iota(jnp.int32, (8, 1), 0)
    y = x * cw_ref[GDN_CONV - 1:GDN_CONV, :]
    for sft in range(1, GDN_CONV):
        xs = pltpu.roll(x, sft, axis=0)
        head = jnp.where(r8 < sft, pltpu.roll(tail, sft, axis=0), xs[:8, :])
        xs = jnp.concatenate([head, xs[8:, :]], axis=0)
        y = y + xs * cw_ref[GDN_CONV - 1 - sft:GDN_CONV - sft, :]
    y = _silu(y)

    ba = ba_ref[0].astype(F32)
    beta_t = _sigmoid(ba)
    g_t = -jnp.exp(pv_ref[0:1, :]) * _softplus(ba + pv_ref[1:2, :])

    row = lax.broadcasted_iota(jnp.int32, (c, c), 0)
    col = lax.broadcasted_iota(jnp.int32, (c, c), 1)
    tri = (row >= col).astype(F32)
    rr = lax.broadcasted_iota(jnp.int32, (gr, gr), 0)
    cc = lax.broadcasted_iota(jnp.int32, (gr, gr), 1)
    same = (rr // c) == (cc // c)
    incl = jnp.logical_and(same, rr >= cc)
    strict = jnp.logical_and(same, rr > cc)
    eye = (rr == cc).astype(F32)
    halves = [(rr // sz) == (cc // sz) for sz in (2 ** e for e in range(1, int(math.log2(c)) + 1))]
    nw = nw_ref[...]

    probs = []
    for ci in range(tg // c):
        rows = slice(ci * c, (ci + 1) * c)
        gcum = jnp.dot(tri, g_t[rows], precision=HI, preferred_element_type=F32)
        gcum_t = gcum.T
        for gi in range(GDN_HEADS // grp):
            hs = [gi * grp + u for u in range(grp)]

            def stack(a, off):
                return jnp.concatenate([a[rows, off + h * dk: off + (h + 1) * dk] for h in hs], axis=0)

            q = stack(y, 0)
            k = stack(y, w)
            v = stack(y, 2 * w)
            q = q * lax.rsqrt(jnp.sum(q * q, axis=-1, keepdims=True) + NORM_EPS) * dk ** -0.5
            k = k * lax.rsqrt(jnp.sum(k * k, axis=-1, keepdims=True) + NORM_EPS)
            beta = jnp.concatenate([beta_t[rows, h:h + 1] for h in hs], axis=0)
            gc = jnp.concatenate([gcum[:, GDN_HEADS + h:GDN_HEADS + h + 1] for h in hs], axis=0)
            gc_row = jnp.concatenate([gcum_t[GDN_HEADS + h:GDN_HEADS + h + 1, :] for h in hs], axis=1)
            gl = jnp.concatenate([jnp.broadcast_to(gcum[c - 1:c, GDN_HEADS + h:GDN_HEADS + h + 1], (c, 1))
                                  for h in hs], axis=0)

            decay = jnp.where(incl, jnp.exp(jnp.where(incl, gc - gc_row, 0.0)), 0.0)
            kb = k * beta
            k16 = k.astype(BF16)
            a = lax.dot_general(kb.astype(BF16), k16, NT, preferred_element_type=F32)
            a = jnp.where(strict, a * decay, 0.0)
            eg = jnp.exp(gc)
            probs.append(dict(
                rows=rows, hs=hs, a16=a.astype(BF16),
                t=eye - jnp.where(halves[0], a, 0.0),
                rhs=jnp.concatenate([v * beta, kb * eg], axis=1).astype(BF16),
                attn=(lax.dot_general(q.astype(BF16), k16, NT, preferred_element_type=F32) * decay).astype(BF16),
                q_g=(q * eg).astype(BF16),
                k_end=(k * jnp.exp(gl - gc)).astype(BF16),
                d_last=[jnp.exp(gcum[c - 1:c, GDN_HEADS + h:GDN_HEADS + h + 1]) for h in hs]))

    for lvl in range(1, len(halves)):
        off16 = jnp.where(jnp.logical_and(halves[lvl], jnp.logical_not(halves[lvl - 1])), 1.0, 0.0).astype(BF16)
        t16s = [p["t"].astype(BF16) for p in probs]
        xs = [jnp.dot(p["a16"] * off16, t16, preferred_element_type=F32).astype(BF16)
              for p, t16 in zip(probs, t16s)]
        for p, t16, x in zip(probs, t16s, xs):
            p["t"] = p["t"] - jnp.dot(t16, x, preferred_element_type=F32)
    for p in probs:
        wk = jnp.dot(p["t"].astype(BF16), p["rhs"], preferred_element_type=F32)
        p["w_val"] = wk[:, :GDN_DV]
        p["k_cum"] = wk[:, GDN_DV:].astype(BF16)

    for p in probs:
        rows, hs = p["rows"], p["hs"]
        v_new = []
        for u, h in enumerate(hs):
            hr = slice(u * c, (u + 1) * c)
            st16 = s_scr[h].astype(BF16)
            v_new.append(p["w_val"][hr] - jnp.dot(p["k_cum"][hr], st16, preferred_element_type=F32))
        v16 = jnp.concatenate(v_new, axis=0).astype(BF16)
        o_intra = jnp.dot(p["attn"], v16, preferred_element_type=F32)
        for u, h in enumerate(hs):
            hr = slice(u * c, (u + 1) * c)
            st = s_scr[h]
            o = o_intra[hr] + jnp.dot(p["q_g"][hr], st.astype(BF16), preferred_element_type=F32)
            s_scr[h] = (st * p["d_last"][u]
                        + lax.dot_general(p["k_end"][hr], v16[hr], TN, preferred_element_type=F32))
            o = o * lax.rsqrt(jnp.mean(o * o, axis=-1, keepdims=True) + NORM_EPS)
            cs = slice(h * GDN_DV, (h + 1) * GDN_DV)
            o_ref[0, rows, cs] = (o * nw * _silu(gate_ref[0, rows, cs].astype(F32))).astype(o_ref.dtype)


def _gdn(proj, conv_w, pvec, o_norm, tg=512):
    bsz, s, _ = proj.shape
    tg = min(tg, s)
    w = GDN_W
    return pl.pallas_call(
        _gdn_kernel,
        grid=(bsz, s // tg),
        in_specs=[pl.BlockSpec((1, tg, 3 * w), lambda b, t: (b, t, 0)),
                  pl.BlockSpec((1, tg, w), lambda b, t: (b, t, 3)),
                  pl.BlockSpec((1, tg, LANES), lambda b, t: (b, t, 4 * w // LANES)),
                  pl.BlockSpec((GDN_CONV, 3 * w), lambda b, t: (0, 0)),
                  pl.BlockSpec((2, LANES), lambda b, t: (0, 0)),
                  pl.BlockSpec((1, GDN_DV), lambda b, t: (0, 0))],
        out_specs=pl.BlockSpec((1, tg, w), lambda b, t: (b, t, 0)),
        out_shape=jax.ShapeDtypeStruct((bsz, s, w), BF16),
        scratch_shapes=[pltpu.VMEM((8, 3 * w), F32),
                        pltpu.VMEM((GDN_HEADS, GDN_DK, GDN_DV), F32)],
        compiler_params=_cparams(("arbitrary", "arbitrary")),
        name="gdn_mixer",
    )(proj, proj, proj, conv_w, pvec, o_norm)


def _mixout_kernel(n_act, *refs):
    acts = refs[:n_act]
    ws = refs[n_act:2 * n_act]
    x_ref, g1_ref, lng_ref, lnb_ref, sc_ref, sh_ref, rw_ref = refs[2 * n_act:2 * n_act + 7]
    xo_ref, h_ref, rl_ref = refs[2 * n_act + 7:]
    y = None
    for a_ref, w_ref in zip(acts, ws):
        t = jnp.dot(a_ref[0].astype(BF16), w_ref[...], preferred_element_type=F32)
        y = t if y is None else y + t
    xn = _layer_norm(DEEPNORM_ALPHA * x_ref[0] + g1_ref[0] * y, lng_ref[...], lnb_ref[...])
    xo_ref[0] = xn
    h = xn * (1.0 + sc_ref[0]) + sh_ref[0]
    h_hi = h.astype(BF16)
    h_ref[0] = h_hi
    h_lo = (h - h_hi.astype(F32)).astype(BF16)
    rw = rw_ref[...]
    rw_hi = rw.astype(BF16)
    rw_lo = (rw - rw_hi.astype(F32)).astype(BF16)
    rl_ref[0] = (lax.dot_general(rw_hi, h_hi, NT, preferred_element_type=F32)
                 + lax.dot_general(rw_hi, h_lo, NT, preferred_element_type=F32)
                 + lax.dot_general(rw_lo, h_hi, NT, preferred_element_type=F32))


def _mixout(acts, ws, x, g1, ln_g, ln_b, sc2, sh2, router_wt, tm=1024):
    bsz, s, d = x.shape
    tm = min(tm, s)
    n_act = len(acts)
    ne = router_wt.shape[0]
    vec = pl.BlockSpec((1, 1, d), lambda b, i: (b, 0, 0))
    par = pl.BlockSpec((1, d), lambda b, i: (0, 0))
    in_specs = ([pl.BlockSpec((1, tm, a.shape[-1]), lambda b, i: (b, i, 0)) for a in acts]
                + [pl.BlockSpec(w.shape, lambda b, i: (0, 0)) for w in ws]
                + [pl.BlockSpec((1, tm, d), lambda b, i: (b, i, 0)), vec, par, par, vec, vec,
                   pl.BlockSpec((ne, d), lambda b, i: (0, 0))])
    return pl.pallas_call(
        functools.partial(_mixout_kernel, n_act),
        grid=(bsz, s // tm),
        in_specs=in_specs,
        out_specs=[pl.BlockSpec((1, tm, d), lambda b, i: (b, i, 0)),
                   pl.BlockSpec((1, tm, d), lambda b, i: (b, i, 0)),
                   pl.BlockSpec((1, ne, tm), lambda b, i: (b, 0, i))],
        out_shape=[jax.ShapeDtypeStruct((bsz, s, d), F32),
                   jax.ShapeDtypeStruct((bsz, s, d), BF16),
                   jax.ShapeDtypeStruct((bsz, ne, s), F32)],
        compiler_params=_cparams(("arbitrary", "arbitrary")),
        name="mix_out",
    )(*acts, *ws, x, g1, ln_g, ln_b, sc2, sh2, router_wt)


def _first_max(vals, idx, axis, sentinel):
    m = jnp.max(vals, axis=axis, keepdims=True)
    first = jnp.min(jnp.where(vals == m, idx, sentinel), axis=axis, keepdims=True)
    return m, idx == first


def _router_kernel(rl_ref, rb_ref, g_ref, gt_ref, cnt_ref):
    ne, tn = rl_ref.shape[1], rl_ref.shape[2]
    gsz = ne // N_GROUPS
    scores = _sigmoid(rl_ref[0])
    sel = scores + rb_ref[...]
    ridx = lax.broadcasted_iota(jnp.int32, (gsz, tn), 0)
    gidx = lax.broadcasted_iota(jnp.int32, (N_GROUPS, tn), 0)
    gs = jnp.zeros((N_GROUPS, tn), F32)
    for g in range(N_GROUPS):
        sg = sel[g * gsz:(g + 1) * gsz, :]
        m1, hit = _first_max(sg, ridx, 0, gsz)
        m2 = jnp.max(jnp.where(hit, -jnp.inf, sg), axis=0, keepdims=True)
        gs = jnp.where(gidx == g, m1 + m2, gs)
    gsel = None
    for _ in range(TOPK_GROUPS):
        _, hit = _first_max(gs, gidx, 0, N_GROUPS)
        gsel = hit if gsel is None else jnp.logical_or(gsel, hit)
        gs = jnp.where(hit, -jnp.inf, gs)
    gself = gsel.astype(F32)
    emask = jnp.concatenate([jnp.broadcast_to(gself[g:g + 1, :], (gsz, tn)) for g in range(N_GROUPS)], axis=0)
    cand = jnp.where(emask > 0.5, sel, -jnp.inf)
    eidx = lax.broadcasted_iota(jnp.int32, cand.shape, 0)
    chosen = None
    for _ in range(TOP_K):
        _, hit = _first_max(cand, eidx, 0, ne)
        chosen = hit if chosen is None else jnp.logical_or(chosen, hit)
        cand = jnp.where(hit, -jnp.inf, cand)
    wsel = jnp.where(chosen, scores, 0.0)
    tot = jnp.sum(wsel, axis=0, keepdims=True)
    gates = wsel / (tot + 1e-20) * ROUTED_SCALE
    g_ref[0] = gates.T
    gt_ref[0] = gates
    tile_of = lax.broadcasted_iota(jnp.int32, (tn, LANES), 0) // MOE_TM
    ind = (tile_of == lax.broadcasted_iota(jnp.int32, (tn, LANES), 1)).astype(BF16)
    routed = jnp.where(gates > 0.0, 1.0, 0.0).astype(BF16)
    cnt_ref[0, 0] = jnp.dot(routed, ind, preferred_element_type=F32)


def _router(rl, router_b, tn=1024):
    bsz, ne, s = rl.shape
    tn = min(tn, s)
    return pl.pallas_call(
        _router_kernel,
        grid=(bsz, s // tn),
        in_specs=[pl.BlockSpec((1, ne, tn), lambda b, i: (b, 0, i)),
                  pl.BlockSpec((ne, 1), lambda b, i: (0, 0))],
        out_specs=[pl.BlockSpec((1, tn, ne), lambda b, i: (b, i, 0)),
                   pl.BlockSpec((1, ne, tn), lambda b, i: (b, 0, i)),
                   pl.BlockSpec((1, 1, ne, LANES), lambda b, i: (b, i, 0, 0))],
        out_shape=[jax.ShapeDtypeStruct((bsz, s, ne), F32),
                   jax.ShapeDtypeStruct((bsz, ne, s), F32),
                   jax.ShapeDtypeStruct((bsz, s // tn, ne, LANES), F32)],
        compiler_params=_cparams(("arbitrary", "arbitrary")),
        name="moe_router",
    )(rl, router_b.reshape(ne, 1))


MOE_TM = 256
MOE_ALIGN = 16
MOE_R = 1024
MOE_LC = 512
MOE_LMAX = -(-(MOE_TM * TOP_K + N_EXPERTS * (MOE_ALIGN - 1)) // MOE_LC) * MOE_LC
MOE_NP = MOE_LMAX // MOE_ALIGN
MOE_LP = MOE_LC // MOE_ALIGN
POS_SPLIT = 256.0


def _ffn(x, wg, wu, wd):
    a = jnp.dot(x, wg, preferred_element_type=F32)
    u = jnp.dot(x, wu, preferred_element_type=F32)
    return jnp.dot((_silu(a) * u).astype(BF16), wd, preferred_element_type=F32)


def _split_pos(pos, axis):
    hi = jnp.floor(pos * (1.0 / POS_SPLIT)) * POS_SPLIT
    return jnp.concatenate([hi, pos - hi], axis=axis).astype(BF16)


def _for_each_piece(n_pieces, fn):
    unroll = 8

    def body(q, carry):
        for u in range(unroll):
            fn(q * unroll + u)
        return carry

    full = n_pieces // unroll
    lax.fori_loop(0, full, body, 0)

    def tail(p, carry):
        fn(p)
        return carry

    lax.fori_loop(full * unroll, n_pieces, tail, 0)


def _dispatch_kernel(dst_s, np_s, h_ref, gt_ref, offc_ref, offr_ref, cntr_ref, xs_hbm, sorted_scr, sems):
    i = pl.program_id(0)
    last = pl.num_programs(0) - 1
    slot = i % 2
    tm = h_ref.shape[0]
    routed = gt_ref[0] > 0.0
    t0 = lax.broadcasted_iota(jnp.int32, (tm, tm), 0)
    t1 = lax.broadcasted_iota(jnp.int32, (tm, tm), 1)
    earlier = jnp.where(t0 < t1, 1.0, 0.0).astype(BF16)
    rank_t = jnp.dot(jnp.where(routed, 1.0, 0.0).astype(BF16), earlier, preferred_element_type=F32)
    pos_t = jnp.where(routed, rank_t + offc_ref[0] + 1.0, 0.0)
    pos2 = _split_pos(pos_t, 0)
    x = h_ref[...]
    offr = offr_ref[0]
    endr = offr + cntr_ref[0]
    for c in range(MOE_LMAX // MOE_LC):
        r = (c * MOE_LC + lax.broadcasted_iota(jnp.int32, (MOE_LC, 1), 0)).astype(F32)
        owner = jnp.where(jnp.logical_and(r >= offr, r < endr), 1.0, 0.0).astype(BF16)
        possel = jnp.dot(jnp.concatenate([owner, owner], axis=1), pos2, preferred_element_type=F32)
        perm = jnp.where(possel == r + 1.0, 1.0, 0.0).astype(BF16)
        srt = jnp.dot(perm, x, preferred_element_type=F32).astype(BF16)
        sorted_scr[slot, c * MOE_LP:(c + 1) * MOE_LP] = srt.reshape(MOE_LP, MOE_ALIGN, srt.shape[-1])

    def piece(sl, tile, p):
        return pltpu.make_async_copy(sorted_scr.at[sl, p], xs_hbm.at[dst_s[tile * MOE_NP + p]], sems.at[sl])

    _for_each_piece(np_s[i], lambda p: piece(slot, i, p).start())

    @pl.when(i > 0)
    def _():
        _for_each_piece(np_s[i - 1], lambda p: piece(1 - slot, i - 1, p).wait())

    @pl.when(i == last)
    def _():
        _for_each_piece(np_s[i], lambda p: piece(slot, i, p).wait())


def _expert_ffn_kernel(be_s, nu_s, x_ref, wg_ref, wu_ref, wd_ref, y_ref):
    @pl.when(pl.program_id(0) < nu_s[0])
    def _():
        y_ref[...] = _ffn(x_ref[...], wg_ref[0, 0].astype(BF16), wu_ref[0, 0].astype(BF16),
                          wd_ref[0, 0].astype(BF16)).astype(BF16)


def _combine_kernel(dst_s, np_s, ys_hbm, g_ref, h_ref, offr_ref, offc_ref, cntc_ref,
                    sg_ref, su_ref, sd_ref, x_ref, g2_ref, lng_ref, lnb_ref, o_ref, ys_scr, sems):
    i = pl.program_id(0)
    last = pl.num_programs(0) - 1
    slot = i % 2
    tm = h_ref.shape[0]

    def piece(sl, tile, p):
        return pltpu.make_async_copy(ys_hbm.at[dst_s[tile * MOE_NP + p]], ys_scr.at[sl, p], sems.at[sl])

    @pl.when(i == 0)
    def _():
        ys_scr[...] = jnp.zeros(ys_scr.shape, BF16)
        _for_each_piece(np_s[0], lambda p: piece(0, 0, p).start())

    @pl.when(i < last)
    def _():
        _for_each_piece(np_s[i + 1], lambda p: piece(1 - slot, i + 1, p).start())

    acc = _ffn(h_ref[...], sg_ref[...], su_ref[...], sd_ref[...])
    g = g_ref[...]
    routed = g > 0.0
    t0 = lax.broadcasted_iota(jnp.int32, (tm, tm), 0)
    t1 = lax.broadcasted_iota(jnp.int32, (tm, tm), 1)
    earlier = jnp.where(t0 > t1, 1.0, 0.0).astype(BF16)
    rank = jnp.dot(earlier, jnp.where(routed, 1.0, 0.0).astype(BF16), preferred_element_type=F32)
    pos = jnp.where(routed, rank + offr_ref[0] + 1.0, 0.0)
    pos2 = _split_pos(pos, 1)
    g16 = g.astype(BF16)
    offc = offc_ref[0]
    endc = offc + cntc_ref[0]
    _for_each_piece(np_s[i], lambda p: piece(slot, i, p).wait())
    for c in range(MOE_LMAX // MOE_LC):
        r = (c * MOE_LC + lax.broadcasted_iota(jnp.int32, (1, MOE_LC), 1)).astype(F32)
        owner = jnp.where(jnp.logical_and(r >= offc, r < endc), 1.0, 0.0).astype(BF16)
        possel = jnp.dot(pos2, jnp.concatenate([owner, owner], axis=0), preferred_element_type=F32)
        gsel = jnp.dot(g16, owner, preferred_element_type=F32)
        w = jnp.where(possel == r + 1.0, gsel, 0.0).astype(BF16)
        ys = ys_scr[slot, c * MOE_LP:(c + 1) * MOE_LP].reshape(MOE_LC, ys_scr.shape[-1])
        acc = acc + jnp.dot(w, ys, preferred_element_type=F32)
    z = DEEPNORM_ALPHA * x_ref[...] + g2_ref[0] * acc
    o_ref[...] = _layer_norm(z, lng_ref[...], lnb_ref[...])


def _ceil_to(v, m):
    return jnp.floor((v + (m - 1.0)) * (1.0 / m)) * m


def _moe_layout_kernel(cnt_ref, dst_ref, np_ref, off_ref, cntp_ref, be_ref, nu_ref):
    cnt = cnt_ref[...]
    ntiles, ne = cnt.shape
    cntp = _ceil_to(cnt, MOE_ALIGN)
    e0 = lax.broadcasted_iota(jnp.int32, (ne, ne), 0)
    e1 = lax.broadcasted_iota(jnp.int32, (ne, ne), 1)
    i0 = lax.broadcasted_iota(jnp.int32, (ntiles, ntiles), 0)
    i1 = lax.broadcasted_iota(jnp.int32, (ntiles, ntiles), 1)

    def mm(a, b):
        return jnp.dot(a, b, precision=HI, preferred_element_type=F32)

    off = mm(cntp, jnp.where(e0 < e1, 1.0, 0.0))
    before = mm(jnp.where(i0 > i1, 1.0, 0.0), cntp)
    tot = jnp.broadcast_to(jnp.sum(cntp, axis=0, keepdims=True), (8, ne))
    totr = _ceil_to(tot, MOE_R)
    base = mm(totr, jnp.where(e0 < e1, 1.0, 0.0))[0:1, :]
    delta = base + before - off
    step = mm(delta, jnp.where(e0 == e1, 1.0, 0.0) - jnp.where(e0 + 1 == e1, 1.0, 0.0))
    rowp = (lax.broadcasted_iota(jnp.int32, (1, MOE_NP), 1) * MOE_ALIGN).astype(F32)
    dst = jnp.broadcast_to(rowp, (ntiles, MOE_NP))
    for e in range(ne):
        dst = dst + jnp.where(off[:, e:e + 1] <= rowp, step[:, e:e + 1], 0.0)
    dst_ref[...] = (dst * (1.0 / MOE_ALIGN)).astype(jnp.int32)
    pieces = jnp.sum(cntp, axis=1, keepdims=True) * (1.0 / MOE_ALIGN)
    np_ref[...] = jnp.broadcast_to(pieces, np_ref.shape).astype(jnp.int32)
    off_ref[...] = off
    cntp_ref[...] = cntp
    tot_c = lax.dot_general(cntp, jnp.ones((ntiles, LANES), F32), TN, precision=HI, preferred_element_type=F32)
    end_c = mm(jnp.where(e0 >= e1, 1.0, 0.0), _ceil_to(tot_c, MOE_R))[:, 0:1]
    total = end_c[ne - 1:ne, :]
    first = jnp.minimum((lax.broadcasted_iota(jnp.int32, be_ref.shape, 1) * MOE_R).astype(F32), total - 1.0)
    be = jnp.sum(jnp.where(end_c <= first, 1.0, 0.0), axis=0, keepdims=True)
    be_ref[...] = jnp.minimum(be, ne - 1.0).astype(jnp.int32)
    nu_ref[...] = jnp.broadcast_to(total * (1.0 / MOE_R), nu_ref.shape).astype(jnp.int32)


def _moe_layout(cnt_raw, tn):
    ne = N_EXPERTS
    nsub = tn // MOE_TM
    cnt = jnp.transpose(cnt_raw[..., :nsub], (0, 1, 3, 2)).reshape(-1, ne)
    ntiles = cnt.shape[0]
    nblk = -(-(ntiles * (MOE_TM * TOP_K + ne * (MOE_ALIGN - 1)) + ne * (MOE_R - 1)) // MOE_R)
    nblk_pad = -(-nblk // LANES) * LANES
    dst, npieces, off, cntp, blk_exp, nused = pl.pallas_call(
        _moe_layout_kernel,
        out_shape=[jax.ShapeDtypeStruct((ntiles, MOE_NP), jnp.int32),
                   jax.ShapeDtypeStruct((ntiles, LANES), jnp.int32),
                   jax.ShapeDtypeStruct((ntiles, ne), F32),
                   jax.ShapeDtypeStruct((ntiles, ne), F32),
                   jax.ShapeDtypeStruct((1, nblk_pad), jnp.int32),
                   jax.ShapeDtypeStruct((1, LANES), jnp.int32)],
        compiler_params=pltpu.CompilerParams(vmem_limit_bytes=VMEM_LIMIT),
        name="moe_layout",
    )(cnt)
    return dst.reshape(-1), npieces[:, 0], off, cntp, nblk, nused[0, :1], blk_exp[0, :nblk]


def _moe(h2, gates, gates_t, cnt_raw, tn, layer, wg, wu, wd, sg, su, sd, x, g2, ln_g, ln_b):
    bsz, s, d = x.shape
    t = bsz * s
    tm = MOE_TM
    _, ne, _, de = wg.shape
    per_b = s // tm
    ntiles = t // tm
    dst, npieces, off_f, cnt_f, nblk, nused, blk_exp = _moe_layout(cnt_raw, tn)
    scalars = (dst, npieces)
    row = pl.BlockSpec((1, 1, ne), lambda i, *_: (i, 0, 0))
    col = pl.BlockSpec((1, ne, 1), lambda i, *_: (i, 0, 0))
    tok = pl.BlockSpec((tm, d), lambda i, *_: (i, 0))
    par = pl.BlockSpec((1, d), lambda i, *_: (0, 0))
    rows = nblk * MOE_R

    xs = pl.pallas_call(
        _dispatch_kernel,
        grid_spec=pltpu.PrefetchScalarGridSpec(
            num_scalar_prefetch=2, grid=(ntiles,),
            in_specs=[tok,
                      pl.BlockSpec((1, ne, tm), lambda i, *_: (i // per_b, 0, i % per_b)),
                      col, row, row],
            out_specs=pl.BlockSpec(memory_space=pl.ANY),
            scratch_shapes=[pltpu.VMEM((2, MOE_NP, MOE_ALIGN, d), BF16), pltpu.SemaphoreType.DMA((2,))]),
        out_shape=jax.ShapeDtypeStruct((rows // MOE_ALIGN, MOE_ALIGN, d), BF16),
        compiler_params=_cparams(("arbitrary",)),
        name="moe_dispatch",
    )(*scalars, h2.reshape(t, d), gates_t, off_f.reshape(ntiles, ne, 1), off_f.reshape(ntiles, 1, ne),
      cnt_f.reshape(ntiles, 1, ne))

    def blk_index(j, be, nu):
        return jnp.maximum(jnp.minimum(j, nu[0] - 1), 0), 0

    blk = pl.BlockSpec((MOE_R, d), blk_index)
    ys = pl.pallas_call(
        _expert_ffn_kernel,
        grid_spec=pltpu.PrefetchScalarGridSpec(
            num_scalar_prefetch=2, grid=(nblk,),
            in_specs=[blk,
                      pl.BlockSpec((1, 1, d, de), lambda j, be, nu: (layer, be[j], 0, 0)),
                      pl.BlockSpec((1, 1, d, de), lambda j, be, nu: (layer, be[j], 0, 0)),
                      pl.BlockSpec((1, 1, de, d), lambda j, be, nu: (layer, be[j], 0, 0))],
            out_specs=blk),
        out_shape=jax.ShapeDtypeStruct((rows, d), BF16),
        compiler_params=_cparams(("arbitrary",)),
        name="moe_expert_ffn",
    )(blk_exp, nused, xs.reshape(rows, d), wg, wu, wd)

    out = pl.pallas_call(
        _combine_kernel,
        grid_spec=pltpu.PrefetchScalarGridSpec(
            num_scalar_prefetch=2, grid=(ntiles,),
            in_specs=[pl.BlockSpec(memory_space=pl.ANY),
                      pl.BlockSpec((tm, ne), lambda i, *_: (i, 0)),
                      tok, row, col, col,
                      pl.BlockSpec(sg.shape, lambda i, *_: (0, 0)),
                      pl.BlockSpec(su.shape, lambda i, *_: (0, 0)),
                      pl.BlockSpec(sd.shape, lambda i, *_: (0, 0)),
                      tok,
                      pl.BlockSpec((1, 1, d), lambda i, *_: (i // per_b, 0, 0)),
                      par, par],
            out_specs=tok,
            scratch_shapes=[pltpu.VMEM((2, MOE_NP, MOE_ALIGN, d), BF16), pltpu.SemaphoreType.DMA((2,))]),
        out_shape=jax.ShapeDtypeStruct((t, d), F32),
        compiler_params=_cparams(("arbitrary",)),
        name="moe_combine",
    )(*scalars, ys.reshape(rows // MOE_ALIGN, MOE_ALIGN, d), gates.reshape(t, ne), h2.reshape(t, d),
      off_f.reshape(ntiles, 1, ne),
      off_f.reshape(ntiles, ne, 1), cnt_f.reshape(ntiles, ne, 1), sg, su, sd, x.reshape(t, d), g2, ln_g, ln_b)
    return out.reshape(bsz, s, d)


def _pad_cols(w, n):
    return jnp.pad(w, ((0, 0), (0, n - w.shape[1])))


def kernel(x, c, rpe_bias, ada_w, ada_b, ln_mix_g, ln_mix_b, ln_ffn_g, ln_ffn_b, ev_w_in, ev_gk_w2, ev_gk_b, ev_norm, ev_w_out, od_w_in, od_conv_w, od_a_log, od_dt_bias, od_norm, od_w_out, moe_router_w, moe_router_b, moe_w_gate, moe_w_up, moe_w_down, sh_w_gate, sh_w_up, sh_w_down):
    bsz, s, d = x.shape
    mod = _ada(c, ada_w, ada_b)
    tiles = _rpe_tiles(rpe_bias)

    for layer in range(DEPTH):
        sh1, sc1, g1, sh2, sc2, g2 = [mod[layer, :, u * d:(u + 1) * d].reshape(bsz, 1, d) for u in range(6)]
        i = layer // 2
        if layer % 2 == 0:
            n_main = 3 * MOBA_W + 2 * GLA_QK_W + 2 * GLA_V_W
            w_in = jnp.concatenate([ev_w_in[i][:, :n_main], _pad_cols(ev_w_in[i][:, n_main:], LANES)], axis=1)
            proj = _mod_matmul(x, sc1, sh1, w_in.astype(BF16))
            nb = MOBA_W // LANES
            o_a = _moba(proj, tiles, 0, nb, 2 * nb)
            gk_w2p = jnp.pad(ev_gk_w2[i], ((0, LANES - GLA_GATE_RANK), (0, 0)))
            gla0 = 3 * MOBA_W
            o_b = _gla(proj, gk_w2p, ev_gk_b[i].reshape(1, -1), ev_norm[i].reshape(1, -1),
                       gla0 // GLA_QK_W, gla0 // GLA_QK_W + 1,
                       (gla0 + 2 * GLA_QK_W) // GLA_V_W, (gla0 + 2 * GLA_QK_W) // GLA_V_W + 1,
                       n_main // LANES)
            w_out = ev_w_out[i].astype(BF16)
            acts, ws = [o_a, o_b], [w_out[:MOBA_W], w_out[MOBA_W:]]
        else:
            n_main = 4 * GDN_W
            w_in = jnp.concatenate([od_w_in[i][:, :n_main], _pad_cols(od_w_in[i][:, n_main:], LANES)], axis=1)
            proj = _mod_matmul(x, sc1, sh1, w_in.astype(BF16))
            pvec = jnp.zeros((2, LANES), F32)
            pvec = pvec.at[0, GDN_HEADS:2 * GDN_HEADS].set(od_a_log[i])
            pvec = pvec.at[1, GDN_HEADS:2 * GDN_HEADS].set(od_dt_bias[i])
            o = _gdn(proj, od_conv_w[i], pvec, od_norm[i].reshape(1, -1))
            acts, ws = [o], [od_w_out[i].astype(BF16)]

        x, h2, rl = _mixout(acts, ws, x, g1, ln_mix_g[layer].reshape(1, d), ln_mix_b[layer].reshape(1, d),
                            sc2, sh2, moe_router_w[layer].T)
        router_tn = min(1024, s)
        gates, gates_t, cnt_raw = _router(rl, moe_router_b[layer], router_tn)
        x = _moe(h2, gates, gates_t, cnt_raw, router_tn, layer, moe_w_gate, moe_w_up, moe_w_down,
                 sh_w_gate[layer].astype(BF16), sh_w_up[layer].astype(BF16),
                 sh_w_down[layer].astype(BF16), x, g2, ln_ffn_g[layer].reshape(1, d), ln_ffn_b[layer].reshape(1, d))
    return x
```

```python
import functools
import math

import numpy as np
import jax
import jax.numpy as jnp
from jax import lax
from jax.experimental import pallas as pl
from jax.experimental.pallas import tpu as pltpu

F32 = jnp.float32
BF16 = jnp.bfloat16
HI = lax.Precision.HIGHEST
NT = (((1,), (1,)), ((), ()))
TN = (((0,), (0,)), ((), ()))
NEG = -1e30
LOG2E = math.log2(math.e)

LANES = 128
VMEM_LIMIT = 56 * 1024 * 1024

DEPTH = 2
MOBA_HEAD_DIM = 128
MOBA_HEADS = 4
MOBA_BLOCK = 256
MOBA_TOPK = 3
MOBA_GROUP = 8
MOBA_SUB = 4
MOBA_QBLOCKS = 4
MOBA_DEN_ROWS = 16
GLA_DV = 128
GLA_HEADS = 4
GLA_DK = 64
GLA_GATE_RANK = 16
GLA_GATE_NORM = 16.0
GLA_CHUNK = 64
GDN_DK = 128
GDN_DV = 128
GDN_HEADS = 8
GDN_CONV = 4
GDN_CHUNK = 64
GDN_GROUP = 4
RPE_BUCKETS = 32
RPE_MAX_DIST = 2048
RPE_TILES = 8
N_EXPERTS = 64
TOP_K = 6
N_GROUPS = 8
TOPK_GROUPS = 4
D_EXPERT = 256
ROUTED_SCALE = 2.5
DEEPNORM_ALPHA = float((2 * DEPTH) ** 0.25)
LN_EPS = 1e-5
NORM_EPS = 1e-6

MOBA_W = MOBA_HEADS * MOBA_HEAD_DIM
GLA_QK_W = GLA_HEADS * GLA_DK
GLA_V_W = GLA_HEADS * GLA_DV
GDN_W = GDN_HEADS * GDN_DK


def _cparams(sem):
    return pltpu.CompilerParams(dimension_semantics=sem, vmem_limit_bytes=VMEM_LIMIT)


def _sigmoid(x):
    return 1.0 / (1.0 + jnp.exp(-x))


def _silu(x):
    return x * _sigmoid(x)


def _softplus(x):
    return jnp.maximum(x, 0.0) + jnp.log(1.0 + jnp.exp(-jnp.abs(x)))


def _layer_norm(z, g, b):
    mu = jnp.mean(z, axis=-1, keepdims=True)
    zc = z - mu
    var = jnp.mean(zc * zc, axis=-1, keepdims=True)
    return zc * lax.rsqrt(var + LN_EPS) * g + b


def _ada_kernel(c_ref, w_ref, b_ref, o_ref):
    ca = _silu(c_ref[...])
    o_ref[0] = jnp.dot(ca, w_ref[0], precision=HI, preferred_element_type=F32) + b_ref[0]


def _ada(c, ada_w, ada_b):
    depth, d, n = ada_w.shape
    bsz = c.shape[0]
    tn = 6 * LANES
    return pl.pallas_call(
        _ada_kernel,
        grid=(depth, n // tn),
        in_specs=[pl.BlockSpec((bsz, d), lambda l, j: (0, 0)),
                  pl.BlockSpec((1, d, tn), lambda l, j: (l, 0, j)),
                  pl.BlockSpec((1, 1, tn), lambda l, j: (l, 0, j))],
        out_specs=pl.BlockSpec((1, bsz, tn), lambda l, j: (l, 0, j)),
        out_shape=jax.ShapeDtypeStruct((depth, bsz, n), F32),
        compiler_params=_cparams(("arbitrary", "arbitrary")),
        name="ada_mod",
    )(c, ada_w, ada_b.reshape(depth, 1, n))


def _modmm_kernel(x_ref, sc_ref, sh_ref, w_ref, o_ref, h_scr):
    @pl.when(pl.program_id(2) == 0)
    def _():
        h_scr[...] = (x_ref[0] * (1.0 + sc_ref[0]) + sh_ref[0]).astype(BF16)

    o_ref[0] = jnp.dot(h_scr[...], w_ref[...], preferred_element_type=F32).astype(o_ref.dtype)


def _col_tile(n, cap):
    best = LANES
    for t in range(LANES, cap + 1, LANES):
        if n % t == 0:
            best = t
    return best


def _mod_matmul(x, sc, sh, w, tm=1024, tn_cap=4608):
    bsz, s, d = x.shape
    n = w.shape[1]
    tm = min(tm, s)
    tn = _col_tile(n, tn_cap)
    return pl.pallas_call(
        _modmm_kernel,
        grid=(bsz, s // tm, n // tn),
        in_specs=[pl.BlockSpec((1, tm, d), lambda b, i, j: (b, i, 0)),
                  pl.BlockSpec((1, 1, d), lambda b, i, j: (b, 0, 0)),
                  pl.BlockSpec((1, 1, d), lambda b, i, j: (b, 0, 0)),
                  pl.BlockSpec((d, tn), lambda b, i, j: (0, j))],
        out_specs=pl.BlockSpec((1, tm, tn), lambda b, i, j: (b, i, j)),
        out_shape=jax.ShapeDtypeStruct((bsz, s, n), BF16),
        scratch_shapes=[pltpu.VMEM((tm, d), BF16)],
        compiler_params=_cparams(("arbitrary", "arbitrary", "arbitrary")),
        name="mod_matmul",
    )(x, sc, sh, w)


def _rpe_lower_bounds():
    exact = RPE_BUCKETS // 2
    d = np.arange(0, 2 * RPE_MAX_DIST, dtype=np.int64)
    logd = np.log(np.maximum(d, 1).astype(np.float64) / exact)
    large = exact + (logd / math.log(RPE_MAX_DIST / exact) * (RPE_BUCKETS - exact)).astype(np.int64)
    large = np.minimum(large, RPE_BUCKETS - 1)
    bucket = np.where(d < exact, d, large)
    return [int(np.argmax(bucket >= k)) for k in range(RPE_BUCKETS)]


def _rpe_tiles_kernel(lo, rpe_ref, o_ref):
    h = pl.program_id(0)
    j = pl.program_id(1)
    blk = o_ref.shape[-1]
    key = lax.broadcasted_iota(jnp.int32, (blk, blk), 0)
    qry = lax.broadcasted_iota(jnp.int32, (blk, blk), 1)
    dist = j * blk + qry - key
    val = jnp.full((blk, blk), rpe_ref[0, h], F32)
    for k in range(1, RPE_BUCKETS):
        val = jnp.where(dist >= lo[k], rpe_ref[k, h], val)
    o_ref[0, 0] = jnp.where(dist >= 0, val * LOG2E, NEG)


def _rpe_tiles(rpe_bias):
    heads = rpe_bias.shape[1]
    lo = _rpe_lower_bounds()
    assert lo[-1] <= (RPE_TILES - 1) * MOBA_BLOCK - (MOBA_BLOCK - 1)
    return pl.pallas_call(
        functools.partial(_rpe_tiles_kernel, lo),
        grid=(heads, RPE_TILES),
        in_specs=[pl.BlockSpec(memory_space=pltpu.SMEM)],
        out_specs=pl.BlockSpec((1, 1, MOBA_BLOCK, MOBA_BLOCK), lambda h, j: (h, j, 0, 0)),
        out_shape=jax.ShapeDtypeStruct((heads, RPE_TILES, MOBA_BLOCK, MOBA_BLOCK), F32),
        compiler_params=_cparams(("arbitrary", "arbitrary")),
        name="rpe_tiles",
    )(rpe_bias)


def _moba_kernel(q_ref, k_ref, v_ref, t_ref, o_ref, kb_scr, vt_scr, km_scr, sel_scr, far_scr):
    step = pl.program_id(2)
    nkb, blk, dh = kb_scr.shape

    @pl.when(step == 0)
    def _():
        for n in range(nkb):
            kn = k_ref[0, n * blk:(n + 1) * blk, :]
            kb_scr[n] = kn.astype(BF16)
            km_scr[n:n + 1, :] = jnp.mean(kn.astype(F32), axis=0, keepdims=True)
            vt_scr[n, :dh] = v_ref[0, n * blk:(n + 1) * blk, :].astype(F32).T.astype(BF16)
            ones_row = lax.broadcasted_iota(jnp.int32, (MOBA_DEN_ROWS, blk), 0) == 0
            vt_scr[n, dh:] = jnp.where(ones_row, 1.0, 0.0).astype(BF16)

    for sub in range(MOBA_QBLOCKS):
        _moba_query_block(step * MOBA_QBLOCKS + sub, slice(sub * blk, (sub + 1) * blk),
                          q_ref, t_ref, o_ref, kb_scr, vt_scr, km_scr, sel_scr, far_scr)


def _moba_query_block(i, rows, q_ref, t_ref, o_ref, kb_scr, vt_scr, km_scr, sel_scr, far_scr):
    nkb, blk, dh = kb_scr.shape
    q = q_ref[0, rows, :].astype(F32)
    gate = lax.dot_general(km_scr[...], q, NT, precision=HI, preferred_element_type=F32)
    bidx = lax.broadcasted_iota(jnp.int32, gate.shape, 0)
    past = bidx < i
    g = jnp.where(past, gate, -jnp.inf)
    sel = None
    for _ in range(MOBA_TOPK):
        m = jnp.max(g, axis=0, keepdims=True)
        first = jnp.min(jnp.where(g == m, bidx, nkb), axis=0, keepdims=True)
        hit = bidx == first
        sel = hit if sel is None else jnp.logical_or(sel, hit)
        g = jnp.where(hit, -jnp.inf, g)
    mask = jnp.where(jnp.logical_or(jnp.logical_and(sel, past), bidx == i), 0.0, NEG)
    sel_scr[...] = mask
    far_bias = t_ref[0, RPE_TILES - 1, 0:1, :]
    far_scr[...] = mask + jnp.where(i - bidx >= RPE_TILES - 1, far_bias, 0.0)

    qs = (q * (dh ** -0.5 * LOG2E)).astype(BF16)

    m0 = jnp.full((1, blk), NEG, F32)
    acc0 = jnp.zeros((dh + MOBA_DEN_ROWS, blk), F32)
    def body(far, g, carry):
        m, acc = carry
        nsub = MOBA_GROUP // MOBA_SUB
        blocks = [[jnp.minimum(g * MOBA_GROUP + k * MOBA_SUB + u, nkb - 1) for u in range(MOBA_SUB)]
                  for k in range(nsub)]
        scores, probs, alphas = {}, {}, {}

        def emit_scores(k):
            out = []
            for n in blocks[k]:
                s = lax.dot_general(kb_scr[n], qs, NT, preferred_element_type=F32)
                if far:
                    out.append(s + far_scr[pl.ds(n, 1), :])
                else:
                    out.append(s + t_ref[0, jnp.clip(i - n, 0, RPE_TILES - 1)] + sel_scr[pl.ds(n, 1), :])
            scores[k] = out

        def emit_softmax(k, m):
            m_new = m
            for s in scores[k]:
                m_new = jnp.maximum(m_new, jnp.max(s, axis=0, keepdims=True))
            alphas[k] = jnp.exp2(m - m_new)
            probs[k] = [jnp.exp2(s - m_new).astype(BF16) for s in scores[k]]
            return m_new

        def emit_values(k, acc):
            acc = alphas[k] * acc
            for n, p in zip(blocks[k], probs[k]):
                acc = acc + jnp.dot(vt_scr[n], p, preferred_element_type=F32)
            return acc

        emit_scores(0)
        for k in range(nsub):
            if k + 1 < nsub:
                emit_scores(k + 1)
            if k >= 1:
                acc = emit_values(k - 1, acc)
            m = emit_softmax(k, m)
        acc = emit_values(nsub - 1, acc)
        return m, acc

    n_far = jnp.maximum(i - (RPE_TILES - 2), 0) // MOBA_GROUP
    n_all = i // MOBA_GROUP + 1
    carry = lax.fori_loop(0, n_far, functools.partial(body, True), (m0, acc0))
    _, acc = lax.fori_loop(n_far, n_all, functools.partial(body, False), carry)
    o_ref[0, rows, :] = (acc[:dh] / acc[dh:dh + 1]).T.astype(o_ref.dtype)


def _moba(proj, tiles, q_col, k_col, v_col):
    bsz, s, _ = proj.shape
    dh, blk, heads = MOBA_HEAD_DIM, MOBA_BLOCK, MOBA_HEADS
    nkb = s // blk
    qrows = blk * MOBA_QBLOCKS
    return pl.pallas_call(
        _moba_kernel,
        grid=(bsz, heads, nkb // MOBA_QBLOCKS),
        in_specs=[pl.BlockSpec((1, qrows, dh), lambda b, h, i: (b, i, q_col + h)),
                  pl.BlockSpec((1, s, dh), lambda b, h, i: (b, 0, k_col + h)),
                  pl.BlockSpec((1, s, dh), lambda b, h, i: (b, 0, v_col + h)),
                  pl.BlockSpec((1, RPE_TILES, blk, blk), lambda b, h, i: (h, 0, 0, 0))],
        out_specs=pl.BlockSpec((1, qrows, dh), lambda b, h, i: (b, i, h)),
        out_shape=jax.ShapeDtypeStruct((bsz, s, heads * dh), BF16),
        scratch_shapes=[pltpu.VMEM((nkb, blk, dh), BF16),
                        pltpu.VMEM((nkb, dh + MOBA_DEN_ROWS, blk), BF16),
                        pltpu.VMEM((nkb, dh), F32),
                        pltpu.VMEM((nkb, blk), F32),
                        pltpu.VMEM((nkb, blk), F32)],
        compiler_params=_cparams(("arbitrary", "arbitrary", "arbitrary")),
        name="moba_attention",
    )(proj, proj, proj, tiles)


def _gla_kernel(q_ref, k_ref, v_ref, gg_ref, glr_ref, w2_ref, gb_ref, nw_ref, o_ref, st_scr):
    @pl.when(pl.program_id(1) == 0)
    def _():
        st_scr[...] = jnp.zeros(st_scr.shape, F32)

    tg = q_ref.shape[1]
    c = GLA_CHUNK
    x = jnp.dot(glr_ref[0].astype(F32), w2_ref[...], precision=HI, preferred_element_type=F32) + gb_ref[...]
    lg = -_softplus(-x) * (1.0 / GLA_GATE_NORM)
    row = lax.broadcasted_iota(jnp.int32, (c, c), 0)
    col = lax.broadcasted_iota(jnp.int32, (c, c), 1)
    tri = (row >= col).astype(F32)
    hc = GLA_HEADS * c
    rr = lax.broadcasted_iota(jnp.int32, (hc, hc), 0)
    cc = lax.broadcasted_iota(jnp.int32, (hc, hc), 1)
    incl = jnp.logical_and(rr // c == cc // c, rr >= cc)
    nw = nw_ref[...]
    heads = range(GLA_HEADS)

    chunks = []
    for ci in range(tg // c):
        rows = slice(ci * c, (ci + 1) * c)
        b = jnp.dot(tri, lg[rows], precision=HI, preferred_element_type=F32)
        bl = b[c - 1:c, :]
        q = q_ref[0, rows, :].astype(F32) * GLA_DK ** -0.5
        k = k_ref[0, rows, :].astype(F32)
        q_e = (q * jnp.exp(b)).astype(BF16)
        k_e = (k * jnp.exp(-b)).astype(BF16)
        chunks.append(dict(rows=rows, q_e=q_e, k_e=k_e, k_end=(k * jnp.exp(bl - b)).astype(BF16), d=jnp.exp(bl)))
    for p in chunks:
        qs = jnp.concatenate([p["q_e"][:, h * GLA_DK:(h + 1) * GLA_DK] for h in heads], axis=0)
        ks = jnp.concatenate([p["k_e"][:, h * GLA_DK:(h + 1) * GLA_DK] for h in heads], axis=0)
        p["v"] = jnp.concatenate([v_ref[0, p["rows"], h * GLA_DV:(h + 1) * GLA_DV] for h in heads],
                                 axis=0).astype(BF16)
        a = lax.dot_general(qs, ks, NT, preferred_element_type=F32)
        p["a"] = jnp.where(incl, a, 0.0).astype(BF16)
    for p in chunks:
        p["o"] = jnp.dot(p["a"], p["v"], preferred_element_type=F32)

    for p in chunks:
        for h in heads:
            ks = slice(h * GLA_DK, (h + 1) * GLA_DK)
            vs = slice(h * GLA_DV, (h + 1) * GLA_DV)
            hr = slice(h * c, (h + 1) * c)
            st = st_scr[h]
            o = p["o"][hr] + lax.dot_general(p["q_e"][:, ks], st.astype(BF16), NT, preferred_element_type=F32)
            st_scr[h] = (st * p["d"][:, ks]
                         + lax.dot_general(p["v"][hr], p["k_end"][:, ks], TN, preferred_element_type=F32))
            o = o * lax.rsqrt(jnp.mean(o * o, axis=-1, keepdims=True) + NORM_EPS)
            o_ref[0, p["rows"], vs] = (o * nw * _silu(gg_ref[0, p["rows"], vs].astype(F32))).astype(o_ref.dtype)


def _gla(proj, gk_w2p, gk_b, o_norm, q_col, k_col, v_col, g_col, r_col, tg=512):
    bsz, s, _ = proj.shape
    tg = min(tg, s)
    qk, vw = GLA_QK_W, GLA_V_W
    return pl.pallas_call(
        _gla_kernel,
        grid=(bsz, s // tg),
        in_specs=[pl.BlockSpec((1, tg, qk), lambda b, t: (b, t, q_col)),
                  pl.BlockSpec((1, tg, qk), lambda b, t: (b, t, k_col)),
                  pl.BlockSpec((1, tg, vw), lambda b, t: (b, t, v_col)),
                  pl.BlockSpec((1, tg, vw), lambda b, t: (b, t, g_col)),
                  pl.BlockSpec((1, tg, LANES), lambda b, t: (b, t, r_col)),
                  pl.BlockSpec((LANES, qk), lambda b, t: (0, 0)),
                  pl.BlockSpec((1, qk), lambda b, t: (0, 0)),
                  pl.BlockSpec((1, GLA_DV), lambda b, t: (0, 0))],
        out_specs=pl.BlockSpec((1, tg, vw), lambda b, t: (b, t, 0)),
        out_shape=jax.ShapeDtypeStruct((bsz, s, vw), BF16),
        scratch_shapes=[pltpu.VMEM((GLA_HEADS, GLA_DV, GLA_DK), F32)],
        compiler_params=_cparams(("arbitrary", "arbitrary")),
        name="gla_mixer",
    )(proj, proj, proj, proj, proj, gk_w2p, gk_b, o_norm)


def _gdn_kernel(qkv_ref, gate_ref, ba_ref, cw_ref, pv_ref, nw_ref, o_ref, tail_scr, s_scr):
    @pl.when(pl.program_id(1) == 0)
    def _():
        tail_scr[...] = jnp.zeros(tail_scr.shape, F32)
        s_scr[...] = jnp.zeros(s_scr.shape, F32)

    tg = qkv_ref.shape[1]
    c, dk, grp = GDN_CHUNK, GDN_DK, GDN_GROUP
    gr = grp * c
    w = GDN_W

    x = qkv_ref[0].astype(F32)
    tail = tail_scr[...]
    tail_scr[...] = x[tg - 8:, :]
    r8 = lax.broadcasted_iota(jnp.int32, (8, 1), 0)
    y = x * cw_ref[GDN_CONV - 1:GDN_CONV, :]
    for sft in range(1, GDN_CONV):
        xs = pltpu.roll(x, sft, axis=0)
        head = jnp.where(r8 < sft, pltpu.roll(tail, sft, axis=0), xs[:8, :])
        xs = jnp.concatenate([head, xs[8:, :]], axis=0)
        y = y + xs * cw_ref[GDN_CONV - 1 - sft:GDN_CONV - sft, :]
    y = _silu(y)

    ba = ba_ref[0].astype(F32)
    beta_t = _sigmoid(ba)
    g_t = -jnp.exp(pv_ref[0:1, :]) * _softplus(ba + pv_ref[1:2, :])

    row = lax.broadcasted_iota(jnp.int32, (c, c), 0)
    col = lax.broadcasted_iota(jnp.int32, (c, c), 1)
    tri = (row >= col).astype(F32)
    rr = lax.broadcasted_iota(jnp.int32, (gr, gr), 0)
    cc = lax.broadcasted_iota(jnp.int32, (gr, gr), 1)
    same = (rr // c) == (cc // c)
    incl = jnp.logical_and(same, rr >= cc)
    strict = jnp.logical_and(same, rr > cc)
    eye = (rr == cc).astype(F32)
    halves = [(rr // sz) == (cc // sz) for sz in (2 ** e for e in range(1, int(math.log2(c)) + 1))]
    nw = nw_ref[...]

    probs = []
    for ci in range(tg // c):
        rows = slice(ci * c, (ci + 1) * c)
        gcum = jnp.dot(tri, g_t[rows], precision=HI, preferred_element_type=F32)
        gcum_t = gcum.T
        for gi in range(GDN_HEADS // grp):
            hs = [gi * grp + u for u in range(grp)]

            def stack(a, off):
                return jnp.concatenate([a[rows, off + h * dk: off + (h + 1) * dk] for h in hs], axis=0)

            q = stack(y, 0)
            k = stack(y, w)
            v = stack(y, 2 * w)
            q = q * lax.rsqrt(jnp.sum(q * q, axis=-1, keepdims=True) + NORM_EPS) * dk ** -0.5
            k = k * lax.rsqrt(jnp.sum(k * k, axis=-1, keepdims=True) + NORM_EPS)
            beta = jnp.concatenate([beta_t[rows, h:h + 1] for h in hs], axis=0)
            gc = jnp.concatenate([gcum[:, GDN_HEADS + h:GDN_HEADS + h + 1] for h in hs], axis=0)
            gc_row = jnp.concatenate([gcum_t[GDN_HEADS + h:GDN_HEADS + h + 1, :] for h in hs], axis=1)
            gl = jnp.concatenate([jnp.broadcast_to(gcum[c - 1:c, GDN_HEADS + h:GDN_HEADS + h + 1], (c, 1))
                                  for h in hs], axis=0)

            decay = jnp.where(incl, jnp.exp(jnp.where(incl, gc - gc_row, 0.0)), 0.0)
            kb = k * beta
            k16 = k.astype(BF16)
            a = lax.dot_general(kb.astype(BF16), k16, NT, preferred_element_type=F32)
            a = jnp.where(strict, a * decay, 0.0)
            eg = jnp.exp(gc)
            probs.append(dict(
                rows=rows, hs=hs, a16=a.astype(BF16),
                t=eye - jnp.where(halves[0], a, 0.0),
                rhs=jnp.concatenate([v * beta, kb * eg], axis=1).astype(BF16),
                attn=(lax.dot_general(q.astype(BF16), k16, NT, preferred_element_type=F32) * decay).astype(BF16),
                q_g=(q * eg).astype(BF16),
                k_end=(k * jnp.exp(gl - gc)).astype(BF16),
                d_last=[jnp.exp(gcum[c - 1:c, GDN_HEADS + h:GDN_HEADS + h + 1]) for h in hs]))

    for lvl in range(1, len(halves)):
        off16 = jnp.where(jnp.logical_and(halves[lvl], jnp.logical_not(halves[lvl - 1])), 1.0, 0.0).astype(BF16)
        t16s = [p["t"].astype(BF16) for p in probs]
        xs = [jnp.dot(p["a16"] * off16, t16, preferred_element_type=F32).astype(BF16)
              for p, t16 in zip(probs, t16s)]
        for p, t16, x in zip(probs, t16s, xs):
            p["t"] = p["t"] - jnp.dot(t16, x, preferred_element_type=F32)
    for p in probs:
        wk = jnp.dot(p["t"].astype(BF16), p["rhs"], preferred_element_type=F32)
        p["w_val"] = wk[:, :GDN_DV]
        p["k_cum"] = wk[:, GDN_DV:].astype(BF16)

    for p in probs:
        rows, hs = p["rows"], p["hs"]
        v_new = []
        for u, h in enumerate(hs):
            hr = slice(u * c, (u + 1) * c)
            st16 = s_scr[h].astype(BF16)
            v_new.append(p["w_val"][hr] - jnp.dot(p["k_cum"][hr], st16, preferred_element_type=F32))
        v16 = jnp.concatenate(v_new, axis=0).astype(BF16)
        o_intra = jnp.dot(p["attn"], v16, preferred_element_type=F32)
        for u, h in enumerate(hs):
            hr = slice(u * c, (u + 1) * c)
            st = s_scr[h]
            o = o_intra[hr] + jnp.dot(p["q_g"][hr], st.astype(BF16), preferred_element_type=F32)
            s_scr[h] = (st * p["d_last"][u]
                        + lax.dot_general(p["k_end"][hr], v16[hr], TN, preferred_element_type=F32))
            o = o * lax.rsqrt(jnp.mean(o * o, axis=-1, keepdims=True) + NORM_EPS)
            cs = slice(h * GDN_DV, (h + 1) * GDN_DV)
            o_ref[0, rows, cs] = (o * nw * _silu(gate_ref[0, rows, cs].astype(F32))).astype(o_ref.dtype)


def _gdn(proj, conv_w, pvec, o_norm, tg=512):
    bsz, s, _ = proj.shape
    tg = min(tg, s)
    w = GDN_W
    return pl.pallas_call(
        _gdn_kernel,
        grid=(bsz, s // tg),
        in_specs=[pl.BlockSpec((1, tg, 3 * w), lambda b, t: (b, t, 0)),
                  pl.BlockSpec((1, tg, w), lambda b, t: (b, t, 3)),
                  pl.BlockSpec((1, tg, LANES), lambda b, t: (b, t, 4 * w // LANES)),
                  pl.BlockSpec((GDN_CONV, 3 * w), lambda b, t: (0, 0)),
                  pl.BlockSpec((2, LANES), lambda b, t: (0, 0)),
                  pl.BlockSpec((1, GDN_DV), lambda b, t: (0, 0))],
        out_specs=pl.BlockSpec((1, tg, w), lambda b, t: (b, t, 0)),
        out_shape=jax.ShapeDtypeStruct((bsz, s, w), BF16),
        scratch_shapes=[pltpu.VMEM((8, 3 * w), F32),
                        pltpu.VMEM((GDN_HEADS, GDN_DK, GDN_DV), F32)],
        compiler_params=_cparams(("arbitrary", "arbitrary")),
        name="gdn_mixer",
    )(proj, proj, proj, conv_w, pvec, o_norm)


def _mixout_kernel(n_act, *refs):
    acts = refs[:n_act]
    ws = refs[n_act:2 * n_act]
    x_ref, g1_ref, lng_ref, lnb_ref, sc_ref, sh_ref, rw_ref = refs[2 * n_act:2 * n_act + 7]
    xo_ref, h_ref, rl_ref = refs[2 * n_act + 7:]
    y = None
    for a_ref, w_ref in zip(acts, ws):
        t = jnp.dot(a_ref[0].astype(BF16), w_ref[...], preferred_element_type=F32)
        y = t if y is None else y + t
    xn = _layer_norm(DEEPNORM_ALPHA * x_ref[0] + g1_ref[0] * y, lng_ref[...], lnb_ref[...])
    xo_ref[0] = xn
    h = xn * (1.0 + sc_ref[0]) + sh_ref[0]
    h_hi = h.astype(BF16)
    h_ref[0] = h_hi
    h_lo = (h - h_hi.astype(F32)).astype(BF16)
    rw = rw_ref[...]
    rw_hi = rw.astype(BF16)
    rw_lo = (rw - rw_hi.astype(F32)).astype(BF16)
    rl_ref[0] = (lax.dot_general(rw_hi, h_hi, NT, preferred_element_type=F32)
                 + lax.dot_general(rw_hi, h_lo, NT, preferred_element_type=F32)
                 + lax.dot_general(rw_lo, h_hi, NT, preferred_element_type=F32))


def _mixout(acts, ws, x, g1, ln_g, ln_b, sc2, sh2, router_wt, tm=1024):
    bsz, s, d = x.shape
    tm = min(tm, s)
    n_act = len(acts)
    ne = router_wt.shape[0]
    vec = pl.BlockSpec((1, 1, d), lambda b, i: (b, 0, 0))
    par = pl.BlockSpec((1, d), lambda b, i: (0, 0))
    in_specs = ([pl.BlockSpec((1, tm, a.shape[-1]), lambda b, i: (b, i, 0)) for a in acts]
                + [pl.BlockSpec(w.shape, lambda b, i: (0, 0)) for w in ws]
                + [pl.BlockSpec((1, tm, d), lambda b, i: (b, i, 0)), vec, par, par, vec, vec,
                   pl.BlockSpec((ne, d), lambda b, i: (0, 0))])
    return pl.pallas_call(
        functools.partial(_mixout_kernel, n_act),
        grid=(bsz, s // tm),
        in_specs=in_specs,
        out_specs=[pl.BlockSpec((1, tm, d), lambda b, i: (b, i, 0)),
                   pl.BlockSpec((1, tm, d), lambda b, i: (b, i, 0)),
                   pl.BlockSpec((1, ne, tm), lambda b, i: (b, 0, i))],
        out_shape=[jax.ShapeDtypeStruct((bsz, s, d), F32),
                   jax.ShapeDtypeStruct((bsz, s, d), BF16),
                   jax.ShapeDtypeStruct((bsz, ne, s), F32)],
        compiler_params=_cparams(("arbitrary", "arbitrary")),
        name="mix_out",
    )(*acts, *ws, x, g1, ln_g, ln_b, sc2, sh2, router_wt)


def _first_max(vals, idx, axis, sentinel):
    m = jnp.max(vals, axis=axis, keepdims=True)
    first = jnp.min(jnp.where(vals == m, idx, sentinel), axis=axis, keepdims=True)
    return m, idx == first


def _router_kernel(rl_ref, rb_ref, g_ref, gt_ref, cnt_ref):
    ne, tn = rl_ref.shape[1], rl_ref.shape[2]
    gsz = ne // N_GROUPS
    scores = _sigmoid(rl_ref[0])
    sel = scores + rb_ref[...]
    ridx = lax.broadcasted_iota(jnp.int32, (gsz, tn), 0)
    gidx = lax.broadcasted_iota(jnp.int32, (N_GROUPS, tn), 0)
    gs = jnp.zeros((N_GROUPS, tn), F32)
    for g in range(N_GROUPS):
        sg = sel[g * gsz:(g + 1) * gsz, :]
        m1, hit = _first_max(sg, ridx, 0, gsz)
        m2 = jnp.max(jnp.where(hit, -jnp.inf, sg), axis=0, keepdims=True)
        gs = jnp.where(gidx == g, m1 + m2, gs)
    gsel = None
    for _ in range(TOPK_GROUPS):
        _, hit = _first_max(gs, gidx, 0, N_GROUPS)
        gsel = hit if gsel is None else jnp.logical_or(gsel, hit)
        gs = jnp.where(hit, -jnp.inf, gs)
    gself = gsel.astype(F32)
    emask = jnp.concatenate([jnp.broadcast_to(gself[g:g + 1, :], (gsz, tn)) for g in range(N_GROUPS)], axis=0)
    cand = jnp.where(emask > 0.5, sel, -jnp.inf)
    eidx = lax.broadcasted_iota(jnp.int32, cand.shape, 0)
    chosen = None
    for _ in range(TOP_K):
        _, hit = _first_max(cand, eidx, 0, ne)
        chosen = hit if chosen is None else jnp.logical_or(chosen, hit)
        cand = jnp.where(hit, -jnp.inf, cand)
    wsel = jnp.where(chosen, scores, 0.0)
    tot = jnp.sum(wsel, axis=0, keepdims=True)
    gates = wsel / (tot + 1e-20) * ROUTED_SCALE
    g_ref[0] = gates.T
    gt_ref[0] = gates
    tile_of = lax.broadcasted_iota(jnp.int32, (tn, LANES), 0) // MOE_TM
    ind = (tile_of == lax.broadcasted_iota(jnp.int32, (tn, LANES), 1)).astype(BF16)
    routed = jnp.where(gates > 0.0, 1.0, 0.0).astype(BF16)
    cnt_ref[0, 0] = jnp.dot(routed, ind, preferred_element_type=F32)


def _router(rl, router_b, tn=1024):
    bsz, ne, s = rl.shape
    tn = min(tn, s)
    return pl.pallas_call(
        _router_kernel,
        grid=(bsz, s // tn),
        in_specs=[pl.BlockSpec((1, ne, tn), lambda b, i: (b, 0, i)),
                  pl.BlockSpec((ne, 1), lambda b, i: (0, 0))],
        out_specs=[pl.BlockSpec((1, tn, ne), lambda b, i: (b, i, 0)),
                   pl.BlockSpec((1, ne, tn), lambda b, i: (b, 0, i)),
                   pl.BlockSpec((1, 1, ne, LANES), lambda b, i: (b, i, 0, 0))],
        out_shape=[jax.ShapeDtypeStruct((bsz, s, ne), F32),
                   jax.ShapeDtypeStruct((bsz, ne, s), F32),
                   jax.ShapeDtypeStruct((bsz, s // tn, ne, LANES), F32)],
        compiler_params=_cparams(("arbitrary", "arbitrary")),
        name="moe_router",
    )(rl, router_b.reshape(ne, 1))


MOE_TM = 256
MOE_ALIGN = 16
MOE_R = 2048
MOE_LC = 512
MOE_LMAX = -(-(MOE_TM * TOP_K + N_EXPERTS * (MOE_ALIGN - 1)) // MOE_LC) * MOE_LC
MOE_NP = MOE_LMAX // MOE_ALIGN
MOE_LP = MOE_LC // MOE_ALIGN
POS_SPLIT = 256.0


def _ffn(x, wg, wu, wd):
    a = jnp.dot(x, wg, preferred_element_type=F32)
    u = jnp.dot(x, wu, preferred_element_type=F32)
    return jnp.dot((_silu(a) * u).astype(BF16), wd, preferred_element_type=F32)


def _split_pos(pos, axis):
    hi = jnp.floor(pos * (1.0 / POS_SPLIT)) * POS_SPLIT
    return jnp.concatenate([hi, pos - hi], axis=axis).astype(BF16)


def _for_each_piece(n_pieces, fn):
    unroll = 8

    def body(q, carry):
        for u in range(unroll):
            fn(q * unroll + u)
        return carry

    full = n_pieces // unroll
    lax.fori_loop(0, full, body, 0)

    def tail(p, carry):
        fn(p)
        return carry

    lax.fori_loop(full * unroll, n_pieces, tail, 0)


def _dispatch_kernel(dst_s, np_s, h_ref, gt_ref, offc_ref, offr_ref, cntr_ref, xs_hbm, sorted_scr, sems):
    i = pl.program_id(0)
    last = pl.num_programs(0) - 1
    slot = i % 2
    tm = h_ref.shape[0]
    routed = gt_ref[0] > 0.0
    t0 = lax.broadcasted_iota(jnp.int32, (tm, tm), 0)
    t1 = lax.broadcasted_iota(jnp.int32, (tm, tm), 1)
    earlier = jnp.where(t0 < t1, 1.0, 0.0).astype(BF16)
    rank_t = jnp.dot(jnp.where(routed, 1.0, 0.0).astype(BF16), earlier, preferred_element_type=F32)
    pos_t = jnp.where(routed, rank_t + offc_ref[0] + 1.0, 0.0)
    pos2 = _split_pos(pos_t, 0)
    x = h_ref[...]
    offr = offr_ref[0]
    endr = offr + cntr_ref[0]
    for c in range(MOE_LMAX // MOE_LC):
        r = (c * MOE_LC + lax.broadcasted_iota(jnp.int32, (MOE_LC, 1), 0)).astype(F32)
        owner = jnp.where(jnp.logical_and(r >= offr, r < endr), 1.0, 0.0).astype(BF16)
        possel = jnp.dot(jnp.concatenate([owner, owner], axis=1), pos2, preferred_element_type=F32)
        perm = jnp.where(possel == r + 1.0, 1.0, 0.0).astype(BF16)
        srt = jnp.dot(perm, x, preferred_element_type=F32).astype(BF16)
        sorted_scr[slot, c * MOE_LP:(c + 1) * MOE_LP] = srt.reshape(MOE_LP, MOE_ALIGN, srt.shape[-1])

    def piece(sl, tile, p):
        return pltpu.make_async_copy(sorted_scr.at[sl, p], xs_hbm.at[dst_s[tile * MOE_NP + p]], sems.at[sl])

    _for_each_piece(np_s[i], lambda p: piece(slot, i, p).start())

    @pl.when(i > 0)
    def _():
        _for_each_piece(np_s[i - 1], lambda p: piece(1 - slot, i - 1, p).wait())

    @pl.when(i == last)
    def _():
        _for_each_piece(np_s[i], lambda p: piece(slot, i, p).wait())


def _expert_ffn_kernel(be_s, nu_s, x_ref, wg_ref, wu_ref, wd_ref, y_ref):
    @pl.when(pl.program_id(0) < nu_s[0])
    def _():
        y_ref[...] = _ffn(x_ref[...], wg_ref[0, 0].astype(BF16), wu_ref[0, 0].astype(BF16),
                          wd_ref[0, 0].astype(BF16)).astype(BF16)


def _combine_kernel(dst_s, np_s, ys_hbm, g_ref, h_ref, offr_ref, offc_ref, cntc_ref,
                    sg_ref, su_ref, sd_ref, x_ref, g2_ref, lng_ref, lnb_ref, o_ref, ys_scr, sems):
    i = pl.program_id(0)
    last = pl.num_programs(0) - 1
    slot = i % 2
    tm = h_ref.shape[0]

    def piece(sl, tile, p):
        return pltpu.make_async_copy(ys_hbm.at[dst_s[tile * MOE_NP + p]], ys_scr.at[sl, p], sems.at[sl])

    @pl.when(i == 0)
    def _():
        ys_scr[...] = jnp.zeros(ys_scr.shape, BF16)
        _for_each_piece(np_s[0], lambda p: piece(0, 0, p).start())

    @pl.when(i < last)
    def _():
        _for_each_piece(np_s[i + 1], lambda p: piece(1 - slot, i + 1, p).start())

    acc = _ffn(h_ref[...], sg_ref[...], su_ref[...], sd_ref[...])
    g = g_ref[...]
    routed = g > 0.0
    t0 = lax.broadcasted_iota(jnp.int32, (tm, tm), 0)
    t1 = lax.broadcasted_iota(jnp.int32, (tm, tm), 1)
    earlier = jnp.where(t0 > t1, 1.0, 0.0).astype(BF16)
    rank = jnp.dot(earlier, jnp.where(routed, 1.0, 0.0).astype(BF16), preferred_element_type=F32)
    pos = jnp.where(routed, rank + offr_ref[0] + 1.0, 0.0)
    pos2 = _split_pos(pos, 1)
    g16 = g.astype(BF16)
    offc = offc_ref[0]
    endc = offc + cntc_ref[0]
    _for_each_piece(np_s[i], lambda p: piece(slot, i, p).wait())
    for c in range(MOE_LMAX // MOE_LC):
        r = (c * MOE_LC + lax.broadcasted_iota(jnp.int32, (1, MOE_LC), 1)).astype(F32)
        owner = jnp.where(jnp.logical_and(r >= offc, r < endc), 1.0, 0.0).astype(BF16)
        possel = jnp.dot(pos2, jnp.concatenate([owner, owner], axis=0), preferred_element_type=F32)
        gsel = jnp.dot(g16, owner, preferred_element_type=F32)
        w = jnp.where(possel == r + 1.0, gsel, 0.0).astype(BF16)
        ys = ys_scr[slot, c * MOE_LP:(c + 1) * MOE_LP].reshape(MOE_LC, ys_scr.shape[-1])
        acc = acc + jnp.dot(w, ys, preferred_element_type=F32)
    z = DEEPNORM_ALPHA * x_ref[...] + g2_ref[0] * acc
    o_ref[...] = _layer_norm(z, lng_ref[...], lnb_ref[...])


def _ceil_to(v, m):
    return jnp.floor((v + (m - 1.0)) * (1.0 / m)) * m


def _moe_layout_kernel(cnt_ref, dst_ref, np_ref, off_ref, cntp_ref, be_ref, nu_ref):
    cnt = cnt_ref[...]
    ntiles, ne = cnt.shape
    cntp = _ceil_to(cnt, MOE_ALIGN)
    e0 = lax.broadcasted_iota(jnp.int32, (ne, ne), 0)
    e1 = lax.broadcasted_iota(jnp.int32, (ne, ne), 1)
    i0 = lax.broadcasted_iota(jnp.int32, (ntiles, ntiles), 0)
    i1 = lax.broadcasted_iota(jnp.int32, (ntiles, ntiles), 1)

    def mm(a, b):
        return jnp.dot(a, b, precision=HI, preferred_element_type=F32)

    off = mm(cntp, jnp.where(e0 < e1, 1.0, 0.0))
    before = mm(jnp.where(i0 > i1, 1.0, 0.0), cntp)
    tot = jnp.broadcast_to(jnp.sum(cntp, axis=0, keepdims=True), (8, ne))
    totr = _ceil_to(tot, MOE_R)
    base = mm(totr, jnp.where(e0 < e1, 1.0, 0.0))[0:1, :]
    delta = base + before - off
    step = mm(delta, jnp.where(e0 == e1, 1.0, 0.0) - jnp.where(e0 + 1 == e1, 1.0, 0.0))
    rowp = (lax.broadcasted_iota(jnp.int32, (1, MOE_NP), 1) * MOE_ALIGN).astype(F32)
    dst = jnp.broadcast_to(rowp, (ntiles, MOE_NP))
    for e in range(ne):
        dst = dst + jnp.where(off[:, e:e + 1] <= rowp, step[:, e:e + 1], 0.0)
    dst_ref[...] = (dst * (1.0 / MOE_ALIGN)).astype(jnp.int32)
    pieces = jnp.sum(cntp, axis=1, keepdims=True) * (1.0 / MOE_ALIGN)
    np_ref[...] = jnp.broadcast_to(pieces, np_ref.shape).astype(jnp.int32)
    off_ref[...] = off
    cntp_ref[...] = cntp
    tot_c = lax.dot_general(cntp, jnp.ones((ntiles, LANES), F32), TN, precision=HI, preferred_element_type=F32)
    end_c = mm(jnp.where(e0 >= e1, 1.0, 0.0), _ceil_to(tot_c, MOE_R))[:, 0:1]
    total = end_c[ne - 1:ne, :]
    first = jnp.minimum((lax.broadcasted_iota(jnp.int32, be_ref.shape, 1) * MOE_R).astype(F32), total - 1.0)
    be = jnp.sum(jnp.where(end_c <= first, 1.0, 0.0), axis=0, keepdims=True)
    be_ref[...] = jnp.minimum(be, ne - 1.0).astype(jnp.int32)
    nu_ref[...] = jnp.broadcast_to(total * (1.0 / MOE_R), nu_ref.shape).astype(jnp.int32)


def _moe_layout(cnt_raw, tn):
    ne = N_EXPERTS
    nsub = tn // MOE_TM
    cnt = jnp.transpose(cnt_raw[..., :nsub], (0, 1, 3, 2)).reshape(-1, ne)
    ntiles = cnt.shape[0]
    nblk = -(-(ntiles * (MOE_TM * TOP_K + ne * (MOE_ALIGN - 1)) + ne * (MOE_R - 1)) // MOE_R)
    nblk_pad = -(-nblk // LANES) * LANES
    dst, npieces, off, cntp, blk_exp, nused = pl.pallas_call(
        _moe_layout_kernel,
        out_shape=[jax.ShapeDtypeStruct((ntiles, MOE_NP), jnp.int32),
                   jax.ShapeDtypeStruct((ntiles, LANES), jnp.int32),
                   jax.ShapeDtypeStruct((ntiles, ne), F32),
                   jax.ShapeDtypeStruct((ntiles, ne), F32),
                   jax.ShapeDtypeStruct((1, nblk_pad), jnp.int32),
                   jax.ShapeDtypeStruct((1, LANES), jnp.int32)],
        compiler_params=pltpu.CompilerParams(vmem_limit_bytes=VMEM_LIMIT),
        name="moe_layout",
    )(cnt)
    return dst.reshape(-1), npieces[:, 0], off, cntp, nblk, nused[0, :1], blk_exp[0, :nblk]


def _moe(h2, gates, gates_t, cnt_raw, tn, layer, wg, wu, wd, sg, su, sd, x, g2, ln_g, ln_b):
    bsz, s, d = x.shape
    t = bsz * s
    tm = MOE_TM
    _, ne, _, de = wg.shape
    per_b = s // tm
    ntiles = t // tm
    dst, npieces, off_f, cnt_f, nblk, nused, blk_exp = _moe_layout(cnt_raw, tn)
    scalars = (dst, npieces)
    row = pl.BlockSpec((1, 1, ne), lambda i, *_: (i, 0, 0))
    col = pl.BlockSpec((1, ne, 1), lambda i, *_: (i, 0, 0))
    tok = pl.BlockSpec((tm, d), lambda i, *_: (i, 0))
    par = pl.BlockSpec((1, d), lambda i, *_: (0, 0))
    rows = nblk * MOE_R

    xs = pl.pallas_call(
        _dispatch_kernel,
        grid_spec=pltpu.PrefetchScalarGridSpec(
            num_scalar_prefetch=2, grid=(ntiles,),
            in_specs=[tok,
                      pl.BlockSpec((1, ne, tm), lambda i, *_: (i // per_b, 0, i % per_b)),
                      col, row, row],
            out_specs=pl.BlockSpec(memory_space=pl.ANY),
            scratch_shapes=[pltpu.VMEM((2, MOE_NP, MOE_ALIGN, d), BF16), pltpu.SemaphoreType.DMA((2,))]),
        out_shape=jax.ShapeDtypeStruct((rows // MOE_ALIGN, MOE_ALIGN, d), BF16),
        compiler_params=_cparams(("arbitrary",)),
        name="moe_dispatch",
    )(*scalars, h2.reshape(t, d), gates_t, off_f.reshape(ntiles, ne, 1), off_f.reshape(ntiles, 1, ne),
      cnt_f.reshape(ntiles, 1, ne))

    def blk_index(j, be, nu):
        return jnp.maximum(jnp.minimum(j, nu[0] - 1), 0), 0

    blk = pl.BlockSpec((MOE_R, d), blk_index)
    ys = pl.pallas_call(
        _expert_ffn_kernel,
        grid_spec=pltpu.PrefetchScalarGridSpec(
            num_scalar_prefetch=2, grid=(nblk,),
            in_specs=[blk,
                      pl.BlockSpec((1, 1, d, de), lambda j, be, nu: (layer, be[j], 0, 0)),
                      pl.BlockSpec((1, 1, d, de), lambda j, be, nu: (layer, be[j], 0, 0)),
                      pl.BlockSpec((1, 1, de, d), lambda j, be, nu: (layer, be[j], 0, 0))],
            out_specs=blk),
        out_shape=jax.ShapeDtypeStruct((rows, d), BF16),
        compiler_params=_cparams(("arbitrary",)),
        name="moe_expert_ffn",
    )(blk_exp, nused, xs.reshape(rows, d), wg, wu, wd)

    out = pl.pallas_call(
        _combine_kernel,
        grid_spec=pltpu.PrefetchScalarGridSpec(
            num_scalar_prefetch=2, grid=(ntiles,),
            in_specs=[pl.BlockSpec(memory_space=pl.ANY),
                      pl.BlockSpec((tm, ne), lambda i, *_: (i, 0)),
                      tok, row, col, col,
                      pl.BlockSpec(sg.shape, lambda i, *_: (0, 0)),
                      pl.BlockSpec(su.shape, lambda i, *_: (0, 0)),
                      pl.BlockSpec(sd.shape, lambda i, *_: (0, 0)),
                      tok,
                      pl.BlockSpec((1, 1, d), lambda i, *_: (i // per_b, 0, 0)),
                      par, par],
            out_specs=tok,
            scratch_shapes=[pltpu.VMEM((2, MOE_NP, MOE_ALIGN, d), BF16), pltpu.SemaphoreType.DMA((2,))]),
        out_shape=jax.ShapeDtypeStruct((t, d), F32),
        compiler_params=_cparams(("arbitrary",)),
        name="moe_combine",
    )(*scalars, ys.reshape(rows // MOE_ALIGN, MOE_ALIGN, d), gates.reshape(t, ne), h2.reshape(t, d),
      off_f.reshape(ntiles, 1, ne),
      off_f.reshape(ntiles, ne, 1), cnt_f.reshape(ntiles, ne, 1), sg, su, sd, x.reshape(t, d), g2, ln_g, ln_b)
    return out.reshape(bsz, s, d)


def _pad_cols(w, n):
    return jnp.pad(w, ((0, 0), (0, n - w.shape[1])))


def kernel(x, c, rpe_bias, ada_w, ada_b, ln_mix_g, ln_mix_b, ln_ffn_g, ln_ffn_b, ev_w_in, ev_gk_w2, ev_gk_b, ev_norm, ev_w_out, od_w_in, od_conv_w, od_a_log, od_dt_bias, od_norm, od_w_out, moe_router_w, moe_router_b, moe_w_gate, moe_w_up, moe_w_down, sh_w_gate, sh_w_up, sh_w_down):
    bsz, s, d = x.shape
    mod = _ada(c, ada_w, ada_b)
    tiles = _rpe_tiles(rpe_bias)

    for layer in range(DEPTH):
        sh1, sc1, g1, sh2, sc2, g2 = [mod[layer, :, u * d:(u + 1) * d].reshape(bsz, 1, d) for u in range(6)]
        i = layer // 2
        if layer % 2 == 0:
            n_main = 3 * MOBA_W + 2 * GLA_QK_W + 2 * GLA_V_W
            w_in = jnp.concatenate([ev_w_in[i][:, :n_main], _pad_cols(ev_w_in[i][:, n_main:], LANES)], axis=1)
            proj = _mod_matmul(x, sc1, sh1, w_in.astype(BF16))
            nb = MOBA_W // LANES
            o_a = _moba(proj, tiles, 0, nb, 2 * nb)
            gk_w2p = jnp.pad(ev_gk_w2[i], ((0, LANES - GLA_GATE_RANK), (0, 0)))
            gla0 = 3 * MOBA_W
            o_b = _gla(proj, gk_w2p, ev_gk_b[i].reshape(1, -1), ev_norm[i].reshape(1, -1),
                       gla0 // GLA_QK_W, gla0 // GLA_QK_W + 1,
                       (gla0 + 2 * GLA_QK_W) // GLA_V_W, (gla0 + 2 * GLA_QK_W) // GLA_V_W + 1,
                       n_main // LANES)
            w_out = ev_w_out[i].astype(BF16)
            acts, ws = [o_a, o_b], [w_out[:MOBA_W], w_out[MOBA_W:]]
        else:
            n_main = 4 * GDN_W
            w_in = jnp.concatenate([od_w_in[i][:, :n_main], _pad_cols(od_w_in[i][:, n_main:], LANES)], axis=1)
            proj = _mod_matmul(x, sc1, sh1, w_in.astype(BF16))
            pvec = jnp.zeros((2, LANES), F32)
            pvec = pvec.at[0, GDN_HEADS:2 * GDN_HEADS].set(od_a_log[i])
            pvec = pvec.at[1, GDN_HEADS:2 * GDN_HEADS].set(od_dt_bias[i])
            o = _gdn(proj, od_conv_w[i], pvec, od_norm[i].reshape(1, -1))
            acts, ws = [o], [od_w_out[i].astype(BF16)]

        x, h2, rl = _mixout(acts, ws, x, g1, ln_mix_g[layer].reshape(1, d), ln_mix_b[layer].reshape(1, d),
                            sc2, sh2, moe_router_w[layer].T)
        router_tn = min(1024, s)
        gates, gates_t, cnt_raw = _router(rl, moe_router_b[layer], router_tn)
        x = _moe(h2, gates, gates_t, cnt_raw, router_tn, layer, moe_w_gate, moe_w_up, moe_w_down,
                 sh_w_gate[layer].astype(BF16), sh_w_up[layer].astype(BF16),
                 sh_w_down[layer].astype(BF16), x, g2, ln_ffn_g[layer].reshape(1, d), ln_ffn_b[layer].reshape(1, d))
    return x
```

```python
import functools
import math

import numpy as np
import jax
import jax.numpy as jnp
from jax import lax
from jax.experimental import pallas as pl
from jax.experimental.pallas import tpu as pltpu

F32 = jnp.float32
BF16 = jnp.bfloat16
HI = lax.Precision.HIGHEST
NT = (((1,), (1,)), ((), ()))
TN = (((0,), (0,)), ((), ()))
NEG = -1e30
LOG2E = math.log2(math.e)

LANES = 128
VMEM_LIMIT = 56 * 1024 * 1024

DEPTH = 2
MOBA_HEAD_DIM = 128
MOBA_HEADS = 4
MOBA_BLOCK = 256
MOBA_TOPK = 3
MOBA_GROUP = 8
MOBA_SUB = 4
MOBA_QBLOCKS = 4
MOBA_DEN_ROWS = 16
GLA_DV = 128
GLA_HEADS = 4
GLA_DK = 64
GLA_GATE_RANK = 16
GLA_GATE_NORM = 16.0
GLA_CHUNK = 64
GDN_DK = 128
GDN_DV = 128
GDN_HEADS = 8
GDN_CONV = 4
GDN_CHUNK = 64
GDN_GROUP = 4
RPE_BUCKETS = 32
RPE_MAX_DIST = 2048
RPE_TILES = 8
N_EXPERTS = 64
TOP_K = 6
N_GROUPS = 8
TOPK_GROUPS = 4
D_EXPERT = 256
ROUTED_SCALE = 2.5
DEEPNORM_ALPHA = float((2 * DEPTH) ** 0.25)
LN_EPS = 1e-5
NORM_EPS = 1e-6

MOBA_W = MOBA_HEADS * MOBA_HEAD_DIM
GLA_QK_W = GLA_HEADS * GLA_DK
GLA_V_W = GLA_HEADS * GLA_DV
GDN_W = GDN_HEADS * GDN_DK


def _cparams(sem):
    return pltpu.CompilerParams(dimension_semantics=sem, vmem_limit_bytes=VMEM_LIMIT)


def _sigmoid(x):
    return 1.0 / (1.0 + jnp.exp(-x))


def _silu(x):
    return x * _sigmoid(x)


def _softplus(x):
    return jnp.maximum(x, 0.0) + jnp.log(1.0 + jnp.exp(-jnp.abs(x)))


def _layer_norm(z, g, b):
    mu = jnp.mean(z, axis=-1, keepdims=True)
    zc = z - mu
    var = jnp.mean(zc * zc, axis=-1, keepdims=True)
    return zc * lax.rsqrt(var + LN_EPS) * g + b


def _ada_kernel(c_ref, w_ref, b_ref, o_ref):
    ca = _silu(c_ref[...])
    o_ref[0] = jnp.dot(ca, w_ref[0], precision=HI, preferred_element_type=F32) + b_ref[0]


def _ada(c, ada_w, ada_b):
    depth, d, n = ada_w.shape
    bsz = c.shape[0]
    tn = 6 * LANES
    return pl.pallas_call(
        _ada_kernel,
        grid=(depth, n // tn),
        in_specs=[pl.BlockSpec((bsz, d), lambda l, j: (0, 0)),
                  pl.BlockSpec((1, d, tn), lambda l, j: (l, 0, j)),
                  pl.BlockSpec((1, 1, tn), lambda l, j: (l, 0, j))],
        out_specs=pl.BlockSpec((1, bsz, tn), lambda l, j: (l, 0, j)),
        out_shape=jax.ShapeDtypeStruct((depth, bsz, n), F32),
        compiler_params=_cparams(("arbitrary", "arbitrary")),
        name="ada_mod",
    )(c, ada_w, ada_b.reshape(depth, 1, n))


def _modmm_kernel(x_ref, sc_ref, sh_ref, w_ref, o_ref, h_scr):
    @pl.when(pl.program_id(2) == 0)
    def _():
        h_scr[...] = (x_ref[0] * (1.0 + sc_ref[0]) + sh_ref[0]).astype(BF16)

    o_ref[0] = jnp.dot(h_scr[...], w_ref[...], preferred_element_type=F32).astype(o_ref.dtype)


def _col_tile(n, cap):
    best = LANES
    for t in range(LANES, cap + 1, LANES):
        if n % t == 0:
            best = t
    return best


def _mod_matmul(x, sc, sh, w, tm=1024, tn_cap=4608):
    bsz, s, d = x.shape
    n = w.shape[1]
    tm = min(tm, s)
    tn = _col_tile(n, tn_cap)
    return pl.pallas_call(
        _modmm_kernel,
        grid=(bsz, s // tm, n // tn),
        in_specs=[pl.BlockSpec((1, tm, d), lambda b, i, j: (b, i, 0)),
                  pl.BlockSpec((1, 1, d), lambda b, i, j: (b, 0, 0)),
                  pl.BlockSpec((1, 1, d), lambda b, i, j: (b, 0, 0)),
                  pl.BlockSpec((d, tn), lambda b, i, j: (0, j))],
        out_specs=pl.BlockSpec((1, tm, tn), lambda b, i, j: (b, i, j)),
        out_shape=jax.ShapeDtypeStruct((bsz, s, n), BF16),
        scratch_shapes=[pltpu.VMEM((tm, d), BF16)],
        compiler_params=_cparams(("arbitrary", "arbitrary", "arbitrary")),
        name="mod_matmul",
    )(x, sc, sh, w)


def _rpe_lower_bounds():
    exact = RPE_BUCKETS // 2
    d = np.arange(0, 2 * RPE_MAX_DIST, dtype=np.int64)
    logd = np.log(np.maximum(d, 1).astype(np.float64) / exact)
    large = exact + (logd / math.log(RPE_MAX_DIST / exact) * (RPE_BUCKETS - exact)).astype(np.int64)
    large = np.minimum(large, RPE_BUCKETS - 1)
    bucket = np.where(d < exact, d, large)
    return [int(np.argmax(bucket >= k)) for k in range(RPE_BUCKETS)]


def _rpe_tiles_kernel(lo, rpe_ref, o_ref):
    h = pl.program_id(0)
    j = pl.program_id(1)
    blk = o_ref.shape[-1]
    key = lax.broadcasted_iota(jnp.int32, (blk, blk), 0)
    qry = lax.broadcasted_iota(jnp.int32, (blk, blk), 1)
    dist = j * blk + qry - key
    val = jnp.full((blk, blk), rpe_ref[0, h], F32)
    for k in range(1, RPE_BUCKETS):
        val = jnp.where(dist >= lo[k], rpe_ref[k, h], val)
    o_ref[0, 0] = jnp.where(dist >= 0, val * LOG2E, NEG)


def _rpe_tiles(rpe_bias):
    heads = rpe_bias.shape[1]
    lo = _rpe_lower_bounds()
    assert lo[-1] <= (RPE_TILES - 1) * MOBA_BLOCK - (MOBA_BLOCK - 1)
    return pl.pallas_call(
        functools.partial(_rpe_tiles_kernel, lo),
        grid=(heads, RPE_TILES),
        in_specs=[pl.BlockSpec(memory_space=pltpu.SMEM)],
        out_specs=pl.BlockSpec((1, 1, MOBA_BLOCK, MOBA_BLOCK), lambda h, j: (h, j, 0, 0)),
        out_shape=jax.ShapeDtypeStruct((heads, RPE_TILES, MOBA_BLOCK, MOBA_BLOCK), F32),
        compiler_params=_cparams(("arbitrary", "arbitrary")),
        name="rpe_tiles",
    )(rpe_bias)


def _moba_kernel(q_ref, k_ref, v_ref, t_ref, o_ref, kb_scr, vt_scr, km_scr, sel_scr, far_scr):
    step = pl.program_id(2)
    nkb, blk, dh = kb_scr.shape

    @pl.when(step == 0)
    def _():
        for n in range(nkb):
            kn = k_ref[0, n * blk:(n + 1) * blk, :]
            kb_scr[n] = kn.astype(BF16)
            km_scr[n:n + 1, :] = jnp.mean(kn.astype(F32), axis=0, keepdims=True)
            vt_scr[n, :dh] = v_ref[0, n * blk:(n + 1) * blk, :].astype(F32).T.astype(BF16)
            ones_row = lax.broadcasted_iota(jnp.int32, (MOBA_DEN_ROWS, blk), 0) == 0
            vt_scr[n, dh:] = jnp.where(ones_row, 1.0, 0.0).astype(BF16)

    for sub in range(MOBA_QBLOCKS):
        _moba_query_block(step * MOBA_QBLOCKS + sub, slice(sub * blk, (sub + 1) * blk),
                          q_ref, t_ref, o_ref, kb_scr, vt_scr, km_scr, sel_scr, far_scr)


def _moba_query_block(i, rows, q_ref, t_ref, o_ref, kb_scr, vt_scr, km_scr, sel_scr, far_scr):
    nkb, blk, dh = kb_scr.shape
    q = q_ref[0, rows, :].astype(F32)
    gate = lax.dot_general(km_scr[...], q, NT, precision=HI, preferred_element_type=F32)
    bidx = lax.broadcasted_iota(jnp.int32, gate.shape, 0)
    past = bidx < i
    g = jnp.where(past, gate, -jnp.inf)
    sel = None
    for _ in range(MOBA_TOPK):
        m = jnp.max(g, axis=0, keepdims=True)
        first = jnp.min(jnp.where(g == m, bidx, nkb), axis=0, keepdims=True)
        hit = bidx == first
        sel = hit if sel is None else jnp.logical_or(sel, hit)
        g = jnp.where(hit, -jnp.inf, g)
    mask = jnp.where(jnp.logical_or(jnp.logical_and(sel, past), bidx == i), 0.0, NEG)
    sel_scr[...] = mask
    far_bias = t_ref[0, RPE_TILES - 1, 0:1, :]
    far_scr[...] = mask + jnp.where(i - bidx >= RPE_TILES - 1, far_bias, 0.0)

    qs = (q * (dh ** -0.5 * LOG2E)).astype(BF16)

    m0 = jnp.full((1, blk), NEG, F32)
    acc0 = jnp.zeros((dh + MOBA_DEN_ROWS, blk), F32)
    def body(far, g, carry):
        m, acc = carry
        nsub = MOBA_GROUP // MOBA_SUB
        blocks = [[jnp.minimum(g * MOBA_GROUP + k * MOBA_SUB + u, nkb - 1) for u in range(MOBA_SUB)]
                  for k in range(nsub)]
        scores, probs, alphas = {}, {}, {}

        def emit_scores(k):
            out = []
            for n in blocks[k]:
                s = lax.dot_general(kb_scr[n], qs, NT, preferred_element_type=F32)
                if far:
                    out.append(s + far_scr[pl.ds(n, 1), :])
                else:
                    out.append(s + t_ref[0, jnp.clip(i - n, 0, RPE_TILES - 1)] + sel_scr[pl.ds(n, 1), :])
            scores[k] = out

        def emit_softmax(k, m):
            m_new = m
            for s in scores[k]:
                m_new = jnp.maximum(m_new, jnp.max(s, axis=0, keepdims=True))
            alphas[k] = jnp.exp2(m - m_new)
            probs[k] = [jnp.exp2(s - m_new).astype(BF16) for s in scores[k]]
            return m_new

        def emit_values(k, acc):
            acc = alphas[k] * acc
            for n, p in zip(blocks[k], probs[k]):
                acc = acc + jnp.dot(vt_scr[n], p, preferred_element_type=F32)
            return acc

        emit_scores(0)
        for k in range(nsub):
            if k + 1 < nsub:
                emit_scores(k + 1)
            if k >= 1:
                acc = emit_values(k - 1, acc)
            m = emit_softmax(k, m)
        acc = emit_values(nsub - 1, acc)
        return m, acc

    n_far = jnp.maximum(i - (RPE_TILES - 2), 0) // MOBA_GROUP
    n_all = i // MOBA_GROUP + 1
    carry = lax.fori_loop(0, n_far, functools.partial(body, True), (m0, acc0))
    _, acc = lax.fori_loop(n_far, n_all, functools.partial(body, False), carry)
    o_ref[0, rows, :] = (acc[:dh] / acc[dh:dh + 1]).T.astype(o_ref.dtype)


def _moba(proj, tiles, q_col, k_col, v_col):
    bsz, s, _ = proj.shape
    dh, blk, heads = MOBA_HEAD_DIM, MOBA_BLOCK, MOBA_HEADS
    nkb = s // blk
    qrows = blk * MOBA_QBLOCKS
    return pl.pallas_call(
        _moba_kernel,
        grid=(bsz, heads, nkb // MOBA_QBLOCKS),
        in_specs=[pl.BlockSpec((1, qrows, dh), lambda b, h, i: (b, i, q_col + h)),
                  pl.BlockSpec((1, s, dh), lambda b, h, i: (b, 0, k_col + h)),
                  pl.BlockSpec((1, s, dh), lambda b, h, i: (b, 0, v_col + h)),
                  pl.BlockSpec((1, RPE_TILES, blk, blk), lambda b, h, i: (h, 0, 0, 0))],
        out_specs=pl.BlockSpec((1, qrows, dh), lambda b, h, i: (b, i, h)),
        out_shape=jax.ShapeDtypeStruct((bsz, s, heads * dh), BF16),
        scratch_shapes=[pltpu.VMEM((nkb, blk, dh), BF16),
                        pltpu.VMEM((nkb, dh + MOBA_DEN_ROWS, blk), BF16),
                        pltpu.VMEM((nkb, dh), F32),
                        pltpu.VMEM((nkb, blk), F32),
                        pltpu.VMEM((nkb, blk), F32)],
        compiler_params=_cparams(("arbitrary", "arbitrary", "arbitrary")),
        name="moba_attention",
    )(proj, proj, proj, tiles)


def _gla_kernel(q_ref, k_ref, v_ref, gg_ref, glr_ref, w2_ref, gb_ref, nw_ref, o_ref, st_scr):
    @pl.when(pl.program_id(1) == 0)
    def _():
        st_scr[...] = jnp.zeros(st_scr.shape, F32)

    tg = q_ref.shape[1]
    c = GLA_CHUNK
    x = jnp.dot(glr_ref[0].astype(F32), w2_ref[...], precision=HI, preferred_element_type=F32) + gb_ref[...]
    lg = -_softplus(-x) * (1.0 / GLA_GATE_NORM)
    row = lax.broadcasted_iota(jnp.int32, (c, c), 0)
    col = lax.broadcasted_iota(jnp.int32, (c, c), 1)
    tri = (row >= col).astype(F32)
    hc = GLA_HEADS * c
    rr = lax.broadcasted_iota(jnp.int32, (hc, hc), 0)
    cc = lax.broadcasted_iota(jnp.int32, (hc, hc), 1)
    incl = jnp.logical_and(rr // c == cc // c, rr >= cc)
    nw = nw_ref[...]
    heads = range(GLA_HEADS)

    chunks = []
    for ci in range(tg // c):
        rows = slice(ci * c, (ci + 1) * c)
        b = jnp.dot(tri, lg[rows], precision=HI, preferred_element_type=F32)
        bl = b[c - 1:c, :]
        q = q_ref[0, rows, :].astype(F32) * GLA_DK ** -0.5
        k = k_ref[0, rows, :].astype(F32)
        q_e = (q * jnp.exp(b)).astype(BF16)
        k_e = (k * jnp.exp(-b)).astype(BF16)
        chunks.append(dict(rows=rows, q_e=q_e, k_e=k_e, k_end=(k * jnp.exp(bl - b)).astype(BF16), d=jnp.exp(bl)))
    for p in chunks:
        qs = jnp.concatenate([p["q_e"][:, h * GLA_DK:(h + 1) * GLA_DK] for h in heads], axis=0)
        ks = jnp.concatenate([p["k_e"][:, h * GLA_DK:(h + 1) * GLA_DK] for h in heads], axis=0)
        p["v"] = jnp.concatenate([v_ref[0, p["rows"], h * GLA_DV:(h + 1) * GLA_DV] for h in heads],
                                 axis=0).astype(BF16)
        a = lax.dot_general(qs, ks, NT, preferred_element_type=F32)
        p["a"] = jnp.where(incl, a, 0.0).astype(BF16)
    for p in chunks:
        p["o"] = jnp.dot(p["a"], p["v"], preferred_element_type=F32)

    for p in chunks:
        for h in heads:
            ks = slice(h * GLA_DK, (h + 1) * GLA_DK)
            vs = slice(h * GLA_DV, (h + 1) * GLA_DV)
            hr = slice(h * c, (h + 1) * c)
            st = st_scr[h]
            o = p["o"][hr] + lax.dot_general(p["q_e"][:, ks], st.astype(BF16), NT, preferred_element_type=F32)
            st_scr[h] = (st * p["d"][:, ks]
                         + lax.dot_general(p["v"][hr], p["k_end"][:, ks], TN, preferred_element_type=F32))
            o = o * lax.rsqrt(jnp.mean(o * o, axis=-1, keepdims=True) + NORM_EPS)
            o_ref[0, p["rows"], vs] = (o * nw * _silu(gg_ref[0, p["rows"], vs].astype(F32))).astype(o_ref.dtype)


def _gla(proj, gk_w2p, gk_b, o_norm, q_col, k_col, v_col, g_col, r_col, tg=512):
    bsz, s, _ = proj.shape
    tg = min(tg, s)
    qk, vw = GLA_QK_W, GLA_V_W
    return pl.pallas_call(
        _gla_kernel,
        grid=(bsz, s // tg),
        in_specs=[pl.BlockSpec((1, tg, qk), lambda b, t: (b, t, q_col)),
                  pl.BlockSpec((1, tg, qk), lambda b, t: (b, t, k_col)),
                  pl.BlockSpec((1, tg, vw), lambda b, t: (b, t, v_col)),
                  pl.BlockSpec((1, tg, vw), lambda b, t: (b, t, g_col)),
                  pl.BlockSpec((1, tg, LANES), lambda b, t: (b, t, r_col)),
                  pl.BlockSpec((LANES, qk), lambda b, t: (0, 0)),
                  pl.BlockSpec((1, qk), lambda b, t: (0, 0)),
                  pl.BlockSpec((1, GLA_DV), lambda b, t: (0, 0))],
        out_specs=pl.BlockSpec((1, tg, vw), lambda b, t: (b, t, 0)),
        out_shape=jax.ShapeDtypeStruct((bsz, s, vw), BF16),
        scratch_shapes=[pltpu.VMEM((GLA_HEADS, GLA_DV, GLA_DK), F32)],
        compiler_params=_cparams(("arbitrary", "arbitrary")),
        name="gla_mixer",
    )(proj, proj, proj, proj, proj, gk_w2p, gk_b, o_norm)


def _gdn_kernel(qkv_ref, gate_ref, ba_ref, cw_ref, pv_ref, nw_ref, o_ref, tail_scr, s_scr):
    @pl.when(pl.program_id(1) == 0)
    def _():
        tail_scr[...] = jnp.zeros(tail_scr.shape, F32)
        s_scr[...] = jnp.zeros(s_scr.shape, F32)

    tg = qkv_ref.shape[1]
    c, dk, grp = GDN_CHUNK, GDN_DK, GDN_GROUP
    gr = grp * c
    w = GDN_W

    x = qkv_ref[0].astype(F32)
    tail = tail_scr[...]
    tail_scr[...] = x[tg - 8:, :]
    r8 = lax.broadcasted_iota(jnp.int32, (8, 1), 0)
    y = x * cw_ref[GDN_CONV - 1:GDN_CONV, :]
    for sft in range(1, GDN_CONV):
        xs = pltpu.roll(x, sft, axis=0)
        head = jnp.where(r8 < sft, pltpu.roll(tail, sft, axis=0), xs[:8, :])
        xs = jnp.concatenate([head, xs[8:, :]], axis=0)
        y = y + xs * cw_ref[GDN_CONV - 1 - sft:GDN_CONV - sft, :]
    y = _silu(y)

    ba = ba_ref[0].astype(F32)
    beta_t = _sigmoid(ba)
    g_t = -jnp.exp(pv_ref[0:1, :]) * _softplus(ba + pv_ref[1:2, :])

    row = lax.broadcasted_iota(jnp.int32, (c, c), 0)
    col = lax.broadcasted_iota(jnp.int32, (c, c), 1)
    tri = (row >= col).astype(F32)
    rr = lax.broadcasted_iota(jnp.int32, (gr, gr), 0)
    cc = lax.broadcasted_iota(jnp.int32, (gr, gr), 1)
    same = (rr // c) == (cc // c)
    incl = jnp.logical_and(same, rr >= cc)
    strict = jnp.logical_and(same, rr > cc)
    eye = (rr == cc).astype(F32)
    halves = [(rr // sz) == (cc // sz) for sz in (2 ** e for e in range(1, int(math.log2(c)) + 1))]
    nw = nw_ref[...]

    probs = []
    for ci in range(tg // c):
        rows = slice(ci * c, (ci + 1) * c)
        gcum = jnp.dot(tri, g_t[rows], precision=HI, preferred_element_type=F32)
        gcum_t = gcum.T
        for gi in range(GDN_HEADS // grp):
            hs = [gi * grp + u for u in range(grp)]

            def stack(a, off):
                return jnp.concatenate([a[rows, off + h * dk: off + (h + 1) * dk] for h in hs], axis=0)

            q = stack(y, 0)
            k = stack(y, w)
            v = stack(y, 2 * w)
            q = q * lax.rsqrt(jnp.sum(q * q, axis=-1, keepdims=True) + NORM_EPS) * dk ** -0.5
            k = k * lax.rsqrt(jnp.sum(k * k, axis=-1, keepdims=True) + NORM_EPS)
            beta = jnp.concatenate([beta_t[rows, h:h + 1] for h in hs], axis=0)
            gc = jnp.concatenate([gcum[:, GDN_HEADS + h:GDN_HEADS + h + 1] for h in hs], axis=0)
            gc_row = jnp.concatenate([gcum_t[GDN_HEADS + h:GDN_HEADS + h + 1, :] for h in hs], axis=1)
            gl = jnp.concatenate([jnp.broadcast_to(gcum[c - 1:c, GDN_HEADS + h:GDN_HEADS + h + 1], (c, 1))
                                  for h in hs], axis=0)

            decay = jnp.where(incl, jnp.exp(jnp.where(incl, gc - gc_row, 0.0)), 0.0)
            kb = k * beta
            k16 = k.astype(BF16)
            a = lax.dot_general(kb.astype(BF16), k16, NT, preferred_element_type=F32)
            a = jnp.where(strict, a * decay, 0.0)
            eg = jnp.exp(gc)
            probs.append(dict(
                rows=rows, hs=hs, a16=a.astype(BF16),
                t=eye - jnp.where(halves[0], a, 0.0),
                rhs=jnp.concatenate([v * beta, kb * eg], axis=1).astype(BF16),
                attn=(lax.dot_general(q.astype(BF16), k16, NT, preferred_element_type=F32) * decay).astype(BF16),
                q_g=(q * eg).astype(BF16),
                k_end=(k * jnp.exp(gl - gc)).astype(BF16),
                d_last=[jnp.exp(gcum[c - 1:c, GDN_HEADS + h:GDN_HEADS + h + 1]) for h in hs]))

    for lvl in range(1, len(halves)):
        off16 = jnp.where(jnp.logical_and(halves[lvl], jnp.logical_not(halves[lvl - 1])), 1.0, 0.0).astype(BF16)
        t16s = [p["t"].astype(BF16) for p in probs]
        xs = [jnp.dot(p["a16"] * off16, t16, preferred_element_type=F32).astype(BF16)
              for p, t16 in zip(probs, t16s)]
        for p, t16, x in zip(probs, t16s, xs):
            p["t"] = p["t"] - jnp.dot(t16, x, preferred_element_type=F32)
    for p in probs:
        wk = jnp.dot(p["t"].astype(BF16), p["rhs"], preferred_element_type=F32)
        p["w_val"] = wk[:, :GDN_DV]
        p["k_cum"] = wk[:, GDN_DV:].astype(BF16)

    for p in probs:
        rows, hs = p["rows"], p["hs"]
        v_new = []
        for u, h in enumerate(hs):
            hr = slice(u * c, (u + 1) * c)
            st16 = s_scr[h].astype(BF16)
            v_new.append(p["w_val"][hr] - jnp.dot(p["k_cum"][hr], st16, preferred_element_type=F32))
        v16 = jnp.concatenate(v_new, axis=0).astype(BF16)
        o_intra = jnp.dot(p["attn"], v16, preferred_element_type=F32)
        for u, h in enumerate(hs):
            hr = slice(u * c, (u + 1) * c)
            st = s_scr[h]
            o = o_intra[hr] + jnp.dot(p["q_g"][hr], st.astype(BF16), preferred_element_type=F32)
            s_scr[h] = (st * p["d_last"][u]
                        + lax.dot_general(p["k_end"][hr], v16[hr], TN, preferred_element_type=F32))
            o = o * lax.rsqrt(jnp.mean(o * o, axis=-1, keepdims=True) + NORM_EPS)
            cs = slice(h * GDN_DV, (h + 1) * GDN_DV)
            o_ref[0, rows, cs] = (o * nw * _silu(gate_ref[0, rows, cs].astype(F32))).astype(o_ref.dtype)


def _gdn(proj, conv_w, pvec, o_norm, tg=512):
    bsz, s, _ = proj.shape
    tg = min(tg, s)
    w = GDN_W
    return pl.pallas_call(
        _gdn_kernel,
        grid=(bsz, s // tg),
        in_specs=[pl.BlockSpec((1, tg, 3 * w), lambda b, t: (b, t, 0)),
                  pl.BlockSpec((1, tg, w), lambda b, t: (b, t, 3)),
                  pl.BlockSpec((1, tg, LANES), lambda b, t: (b, t, 4 * w // LANES)),
                  pl.BlockSpec((GDN_CONV, 3 * w), lambda b, t: (0, 0)),
                  pl.BlockSpec((2, LANES), lambda b, t: (0, 0)),
                  pl.BlockSpec((1, GDN_DV), lambda b, t: (0, 0))],
        out_specs=pl.BlockSpec((1, tg, w), lambda b, t: (b, t, 0)),
        out_shape=jax.ShapeDtypeStruct((bsz, s, w), BF16),
        scratch_shapes=[pltpu.VMEM((8, 3 * w), F32),
                        pltpu.VMEM((GDN_HEADS, GDN_DK, GDN_DV), F32)],
        compiler_params=_cparams(("arbitrary", "arbitrary")),
        name="gdn_mixer",
    )(proj, proj, proj, conv_w, pvec, o_norm)


def _mixout_kernel(n_act, *refs):
    acts = refs[:n_act]
    ws = refs[n_act:2 * n_act]
    x_ref, g1_ref, lng_ref, lnb_ref, sc_ref, sh_ref, rw_ref = refs[2 * n_act:2 * n_act + 7]
    xo_ref, h_ref, rl_ref = refs[2 * n_act + 7:]
    y = None
    for a_ref, w_ref in zip(acts, ws):
        t = jnp.dot(a_ref[0].astype(BF16), w_ref[...], preferred_element_type=F32)
        y = t if y is None else y + t
    xn = _layer_norm(DEEPNORM_ALPHA * x_ref[0] + g1_ref[0] * y, lng_ref[...], lnb_ref[...])
    xo_ref[0] = xn
    h = xn * (1.0 + sc_ref[0]) + sh_ref[0]
    h_hi = h.astype(BF16)
    h_ref[0] = h_hi
    h_lo = (h - h_hi.astype(F32)).astype(BF16)
    rw = rw_ref[...]
    rw_hi = rw.astype(BF16)
    rw_lo = (rw - rw_hi.astype(F32)).astype(BF16)
    rl_ref[0] = (lax.dot_general(rw_hi, h_hi, NT, preferred_element_type=F32)
                 + lax.dot_general(rw_hi, h_lo, NT, preferred_element_type=F32)
                 + lax.dot_general(rw_lo, h_hi, NT, preferred_element_type=F32))


def _mixout(acts, ws, x, g1, ln_g, ln_b, sc2, sh2, router_wt, tm=1024):
    bsz, s, d = x.shape
    tm = min(tm, s)
    n_act = len(acts)
    ne = router_wt.shape[0]
    vec = pl.BlockSpec((1, 1, d), lambda b, i: (b, 0, 0))
    par = pl.BlockSpec((1, d), lambda b, i: (0, 0))
    in_specs = ([pl.BlockSpec((1, tm, a.shape[-1]), lambda b, i: (b, i, 0)) for a in acts]
                + [pl.BlockSpec(w.shape, lambda b, i: (0, 0)) for w in ws]
                + [pl.BlockSpec((1, tm, d), lambda b, i: (b, i, 0)), vec, par, par, vec, vec,
                   pl.BlockSpec((ne, d), lambda b, i: (0, 0))])
    return pl.pallas_call(
        functools.partial(_mixout_kernel, n_act),
        grid=(bsz, s // tm),
        in_specs=in_specs,
        out_specs=[pl.BlockSpec((1, tm, d), lambda b, i: (b, i, 0)),
                   pl.BlockSpec((1, tm, d), lambda b, i: (b, i, 0)),
                   pl.BlockSpec((1, ne, tm), lambda b, i: (b, 0, i))],
        out_shape=[jax.ShapeDtypeStruct((bsz, s, d), F32),
                   jax.ShapeDtypeStruct((bsz, s, d), BF16),
                   jax.ShapeDtypeStruct((bsz, ne, s), F32)],
        compiler_params=_cparams(("arbitrary", "arbitrary")),
        name="mix_out",
    )(*acts, *ws, x, g1, ln_g, ln_b, sc2, sh2, router_wt)


def _first_max(vals, idx, axis, sentinel):
    m = jnp.max(vals, axis=axis, keepdims=True)
    first = jnp.min(jnp.where(vals == m, idx, sentinel), axis=axis, keepdims=True)
    return m, idx == first


def _router_kernel(rl_ref, rb_ref, g_ref, gt_ref, cnt_ref):
    ne, tn = rl_ref.shape[1], rl_ref.shape[2]
    gsz = ne // N_GROUPS
    scores = _sigmoid(rl_ref[0])
    sel = scores + rb_ref[...]
    ridx = lax.broadcasted_iota(jnp.int32, (gsz, tn), 0)
    gidx = lax.broadcasted_iota(jnp.int32, (N_GROUPS, tn), 0)
    gs = jnp.zeros((N_GROUPS, tn), F32)
    for g in range(N_GROUPS):
        sg = sel[g * gsz:(g + 1) * gsz, :]
        m1, hit = _first_max(sg, ridx, 0, gsz)
        m2 = jnp.max(jnp.where(hit, -jnp.inf, sg), axis=0, keepdims=True)
        gs = jnp.where(gidx == g, m1 + m2, gs)
    gsel = None
    for _ in range(TOPK_GROUPS):
        _, hit = _first_max(gs, gidx, 0, N_GROUPS)
        gsel = hit if gsel is None else jnp.logical_or(gsel, hit)
        gs = jnp.where(hit, -jnp.inf, gs)
    gself = gsel.astype(F32)
    emask = jnp.concatenate([jnp.broadcast_to(gself[g:g + 1, :], (gsz, tn)) for g in range(N_GROUPS)], axis=0)
    cand = jnp.where(emask > 0.5, sel, -jnp.inf)
    eidx = lax.broadcasted_iota(jnp.int32, cand.shape, 0)
    chosen = None
    for _ in range(TOP_K):
        _, hit = _first_max(cand, eidx, 0, ne)
        chosen = hit if chosen is None else jnp.logical_or(chosen, hit)
        cand = jnp.where(hit, -jnp.inf, cand)
    wsel = jnp.where(chosen, scores, 0.0)
    tot = jnp.sum(wsel, axis=0, keepdims=True)
    gates = wsel / (tot + 1e-20) * ROUTED_SCALE
    g_ref[0] = gates.T
    gt_ref[0] = gates
    tile_of = lax.broadcasted_iota(jnp.int32, (tn, LANES), 0) // MOE_TM
    ind = (tile_of == lax.broadcasted_iota(jnp.int32, (tn, LANES), 1)).astype(BF16)
    routed = jnp.where(gates > 0.0, 1.0, 0.0).astype(BF16)
    cnt_ref[0, 0] = jnp.dot(routed, ind, preferred_element_type=F32)


def _router(rl, router_b, tn=1024):
    bsz, ne, s = rl.shape
    tn = min(tn, s)
    return pl.pallas_call(
        _router_kernel,
        grid=(bsz, s // tn),
        in_specs=[pl.BlockSpec((1, ne, tn), lambda b, i: (b, 0, i)),
                  pl.BlockSpec((ne, 1), lambda b, i: (0, 0))],
        out_specs=[pl.BlockSpec((1, tn, ne), lambda b, i: (b, i, 0)),
                   pl.BlockSpec((1, ne, tn), lambda b, i: (b, 0, i)),
                   pl.BlockSpec((1, 1, ne, LANES), lambda b, i: (b, i, 0, 0))],
        out_shape=[jax.ShapeDtypeStruct((bsz, s, ne), F32),
                   jax.ShapeDtypeStruct((bsz, ne, s), F32),
                   jax.ShapeDtypeStruct((bsz, s // tn, ne, LANES), F32)],
        compiler_params=_cparams(("arbitrary", "arbitrary")),
        name="moe_router",
    )(rl, router_b.reshape(ne, 1))


MOE_TM = 256
MOE_ALIGN = 16
MOE_FFN_BUFS = 3
MOE_R = 2048
MOE_LC = 512
MOE_LMAX = -(-(MOE_TM * TOP_K + N_EXPERTS * (MOE_ALIGN - 1)) // MOE_LC) * MOE_LC
MOE_NP = MOE_LMAX // MOE_ALIGN
MOE_LP = MOE_LC // MOE_ALIGN
POS_SPLIT = 256.0


def _ffn(x, wg, wu, wd):
    a = jnp.dot(x, wg, preferred_element_type=F32)
    u = jnp.dot(x, wu, preferred_element_type=F32)
    return jnp.dot((_silu(a) * u).astype(BF16), wd, preferred_element_type=F32)


def _split_pos(pos, axis):
    hi = jnp.floor(pos * (1.0 / POS_SPLIT)) * POS_SPLIT
    return jnp.concatenate([hi, pos - hi], axis=axis).astype(BF16)


def _for_each_piece(n_pieces, fn):
    unroll = 8

    def body(q, carry):
        for u in range(unroll):
            fn(q * unroll + u)
        return carry

    full = n_pieces // unroll
    lax.fori_loop(0, full, body, 0)

    def tail(p, carry):
        fn(p)
        return carry

    lax.fori_loop(full * unroll, n_pieces, tail, 0)


def _dispatch_kernel(dst_s, np_s, h_ref, gt_ref, offc_ref, offr_ref, cntr_ref, xs_hbm, sorted_scr, sems):
    i = pl.program_id(0)
    last = pl.num_programs(0) - 1
    slot = i % 2
    tm = h_ref.shape[0]
    routed = gt_ref[0] > 0.0
    t0 = lax.broadcasted_iota(jnp.int32, (tm, tm), 0)
    t1 = lax.broadcasted_iota(jnp.int32, (tm, tm), 1)
    earlier = jnp.where(t0 < t1, 1.0, 0.0).astype(BF16)
    rank_t = jnp.dot(jnp.where(routed, 1.0, 0.0).astype(BF16), earlier, preferred_element_type=F32)
    pos_t = jnp.where(routed, rank_t + offc_ref[0] + 1.0, 0.0)
    pos2 = _split_pos(pos_t, 0)
    x = h_ref[...]
    offr = offr_ref[0]
    endr = offr + cntr_ref[0]
    for c in range(MOE_LMAX // MOE_LC):
        r = (c * MOE_LC + lax.broadcasted_iota(jnp.int32, (MOE_LC, 1), 0)).astype(F32)
        owner = jnp.where(jnp.logical_and(r >= offr, r < endr), 1.0, 0.0).astype(BF16)
        possel = jnp.dot(jnp.concatenate([owner, owner], axis=1), pos2, preferred_element_type=F32)
        perm = jnp.where(possel == r + 1.0, 1.0, 0.0).astype(BF16)
        srt = jnp.dot(perm, x, preferred_element_type=F32).astype(BF16)
        sorted_scr[slot, c * MOE_LP:(c + 1) * MOE_LP] = srt.reshape(MOE_LP, MOE_ALIGN, srt.shape[-1])

    def piece(sl, tile, p):
        return pltpu.make_async_copy(sorted_scr.at[sl, p], xs_hbm.at[dst_s[tile * MOE_NP + p]], sems.at[sl])

    _for_each_piece(np_s[i], lambda p: piece(slot, i, p).start())

    @pl.when(i > 0)
    def _():
        _for_each_piece(np_s[i - 1], lambda p: piece(1 - slot, i - 1, p).wait())

    @pl.when(i == last)
    def _():
        _for_each_piece(np_s[i], lambda p: piece(slot, i, p).wait())


def _expert_ffn_kernel(be_s, nu_s, xs_hbm, wg_ref, wu_ref, wd_ref, y_ref, x_buf, sems):
    j = pl.program_id(0)
    n_used = nu_s[0]

    def copy(b):
        slot = b % MOE_FFN_BUFS
        src = xs_hbm.at[pl.ds(pl.multiple_of(b * MOE_R, MOE_R), MOE_R)]
        return pltpu.make_async_copy(src, x_buf.at[slot], sems.at[slot])

    @pl.when(j == 0)
    def _():
        for b in range(MOE_FFN_BUFS - 1):
            pl.when(b < n_used)(lambda b=b: copy(b).start())

    ahead = j + (MOE_FFN_BUFS - 1)

    @pl.when(ahead < n_used)
    def _():
        copy(ahead).start()

    @pl.when(j < n_used)
    def _():
        copy(j).wait()
        y_ref[...] = _ffn(x_buf[j % MOE_FFN_BUFS], wg_ref[0, 0].astype(BF16), wu_ref[0, 0].astype(BF16),
                          wd_ref[0, 0].astype(BF16)).astype(BF16)


def _combine_kernel(dst_s, np_s, ys_hbm, g_ref, h_ref, offr_ref, offc_ref, cntc_ref,
                    sg_ref, su_ref, sd_ref, x_ref, g2_ref, lng_ref, lnb_ref, o_ref, ys_scr, sems):
    i = pl.program_id(0)
    last = pl.num_programs(0) - 1
    slot = i % 2
    tm = h_ref.shape[0]

    def piece(sl, tile, p):
        return pltpu.make_async_copy(ys_hbm.at[dst_s[tile * MOE_NP + p]], ys_scr.at[sl, p], sems.at[sl])

    @pl.when(i == 0)
    def _():
        ys_scr[...] = jnp.zeros(ys_scr.shape, BF16)
        _for_each_piece(np_s[0], lambda p: piece(0, 0, p).start())

    @pl.when(i < last)
    def _():
        _for_each_piece(np_s[i + 1], lambda p: piece(1 - slot, i + 1, p).start())

    acc = _ffn(h_ref[...], sg_ref[...], su_ref[...], sd_ref[...])
    g = g_ref[...]
    routed = g > 0.0
    t0 = lax.broadcasted_iota(jnp.int32, (tm, tm), 0)
    t1 = lax.broadcasted_iota(jnp.int32, (tm, tm), 1)
    earlier = jnp.where(t0 > t1, 1.0, 0.0).astype(BF16)
    rank = jnp.dot(earlier, jnp.where(routed, 1.0, 0.0).astype(BF16), preferred_element_type=F32)
    pos = jnp.where(routed, rank + offr_ref[0] + 1.0, 0.0)
    pos2 = _split_pos(pos, 1)
    g16 = g.astype(BF16)
    offc = offc_ref[0]
    endc = offc + cntc_ref[0]
    _for_each_piece(np_s[i], lambda p: piece(slot, i, p).wait())
    for c in range(MOE_LMAX // MOE_LC):
        r = (c * MOE_LC + lax.broadcasted_iota(jnp.int32, (1, MOE_LC), 1)).astype(F32)
        owner = jnp.where(jnp.logical_and(r >= offc, r < endc), 1.0, 0.0).astype(BF16)
        possel = jnp.dot(pos2, jnp.concatenate([owner, owner], axis=0), preferred_element_type=F32)
        gsel = jnp.dot(g16, owner, preferred_element_type=F32)
        w = jnp.where(possel == r + 1.0, gsel, 0.0).astype(BF16)
        ys = ys_scr[slot, c * MOE_LP:(c + 1) * MOE_LP].reshape(MOE_LC, ys_scr.shape[-1])
        acc = acc + jnp.dot(w, ys, preferred_element_type=F32)
    z = DEEPNORM_ALPHA * x_ref[...] + g2_ref[0] * acc
    o_ref[...] = _layer_norm(z, lng_ref[...], lnb_ref[...])


def _ceil_to(v, m):
    return jnp.floor((v + (m - 1.0)) * (1.0 / m)) * m


def _moe_layout_kernel(cnt_ref, dst_ref, np_ref, off_ref, cntp_ref, be_ref, nu_ref):
    cnt = cnt_ref[...]
    ntiles, ne = cnt.shape
    cntp = _ceil_to(cnt, MOE_ALIGN)
    e0 = lax.broadcasted_iota(jnp.int32, (ne, ne), 0)
    e1 = lax.broadcasted_iota(jnp.int32, (ne, ne), 1)
    i0 = lax.broadcasted_iota(jnp.int32, (ntiles, ntiles), 0)
    i1 = lax.broadcasted_iota(jnp.int32, (ntiles, ntiles), 1)

    def mm(a, b):
        return jnp.dot(a, b, precision=HI, preferred_element_type=F32)

    off = mm(cntp, jnp.where(e0 < e1, 1.0, 0.0))
    before = mm(jnp.where(i0 > i1, 1.0, 0.0), cntp)
    tot = jnp.broadcast_to(jnp.sum(cntp, axis=0, keepdims=True), (8, ne))
    totr = _ceil_to(tot, MOE_R)
    base = mm(totr, jnp.where(e0 < e1, 1.0, 0.0))[0:1, :]
    delta = base + before - off
    step = mm(delta, jnp.where(e0 == e1, 1.0, 0.0) - jnp.where(e0 + 1 == e1, 1.0, 0.0))
    rowp = (lax.broadcasted_iota(jnp.int32, (1, MOE_NP), 1) * MOE_ALIGN).astype(F32)
    dst = jnp.broadcast_to(rowp, (ntiles, MOE_NP))
    for e in range(ne):
        dst = dst + jnp.where(off[:, e:e + 1] <= rowp, step[:, e:e + 1], 0.0)
    dst_ref[...] = (dst * (1.0 / MOE_ALIGN)).astype(jnp.int32)
    pieces = jnp.sum(cntp, axis=1, keepdims=True) * (1.0 / MOE_ALIGN)
    np_ref[...] = jnp.broadcast_to(pieces, np_ref.shape).astype(jnp.int32)
    off_ref[...] = off
    cntp_ref[...] = cntp
    tot_c = lax.dot_general(cntp, jnp.ones((ntiles, LANES), F32), TN, precision=HI, preferred_element_type=F32)
    end_c = mm(jnp.where(e0 >= e1, 1.0, 0.0), _ceil_to(tot_c, MOE_R))[:, 0:1]
    total = end_c[ne - 1:ne, :]
    first = jnp.minimum((lax.broadcasted_iota(jnp.int32, be_ref.shape, 1) * MOE_R).astype(F32), total - 1.0)
    be = jnp.sum(jnp.where(end_c <= first, 1.0, 0.0), axis=0, keepdims=True)
    be_ref[...] = jnp.minimum(be, ne - 1.0).astype(jnp.int32)
    nu_ref[...] = jnp.broadcast_to(total * (1.0 / MOE_R), nu_ref.shape).astype(jnp.int32)


def _moe_layout(cnt_raw, tn):
    ne = N_EXPERTS
    nsub = tn // MOE_TM
    cnt = jnp.transpose(cnt_raw[..., :nsub], (0, 1, 3, 2)).reshape(-1, ne)
    ntiles = cnt.shape[0]
    nblk = -(-(ntiles * (MOE_TM * TOP_K + ne * (MOE_ALIGN - 1)) + ne * (MOE_R - 1)) // MOE_R)
    nblk_pad = -(-nblk // LANES) * LANES
    dst, npieces, off, cntp, blk_exp, nused = pl.pallas_call(
        _moe_layout_kernel,
        out_shape=[jax.ShapeDtypeStruct((ntiles, MOE_NP), jnp.int32),
                   jax.ShapeDtypeStruct((ntiles, LANES), jnp.int32),
                   jax.ShapeDtypeStruct((ntiles, ne), F32),
                   jax.ShapeDtypeStruct((ntiles, ne), F32),
                   jax.ShapeDtypeStruct((1, nblk_pad), jnp.int32),
                   jax.ShapeDtypeStruct((1, LANES), jnp.int32)],
        compiler_params=pltpu.CompilerParams(vmem_limit_bytes=VMEM_LIMIT),
        name="moe_layout",
    )(cnt)
    return dst.reshape(-1), npieces[:, 0], off, cntp, nblk, nused[0, :1], blk_exp[0, :nblk]


def _moe(h2, gates, gates_t, cnt_raw, tn, layer, wg, wu, wd, sg, su, sd, x, g2, ln_g, ln_b):
    bsz, s, d = x.shape
    t = bsz * s
    tm = MOE_TM
    _, ne, _, de = wg.shape
    per_b = s // tm
    ntiles = t // tm
    dst, npieces, off_f, cnt_f, nblk, nused, blk_exp = _moe_layout(cnt_raw, tn)
    scalars = (dst, npieces)
    row = pl.BlockSpec((1, 1, ne), lambda i, *_: (i, 0, 0))
    col = pl.BlockSpec((1, ne, 1), lambda i, *_: (i, 0, 0))
    tok = pl.BlockSpec((tm, d), lambda i, *_: (i, 0))
    par = pl.BlockSpec((1, d), lambda i, *_: (0, 0))
    rows = nblk * MOE_R

    xs = pl.pallas_call(
        _dispatch_kernel,
        grid_spec=pltpu.PrefetchScalarGridSpec(
            num_scalar_prefetch=2, grid=(ntiles,),
            in_specs=[tok,
                      pl.BlockSpec((1, ne, tm), lambda i, *_: (i // per_b, 0, i % per_b)),
                      col, row, row],
            out_specs=pl.BlockSpec(memory_space=pl.ANY),
            scratch_shapes=[pltpu.VMEM((2, MOE_NP, MOE_ALIGN, d), BF16), pltpu.SemaphoreType.DMA((2,))]),
        out_shape=jax.ShapeDtypeStruct((rows // MOE_ALIGN, MOE_ALIGN, d), BF16),
        compiler_params=_cparams(("arbitrary",)),
        name="moe_dispatch",
    )(*scalars, h2.reshape(t, d), gates_t, off_f.reshape(ntiles, ne, 1), off_f.reshape(ntiles, 1, ne),
      cnt_f.reshape(ntiles, 1, ne))

    def blk_index(j, be, nu):
        return jnp.maximum(jnp.minimum(j, nu[0] - 1), 0), 0

    blk = pl.BlockSpec((MOE_R, d), blk_index)
    ys = pl.pallas_call(
        _expert_ffn_kernel,
        grid_spec=pltpu.PrefetchScalarGridSpec(
            num_scalar_prefetch=2, grid=(nblk,),
            in_specs=[pl.BlockSpec(memory_space=pl.ANY),
                      pl.BlockSpec((1, 1, d, de), lambda j, be, nu: (layer, be[j], 0, 0)),
                      pl.BlockSpec((1, 1, d, de), lambda j, be, nu: (layer, be[j], 0, 0)),
                      pl.BlockSpec((1, 1, de, d), lambda j, be, nu: (layer, be[j], 0, 0))],
            out_specs=blk,
            scratch_shapes=[pltpu.VMEM((MOE_FFN_BUFS, MOE_R, d), BF16),
                            pltpu.SemaphoreType.DMA((MOE_FFN_BUFS,))]),
        out_shape=jax.ShapeDtypeStruct((rows, d), BF16),
        compiler_params=_cparams(("arbitrary",)),
        name="moe_expert_ffn",
    )(blk_exp, nused, xs.reshape(rows, d), wg, wu, wd)

    out = pl.pallas_call(
        _combine_kernel,
        grid_spec=pltpu.PrefetchScalarGridSpec(
            num_scalar_prefetch=2, grid=(ntiles,),
            in_specs=[pl.BlockSpec(memory_space=pl.ANY),
                      pl.BlockSpec((tm, ne), lambda i, *_: (i, 0)),
                      tok, row, col, col,
                      pl.BlockSpec(sg.shape, lambda i, *_: (0, 0)),
                      pl.BlockSpec(su.shape, lambda i, *_: (0, 0)),
                      pl.BlockSpec(sd.shape, lambda i, *_: (0, 0)),
                      tok,
                      pl.BlockSpec((1, 1, d), lambda i, *_: (i // per_b, 0, 0)),
                      par, par],
            out_specs=tok,
            scratch_shapes=[pltpu.VMEM((2, MOE_NP, MOE_ALIGN, d), BF16), pltpu.SemaphoreType.DMA((2,))]),
        out_shape=jax.ShapeDtypeStruct((t, d), F32),
        compiler_params=_cparams(("arbitrary",)),
        name="moe_combine",
    )(*scalars, ys.reshape(rows // MOE_ALIGN, MOE_ALIGN, d), gates.reshape(t, ne), h2.reshape(t, d),
      off_f.reshape(ntiles, 1, ne),
      off_f.reshape(ntiles, ne, 1), cnt_f.reshape(ntiles, ne, 1), sg, su, sd, x.reshape(t, d), g2, ln_g, ln_b)
    return out.reshape(bsz, s, d)


def _pad_cols(w, n):
    return jnp.pad(w, ((0, 0), (0, n - w.shape[1])))


def kernel(x, c, rpe_bias, ada_w, ada_b, ln_mix_g, ln_mix_b, ln_ffn_g, ln_ffn_b, ev_w_in, ev_gk_w2, ev_gk_b, ev_norm, ev_w_out, od_w_in, od_conv_w, od_a_log, od_dt_bias, od_norm, od_w_out, moe_router_w, moe_router_b, moe_w_gate, moe_w_up, moe_w_down, sh_w_gate, sh_w_up, sh_w_down):
    bsz, s, d = x.shape
    mod = _ada(c, ada_w, ada_b)
    tiles = _rpe_tiles(rpe_bias)

    for layer in range(DEPTH):
        sh1, sc1, g1, sh2, sc2, g2 = [mod[layer, :, u * d:(u + 1) * d].reshape(bsz, 1, d) for u in range(6)]
        i = layer // 2
        if layer % 2 == 0:
            n_main = 3 * MOBA_W + 2 * GLA_QK_W + 2 * GLA_V_W
            w_in = jnp.concatenate([ev_w_in[i][:, :n_main], _pad_cols(ev_w_in[i][:, n_main:], LANES)], axis=1)
            proj = _mod_matmul(x, sc1, sh1, w_in.astype(BF16))
            nb = MOBA_W // LANES
            o_a = _moba(proj, tiles, 0, nb, 2 * nb)
            gk_w2p = jnp.pad(ev_gk_w2[i], ((0, LANES - GLA_GATE_RANK), (0, 0)))
            gla0 = 3 * MOBA_W
            o_b = _gla(proj, gk_w2p, ev_gk_b[i].reshape(1, -1), ev_norm[i].reshape(1, -1),
                       gla0 // GLA_QK_W, gla0 // GLA_QK_W + 1,
                       (gla0 + 2 * GLA_QK_W) // GLA_V_W, (gla0 + 2 * GLA_QK_W) // GLA_V_W + 1,
                       n_main // LANES)
            w_out = ev_w_out[i].astype(BF16)
            acts, ws = [o_a, o_b], [w_out[:MOBA_W], w_out[MOBA_W:]]
        else:
            n_main = 4 * GDN_W
            w_in = jnp.concatenate([od_w_in[i][:, :n_main], _pad_cols(od_w_in[i][:, n_main:], LANES)], axis=1)
            proj = _mod_matmul(x, sc1, sh1, w_in.astype(BF16))
            pvec = jnp.zeros((2, LANES), F32)
            pvec = pvec.at[0, GDN_HEADS:2 * GDN_HEADS].set(od_a_log[i])
            pvec = pvec.at[1, GDN_HEADS:2 * GDN_HEADS].set(od_dt_bias[i])
            o = _gdn(proj, od_conv_w[i], pvec, od_norm[i].reshape(1, -1))
            acts, ws = [o], [od_w_out[i].astype(BF16)]

        x, h2, rl = _mixout(acts, ws, x, g1, ln_mix_g[layer].reshape(1, d), ln_mix_b[layer].reshape(1, d),
                            sc2, sh2, moe_router_w[layer].T)
        router_tn = min(1024, s)
        gates, gates_t, cnt_raw = _router(rl, moe_router_b[layer], router_tn)
        x = _moe(h2, gates, gates_t, cnt_raw, router_tn, layer, moe_w_gate, moe_w_up, moe_w_down,
                 sh_w_gate[layer].astype(BF16), sh_w_up[layer].astype(BF16),
                 sh_w_down[layer].astype(BF16), x, g2, ln_ffn_g[layer].reshape(1, d), ln_ffn_b[layer].reshape(1, d))
    return x
```
